```python
import math
import jax, jax.numpy as jnp
from jax import lax
import numpy as np

D_MODEL = 1024
BATCH = 8
SEQ = 8192
DEPTH = 2

RMS_EPS = 1e-6
LN_EPS = 1e-5
MIX_WIDTH = D_MODEL
POOL_WIDTH = MIX_WIDTH // 2
POOL_WINDOWS = (2, 4, 8, 16)
POOL_GROUPS = len(POOL_WINDOWS)
POOL_GROUP_DIM = POOL_WIDTH // POOL_GROUPS
SSM_WIDTH = MIX_WIDTH - POOL_WIDTH
SSM_GROUP_DIM = 16
SSM_GROUPS = SSM_WIDTH // SSM_GROUP_DIM
SSM_STATE = 64
DT_MIN = 0.001
DT_MAX = 0.1
CONV_CHANNELS = D_MODEL
CONV_KERNEL = 31
N_EVEN = (DEPTH + 1) // 2
N_ODD = DEPTH // 2

kernel_name = "hybrid_pool_s5_conformer_gated"


def _rmsnorm(x, g):
    x32 = x.astype(jnp.float32)
    y = x32 * lax.rsqrt(jnp.mean(x32 * x32, axis=-1, keepdims=True) + RMS_EPS)
    return (y * g.astype(jnp.float32)).astype(x.dtype)


def _multiscale_pool(u, pool_w, pool_scale):
    b, s, _ = u.shape
    u32 = u.astype(jnp.float32).reshape(b, s, POOL_GROUPS, POOL_GROUP_DIM)
    csum = jnp.cumsum(u32, axis=1)
    pos = jnp.arange(1, s + 1, dtype=jnp.float32)[None, :, None]
    outs = []
    for g, w in enumerate(POOL_WINDOWS):
        c = csum[:, :, g, :]
        lag = jnp.pad(c, ((0, 0), (w, 0), (0, 0)))[:, :s, :]
        outs.append((c - lag) / jnp.minimum(pos, float(w)) - u32[:, :, g, :])
    pooled = jnp.stack(outs, axis=2)
    mixed = jnp.einsum('bsgc,gcd->bsgd', pooled, pool_w.astype(jnp.float32))
    return (mixed.reshape(b, s, POOL_WIDTH) * pool_scale.astype(jnp.float32)).astype(u.dtype)


def _scan_combine(left, right):
    a1r, a1i, b1r, b1i = left
    a2r, a2i, b2r, b2i = right
    ar = a2r * a1r - a2i * a1i
    ai = a2r * a1i + a2i * a1r
    br = a2r * b1r - a2i * b1i + b2r
    bi = a2r * b1i + a2i * b1r + b2i
    return (ar, ai, br, bi)


def _s5(u, log_dt, a_re, a_im, b_re, b_im, c_re, c_im, d_skip, w_glu):
    bsz, s, _ = u.shape
    f32 = jnp.float32
    u32 = u.astype(f32).reshape(bsz, s, SSM_GROUPS, SSM_GROUP_DIM)
    dt = jnp.exp(log_dt.astype(f32))[:, None]
    ar = a_re.astype(f32)
    ai = a_im.astype(f32)
    mag = jnp.exp(ar * dt)
    ang = ai * dt
    abar_re = mag * jnp.cos(ang)
    abar_im = mag * jnp.sin(ang)
    den = ar * ar + ai * ai
    nr = abar_re - 1.0
    ni = abar_im
    k_re = (nr * ar + ni * ai) / den
    k_im = (ni * ar - nr * ai) / den
    br = b_re.astype(f32)
    bi = b_im.astype(f32)
    bb_re = k_re[..., None] * br - k_im[..., None] * bi
    bb_im = k_re[..., None] * bi + k_im[..., None] * br
    bu_re = jnp.einsum('bsgh,gph->bsgp', u32, bb_re)
    bu_im = jnp.einsum('bsgh,gph->bsgp', u32, bb_im)
    shape = bu_re.shape
    _, _, x_re, x_im = lax.associative_scan(
        _scan_combine,
        (jnp.broadcast_to(abar_re, shape), jnp.broadcast_to(abar_im, shape), bu_re, bu_im),
        axis=1)
    y = (jnp.einsum('bsgp,ghp->bsgh', x_re, c_re.astype(f32))
         - jnp.einsum('bsgp,ghp->bsgh', x_im, c_im.astype(f32)))
    y = y.reshape(bsz, s, SSM_WIDTH) + d_skip.astype(f32) * u32.reshape(bsz, s, SSM_WIDTH)
    gv = jnp.einsum('bsc,ce->bse', y, w_glu.astype(f32))
    val, gate = jnp.split(gv, 2, axis=-1)
    return (val * jax.nn.sigmoid(gate)).astype(u.dtype)


def _even_mixer(h, w_in, pool_w, pool_scale, log_dt, a_re, a_im, b_re, b_im,
                c_re, c_im, d_skip, w_glu, w_out):
    proj = jnp.einsum('bsd,de->bse', h, w_in)
    u_pool = proj[..., :POOL_WIDTH]
    u_ssm = proj[..., POOL_WIDTH:MIX_WIDTH]
    z = proj[..., MIX_WIDTH:]
    y_pool = _multiscale_pool(u_pool, pool_w, pool_scale)
    y_ssm = _s5(u_ssm, log_dt, a_re, a_im, b_re, b_im, c_re, c_im, d_skip, w_glu)
    y = jnp.concatenate([y_pool, y_ssm], axis=-1) * jax.nn.silu(z)
    return jnp.einsum('bse,ed->bsd', y, w_out)


def _odd_mixer(h, w_in, conv_w, conv_b, ln_g, ln_b, w_out):
    proj = jnp.einsum('bsd,de->bse', h, w_in)
    val = proj[..., :CONV_CHANNELS]
    gt = proj[..., CONV_CHANNELS:2 * CONV_CHANNELS]
    z = proj[..., 2 * CONV_CHANNELS:]
    g = (val * jax.nn.sigmoid(gt)).astype(jnp.float32)
    kern = conv_w.astype(jnp.float32)[:, None, :]
    c = lax.conv_general_dilated(
        g, kern, window_strides=(1,), padding=[(CONV_KERNEL - 1, 0)],
        dimension_numbers=('NWC', 'WIO', 'NWC'), feature_group_count=CONV_CHANNELS)
    c = c + conv_b.astype(jnp.float32)
    mu = jnp.mean(c, axis=-1, keepdims=True)
    cc = c - mu
    var = jnp.mean(cc * cc, axis=-1, keepdims=True)
    c = cc * lax.rsqrt(var + LN_EPS) * ln_g.astype(jnp.float32) + ln_b.astype(jnp.float32)
    y = (jax.nn.silu(c) * jax.nn.silu(z.astype(jnp.float32))).astype(h.dtype)
    return jnp.einsum('bse,ed->bsd', y, w_out)


def _fwd_setup_inputs(seed: int = 0) -> dict:
    key = jax.random.key(seed)
    ks = jax.random.split(key, 24)
    f32 = jnp.float32

    def nrm(k, shape, scale):
        return jax.random.normal(k, shape, f32) * scale

    n_idx = jnp.arange(SSM_STATE, dtype=f32)
    a_im0 = jnp.broadcast_to(math.pi * n_idx, (N_EVEN, SSM_GROUPS, SSM_STATE))
    return {
        "x": jax.random.normal(ks[0], (BATCH, SEQ, D_MODEL), f32),
        "even_norm": 1.0 + nrm(ks[1], (N_EVEN, D_MODEL), 0.05),
        "even_w_in": nrm(ks[2], (N_EVEN, D_MODEL, 2 * MIX_WIDTH), D_MODEL ** -0.5),
        "pool_w": nrm(ks[3], (N_EVEN, POOL_GROUPS, POOL_GROUP_DIM, POOL_GROUP_DIM), POOL_GROUP_DIM ** -0.5),
        "pool_scale": 1.0 + nrm(ks[4], (N_EVEN, POOL_WIDTH), 0.05),
        "ssm_log_dt": jax.random.uniform(ks[5], (N_EVEN, SSM_GROUPS), f32,
                                         math.log(DT_MIN), math.log(DT_MAX)),
        "ssm_a_re": -0.5 * jnp.exp(nrm(ks[6], (N_EVEN, SSM_GROUPS, SSM_STATE), 0.05)),
        "ssm_a_im": a_im0 + nrm(ks[7], (N_EVEN, SSM_GROUPS, SSM_STATE), 0.01),
        "ssm_b_re": nrm(ks[8], (N_EVEN, SSM_GROUPS, SSM_STATE, SSM_GROUP_DIM), (2 * SSM_GROUP_DIM) ** -0.5),
        "ssm_b_im": nrm(ks[9], (N_EVEN, SSM_GROUPS, SSM_STATE, SSM_GROUP_DIM), (2 * SSM_GROUP_DIM) ** -0.5),
        "ssm_c_re": nrm(ks[10], (N_EVEN, SSM_GROUPS, SSM_GROUP_DIM, SSM_STATE), SSM_STATE ** -0.5),
        "ssm_c_im": nrm(ks[11], (N_EVEN, SSM_GROUPS, SSM_GROUP_DIM, SSM_STATE), SSM_STATE ** -0.5),
        "ssm_d": nrm(ks[12], (N_EVEN, SSM_WIDTH), 1.0),
        "ssm_w_glu": nrm(ks[13], (N_EVEN, SSM_WIDTH, 2 * SSM_WIDTH), SSM_WIDTH ** -0.5),
        "even_w_out": nrm(ks[14], (N_EVEN, MIX_WIDTH, D_MODEL), MIX_WIDTH ** -0.5),
        "odd_norm": 1.0 + nrm(ks[15], (N_ODD, D_MODEL), 0.05),
        "odd_w_in": nrm(ks[16], (N_ODD, D_MODEL, 3 * CONV_CHANNELS), D_MODEL ** -0.5),
        "conv_w": nrm(ks[17], (N_ODD, CONV_KERNEL, CONV_CHANNELS), CONV_KERNEL ** -0.5),
        "conv_b": nrm(ks[18], (N_ODD, CONV_CHANNELS), 0.02),
        "conv_ln_g": 1.0 + nrm(ks[19], (N_ODD, CONV_CHANNELS), 0.05),
        "conv_ln_b": nrm(ks[20], (N_ODD, CONV_CHANNELS), 0.02),
        "odd_w_out": nrm(ks[21], (N_ODD, CONV_CHANNELS, D_MODEL), CONV_CHANNELS ** -0.5),
        "final_norm": 1.0 + nrm(ks[22], (D_MODEL,), 0.05),
    }


def _fwd_reference(x, even_norm, even_w_in, pool_w, pool_scale, ssm_log_dt, ssm_a_re, ssm_a_im,
              ssm_b_re, ssm_b_im, ssm_c_re, ssm_c_im, ssm_d, ssm_w_glu, even_w_out,
              odd_norm, odd_w_in, conv_w, conv_b, conv_ln_g, conv_ln_b, odd_w_out, final_norm):
    for i in range(DEPTH):
        j = i // 2
        if i % 2 == 0:
            h = _rmsnorm(x, even_norm[j])
            x = x + _even_mixer(h, even_w_in[j], pool_w[j], pool_scale[j], ssm_log_dt[j],
                                ssm_a_re[j], ssm_a_im[j], ssm_b_re[j], ssm_b_im[j],
                                ssm_c_re[j], ssm_c_im[j], ssm_d[j], ssm_w_glu[j], even_w_out[j])
        else:
            h = _rmsnorm(x, odd_norm[j])
            x = x + _odd_mixer(h, odd_w_in[j], conv_w[j], conv_b[j], conv_ln_g[j],
                               conv_ln_b[j], odd_w_out[j])
    return _rmsnorm(x, final_norm)


import jax as _jax
import jax.numpy as _jnp

TWIN_FORMAT = 'train_step'
FWD_PARAMS = ['x', 'even_norm', 'even_w_in', 'pool_w', 'pool_scale', 'ssm_log_dt', 'ssm_a_re', 'ssm_a_im', 'ssm_b_re', 'ssm_b_im', 'ssm_c_re', 'ssm_c_im', 'ssm_d', 'ssm_w_glu', 'even_w_out', 'odd_norm', 'odd_w_in', 'conv_w', 'conv_b', 'conv_ln_g', 'conv_ln_b', 'odd_w_out', 'final_norm']
TWIN_WEIGHTS = ['even_norm', 'even_w_in', 'pool_w', 'pool_scale', 'ssm_log_dt', 'ssm_a_re', 'ssm_a_im', 'ssm_b_re', 'ssm_b_im', 'ssm_c_re', 'ssm_c_im', 'ssm_d', 'ssm_w_glu', 'even_w_out', 'odd_norm', 'odd_w_in', 'conv_w', 'conv_b', 'conv_ln_g', 'conv_ln_b', 'odd_w_out', 'final_norm']
TWIN_DIFF_INPUT = 'x'
TWIN_INPUTS = ['x', 'even_norm', 'even_w_in', 'pool_w', 'pool_scale', 'ssm_log_dt', 'ssm_a_re', 'ssm_a_im', 'ssm_b_re', 'ssm_b_im', 'ssm_c_re', 'ssm_c_im', 'ssm_d', 'ssm_w_glu', 'even_w_out', 'odd_norm', 'odd_w_in', 'conv_w', 'conv_b', 'conv_ln_g', 'conv_ln_b', 'odd_w_out', 'final_norm', 'loss_target', 'm_even_norm', 'm_even_w_in', 'm_pool_w', 'm_pool_scale', 'm_ssm_log_dt', 'm_ssm_a_re', 'm_ssm_a_im', 'm_ssm_b_re', 'm_ssm_b_im', 'm_ssm_c_re', 'm_ssm_c_im', 'm_ssm_d', 'm_ssm_w_glu', 'm_even_w_out', 'm_odd_norm', 'm_odd_w_in', 'm_conv_w', 'm_conv_b', 'm_conv_ln_g', 'm_conv_ln_b', 'm_odd_w_out', 'm_final_norm', 'v_even_norm', 'v_even_w_in', 'v_pool_w', 'v_pool_scale', 'v_ssm_log_dt', 'v_ssm_a_re', 'v_ssm_a_im', 'v_ssm_b_re', 'v_ssm_b_im', 'v_ssm_c_re', 'v_ssm_c_im', 'v_ssm_d', 'v_ssm_w_glu', 'v_even_w_out', 'v_odd_norm', 'v_odd_w_in', 'v_conv_w', 'v_conv_b', 'v_conv_ln_g', 'v_conv_ln_b', 'v_odd_w_out', 'v_final_norm']
TWIN_OUTPUTS = ['loss', 'grad_x', 'grad_even_norm', 'grad_even_w_in', 'grad_pool_w', 'grad_pool_scale', 'grad_ssm_log_dt', 'grad_ssm_a_re', 'grad_ssm_a_im', 'grad_ssm_b_re', 'grad_ssm_b_im', 'grad_ssm_c_re', 'grad_ssm_c_im', 'grad_ssm_d', 'grad_ssm_w_glu', 'grad_even_w_out', 'grad_odd_norm', 'grad_odd_w_in', 'grad_conv_w', 'grad_conv_b', 'grad_conv_ln_g', 'grad_conv_ln_b', 'grad_odd_w_out', 'grad_final_norm', 'delta_even_norm', 'delta_even_w_in', 'delta_pool_w', 'delta_pool_scale', 'delta_ssm_log_dt', 'delta_ssm_a_re', 'delta_ssm_a_im', 'delta_ssm_b_re', 'delta_ssm_b_im', 'delta_ssm_c_re', 'delta_ssm_c_im', 'delta_ssm_d', 'delta_ssm_w_glu', 'delta_even_w_out', 'delta_odd_norm', 'delta_odd_w_in', 'delta_conv_w', 'delta_conv_b', 'delta_conv_ln_g', 'delta_conv_ln_b', 'delta_odd_w_out', 'delta_final_norm', 'new_m_even_norm', 'new_m_even_w_in', 'new_m_pool_w', 'new_m_pool_scale', 'new_m_ssm_log_dt', 'new_m_ssm_a_re', 'new_m_ssm_a_im', 'new_m_ssm_b_re', 'new_m_ssm_b_im', 'new_m_ssm_c_re', 'new_m_ssm_c_im', 'new_m_ssm_d', 'new_m_ssm_w_glu', 'new_m_even_w_out', 'new_m_odd_norm', 'new_m_odd_w_in', 'new_m_conv_w', 'new_m_conv_b', 'new_m_conv_ln_g', 'new_m_conv_ln_b', 'new_m_odd_w_out', 'new_m_final_norm', 'new_v_even_norm', 'new_v_even_w_in', 'new_v_pool_w', 'new_v_pool_scale', 'new_v_ssm_log_dt', 'new_v_ssm_a_re', 'new_v_ssm_a_im', 'new_v_ssm_b_re', 'new_v_ssm_b_im', 'new_v_ssm_c_re', 'new_v_ssm_c_im', 'new_v_ssm_d', 'new_v_ssm_w_glu', 'new_v_even_w_out', 'new_v_odd_norm', 'new_v_odd_w_in', 'new_v_conv_w', 'new_v_conv_b', 'new_v_conv_ln_g', 'new_v_conv_ln_b', 'new_v_odd_w_out', 'new_v_final_norm']
TWIN_LEAF_KINDS = {'loss': 'loss', 'grad_x': 'grad_x', 'grad_even_norm': 'grad_w', 'grad_even_w_in': 'grad_w', 'grad_pool_w': 'grad_w', 'grad_pool_scale': 'grad_w', 'grad_ssm_log_dt': 'grad_w', 'grad_ssm_a_re': 'grad_w', 'grad_ssm_a_im': 'grad_w', 'grad_ssm_b_re': 'grad_w', 'grad_ssm_b_im': 'grad_w', 'grad_ssm_c_re': 'grad_w', 'grad_ssm_c_im': 'grad_w', 'grad_ssm_d': 'grad_w', 'grad_ssm_w_glu': 'grad_w', 'grad_even_w_out': 'grad_w', 'grad_odd_norm': 'grad_w', 'grad_odd_w_in': 'grad_w', 'grad_conv_w': 'grad_w', 'grad_conv_b': 'grad_w', 'grad_conv_ln_g': 'grad_w', 'grad_conv_ln_b': 'grad_w', 'grad_odd_w_out': 'grad_w', 'grad_final_norm': 'grad_w', 'delta_even_norm': 'delta_w', 'delta_even_w_in': 'delta_w', 'delta_pool_w': 'delta_w', 'delta_pool_scale': 'delta_w', 'delta_ssm_log_dt': 'delta_w', 'delta_ssm_a_re': 'delta_w', 'delta_ssm_a_im': 'delta_w', 'delta_ssm_b_re': 'delta_w', 'delta_ssm_b_im': 'delta_w', 'delta_ssm_c_re': 'delta_w', 'delta_ssm_c_im': 'delta_w', 'delta_ssm_d': 'delta_w', 'delta_ssm_w_glu': 'delta_w', 'delta_even_w_out': 'delta_w', 'delta_odd_norm': 'delta_w', 'delta_odd_w_in': 'delta_w', 'delta_conv_w': 'delta_w', 'delta_conv_b': 'delta_w', 'delta_conv_ln_g': 'delta_w', 'delta_conv_ln_b': 'delta_w', 'delta_odd_w_out': 'delta_w', 'delta_final_norm': 'delta_w', 'new_m_even_norm': 'new_m', 'new_m_even_w_in': 'new_m', 'new_m_pool_w': 'new_m', 'new_m_pool_scale': 'new_m', 'new_m_ssm_log_dt': 'new_m', 'new_m_ssm_a_re': 'new_m', 'new_m_ssm_a_im': 'new_m', 'new_m_ssm_b_re': 'new_m', 'new_m_ssm_b_im': 'new_m', 'new_m_ssm_c_re': 'new_m', 'new_m_ssm_c_im': 'new_m', 'new_m_ssm_d': 'new_m', 'new_m_ssm_w_glu': 'new_m', 'new_m_even_w_out': 'new_m', 'new_m_odd_norm': 'new_m', 'new_m_odd_w_in': 'new_m', 'new_m_conv_w': 'new_m', 'new_m_conv_b': 'new_m', 'new_m_conv_ln_g': 'new_m', 'new_m_conv_ln_b': 'new_m', 'new_m_odd_w_out': 'new_m', 'new_m_final_norm': 'new_m', 'new_v_even_norm': 'new_v', 'new_v_even_w_in': 'new_v', 'new_v_pool_w': 'new_v', 'new_v_pool_scale': 'new_v', 'new_v_ssm_log_dt': 'new_v', 'new_v_ssm_a_re': 'new_v', 'new_v_ssm_a_im': 'new_v', 'new_v_ssm_b_re': 'new_v', 'new_v_ssm_b_im': 'new_v', 'new_v_ssm_c_re': 'new_v', 'new_v_ssm_c_im': 'new_v', 'new_v_ssm_d': 'new_v', 'new_v_ssm_w_glu': 'new_v', 'new_v_even_w_out': 'new_v', 'new_v_odd_norm': 'new_v', 'new_v_odd_w_in': 'new_v', 'new_v_conv_w': 'new_v', 'new_v_conv_b': 'new_v', 'new_v_conv_ln_g': 'new_v', 'new_v_conv_ln_b': 'new_v', 'new_v_odd_w_out': 'new_v', 'new_v_final_norm': 'new_v'}


def _forward(args):
    return _fwd_reference(*[args[k] for k in FWD_PARAMS])


def _output_shape():
    def fwd():
        inp = _fwd_setup_inputs(0)
        return _fwd_reference(*[inp[k] for k in FWD_PARAMS])
    out = _jax.eval_shape(fwd)
    return out.shape, out.dtype

N_MICROBATCH = 1
ADAM_LR = 0.001
ADAM_B1 = 0.9
ADAM_B2 = 0.999
ADAM_EPS = 1e-08
ADAM_WD = 0.01
ADAM_STEP = 10
PER_EXAMPLE_BATCH_AXIS = {'x': 0, 'loss_target': 0}
SHARED_INPUTS = []
_WEIGHT_DTYPES = {'even_norm': _jnp.float32, 'even_w_in': _jnp.float32, 'pool_w': _jnp.float32, 'pool_scale': _jnp.float32, 'ssm_log_dt': _jnp.float32, 'ssm_a_re': _jnp.float32, 'ssm_a_im': _jnp.float32, 'ssm_b_re': _jnp.float32, 'ssm_b_im': _jnp.float32, 'ssm_c_re': _jnp.float32, 'ssm_c_im': _jnp.float32, 'ssm_d': _jnp.float32, 'ssm_w_glu': _jnp.float32, 'even_w_out': _jnp.float32, 'odd_norm': _jnp.float32, 'odd_w_in': _jnp.float32, 'conv_w': _jnp.float32, 'conv_b': _jnp.float32, 'conv_ln_g': _jnp.float32, 'conv_ln_b': _jnp.float32, 'odd_w_out': _jnp.float32, 'final_norm': _jnp.float32}
MOMENT_SCALE = {'even_norm': 1.509390e-01, 'even_w_in': 1.097264e-01, 'pool_w': 1.299582e-01, 'pool_scale': 1.338410e-01, 'ssm_log_dt': 4.570705e+00, 'ssm_a_re': 7.292589e-03, 'ssm_a_im': 6.324016e-03, 'ssm_b_re': 4.660744e-03, 'ssm_b_im': 4.629485e-03, 'ssm_c_re': 6.801478e-03, 'ssm_c_im': 6.546309e-03, 'ssm_d': 9.365224e-02, 'ssm_w_glu': 5.945122e-02, 'even_w_out': 1.081113e-01, 'odd_norm': 1.191033e-01, 'odd_w_in': 6.946213e-02, 'conv_w': 8.223529e-02, 'conv_b': 1.832774e-01, 'conv_ln_g': 9.620846e-02, 'conv_ln_b': 8.460236e-02, 'odd_w_out': 8.070551e-02, 'final_norm': 6.408756e+01}


def _to_microbatches(a, axis):
    t = _jnp.moveaxis(a, axis, 0)
    t = t.reshape((N_MICROBATCH, t.shape[0] // N_MICROBATCH) + t.shape[1:])
    return _jnp.moveaxis(t, 1, axis + 1)


def setup_inputs(seed: int = 0) -> dict:
    inp = _fwd_setup_inputs(seed)
    key = _jax.random.fold_in(_jax.random.key(seed), 7919)
    shape, _ = _output_shape()
    out = dict(inp)
    out["loss_target"] = _jax.random.normal(_jax.random.fold_in(key, 0), shape, _jnp.float32)
    for i, name in enumerate(TWIN_WEIGHTS):
        w = inp[name].astype(_jnp.float32)
        if MOMENT_SCALE is None:
            s = _jnp.sqrt(_jnp.mean(_jnp.square(w)) + 1e-30)
        else:
            s = MOMENT_SCALE[name]
        km, kv = _jax.random.split(_jax.random.fold_in(key, i + 1))
        out[name] = w
        out["m_" + name] = s * _jax.random.normal(km, w.shape, _jnp.float32)
        out["v_" + name] = (s * s) * _jax.random.uniform(kv, w.shape, _jnp.float32, 0.5, 1.5)
    if N_MICROBATCH > 1:
        for name, axis in PER_EXAMPLE_BATCH_AXIS.items():
            out[name] = _to_microbatches(out[name], axis)
    return {'x': out['x'], 'even_norm': out['even_norm'], 'even_w_in': out['even_w_in'], 'pool_w': out['pool_w'], 'pool_scale': out['pool_scale'], 'ssm_log_dt': out['ssm_log_dt'], 'ssm_a_re': out['ssm_a_re'], 'ssm_a_im': out['ssm_a_im'], 'ssm_b_re': out['ssm_b_re'], 'ssm_b_im': out['ssm_b_im'], 'ssm_c_re': out['ssm_c_re'], 'ssm_c_im': out['ssm_c_im'], 'ssm_d': out['ssm_d'], 'ssm_w_glu': out['ssm_w_glu'], 'even_w_out': out['even_w_out'], 'odd_norm': out['odd_norm'], 'odd_w_in': out['odd_w_in'], 'conv_w': out['conv_w'], 'conv_b': out['conv_b'], 'conv_ln_g': out['conv_ln_g'], 'conv_ln_b': out['conv_ln_b'], 'odd_w_out': out['odd_w_out'], 'final_norm': out['final_norm'], 'loss_target': out['loss_target'], 'm_even_norm': out['m_even_norm'], 'm_even_w_in': out['m_even_w_in'], 'm_pool_w': out['m_pool_w'], 'm_pool_scale': out['m_pool_scale'], 'm_ssm_log_dt': out['m_ssm_log_dt'], 'm_ssm_a_re': out['m_ssm_a_re'], 'm_ssm_a_im': out['m_ssm_a_im'], 'm_ssm_b_re': out['m_ssm_b_re'], 'm_ssm_b_im': out['m_ssm_b_im'], 'm_ssm_c_re': out['m_ssm_c_re'], 'm_ssm_c_im': out['m_ssm_c_im'], 'm_ssm_d': out['m_ssm_d'], 'm_ssm_w_glu': out['m_ssm_w_glu'], 'm_even_w_out': out['m_even_w_out'], 'm_odd_norm': out['m_odd_norm'], 'm_odd_w_in': out['m_odd_w_in'], 'm_conv_w': out['m_conv_w'], 'm_conv_b': out['m_conv_b'], 'm_conv_ln_g': out['m_conv_ln_g'], 'm_conv_ln_b': out['m_conv_ln_b'], 'm_odd_w_out': out['m_odd_w_out'], 'm_final_norm': out['m_final_norm'], 'v_even_norm': out['v_even_norm'], 'v_even_w_in': out['v_even_w_in'], 'v_pool_w': out['v_pool_w'], 'v_pool_scale': out['v_pool_scale'], 'v_ssm_log_dt': out['v_ssm_log_dt'], 'v_ssm_a_re': out['v_ssm_a_re'], 'v_ssm_a_im': out['v_ssm_a_im'], 'v_ssm_b_re': out['v_ssm_b_re'], 'v_ssm_b_im': out['v_ssm_b_im'], 'v_ssm_c_re': out['v_ssm_c_re'], 'v_ssm_c_im': out['v_ssm_c_im'], 'v_ssm_d': out['v_ssm_d'], 'v_ssm_w_glu': out['v_ssm_w_glu'], 'v_even_w_out': out['v_even_w_out'], 'v_odd_norm': out['v_odd_norm'], 'v_odd_w_in': out['v_odd_w_in'], 'v_conv_w': out['v_conv_w'], 'v_conv_b': out['v_conv_b'], 'v_conv_ln_g': out['v_conv_ln_g'], 'v_conv_ln_b': out['v_conv_ln_b'], 'v_odd_w_out': out['v_odd_w_out'], 'v_final_norm': out['v_final_norm']}


def _loss(weights, diff, rest, loss_target):
    with _jax.named_scope("forward"):
        args = {**rest, TWIN_DIFF_INPUT: diff, **{k: w.astype(_WEIGHT_DTYPES[k]) for k, w in weights.items()}}
        y = _forward(args)
    with _jax.named_scope("loss_head"):
        err = _jnp.square(y.astype(_jnp.float32) - loss_target)
        return 0.5 * _jnp.sum(_jnp.mean(err, axis=-1)) if err.ndim else 0.5 * err


def _adamw(w, g, m, v):
    m = ADAM_B1 * m + (1.0 - ADAM_B1) * g
    v = ADAM_B2 * v + (1.0 - ADAM_B2) * _jnp.square(g)
    m_hat = m / (1.0 - ADAM_B1 ** ADAM_STEP)
    v_hat = v / (1.0 - ADAM_B2 ** ADAM_STEP)
    delta = -ADAM_LR * (m_hat / (_jnp.sqrt(v_hat) + ADAM_EPS) + ADAM_WD * w)
    return delta, m, v


def reference(x, even_norm, even_w_in, pool_w, pool_scale, ssm_log_dt, ssm_a_re, ssm_a_im, ssm_b_re, ssm_b_im, ssm_c_re, ssm_c_im, ssm_d, ssm_w_glu, even_w_out, odd_norm, odd_w_in, conv_w, conv_b, conv_ln_g, conv_ln_b, odd_w_out, final_norm, loss_target, m_even_norm, m_even_w_in, m_pool_w, m_pool_scale, m_ssm_log_dt, m_ssm_a_re, m_ssm_a_im, m_ssm_b_re, m_ssm_b_im, m_ssm_c_re, m_ssm_c_im, m_ssm_d, m_ssm_w_glu, m_even_w_out, m_odd_norm, m_odd_w_in, m_conv_w, m_conv_b, m_conv_ln_g, m_conv_ln_b, m_odd_w_out, m_final_norm, v_even_norm, v_even_w_in, v_pool_w, v_pool_scale, v_ssm_log_dt, v_ssm_a_re, v_ssm_a_im, v_ssm_b_re, v_ssm_b_im, v_ssm_c_re, v_ssm_c_im, v_ssm_d, v_ssm_w_glu, v_even_w_out, v_odd_norm, v_odd_w_in, v_conv_w, v_conv_b, v_conv_ln_g, v_conv_ln_b, v_odd_w_out, v_final_norm):
    given = dict(x=x, even_norm=even_norm, even_w_in=even_w_in, pool_w=pool_w, pool_scale=pool_scale, ssm_log_dt=ssm_log_dt, ssm_a_re=ssm_a_re, ssm_a_im=ssm_a_im, ssm_b_re=ssm_b_re, ssm_b_im=ssm_b_im, ssm_c_re=ssm_c_re, ssm_c_im=ssm_c_im, ssm_d=ssm_d, ssm_w_glu=ssm_w_glu, even_w_out=even_w_out, odd_norm=odd_norm, odd_w_in=odd_w_in, conv_w=conv_w, conv_b=conv_b, conv_ln_g=conv_ln_g, conv_ln_b=conv_ln_b, odd_w_out=odd_w_out, final_norm=final_norm, loss_target=loss_target, m_even_norm=m_even_norm, m_even_w_in=m_even_w_in, m_pool_w=m_pool_w, m_pool_scale=m_pool_scale, m_ssm_log_dt=m_ssm_log_dt, m_ssm_a_re=m_ssm_a_re, m_ssm_a_im=m_ssm_a_im, m_ssm_b_re=m_ssm_b_re, m_ssm_b_im=m_ssm_b_im, m_ssm_c_re=m_ssm_c_re, m_ssm_c_im=m_ssm_c_im, m_ssm_d=m_ssm_d, m_ssm_w_glu=m_ssm_w_glu, m_even_w_out=m_even_w_out, m_odd_norm=m_odd_norm, m_odd_w_in=m_odd_w_in, m_conv_w=m_conv_w, m_conv_b=m_conv_b, m_conv_ln_g=m_conv_ln_g, m_conv_ln_b=m_conv_ln_b, m_odd_w_out=m_odd_w_out, m_final_norm=m_final_norm, v_even_norm=v_even_norm, v_even_w_in=v_even_w_in, v_pool_w=v_pool_w, v_pool_scale=v_pool_scale, v_ssm_log_dt=v_ssm_log_dt, v_ssm_a_re=v_ssm_a_re, v_ssm_a_im=v_ssm_a_im, v_ssm_b_re=v_ssm_b_re, v_ssm_b_im=v_ssm_b_im, v_ssm_c_re=v_ssm_c_re, v_ssm_c_im=v_ssm_c_im, v_ssm_d=v_ssm_d, v_ssm_w_glu=v_ssm_w_glu, v_even_w_out=v_even_w_out, v_odd_norm=v_odd_norm, v_odd_w_in=v_odd_w_in, v_conv_w=v_conv_w, v_conv_b=v_conv_b, v_conv_ln_g=v_conv_ln_g, v_conv_ln_b=v_conv_ln_b, v_odd_w_out=v_odd_w_out, v_final_norm=v_final_norm)
    weights = {n: given[n] for n in TWIN_WEIGHTS}
    shared = {n: given[n] for n in SHARED_INPUTS}
    per_example = {n: given[n] for n in ['x']}
    grad_fn = _jax.value_and_grad(_loss, argnums=(0, 1))

    def one_microbatch(ex, loss_target):
        ex = dict(ex)
        diff = ex.pop(TWIN_DIFF_INPUT)
        return grad_fn(weights, diff, {**shared, **ex}, loss_target)

    if N_MICROBATCH == 1:
        loss, (grad_w, grad_x) = one_microbatch(per_example, given["loss_target"])
    else:
        def body(carry, xs):
            loss_sum, grad_sum = carry
            l_k, (gw_k, gx_k) = one_microbatch(xs[0], xs[1])
            with _jax.named_scope("update"):
                return (loss_sum + l_k, _jax.tree.map(_jnp.add, grad_sum, gw_k)), gx_k

        init = (_jnp.zeros((), _jnp.float32), _jax.tree.map(_jnp.zeros_like, weights))
        (loss, grad_w), grad_x = _jax.lax.scan(body, init, (per_example, given["loss_target"]))
    with _jax.named_scope("update"):
        delta_w, new_m, new_v = {}, {}, {}
        for n in TWIN_WEIGHTS:
            delta_w[n], new_m[n], new_v[n] = _adamw(weights[n], grad_w[n], given["m_" + n], given["v_" + n])
    return (loss, grad_x, *[grad_w[n] for n in TWIN_WEIGHTS], *[delta_w[n] for n in TWIN_WEIGHTS],
            *[new_m[n] for n in TWIN_WEIGHTS], *[new_v[n] for n in TWIN_WEIGHTS])
```

```python
import functools

import jax
import jax.numpy as jnp
from jax import lax
from jax.experimental import pallas as pl
from jax.experimental.pallas import tpu as pltpu

F32 = jnp.float32
MM = jnp.bfloat16
SDS = jax.ShapeDtypeStruct

RMS_EPS = 1e-6
LN_EPS = 1e-5
POOL_WINDOWS = (2, 4, 8, 16)
POOL_HALO = 16
CONV_KERNEL = 31
CONV_HALO = 32
N_DEV = 8
LANES = 128
SUBLANES = 8
PACK_TILE = SUBLANES * LANES
PACK_ROW_BLOCK = 512
MIB = 1024 * 1024

ADAM_LR = 0.001
ADAM_B1 = 0.9
ADAM_B2 = 0.999
ADAM_EPS = 1e-08
ADAM_WD = 0.01
ADAM_STEP = 10

WEIGHTS = ['even_norm', 'even_w_in', 'pool_w', 'pool_scale', 'ssm_log_dt', 'ssm_a_re', 'ssm_a_im', 'ssm_b_re',
           'ssm_b_im', 'ssm_c_re', 'ssm_c_im', 'ssm_d', 'ssm_w_glu', 'even_w_out', 'odd_norm', 'odd_w_in', 'conv_w',
           'conv_b', 'conv_ln_g', 'conv_ln_b', 'odd_w_out', 'final_norm']
SHARDED = {'even_w_in': 1, 'ssm_w_glu': 1, 'even_w_out': 0, 'odd_norm': 0, 'odd_w_in': 1, 'conv_w': 1, 'conv_b': 0,
           'conv_ln_g': 0, 'conv_ln_b': 0, 'odd_w_out': 0}
SHARDED_ORDER = [n for n in WEIGHTS if n in SHARDED]
REPLICATED_ORDER = [n for n in WEIGHTS if n not in SHARDED]


def _round_up(n, m):
    return (n + m - 1) // m * m


def _sigmoid(x):
    return jax.nn.sigmoid(x)


def _silu(x):
    return x * jax.nn.sigmoid(x)


def _rms(x, g):
    return x * lax.rsqrt(jnp.mean(x * x, axis=-1, keepdims=True) + RMS_EPS) * g


def _dot(a, b):
    return jnp.dot(a.astype(MM), b.astype(MM), preferred_element_type=F32)


def _dot_nt(a, b):
    return lax.dot_general(a.astype(MM), b.astype(MM), (((1,), (1,)), ((), ())), preferred_element_type=F32)


def _dot_tn(a, b):
    return lax.dot_general(a.astype(MM), b.astype(MM), (((0,), (0,)), ((), ())), preferred_element_type=F32)


def _rows(ts, width, col=0):
    return pl.BlockSpec((ts, width), lambda i: (i, col))


def _rows_rev(ts, width, n, col=0):
    return pl.BlockSpec((ts, width), lambda i: (n - 1 - i, col))


def _full(shape):
    zeros = (0,) * len(shape)
    return pl.BlockSpec(shape, lambda i: zeros)


def _seq_params(vmem_mib=48, dims=1):
    return pltpu.CompilerParams(dimension_semantics=("arbitrary",) * dims, vmem_limit_bytes=vmem_mib * MIB)


def _mesh_position():
    x, y, c = lax.axis_index("x"), lax.axis_index("y"), lax.axis_index("c")
    return x, y, c


def _peer(pos, relation):
    x, y, c = pos
    px = 1 - x if relation & 4 else x
    py = 1 - y if relation & 2 else y
    pc = 1 - c if relation & 1 else c
    return px, py, pc


def _all_gather(block):
    rows = block.shape[0]

    def body(src, out, send_sems, recv_sems, local_sem):
        pos = _mesh_position()
        me = 4 * pos[0] + 2 * pos[1] + pos[2]
        mine = pltpu.make_async_copy(src, out.at[me], local_sem)
        mine.start()
        copies = []
        for relation in range(1, N_DEV):
            cp = pltpu.make_async_remote_copy(
                src_ref=src, dst_ref=out.at[me], send_sem=send_sems.at[relation - 1],
                recv_sem=recv_sems.at[relation - 1], device_id=_peer(pos, relation),
                device_id_type=pl.DeviceIdType.MESH)
            cp.start()
            copies.append(cp)
        for cp in copies:
            cp.wait()
        mine.wait()

    return pl.pallas_call(
        body, name="gather_weights",
        out_shape=SDS((N_DEV, rows, LANES), block.dtype),
        in_specs=[pl.BlockSpec(memory_space=pl.ANY)],
        out_specs=pl.BlockSpec(memory_space=pl.ANY),
        scratch_shapes=[pltpu.SemaphoreType.DMA((N_DEV - 1,)), pltpu.SemaphoreType.DMA((N_DEV - 1,)),
                        pltpu.SemaphoreType.DMA(())],
    )(block)


def _exchange(parts):
    rows = parts.shape[1]

    def body(src, out, send_sems, recv_sems, local_sem):
        pos = _mesh_position()
        me = 4 * pos[0] + 2 * pos[1] + pos[2]
        mine = pltpu.make_async_copy(src.at[me], out.at[me], local_sem)
        mine.start()
        copies = []
        for relation in range(1, N_DEV):
            peer = _peer(pos, relation)
            peer_id = 4 * peer[0] + 2 * peer[1] + peer[2]
            cp = pltpu.make_async_remote_copy(
                src_ref=src.at[peer_id], dst_ref=out.at[me], send_sem=send_sems.at[relation - 1],
                recv_sem=recv_sems.at[relation - 1], device_id=peer, device_id_type=pl.DeviceIdType.MESH)
            cp.start()
            copies.append(cp)
        for cp in copies:
            cp.wait()
        mine.wait()

    return pl.pallas_call(
        body, name="exchange_grads",
        out_shape=SDS((N_DEV, rows, LANES), parts.dtype),
        in_specs=[pl.BlockSpec(memory_space=pl.ANY)],
        out_specs=pl.BlockSpec(memory_space=pl.ANY),
        scratch_shapes=[pltpu.SemaphoreType.DMA((N_DEV - 1,)), pltpu.SemaphoreType.DMA((N_DEV - 1,)),
                        pltpu.SemaphoreType.DMA(())],
    )(parts)


def _mm_tn(name, a, b, m, n, a_col=0, b_col=0, sign=1.0, ts=512):
    s = a.shape[0]
    tn = min(n, 512)
    nj = n // tn

    def body(a_ref, b_ref, o_ref):
        @pl.when(pl.program_id(1) == 0)
        def _():
            o_ref[...] = jnp.zeros_like(o_ref)

        o_ref[...] += sign * _dot_tn(a_ref[...], b_ref[...])

    return pl.pallas_call(
        body, name=name, grid=(nj, s // ts),
        in_specs=[pl.BlockSpec((ts, m), lambda j, k: (k, a_col)),
                  pl.BlockSpec((ts, tn), lambda j, k: (k, b_col * nj + j))],
        out_specs=pl.BlockSpec((m, tn), lambda j, k: (0, j)),
        out_shape=SDS((m, n), F32),
        compiler_params=_seq_params(48, dims=2),
    )(a, b)


def _even_in(x, g0, w_in):
    s, d = x.shape
    e = w_in.shape[1]
    ts = 512

    def body(x_ref, g_ref, w_ref, proj_ref, h_ref):
        hb = _rms(x_ref[...], g_ref[...]).astype(MM)
        h_ref[...] = hb
        proj_ref[...] = jnp.dot(hb, w_ref[...], preferred_element_type=F32)

    return pl.pallas_call(
        body, name="even_in", grid=(s // ts,),
        in_specs=[_rows(ts, d), _full((1, d)), _full((d, e))],
        out_specs=[_rows(ts, e), _rows(ts, d)],
        out_shape=[SDS((s, e), F32), SDS((s, d), MM)],
        compiler_params=_seq_params(48),
    )(x, g0, w_in)


def _pool_counts(t0, ts, w):
    pos = (t0 + lax.broadcasted_iota(jnp.int32, (ts, LANES), 0) + 1).astype(F32)
    return jnp.minimum(pos, float(w))


def _pool_fwd(proj):
    s = proj.shape[0]
    width = LANES * len(POOL_WINDOWS)
    ts = 512
    per = ts // POOL_HALO

    def body(prev_ref, u_ref, out_ref, ext):
        i = pl.program_id(0)
        ext[0:POOL_HALO, :] = jnp.where(i == 0, 0.0, prev_ref[...])
        ext[POOL_HALO:, :] = u_ref[...]
        for g, w in enumerate(POOL_WINDOWS):
            cols = slice(g * LANES, (g + 1) * LANES)
            tok = ext[pl.ds(POOL_HALO, ts), cols]
            acc = tok
            for k in range(1, w):
                acc = acc + ext[pl.ds(POOL_HALO - k, ts), cols]
            out_ref[:, cols] = acc / _pool_counts(i * ts, ts, w) - tok

    return pl.pallas_call(
        body, name="pool_fwd", grid=(s // ts,),
        in_specs=[pl.BlockSpec((POOL_HALO, width), lambda i: (jnp.maximum(i * per - 1, 0), 0)),
                  _rows(ts, width)],
        out_specs=_rows(ts, width),
        out_shape=SDS((s, width), F32),
        scratch_shapes=[pltpu.VMEM((ts + POOL_HALO, width), F32)],
        compiler_params=_seq_params(32),
    )(proj, proj)


def _pool_bwd(dp):
    s, width = dp.shape
    ts = 512
    per = ts // POOL_HALO
    n = s // ts

    def body(dp_ref, next_ref, out_ref, ext):
        i = pl.program_id(0)
        nxt = jnp.where(i == n - 1, 0.0, next_ref[...])
        for g, w in enumerate(POOL_WINDOWS):
            cols = slice(g * LANES, (g + 1) * LANES)
            cur = dp_ref[:, cols]
            ext[0:ts, cols] = cur / _pool_counts(i * ts, ts, w)
            ext[ts:, cols] = nxt[:, cols] / _pool_counts((i + 1) * ts, POOL_HALO, w)
            acc = -cur
            for k in range(w):
                acc = acc + ext[pl.ds(k, ts), cols]
            out_ref[:, cols] = acc

    return pl.pallas_call(
        body, name="pool_bwd", grid=(n,),
        in_specs=[_rows(ts, width),
                  pl.BlockSpec((POOL_HALO, width), lambda i: (jnp.minimum((i + 1) * per, s // POOL_HALO - 1), 0))],
        out_specs=_rows(ts, width),
        out_shape=SDS((s, width), F32),
        scratch_shapes=[pltpu.VMEM((ts + POOL_HALO, width), F32)],
        compiler_params=_seq_params(32),
    )(dp, dp)


def _ssm_bu(proj, bb):
    s = proj.shape[0]
    cin, two_n = bb.shape
    n = two_n // 2
    ts = 512

    def body(u_ref, bb_ref, re_ref, im_ref):
        bu = jnp.dot(u_ref[...].astype(MM), bb_ref[...], preferred_element_type=F32)
        re_ref[...] = bu[:, :n]
        im_ref[...] = bu[:, n:]

    return pl.pallas_call(
        body, name="ssm_bu", grid=(s // ts,),
        in_specs=[_rows(ts, cin, col=1), _full((cin, two_n))],
        out_specs=[_rows(ts, n), _rows(ts, n)],
        out_shape=[SDS((s, n), F32), SDS((s, n), F32)],
        compiler_params=_seq_params(48),
    )(proj, bb)


SCAN_LANES = 256


def _cmul(a, b):
    return a[0] * b[0] - a[1] * b[1], a[0] * b[1] + a[1] * b[0]


def _scan_constants(ar_row, ai_row, reverse):
    shape = (SUBLANES, ar_row.shape[1])
    a1 = (jnp.broadcast_to(ar_row, shape), jnp.broadcast_to(ai_row, shape))
    a2 = _cmul(a1, a1)
    a4 = _cmul(a2, a2)
    a8 = _cmul(a4, a4)
    row = lax.broadcasted_iota(jnp.int32, shape, 0)
    expo = (SUBLANES - row) if reverse else (row + 1)
    pr, pi = jnp.ones(shape, F32), jnp.zeros(shape, F32)
    for bit, q in enumerate((a1, a2, a4, a8)):
        take = ((expo >> bit) & 1) == 1
        nr, ni = _cmul((pr, pi), q)
        pr, pi = jnp.where(take, nr, pr), jnp.where(take, ni, pi)
    return row, (a1, a2, a4), (pr, pi)


def _group_scan(v, row, powers, reverse):
    vr, vi = v
    for k, q in zip((1, 2, 4), powers):
        if reverse:
            keep = row < SUBLANES - k
            sr, si = pltpu.roll(vr, SUBLANES - k, 0), pltpu.roll(vi, SUBLANES - k, 0)
        else:
            keep = row >= k
            sr, si = pltpu.roll(vr, k, 0), pltpu.roll(vi, k, 0)
        sr, si = jnp.where(keep, sr, 0.0), jnp.where(keep, si, 0.0)
        tr, ti = _cmul(q, (sr, si))
        vr, vi = vr + tr, vi + ti
    return vr, vi


def _scan_fwd(a_re, a_im, bu_re, bu_im):
    s, n = bu_re.shape
    ts = 512
    groups = ts // SUBLANES

    def body(ar_ref, ai_ref, bre_ref, bim_ref, xre_ref, xim_ref, cre, cim):
        @pl.when(pl.program_id(0) == 0)
        def _():
            cre[...] = jnp.zeros_like(cre)
            cim[...] = jnp.zeros_like(cim)

        for c in range(n // SCAN_LANES):
            cols = slice(c * SCAN_LANES, (c + 1) * SCAN_LANES)
            row, powers, carry_pow = _scan_constants(ar_ref[:, cols], ai_ref[:, cols], False)

            def group(gi, carry):
                r0 = pl.multiple_of(gi * SUBLANES, SUBLANES)
                v = (bre_ref[pl.ds(r0, SUBLANES), cols], bim_ref[pl.ds(r0, SUBLANES), cols])
                vr, vi = _group_scan(v, row, powers, False)
                tr, ti = _cmul(carry_pow, carry)
                vr, vi = vr + tr, vi + ti
                xre_ref[pl.ds(r0, SUBLANES), cols] = vr
                xim_ref[pl.ds(r0, SUBLANES), cols] = vi
                shape = vr.shape
                return (jnp.broadcast_to(vr[SUBLANES - 1:SUBLANES, :], shape),
                        jnp.broadcast_to(vi[SUBLANES - 1:SUBLANES, :], shape))

            cr, ci = lax.fori_loop(0, groups, group, (cre[:, cols], cim[:, cols]), unroll=2)
            cre[:, cols] = cr
            cim[:, cols] = ci

    return pl.pallas_call(
        body, name="ssm_scan", grid=(s // ts,),
        in_specs=[_full((1, n)), _full((1, n)), _rows(ts, n), _rows(ts, n)],
        out_specs=[_rows(ts, n), _rows(ts, n)],
        out_shape=[SDS((s, n), F32), SDS((s, n), F32)],
        scratch_shapes=[pltpu.VMEM((SUBLANES, n), F32), pltpu.VMEM((SUBLANES, n), F32)],
        compiler_params=_seq_params(48),
    )(a_re, a_im, bu_re, bu_im)


def _scan_bwd(a_re, a_im, dx_re, dx_im, x_re, x_im):
    s, n = dx_re.shape
    ts = 256
    nt = s // ts
    groups = ts // SUBLANES

    def body(ar_ref, ai_ref, dre_ref, dim_ref, xre_ref, xim_ref, gre_ref, gim_ref, dar_ref, dai_ref,
             cre, cim, accr, acci):
        i = pl.program_id(0)

        @pl.when(i == 0)
        def _():
            for ref in (cre, cim, accr, acci):
                ref[...] = jnp.zeros_like(ref)

        for c in range(n // SCAN_LANES):
            cols = slice(c * SCAN_LANES, (c + 1) * SCAN_LANES)
            row, powers, carry_pow = _scan_constants(ar_ref[:, cols], -ai_ref[:, cols], True)

            def group(k, state):
                carry, acc = state[:2], state[2:]
                r0 = pl.multiple_of((groups - 1 - k) * SUBLANES, SUBLANES)
                v = (dre_ref[pl.ds(r0, SUBLANES), cols], dim_ref[pl.ds(r0, SUBLANES), cols])
                vr, vi = _group_scan(v, row, powers, True)
                tr, ti = _cmul(carry_pow, carry)
                vr, vi = vr + tr, vi + ti
                gre_ref[pl.ds(r0, SUBLANES), cols] = vr
                gim_ref[pl.ds(r0, SUBLANES), cols] = vi
                nr = jnp.where(row < SUBLANES - 1, pltpu.roll(vr, SUBLANES - 1, 0), carry[0])
                ni = jnp.where(row < SUBLANES - 1, pltpu.roll(vi, SUBLANES - 1, 0), carry[1])
                xr, xi = xre_ref[pl.ds(r0, SUBLANES), cols], xim_ref[pl.ds(r0, SUBLANES), cols]
                shape = vr.shape
                return (jnp.broadcast_to(vr[0:1, :], shape), jnp.broadcast_to(vi[0:1, :], shape),
                        acc[0] + nr * xr + ni * xi, acc[1] + ni * xr - nr * xi)

            out = lax.fori_loop(0, groups, group, (cre[:, cols], cim[:, cols], accr[:, cols], acci[:, cols]),
                                unroll=2)
            cre[:, cols], cim[:, cols], accr[:, cols], acci[:, cols] = out

        @pl.when(i == nt - 1)
        def _():
            dar_ref[...] = jnp.sum(accr[...], axis=0, keepdims=True)
            dai_ref[...] = jnp.sum(acci[...], axis=0, keepdims=True)

    small = pltpu.VMEM((SUBLANES, n), F32)
    return pl.pallas_call(
        body, name="ssm_scan_bwd", grid=(nt,),
        in_specs=[_full((1, n)), _full((1, n))] + [_rows_rev(ts, n, nt)] * 4,
        out_specs=[_rows_rev(ts, n, nt), _rows_rev(ts, n, nt), _full((1, n)), _full((1, n))],
        out_shape=[SDS((s, n), F32), SDS((s, n), F32), SDS((1, n), F32), SDS((1, n), F32)],
        scratch_shapes=[small, small, small, small],
        compiler_params=_seq_params(48),
    )(a_re, a_im, dx_re, dx_im, x_re, x_im)


def _mix_a(mixed, scale, za):
    return mixed * scale * _silu(za)


def _mix_b(val, gate, zb):
    return val * _sigmoid(gate) * _silu(zb)


def _even_mix(pooled, proj, x_re, x_im, x, pw, scale, cc_re, cc_im, d_skip, w_glu, w_out):
    s, d = x.shape
    half = pooled.shape[1]
    n = x_re.shape[1]
    ts = 256

    def body(pooled_ref, u_ref, z_ref, xre_ref, xim_ref, x_ref, pw_ref, scale_ref, ccre_ref, ccim_ref, d_ref,
             wglu_ref, wout_ref, x1_ref, ymix_ref, mixed_ref, yin_ref, gv_ref):
        for g in range(len(POOL_WINDOWS)):
            cols = slice(g * LANES, (g + 1) * LANES)
            mixed_ref[:, cols] = _dot(pooled_ref[:, cols], pw_ref[g])
        yin = (_dot(xre_ref[...], ccre_ref[...]) - _dot(xim_ref[...], ccim_ref[...])
               + d_ref[...] * u_ref[...])
        yin_ref[...] = yin
        gv = _dot(yin, wglu_ref[...])
        gv_ref[...] = gv
        ya = _mix_a(mixed_ref[...], scale_ref[...], z_ref[:, :half])
        yb = _mix_b(gv[:, :half], gv[:, half:], z_ref[:, half:])
        ymix = jnp.concatenate([ya, yb], axis=1).astype(MM)
        ymix_ref[...] = ymix
        x1_ref[...] = x_ref[...] + jnp.dot(ymix, wout_ref[...], preferred_element_type=F32)

    return pl.pallas_call(
        body, name="even_mix", grid=(s // ts,),
        in_specs=[_rows(ts, half), _rows(ts, half, col=1), _rows(ts, d, col=1), _rows(ts, n), _rows(ts, n),
                  _rows(ts, d), _full(pw.shape), _full((1, half)), _full((n, half)), _full((n, half)),
                  _full((1, half)), _full((half, d)), _full((d, d))],
        out_specs=[_rows(ts, d), _rows(ts, d), _rows(ts, half), _rows(ts, half), _rows(ts, d)],
        out_shape=[SDS((s, d), F32), SDS((s, d), MM), SDS((s, half), F32), SDS((s, half), F32), SDS((s, d), F32)],
        compiler_params=_seq_params(56),
    )(pooled, proj, proj, x_re, x_im, x, pw, scale, cc_re, cc_im, d_skip, w_glu, w_out)


def _glu(val, gt):
    return val * _sigmoid(gt)


def _odd_in(x1, g1, w_in):
    s, d = x1.shape
    e = w_in.shape[1]
    ts = 256

    def body(x_ref, g_ref, w_ref, proj_ref, glu_ref, h_ref):
        hb = _rms(x_ref[...], g_ref[...]).astype(MM)
        h_ref[...] = hb
        proj = jnp.dot(hb, w_ref[...], preferred_element_type=F32)
        proj_ref[...] = proj
        glu_ref[...] = _glu(proj[:, :d], proj[:, d:2 * d])

    return pl.pallas_call(
        body, name="odd_in", grid=(s // ts,),
        in_specs=[_rows(ts, d), _full((1, d)), _full((d, e))],
        out_specs=[_rows(ts, e), _rows(ts, d), _rows(ts, d)],
        out_shape=[SDS((s, e), F32), SDS((s, d), F32), SDS((s, d), MM)],
        compiler_params=_seq_params(48),
    )(x1, g1, w_in)


def _conv_fwd(g, w):
    s, ch = g.shape
    ts = 256
    per = ts // CONV_HALO
    lead = CONV_HALO - (CONV_KERNEL - 1)

    def body(prev_ref, g_ref, w_ref, out_ref, ext):
        i = pl.program_id(0)
        ext[0:CONV_HALO, :] = jnp.where(i == 0, 0.0, prev_ref[...])
        ext[CONV_HALO:, :] = g_ref[...]
        acc = jnp.zeros((ts, ch), F32)
        for j in range(CONV_KERNEL):
            acc = acc + w_ref[j:j + 1, :] * ext[pl.ds(lead + j, ts), :]
        out_ref[...] = acc

    return pl.pallas_call(
        body, name="conv_fwd", grid=(s // ts,),
        in_specs=[pl.BlockSpec((CONV_HALO, ch), lambda i: (jnp.maximum(i * per - 1, 0), 0)),
                  _rows(ts, ch), _full(w.shape)],
        out_specs=_rows(ts, ch),
        out_shape=SDS((s, ch), F32),
        scratch_shapes=[pltpu.VMEM((ts + CONV_HALO, ch), F32)],
        compiler_params=_seq_params(32),
    )(g, g, w)


def _conv_bwd(dc, g, w):
    s, ch = dc.shape
    ts = 256
    per = ts // CONV_HALO
    n = s // ts
    lead = CONV_HALO - (CONV_KERNEL - 1)

    def body(dc_ref, next_ref, prev_ref, g_ref, w_ref, dg_ref, dw_ref, ext_dc, ext_g):
        i = pl.program_id(0)

        @pl.when(i == 0)
        def _():
            dw_ref[...] = jnp.zeros_like(dw_ref)

        cur = dc_ref[...]
        ext_dc[0:ts, :] = cur
        ext_dc[ts:, :] = jnp.where(i == n - 1, 0.0, next_ref[...])
        ext_g[0:CONV_HALO, :] = jnp.where(i == 0, 0.0, prev_ref[...])
        ext_g[CONV_HALO:, :] = g_ref[...]
        acc = jnp.zeros((ts, ch), F32)
        for j in range(CONV_KERNEL):
            acc = acc + w_ref[j:j + 1, :] * ext_dc[pl.ds(CONV_KERNEL - 1 - j, ts), :]
            dw_ref[j:j + 1, :] += jnp.sum(cur * ext_g[pl.ds(lead + j, ts), :], axis=0, keepdims=True)
        dg_ref[...] = acc

    return pl.pallas_call(
        body, name="conv_bwd", grid=(n,),
        in_specs=[_rows(ts, ch),
                  pl.BlockSpec((CONV_HALO, ch), lambda i: (jnp.minimum((i + 1) * per, s // CONV_HALO - 1), 0)),
                  pl.BlockSpec((CONV_HALO, ch), lambda i: (jnp.maximum(i * per - 1, 0), 0)),
                  _rows(ts, ch), _full(w.shape)],
        out_specs=[_rows(ts, ch), _full((CONV_HALO, ch))],
        out_shape=[SDS((s, ch), F32), SDS((CONV_HALO, ch), F32)],
        scratch_shapes=[pltpu.VMEM((ts + CONV_HALO, ch), F32), pltpu.VMEM((ts + CONV_HALO, ch), F32)],
        compiler_params=_seq_params(40),
    )(dc, dc, g, g, w)


def _conv_act(c, z1, cb, lg, lb):
    cc = c + cb
    mu = jnp.mean(cc, axis=-1, keepdims=True)
    dev = cc - mu
    var = jnp.mean(dev * dev, axis=-1, keepdims=True)
    cn = dev * lax.rsqrt(var + LN_EPS) * lg + lb
    return _silu(cn) * _silu(z1)


def _odd_out(c, proj1, x1, tgt, cb, lg, lb, w_out, fg):
    s, d = c.shape
    ts = 256

    def body(c_ref, z_ref, x1_ref, t_ref, cb_ref, lg_ref, lb_ref, w_ref, fg_ref,
             dc_ref, dz_ref, dx2_ref, y1_ref, loss_ref, dcb_ref, dlg_ref, dlb_ref, dfg_ref):
        @pl.when(pl.program_id(0) == 0)
        def _():
            for ref in (loss_ref, dcb_ref, dlg_ref, dlb_ref, dfg_ref):
                ref[...] = jnp.zeros_like(ref)

        y1, act_vjp = jax.vjp(_conv_act, c_ref[...], z_ref[...], cb_ref[...], lg_ref[...], lb_ref[...])
        y1b = y1.astype(MM)
        y1_ref[...] = y1b
        x2 = x1_ref[...] + jnp.dot(y1b, w_ref[...], preferred_element_type=F32)
        tgt_tile = t_ref[...]

        def head(x2, fg):
            err = jnp.square(_rms(x2, fg) - tgt_tile)
            return 0.5 * jnp.sum(jnp.mean(err, axis=-1))

        loss, (dx2, dfg) = jax.value_and_grad(head, argnums=(0, 1))(x2, fg_ref[...])
        loss_ref[...] += loss
        dfg_ref[...] += dfg
        dx2_ref[...] = dx2
        dy1 = _dot_nt(dx2, w_ref[...])
        dc, dz, dcb, dlg, dlb = act_vjp(dy1)
        dc_ref[...] = dc
        dz_ref[...] = dz
        dcb_ref[...] += dcb
        dlg_ref[...] += dlg
        dlb_ref[...] += dlb

    vec = SDS((1, d), F32)
    return pl.pallas_call(
        body, name="odd_out", grid=(s // ts,),
        in_specs=[_rows(ts, d), _rows(ts, d, col=2), _rows(ts, d), _rows(ts, d), _full((1, d)), _full((1, d)),
                  _full((1, d)), _full((d, d)), _full((1, d))],
        out_specs=[_rows(ts, d), _rows(ts, d), _rows(ts, d), _rows(ts, d), _full((SUBLANES, LANES)),
                   _full((1, d)), _full((1, d)), _full((1, d)), _full((1, d))],
        out_shape=[SDS((s, d), F32), SDS((s, d), F32), SDS((s, d), F32), SDS((s, d), MM),
                   SDS((SUBLANES, LANES), F32), vec, vec, vec, vec],
        compiler_params=_seq_params(56),
    )(c, proj1, x1, tgt, cb, lg, lb, w_out, fg)


def _odd_in_bwd(x1, proj1, dglu, dz1, dx2, g1, w_in):
    s, d = x1.shape
    e = w_in.shape[1]
    ts = 256

    def body(x_ref, vg_ref, dglu_ref, dz_ref, dx2_ref, g_ref, w_ref, dx1_ref, dp_ref, dg_ref):
        @pl.when(pl.program_id(0) == 0)
        def _():
            dg_ref[...] = jnp.zeros_like(dg_ref)

        _, glu_vjp = jax.vjp(_glu, vg_ref[:, :d], vg_ref[:, d:])
        dval, dgt = glu_vjp(dglu_ref[...])
        dp = jnp.concatenate([dval, dgt, dz_ref[...]], axis=1).astype(MM)
        dp_ref[...] = dp
        dh = _dot_nt(dp, w_ref[...])
        _, rms_vjp = jax.vjp(_rms, x_ref[...], g_ref[...])
        dxa, dg = rms_vjp(dh)
        dx1_ref[...] = dxa + dx2_ref[...]
        dg_ref[...] += dg

    return pl.pallas_call(
        body, name="odd_in_bwd", grid=(s // ts,),
        in_specs=[_rows(ts, d), _rows(ts, 2 * d), _rows(ts, d), _rows(ts, d), _rows(ts, d), _full((1, d)),
                  _full((d, e))],
        out_specs=[_rows(ts, d), _rows(ts, e), _full((1, d))],
        out_shape=[SDS((s, d), F32), SDS((s, e), MM), SDS((1, d), F32)],
        compiler_params=_seq_params(56),
    )(x1, proj1, dglu, dz1, dx2, g1, w_in)


def _even_mix_bwd(pooled, proj, mixed, gv, dx1, pw, scale, cc_re, cc_im, d_skip, w_glu, w_out):
    s, d = dx1.shape
    half = pooled.shape[1]
    n = cc_re.shape[0]
    ts = 256
    groups = len(POOL_WINDOWS)

    def body(pooled_ref, u_ref, z_ref, mixed_ref, gv_ref, dx1_ref, pw_ref, scale_ref, ccre_ref, ccim_ref, d_ref,
             wglu_ref, wout_ref, dpooled_ref, du_ref, dz_ref, dxre_ref, dxim_ref, dyin_ref, dgv_ref,
             dpw_ref, dscale_ref, dd_ref):
        @pl.when(pl.program_id(0) == 0)
        def _():
            for ref in (dpw_ref, dscale_ref, dd_ref):
                ref[...] = jnp.zeros_like(ref)

        dymix = _dot_nt(dx1_ref[...], wout_ref[...])
        _, a_vjp = jax.vjp(_mix_a, mixed_ref[...], scale_ref[...], z_ref[:, :half])
        dmixed, dscale, dza = a_vjp(dymix[:, :half])
        _, b_vjp = jax.vjp(_mix_b, gv_ref[:, :half], gv_ref[:, half:], z_ref[:, half:])
        dval, dgate, dzb = b_vjp(dymix[:, half:])
        dz_ref[:, :half] = dza
        dz_ref[:, half:] = dzb
        dscale_ref[...] += dscale
        dgv = jnp.concatenate([dval, dgate], axis=1).astype(MM)
        dgv_ref[...] = dgv
        dyin = _dot_nt(dgv, wglu_ref[...])
        dd_ref[...] += jnp.sum(dyin * u_ref[...], axis=0, keepdims=True)
        du_ref[...] = d_ref[...] * dyin
        dyb = dyin.astype(MM)
        dyin_ref[...] = dyb
        dxre_ref[...] = _dot_nt(dyb, ccre_ref[...])
        dxim_ref[...] = -_dot_nt(dyb, ccim_ref[...])
        for g in range(groups):
            cols = slice(g * LANES, (g + 1) * LANES)
            dm = dmixed[:, cols].astype(MM)
            dpooled_ref[:, cols] = _dot_nt(dm, pw_ref[g])
            dpw_ref[g] += _dot_tn(pooled_ref[:, cols], dm)

    return pl.pallas_call(
        body, name="even_mix_bwd", grid=(s // ts,),
        in_specs=[_rows(ts, half), _rows(ts, half, col=1), _rows(ts, d, col=1), _rows(ts, half), _rows(ts, d),
                  _rows(ts, d), _full(pw.shape), _full((1, half)), _full((n, half)), _full((n, half)),
                  _full((1, half)), _full((half, d)), _full((d, d))],
        out_specs=[_rows(ts, half), _rows(ts, half), _rows(ts, d), _rows(ts, n), _rows(ts, n), _rows(ts, half),
                   _rows(ts, d), _full(pw.shape), _full((1, half)), _full((1, half))],
        out_shape=[SDS((s, half), F32), SDS((s, half), F32), SDS((s, d), F32), SDS((s, n), F32), SDS((s, n), F32),
                   SDS((s, half), MM), SDS((s, d), MM), SDS(pw.shape, F32), SDS((1, half), F32),
                   SDS((1, half), F32)],
        compiler_params=_seq_params(56),
    )(pooled, proj, proj, mixed, gv, dx1, pw, scale, cc_re, cc_im, d_skip, w_glu, w_out)


def _ssm_bu_bwd(g_re, g_im, du_skip, bb):
    s, n = g_re.shape
    cin = bb.shape[0]
    ts = 512

    def body(gre_ref, gim_ref, du_ref, bb_ref, out_ref):
        out_ref[...] = (du_ref[...] + _dot_nt(gre_ref[...], bb_ref[:, :n]) + _dot_nt(gim_ref[...], bb_ref[:, n:]))

    return pl.pallas_call(
        body, name="ssm_bu_bwd", grid=(s // ts,),
        in_specs=[_rows(ts, n), _rows(ts, n), _rows(ts, cin), _full(bb.shape)],
        out_specs=_rows(ts, cin),
        out_shape=SDS((s, cin), F32),
        compiler_params=_seq_params(48),
    )(g_re, g_im, du_skip, bb)


def _even_in_bwd(x, du_pool, du_ssm, dz, dx1, g0, w_in):
    s, d = x.shape
    half = du_pool.shape[1]
    e = w_in.shape[1]
    ts = 256

    def body(x_ref, dup_ref, dus_ref, dz_ref, dx1_ref, g_ref, w_ref, gx_ref, dg_ref):
        @pl.when(pl.program_id(0) == 0)
        def _():
            dg_ref[...] = jnp.zeros_like(dg_ref)

        dp = jnp.concatenate([dup_ref[...], dus_ref[...], dz_ref[...]], axis=1)
        dh = _dot_nt(dp, w_ref[...])
        _, rms_vjp = jax.vjp(_rms, x_ref[...], g_ref[...])
        dxa, dg = rms_vjp(dh)
        gx_ref[...] = dxa + dx1_ref[...]
        dg_ref[...] += dg

    return pl.pallas_call(
        body, name="even_in_bwd", grid=(s // ts,),
        in_specs=[_rows(ts, d), _rows(ts, half), _rows(ts, half), _rows(ts, d), _rows(ts, d), _full((1, d)),
                  _full((d, e))],
        out_specs=[_rows(ts, d), _full((1, d))],
        out_shape=[SDS((s, d), F32), SDS((1, d), F32)],
        compiler_params=_seq_params(48),
    )(x, du_pool, du_ssm, dz, dx1, g0, w_in)


def _discretise(log_dt, ar, ai, br, bi):
    dt = jnp.exp(log_dt)
    mag = jnp.exp(ar * dt)
    ang = ai * dt
    abr = mag * jnp.cos(ang)
    abi = mag * jnp.sin(ang)
    den = ar * ar + ai * ai
    nr = abr - 1.0
    ni = abi
    kr = (nr * ar + ni * ai) / den
    ki = (ni * ar - nr * ai) / den
    bbr = kr[None] * br - ki[None] * bi
    bbi = kr[None] * bi + ki[None] * br
    return abr, abi, bbr, bbi


def _whole(n):
    return [pl.BlockSpec(memory_space=pltpu.VMEM)] * n


def _disc_fwd(log_dt, ar, ai, br, bi):
    def body(ld_ref, ar_ref, ai_ref, br_ref, bi_ref, abr_ref, abi_ref, bbr_ref, bbi_ref):
        out = _discretise(ld_ref[...], ar_ref[...], ai_ref[...], br_ref[...], bi_ref[...])
        for ref, val in zip((abr_ref, abi_ref, bbr_ref, bbi_ref), out):
            ref[...] = val

    return pl.pallas_call(
        body, name="ssm_discretise", in_specs=_whole(5), out_specs=_whole(4),
        out_shape=[SDS(ar.shape, F32), SDS(ar.shape, F32), SDS(br.shape, F32), SDS(br.shape, F32)],
    )(log_dt, ar, ai, br, bi)


def _disc_bwd(log_dt, ar, ai, br, bi, dabr, dabi, dbbr, dbbi):
    def body(ld_ref, ar_ref, ai_ref, br_ref, bi_ref, dabr_ref, dabi_ref, dbbr_ref, dbbi_ref,
             dld_ref, dar_ref, dai_ref, dbr_ref, dbi_ref):
        _, vjp = jax.vjp(_discretise, ld_ref[...], ar_ref[...], ai_ref[...], br_ref[...], bi_ref[...])
        grads = vjp((dabr_ref[...], dabi_ref[...], dbbr_ref[...], dbbi_ref[...]))
        for ref, val in zip((dld_ref, dar_ref, dai_ref, dbr_ref, dbi_ref), grads):
            ref[...] = val

    return pl.pallas_call(
        body, name="ssm_discretise_bwd", in_specs=_whole(9), out_specs=_whole(5),
        out_shape=[SDS(log_dt.shape, F32), SDS(ar.shape, F32), SDS(ar.shape, F32), SDS(br.shape, F32),
                   SDS(br.shape, F32)],
    )(log_dt, ar, ai, br, bi, dabr, dabi, dbbr, dbbi)


def _block_diag(t):
    g, a, b = t.shape
    same = jnp.eye(g, dtype=bool)[:, None, :, None]
    return jnp.where(same, t[:, :, None, :], 0.0).reshape(g * a, g * b)


def _diag_blocks(m, g):
    a, b = m.shape[0] // g, m.shape[1] // g
    return jnp.moveaxis(jnp.diagonal(m.reshape(g, a, g, b), axis1=0, axis2=2), -1, 0)


def _adamw(parts, w, m, v):
    rows = w.shape[0]
    tr = PACK_ROW_BLOCK

    def body(p_ref, w_ref, m_ref, v_ref, g_ref, d_ref, nm_ref, nv_ref):
        g = p_ref[0]
        for dev in range(1, N_DEV):
            g = g + p_ref[dev]
        g_ref[...] = g
        nm = ADAM_B1 * m_ref[...] + (1.0 - ADAM_B1) * g
        nv = ADAM_B2 * v_ref[...] + (1.0 - ADAM_B2) * jnp.square(g)
        nm_ref[...] = nm
        nv_ref[...] = nv
        m_hat = nm / (1.0 - ADAM_B1 ** ADAM_STEP)
        v_hat = nv / (1.0 - ADAM_B2 ** ADAM_STEP)
        d_ref[...] = -ADAM_LR * (m_hat / (jnp.sqrt(v_hat) + ADAM_EPS) + ADAM_WD * w_ref[...])

    tile = _rows(tr, LANES)
    out = SDS((rows, LANES), F32)
    return pl.pallas_call(
        body, name="adamw", grid=(rows // tr,),
        in_specs=[pl.BlockSpec((N_DEV, tr, LANES), lambda i: (0, i, 0)), tile, tile, tile],
        out_specs=[tile, tile, tile, tile],
        out_shape=[out, out, out, out],
        compiler_params=_seq_params(32),
    )(parts, w, m, v)


def _as_rows(flat):
    n = flat.shape[-1]
    padded = _round_up(n, PACK_TILE)
    if padded != n:
        flat = jnp.pad(flat, [(0, 0)] * (flat.ndim - 1) + [(0, padded - n)])
    return flat.reshape(flat.shape[:-1] + (padded // LANES, LANES))


def _shard_major(full, axis):
    shape = full.shape
    split = shape[:axis] + (N_DEV, shape[axis] // N_DEV) + shape[axis + 1:]
    return jnp.moveaxis(full.reshape(split), axis, 0).reshape(N_DEV, -1)


def _from_shard_major(rows8, shape, axis):
    local = shape[:axis] + (shape[axis] // N_DEV,) + shape[axis + 1:]
    return jnp.moveaxis(rows8.reshape((N_DEV,) + local), 0, axis).reshape(shape)


def _pack_local(tensors):
    return jnp.concatenate([_as_rows(t.reshape(-1)) for t in tensors], axis=0)


def _unpack_local(packed, shapes):
    out, r = [], 0
    for shape in shapes:
        n = 1
        for dim in shape:
            n *= dim
        rows = _round_up(n, PACK_TILE) // LANES
        out.append(packed[r:r + rows].reshape(-1)[:n].reshape(shape))
        r += rows
    return out


def kernel(x, even_norm, even_w_in, pool_w, pool_scale, ssm_log_dt, ssm_a_re, ssm_a_im, ssm_b_re, ssm_b_im, ssm_c_re, ssm_c_im, ssm_d, ssm_w_glu, even_w_out, odd_norm, odd_w_in, conv_w, conv_b, conv_ln_g, conv_ln_b, odd_w_out, final_norm, loss_target, m_even_norm, m_even_w_in, m_pool_w, m_pool_scale, m_ssm_log_dt, m_ssm_a_re, m_ssm_a_im, m_ssm_b_re, m_ssm_b_im, m_ssm_c_re, m_ssm_c_im, m_ssm_d, m_ssm_w_glu, m_even_w_out, m_odd_norm, m_odd_w_in, m_conv_w, m_conv_b, m_conv_ln_g, m_conv_ln_b, m_odd_w_out, m_final_norm, v_even_norm, v_even_w_in, v_pool_w, v_pool_scale, v_ssm_log_dt, v_ssm_a_re, v_ssm_a_im, v_ssm_b_re, v_ssm_b_im, v_ssm_c_re, v_ssm_c_im, v_ssm_d, v_ssm_w_glu, v_even_w_out, v_odd_norm, v_odd_w_in, v_conv_w, v_conv_b, v_conv_ln_g, v_conv_ln_b, v_odd_w_out, v_final_norm):
    given = dict(locals())
    xs = x[0]
    tgt = loss_target[0]
    d_model = xs.shape[1]

    def local(prefix, name):
        t = given[prefix + name]
        return t if name == 'final_norm' else t[0]

    local_shapes = {n: local('', n).shape for n in WEIGHTS}
    full_shapes = {n: tuple(dim * N_DEV if i == SHARDED[n] else dim for i, dim in enumerate(local_shapes[n]))
                   for n in SHARDED_ORDER}

    gathered = _all_gather(_pack_local([local('', n) for n in SHARDED_ORDER]))
    full = {}
    r = 0
    for n in SHARDED_ORDER:
        count = 1
        for dim in local_shapes[n]:
            count *= dim
        rows = _round_up(count, PACK_TILE) // LANES
        rows8 = gathered[:, r:r + rows].reshape(N_DEV, -1)[:, :count]
        full[n] = _from_shard_major(rows8, full_shapes[n], SHARDED[n])
        r += rows
    for n in REPLICATED_ORDER:
        full[n] = local('', n)

    groups, state = full['ssm_a_re'].shape
    log_dt = full['ssm_log_dt'].reshape(groups, 1)
    b_re_t = jnp.transpose(full['ssm_b_re'], (2, 0, 1))
    b_im_t = jnp.transpose(full['ssm_b_im'], (2, 0, 1))
    abr, abi, bbr, bbi = _disc_fwd(log_dt, full['ssm_a_re'], full['ssm_a_im'], b_re_t, b_im_t)
    a_re_row = abr.reshape(1, groups * state)
    a_im_row = abi.reshape(1, groups * state)
    bb = jnp.concatenate([_block_diag(jnp.transpose(bbr, (1, 0, 2))), _block_diag(jnp.transpose(bbi, (1, 0, 2)))],
                         axis=1).astype(MM)
    cc_re = _block_diag(jnp.transpose(full['ssm_c_re'], (0, 2, 1))).astype(MM)
    cc_im = _block_diag(jnp.transpose(full['ssm_c_im'], (0, 2, 1))).astype(MM)

    w_in_e = full['even_w_in'].astype(MM)
    w_glu = full['ssm_w_glu'].astype(MM)
    w_out_e = full['even_w_out'].astype(MM)
    w_in_o = full['odd_w_in'].astype(MM)
    w_out_o = full['odd_w_out'].astype(MM)
    pw = full['pool_w'].astype(MM)
    g0 = full['even_norm'].reshape(1, d_model)
    g1 = full['odd_norm'].reshape(1, d_model)
    fg = full['final_norm'].reshape(1, d_model)
    scale = full['pool_scale'].reshape(1, -1)
    d_skip = full['ssm_d'].reshape(1, -1)
    cb = full['conv_b'].reshape(1, d_model)
    lg = full['conv_ln_g'].reshape(1, d_model)
    lb = full['conv_ln_b'].reshape(1, d_model)
    cw = full['conv_w']

    proj0, h0 = _even_in(xs, g0, w_in_e)
    pooled = _pool_fwd(proj0)
    bu_re, bu_im = _ssm_bu(proj0, bb)
    x_re, x_im = _scan_fwd(a_re_row, a_im_row, bu_re, bu_im)
    x1, ymix, mixed, yin, gv = _even_mix(pooled, proj0, x_re, x_im, xs, pw, scale, cc_re, cc_im, d_skip, w_glu,
                                         w_out_e)
    proj1, glu, h1 = _odd_in(x1, g1, w_in_o)
    conv = _conv_fwd(glu, cw)

    dc, dz1, dx2, y1, loss_tile, dcb, dlg, dlb, dfg = _odd_out(conv, proj1, x1, tgt, cb, lg, lb, w_out_o, fg)
    dglu, dcw = _conv_bwd(dc, glu, cw)
    dx1, dproj1, dg1 = _odd_in_bwd(x1, proj1, dglu, dz1, dx2, g1, w_in_o)
    (dpooled, du_skip, dz0, dx_re, dx_im, dyin, dgv, dpw, dscale, dd) = _even_mix_bwd(
        pooled, proj0, mixed, gv, dx1, pw, scale, cc_re, cc_im, d_skip, w_glu, w_out_e)
    g_re, g_im, dabr, dabi = _scan_bwd(a_re_row, a_im_row, dx_re, dx_im, x_re, x_im)
    du_ssm = _ssm_bu_bwd(g_re, g_im, du_skip, bb)
    du_pool = _pool_bwd(dpooled)
    grad_x, dg0 = _even_in_bwd(xs, du_pool, du_ssm, dz0, dx1, g0, w_in_e)

    half = du_pool.shape[1]
    n_state = groups * state
    grads = {}
    grads['odd_w_out'] = _mm_tn("dw_odd_out", y1, dx2, d_model, d_model)
    grads['odd_w_in'] = _mm_tn("dw_odd_in", h1, dproj1, d_model, 3 * d_model)
    grads['even_w_out'] = _mm_tn("dw_even_out", ymix, dx1, d_model, d_model)
    grads['ssm_w_glu'] = _mm_tn("dw_glu", yin, dgv, half, d_model)
    grads['even_w_in'] = jnp.concatenate(
        [_mm_tn("dw_even_in_pool", h0, du_pool, d_model, half), _mm_tn("dw_even_in_ssm", h0, du_ssm, d_model, half),
         _mm_tn("dw_even_in_gate", h0, dz0, d_model, d_model)], axis=1)
    dcc_re = _mm_tn("dw_c_re", x_re, dyin, n_state, half)
    dcc_im = _mm_tn("dw_c_im", x_im, dyin, n_state, half, sign=-1.0)
    dbb_re = _mm_tn("dw_b_re", proj0, g_re, half, n_state, a_col=1)
    dbb_im = _mm_tn("dw_b_im", proj0, g_im, half, n_state, a_col=1)

    grads['ssm_c_re'] = jnp.transpose(_diag_blocks(dcc_re, groups), (0, 2, 1))
    grads['ssm_c_im'] = jnp.transpose(_diag_blocks(dcc_im, groups), (0, 2, 1))
    dbbr = jnp.transpose(_diag_blocks(dbb_re, groups), (1, 0, 2))
    dbbi = jnp.transpose(_diag_blocks(dbb_im, groups), (1, 0, 2))
    dld, dar, dai, dbr, dbi = _disc_bwd(log_dt, full['ssm_a_re'], full['ssm_a_im'], b_re_t, b_im_t,
                                        dabr.reshape(groups, state), dabi.reshape(groups, state), dbbr, dbbi)
    grads['ssm_log_dt'] = dld.reshape(groups)
    grads['ssm_a_re'] = dar
    grads['ssm_a_im'] = dai
    grads['ssm_b_re'] = jnp.transpose(dbr, (1, 2, 0))
    grads['ssm_b_im'] = jnp.transpose(dbi, (1, 2, 0))
    grads['even_norm'] = dg0
    grads['odd_norm'] = dg1
    grads['final_norm'] = dfg
    grads['pool_w'] = dpw
    grads['pool_scale'] = dscale
    grads['ssm_d'] = dd
    grads['conv_w'] = dcw[:CONV_KERNEL]
    grads['conv_b'] = dcb
    grads['conv_ln_g'] = dlg
    grads['conv_ln_b'] = dlb

    pieces = [_as_rows(_shard_major(grads[n].reshape(full_shapes[n]), SHARDED[n])) for n in SHARDED_ORDER]
    for n in REPLICATED_ORDER:
        rows = _as_rows(grads[n].reshape(-1))
        pieces.append(jnp.broadcast_to(rows[None], (N_DEV,) + rows.shape))
    used = sum(p.shape[1] for p in pieces)
    total = _round_up(used, PACK_ROW_BLOCK)
    if total != used:
        pieces.append(jnp.zeros((N_DEV, total - used, LANES), F32))
    received = _exchange(jnp.concatenate(pieces, axis=1))

    order = SHARDED_ORDER + REPLICATED_ORDER

    def packed(prefix):
        rows = _pack_local([local(prefix, n) for n in order])
        return jnp.pad(rows, ((0, total - used), (0, 0)))

    g_p, delta_p, m_p, v_p = _adamw(received, packed(''), packed('m_'), packed('v_'))

    out_shapes = [given[n].shape for n in order]
    unpacked = {}
    for kind, arr in (('grad', g_p), ('delta', delta_p), ('new_m', m_p), ('new_v', v_p)):
        for n, t in zip(order, _unpack_local(arr, out_shapes)):
            unpacked[kind, n] = t

    loss = lax.psum(loss_tile[0, 0], ("x", "y", "c"))
    outs = [loss, grad_x[None]]
    for kind in ('grad', 'delta', 'new_m', 'new_v'):
        outs.extend(unpacked[kind, n] for n in WEIGHTS)
    return tuple(outs)
```

```python
import functools

import jax
import jax.numpy as jnp
from jax import lax
from jax.experimental import pallas as pl
from jax.experimental.pallas import tpu as pltpu

F32 = jnp.float32
MM = jnp.bfloat16
SDS = jax.ShapeDtypeStruct

RMS_EPS = 1e-6
LN_EPS = 1e-5
POOL_WINDOWS = (2, 4, 8, 16)
POOL_HALO = 16
CONV_KERNEL = 31
CONV_HALO = 32
N_DEV = 8
LANES = 128
SUBLANES = 8
PACK_TILE = SUBLANES * LANES
PACK_ROW_BLOCK = 512
MIB = 1024 * 1024

ADAM_LR = 0.001
ADAM_B1 = 0.9
ADAM_B2 = 0.999
ADAM_EPS = 1e-08
ADAM_WD = 0.01
ADAM_STEP = 10

WEIGHTS = ['even_norm', 'even_w_in', 'pool_w', 'pool_scale', 'ssm_log_dt', 'ssm_a_re', 'ssm_a_im', 'ssm_b_re',
           'ssm_b_im', 'ssm_c_re', 'ssm_c_im', 'ssm_d', 'ssm_w_glu', 'even_w_out', 'odd_norm', 'odd_w_in', 'conv_w',
           'conv_b', 'conv_ln_g', 'conv_ln_b', 'odd_w_out', 'final_norm']
SHARDED = {'even_w_in': 1, 'ssm_w_glu': 1, 'even_w_out': 0, 'odd_norm': 0, 'odd_w_in': 1, 'conv_w': 1, 'conv_b': 0,
           'conv_ln_g': 0, 'conv_ln_b': 0, 'odd_w_out': 0}
SHARDED_ORDER = [n for n in WEIGHTS if n in SHARDED]
REPLICATED_ORDER = [n for n in WEIGHTS if n not in SHARDED]


def _round_up(n, m):
    return (n + m - 1) // m * m


def _sigmoid(x):
    return jax.nn.sigmoid(x)


def _silu(x):
    return x * jax.nn.sigmoid(x)


def _rms(x, g):
    return x * lax.rsqrt(jnp.mean(x * x, axis=-1, keepdims=True) + RMS_EPS) * g


def _dot(a, b):
    return jnp.dot(a.astype(MM), b.astype(MM), preferred_element_type=F32)


def _dot_nt(a, b):
    return lax.dot_general(a.astype(MM), b.astype(MM), (((1,), (1,)), ((), ())), preferred_element_type=F32)


def _dot_tn(a, b):
    return lax.dot_general(a.astype(MM), b.astype(MM), (((0,), (0,)), ((), ())), preferred_element_type=F32)


def _rows(ts, width, col=0):
    return pl.BlockSpec((ts, width), lambda i: (i, col))


def _rows_rev(ts, width, n, col=0):
    return pl.BlockSpec((ts, width), lambda i: (n - 1 - i, col))


def _full(shape):
    zeros = (0,) * len(shape)
    return pl.BlockSpec(shape, lambda i: zeros)


def _seq_params(vmem_mib=48, dims=1):
    return pltpu.CompilerParams(dimension_semantics=("arbitrary",) * dims, vmem_limit_bytes=vmem_mib * MIB)


def _mesh_position():
    x, y, c = lax.axis_index("x"), lax.axis_index("y"), lax.axis_index("c")
    return x, y, c


def _peer(pos, relation):
    x, y, c = pos
    px = 1 - x if relation & 4 else x
    py = 1 - y if relation & 2 else y
    pc = 1 - c if relation & 1 else c
    return px, py, pc


def _all_gather(block):
    rows = block.shape[0]

    def body(src, out, send_sems, recv_sems, local_sem):
        pos = _mesh_position()
        me = 4 * pos[0] + 2 * pos[1] + pos[2]
        mine = pltpu.make_async_copy(src, out.at[me], local_sem)
        mine.start()
        copies = []
        for relation in range(1, N_DEV):
            cp = pltpu.make_async_remote_copy(
                src_ref=src, dst_ref=out.at[me], send_sem=send_sems.at[relation - 1],
                recv_sem=recv_sems.at[relation - 1], device_id=_peer(pos, relation),
                device_id_type=pl.DeviceIdType.MESH)
            cp.start()
            copies.append(cp)
        for cp in copies:
            cp.wait()
        mine.wait()

    return pl.pallas_call(
        body, name="gather_weights",
        out_shape=SDS((N_DEV, rows, LANES), block.dtype),
        in_specs=[pl.BlockSpec(memory_space=pl.ANY)],
        out_specs=pl.BlockSpec(memory_space=pl.ANY),
        scratch_shapes=[pltpu.SemaphoreType.DMA((N_DEV - 1,)), pltpu.SemaphoreType.DMA((N_DEV - 1,)),
                        pltpu.SemaphoreType.DMA(())],
    )(block)


def _exchange(parts):
    rows = parts.shape[1]

    def body(src, out, send_sems, recv_sems, local_sem):
        pos = _mesh_position()
        me = 4 * pos[0] + 2 * pos[1] + pos[2]
        mine = pltpu.make_async_copy(src.at[me], out.at[me], local_sem)
        mine.start()
        copies = []
        for relation in range(1, N_DEV):
            peer = _peer(pos, relation)
            peer_id = 4 * peer[0] + 2 * peer[1] + peer[2]
            cp = pltpu.make_async_remote_copy(
                src_ref=src.at[peer_id], dst_ref=out.at[me], send_sem=send_sems.at[relation - 1],
                recv_sem=recv_sems.at[relation - 1], device_id=peer, device_id_type=pl.DeviceIdType.MESH)
            cp.start()
            copies.append(cp)
        for cp in copies:
            cp.wait()
        mine.wait()

    return pl.pallas_call(
        body, name="exchange_grads",
        out_shape=SDS((N_DEV, rows, LANES), parts.dtype),
        in_specs=[pl.BlockSpec(memory_space=pl.ANY)],
        out_specs=pl.BlockSpec(memory_space=pl.ANY),
        scratch_shapes=[pltpu.SemaphoreType.DMA((N_DEV - 1,)), pltpu.SemaphoreType.DMA((N_DEV - 1,)),
                        pltpu.SemaphoreType.DMA(())],
    )(parts)


def _mm_tn(name, a, b, m, n, a_col=0, b_col=0, sign=1.0, ts=512):
    s = a.shape[0]
    tn = min(n, 512)
    nj = n // tn

    def body(a_ref, b_ref, o_ref):
        @pl.when(pl.program_id(1) == 0)
        def _():
            o_ref[...] = jnp.zeros_like(o_ref)

        o_ref[...] += sign * _dot_tn(a_ref[...], b_ref[...])

    return pl.pallas_call(
        body, name=name, grid=(nj, s // ts),
        in_specs=[pl.BlockSpec((ts, m), lambda j, k: (k, a_col)),
                  pl.BlockSpec((ts, tn), lambda j, k: (k, b_col * nj + j))],
        out_specs=pl.BlockSpec((m, tn), lambda j, k: (0, j)),
        out_shape=SDS((m, n), F32),
        compiler_params=_seq_params(48, dims=2),
    )(a, b)


def _mm_tn_blocks(name, a, b, m, n, nb, a_col=0, sign=1.0, ts=512):
    s = a.shape[0]

    def body(a_ref, b_ref, o_ref):
        @pl.when(pl.program_id(1) == 0)
        def _():
            o_ref[...] = jnp.zeros_like(o_ref)

        o_ref[...] += sign * _dot_tn(a_ref[...], b_ref[...])

    return pl.pallas_call(
        body, name=name, grid=(nb, s // ts),
        in_specs=[pl.BlockSpec((ts, m), lambda j, k: (k, a_col + j)),
                  pl.BlockSpec((ts, n), lambda j, k: (k, j))],
        out_specs=pl.BlockSpec((None, m, n), lambda j, k: (j, 0, 0)),
        out_shape=SDS((nb, m, n), F32),
        compiler_params=_seq_params(48, dims=2),
    )(a, b)


def _even_in(x, g0, w_in):
    s, d = x.shape
    e = w_in.shape[1]
    ts = 512

    def body(x_ref, g_ref, w_ref, proj_ref, h_ref):
        hb = _rms(x_ref[...], g_ref[...]).astype(MM)
        h_ref[...] = hb
        proj_ref[...] = jnp.dot(hb, w_ref[...], preferred_element_type=F32)

    return pl.pallas_call(
        body, name="even_in", grid=(s // ts,),
        in_specs=[_rows(ts, d), _full((1, d)), _full((d, e))],
        out_specs=[_rows(ts, e), _rows(ts, d)],
        out_shape=[SDS((s, e), F32), SDS((s, d), MM)],
        compiler_params=_seq_params(48),
    )(x, g0, w_in)


def _pool_counts(t0, ts, w):
    pos = (t0 + lax.broadcasted_iota(jnp.int32, (ts, LANES), 0) + 1).astype(F32)
    return jnp.minimum(pos, float(w))


def _pool_fwd(proj):
    s = proj.shape[0]
    width = LANES * len(POOL_WINDOWS)
    ts = 512
    per = ts // POOL_HALO

    def body(prev_ref, u_ref, out_ref, ext):
        i = pl.program_id(0)
        ext[0:POOL_HALO, :] = jnp.where(i == 0, 0.0, prev_ref[...])
        ext[POOL_HALO:, :] = u_ref[...]
        for g, w in enumerate(POOL_WINDOWS):
            cols = slice(g * LANES, (g + 1) * LANES)
            tok = ext[pl.ds(POOL_HALO, ts), cols]
            acc = tok
            for k in range(1, w):
                acc = acc + ext[pl.ds(POOL_HALO - k, ts), cols]
            out_ref[:, cols] = acc / _pool_counts(i * ts, ts, w) - tok

    return pl.pallas_call(
        body, name="pool_fwd", grid=(s // ts,),
        in_specs=[pl.BlockSpec((POOL_HALO, width), lambda i: (jnp.maximum(i * per - 1, 0), 0)),
                  _rows(ts, width)],
        out_specs=_rows(ts, width),
        out_shape=SDS((s, width), F32),
        scratch_shapes=[pltpu.VMEM((ts + POOL_HALO, width), F32)],
        compiler_params=_seq_params(32),
    )(proj, proj)


def _pool_bwd(dp):
    s, width = dp.shape
    ts = 512
    per = ts // POOL_HALO
    n = s // ts

    def body(dp_ref, next_ref, out_ref, ext):
        i = pl.program_id(0)
        nxt = jnp.where(i == n - 1, 0.0, next_ref[...])
        for g, w in enumerate(POOL_WINDOWS):
            cols = slice(g * LANES, (g + 1) * LANES)
            cur = dp_ref[:, cols]
            ext[0:ts, cols] = cur / _pool_counts(i * ts, ts, w)
            ext[ts:, cols] = nxt[:, cols] / _pool_counts((i + 1) * ts, POOL_HALO, w)
            acc = -cur
            for k in range(w):
                acc = acc + ext[pl.ds(k, ts), cols]
            out_ref[:, cols] = acc

    return pl.pallas_call(
        body, name="pool_bwd", grid=(n,),
        in_specs=[_rows(ts, width),
                  pl.BlockSpec((POOL_HALO, width), lambda i: (jnp.minimum((i + 1) * per, s // POOL_HALO - 1), 0))],
        out_specs=_rows(ts, width),
        out_shape=SDS((s, width), F32),
        scratch_shapes=[pltpu.VMEM((ts + POOL_HALO, width), F32)],
        compiler_params=_seq_params(32),
    )(dp, dp)


SSM_BLOCKS = 4


def _ssm_bu(proj, bb):
    s = proj.shape[0]
    nb, cb, two_nb = bb.shape
    sb = two_nb // 2
    cin, n = nb * cb, nb * sb
    ts = 512

    def body(u_ref, bb_ref, re_ref, im_ref):
        for b in range(nb):
            bu = _dot(u_ref[:, b * cb:(b + 1) * cb], bb_ref[b])
            re_ref[:, b * sb:(b + 1) * sb] = bu[:, :sb]
            im_ref[:, b * sb:(b + 1) * sb] = bu[:, sb:]

    return pl.pallas_call(
        body, name="ssm_bu", grid=(s // ts,),
        in_specs=[_rows(ts, cin, col=1), _full(bb.shape)],
        out_specs=[_rows(ts, n), _rows(ts, n)],
        out_shape=[SDS((s, n), F32), SDS((s, n), F32)],
        compiler_params=_seq_params(48),
    )(proj, bb)


SCAN_LANES = 256


def _cmul(a, b):
    return a[0] * b[0] - a[1] * b[1], a[0] * b[1] + a[1] * b[0]


def _scan_constants(ar_row, ai_row, reverse):
    shape = (SUBLANES, ar_row.shape[1])
    a1 = (jnp.broadcast_to(ar_row, shape), jnp.broadcast_to(ai_row, shape))
    a2 = _cmul(a1, a1)
    a4 = _cmul(a2, a2)
    a8 = _cmul(a4, a4)
    row = lax.broadcasted_iota(jnp.int32, shape, 0)
    expo = (SUBLANES - row) if reverse else (row + 1)
    pr, pi = jnp.ones(shape, F32), jnp.zeros(shape, F32)
    for bit, q in enumerate((a1, a2, a4, a8)):
        take = ((expo >> bit) & 1) == 1
        nr, ni = _cmul((pr, pi), q)
        pr, pi = jnp.where(take, nr, pr), jnp.where(take, ni, pi)
    return row, (a1, a2, a4), (pr, pi)


def _group_scan(v, row, powers, reverse):
    vr, vi = v
    for k, q in zip((1, 2, 4), powers):
        if reverse:
            keep = row < SUBLANES - k
            sr, si = pltpu.roll(vr, SUBLANES - k, 0), pltpu.roll(vi, SUBLANES - k, 0)
        else:
            keep = row >= k
            sr, si = pltpu.roll(vr, k, 0), pltpu.roll(vi, k, 0)
        sr, si = jnp.where(keep, sr, 0.0), jnp.where(keep, si, 0.0)
        tr, ti = _cmul(q, (sr, si))
        vr, vi = vr + tr, vi + ti
    return vr, vi


def _scan_fwd(a_re, a_im, bu_re, bu_im):
    s, n = bu_re.shape
    ts = 512
    groups = ts // SUBLANES

    def body(ar_ref, ai_ref, bre_ref, bim_ref, xre_ref, xim_ref, cre, cim):
        @pl.when(pl.program_id(0) == 0)
        def _():
            cre[...] = jnp.zeros_like(cre)
            cim[...] = jnp.zeros_like(cim)

        for c in range(n // SCAN_LANES):
            cols = slice(c * SCAN_LANES, (c + 1) * SCAN_LANES)
            row, powers, carry_pow = _scan_constants(ar_ref[:, cols], ai_ref[:, cols], False)

            def group(gi, carry):
                r0 = pl.multiple_of(gi * SUBLANES, SUBLANES)
                v = (bre_ref[pl.ds(r0, SUBLANES), cols], bim_ref[pl.ds(r0, SUBLANES), cols])
                vr, vi = _group_scan(v, row, powers, False)
                tr, ti = _cmul(carry_pow, carry)
                vr, vi = vr + tr, vi + ti
                xre_ref[pl.ds(r0, SUBLANES), cols] = vr
                xim_ref[pl.ds(r0, SUBLANES), cols] = vi
                shape = vr.shape
                return (jnp.broadcast_to(vr[SUBLANES - 1:SUBLANES, :], shape),
                        jnp.broadcast_to(vi[SUBLANES - 1:SUBLANES, :], shape))

            cr, ci = lax.fori_loop(0, groups, group, (cre[:, cols], cim[:, cols]), unroll=2)
            cre[:, cols] = cr
            cim[:, cols] = ci

    return pl.pallas_call(
        body, name="ssm_scan", grid=(s // ts,),
        in_specs=[_full((1, n)), _full((1, n)), _rows(ts, n), _rows(ts, n)],
        out_specs=[_rows(ts, n), _rows(ts, n)],
        out_shape=[SDS((s, n), F32), SDS((s, n), F32)],
        scratch_shapes=[pltpu.VMEM((SUBLANES, n), F32), pltpu.VMEM((SUBLANES, n), F32)],
        compiler_params=_seq_params(48),
    )(a_re, a_im, bu_re, bu_im)


def _scan_bwd(a_re, a_im, dx_re, dx_im, x_re, x_im):
    s, n = dx_re.shape
    ts = 256
    nt = s // ts
    groups = ts // SUBLANES

    def body(ar_ref, ai_ref, dre_ref, dim_ref, xre_ref, xim_ref, gre_ref, gim_ref, dar_ref, dai_ref,
             cre, cim, accr, acci):
        i = pl.program_id(0)

        @pl.when(i == 0)
        def _():
            for ref in (cre, cim, accr, acci):
                ref[...] = jnp.zeros_like(ref)

        for c in range(n // SCAN_LANES):
            cols = slice(c * SCAN_LANES, (c + 1) * SCAN_LANES)
            row, powers, carry_pow = _scan_constants(ar_ref[:, cols], -ai_ref[:, cols], True)

            def group(k, state):
                carry, acc = state[:2], state[2:]
                r0 = pl.multiple_of((groups - 1 - k) * SUBLANES, SUBLANES)
                v = (dre_ref[pl.ds(r0, SUBLANES), cols], dim_ref[pl.ds(r0, SUBLANES), cols])
                vr, vi = _group_scan(v, row, powers, True)
                tr, ti = _cmul(carry_pow, carry)
                vr, vi = vr + tr, vi + ti
                gre_ref[pl.ds(r0, SUBLANES), cols] = vr
                gim_ref[pl.ds(r0, SUBLANES), cols] = vi
                nr = jnp.where(row < SUBLANES - 1, pltpu.roll(vr, SUBLANES - 1, 0), carry[0])
                ni = jnp.where(row < SUBLANES - 1, pltpu.roll(vi, SUBLANES - 1, 0), carry[1])
                xr, xi = xre_ref[pl.ds(r0, SUBLANES), cols], xim_ref[pl.ds(r0, SUBLANES), cols]
                shape = vr.shape
                return (jnp.broadcast_to(vr[0:1, :], shape), jnp.broadcast_to(vi[0:1, :], shape),
                        acc[0] + nr * xr + ni * xi, acc[1] + ni * xr - nr * xi)

            out = lax.fori_loop(0, groups, group, (cre[:, cols], cim[:, cols], accr[:, cols], acci[:, cols]),
                                unroll=2)
            cre[:, cols], cim[:, cols], accr[:, cols], acci[:, cols] = out

        @pl.when(i == nt - 1)
        def _():
            dar_ref[...] = jnp.sum(accr[...], axis=0, keepdims=True)
            dai_ref[...] = jnp.sum(acci[...], axis=0, keepdims=True)

    small = pltpu.VMEM((SUBLANES, n), F32)
    return pl.pallas_call(
        body, name="ssm_scan_bwd", grid=(nt,),
        in_specs=[_full((1, n)), _full((1, n))] + [_rows_rev(ts, n, nt)] * 4,
        out_specs=[_rows_rev(ts, n, nt), _rows_rev(ts, n, nt), _full((1, n)), _full((1, n))],
        out_shape=[SDS((s, n), F32), SDS((s, n), F32), SDS((1, n), F32), SDS((1, n), F32)],
        scratch_shapes=[small, small, small, small],
        compiler_params=_seq_params(48),
    )(a_re, a_im, dx_re, dx_im, x_re, x_im)


def _mix_a(mixed, scale, za):
    return mixed * scale * _silu(za)


def _mix_b(val, gate, zb):
    return val * _sigmoid(gate) * _silu(zb)


def _even_mix(pooled, proj, x_re, x_im, x, pw, scale, cc_re, cc_im, d_skip, w_glu, w_out):
    s, d = x.shape
    half = pooled.shape[1]
    n = x_re.shape[1]
    ts = 256

    def body(pooled_ref, u_ref, z_ref, xre_ref, xim_ref, x_ref, pw_ref, scale_ref, ccre_ref, ccim_ref, d_ref,
             wglu_ref, wout_ref, x1_ref, ymix_ref, mixed_ref, yin_ref, gv_ref):
        for g in range(len(POOL_WINDOWS)):
            cols = slice(g * LANES, (g + 1) * LANES)
            mixed_ref[:, cols] = _dot(pooled_ref[:, cols], pw_ref[g])
        sb, cb = ccre_ref.shape[1:]
        for b in range(SSM_BLOCKS):
            states, chans = slice(b * sb, (b + 1) * sb), slice(b * cb, (b + 1) * cb)
            yin_ref[:, chans] = (_dot(xre_ref[:, states], ccre_ref[b]) - _dot(xim_ref[:, states], ccim_ref[b])
                                 + d_ref[:, chans] * u_ref[:, chans])
        yin = yin_ref[...]
        gv = _dot(yin, wglu_ref[...])
        gv_ref[...] = gv
        ya = _mix_a(mixed_ref[...], scale_ref[...], z_ref[:, :half])
        yb = _mix_b(gv[:, :half], gv[:, half:], z_ref[:, half:])
        ymix = jnp.concatenate([ya, yb], axis=1).astype(MM)
        ymix_ref[...] = ymix
        x1_ref[...] = x_ref[...] + jnp.dot(ymix, wout_ref[...], preferred_element_type=F32)

    return pl.pallas_call(
        body, name="even_mix", grid=(s // ts,),
        in_specs=[_rows(ts, half), _rows(ts, half, col=1), _rows(ts, d, col=1), _rows(ts, n), _rows(ts, n),
                  _rows(ts, d), _full(pw.shape), _full((1, half)), _full(cc_re.shape), _full(cc_im.shape),
                  _full((1, half)), _full((half, d)), _full((d, d))],
        out_specs=[_rows(ts, d), _rows(ts, d), _rows(ts, half), _rows(ts, half), _rows(ts, d)],
        out_shape=[SDS((s, d), F32), SDS((s, d), MM), SDS((s, half), F32), SDS((s, half), F32), SDS((s, d), F32)],
        compiler_params=_seq_params(56),
    )(pooled, proj, proj, x_re, x_im, x, pw, scale, cc_re, cc_im, d_skip, w_glu, w_out)


def _glu(val, gt):
    return val * _sigmoid(gt)


def _odd_in(x1, g1, w_in):
    s, d = x1.shape
    e = w_in.shape[1]
    ts = 256

    def body(x_ref, g_ref, w_ref, proj_ref, glu_ref, h_ref):
        hb = _rms(x_ref[...], g_ref[...]).astype(MM)
        h_ref[...] = hb
        proj = jnp.dot(hb, w_ref[...], preferred_element_type=F32)
        proj_ref[...] = proj
        glu_ref[...] = _glu(proj[:, :d], proj[:, d:2 * d])

    return pl.pallas_call(
        body, name="odd_in", grid=(s // ts,),
        in_specs=[_rows(ts, d), _full((1, d)), _full((d, e))],
        out_specs=[_rows(ts, e), _rows(ts, d), _rows(ts, d)],
        out_shape=[SDS((s, e), F32), SDS((s, d), F32), SDS((s, d), MM)],
        compiler_params=_seq_params(48),
    )(x1, g1, w_in)


CONV_ROWS = 32
CONV_COLS = 256


def _phase_copies(sh, rows):
    for o in range(1, SUBLANES):
        sh[o, 0:rows, :] = sh[0, pl.ds(o, rows), :]


def _conv_fwd(g, w):
    s, ch = g.shape
    ts = 256
    per = ts // CONV_HALO
    lead = CONV_HALO - (CONV_KERNEL - 1)
    span = ts + CONV_HALO - SUBLANES

    def body(prev_ref, g_ref, w_ref, out_ref, sh):
        i = pl.program_id(0)
        sh[0, 0:CONV_HALO, :] = jnp.where(i == 0, 0.0, prev_ref[...])
        sh[0, CONV_HALO:, :] = g_ref[...]
        _phase_copies(sh, span)

        def block(rb, carry):
            r0 = pl.multiple_of(rb * CONV_ROWS, CONV_ROWS)
            for c in range(ch // CONV_COLS):
                cols = slice(c * CONV_COLS, (c + 1) * CONV_COLS)
                acc = jnp.zeros((CONV_ROWS, CONV_COLS), F32)
                for j in range(CONV_KERNEL):
                    q, o = divmod(lead + j, SUBLANES)
                    acc = acc + w_ref[j:j + 1, cols] * sh[o, pl.ds(r0 + SUBLANES * q, CONV_ROWS), cols]
                out_ref[pl.ds(r0, CONV_ROWS), cols] = acc
            return carry

        lax.fori_loop(0, ts // CONV_ROWS, block, 0)

    return pl.pallas_call(
        body, name="conv_fwd", grid=(s // ts,),
        in_specs=[pl.BlockSpec((CONV_HALO, ch), lambda i: (jnp.maximum(i * per - 1, 0), 0)),
                  _rows(ts, ch), _full(w.shape)],
        out_specs=_rows(ts, ch),
        out_shape=SDS((s, ch), F32),
        scratch_shapes=[pltpu.VMEM((SUBLANES, ts + CONV_HALO, ch), F32)],
        compiler_params=_seq_params(40),
    )(g, g, w)


def _conv_bwd(dc, g, w):
    s, ch = dc.shape
    ts = 256
    per = ts // CONV_HALO
    n = s // ts
    lead = CONV_HALO - (CONV_KERNEL - 1)
    span = ts + CONV_HALO - SUBLANES

    def body(dc_ref, next_ref, prev_ref, g_ref, w_ref, dg_ref, dw_ref, shd, shg, wacc):
        i = pl.program_id(0)

        @pl.when(i == 0)
        def _():
            wacc[...] = jnp.zeros_like(wacc)

        shd[0, 0:ts, :] = dc_ref[...]
        shd[0, ts:, :] = jnp.where(i == n - 1, 0.0, next_ref[...])
        shg[0, 0:CONV_HALO, :] = jnp.where(i == 0, 0.0, prev_ref[...])
        shg[0, CONV_HALO:, :] = g_ref[...]
        _phase_copies(shd, span)
        _phase_copies(shg, span)

        def dg_block(rb, carry):
            r0 = pl.multiple_of(rb * CONV_ROWS, CONV_ROWS)
            for c in range(ch // CONV_COLS):
                cols = slice(c * CONV_COLS, (c + 1) * CONV_COLS)
                acc = jnp.zeros((CONV_ROWS, CONV_COLS), F32)
                for j in range(CONV_KERNEL):
                    q, o = divmod(CONV_KERNEL - 1 - j, SUBLANES)
                    acc = acc + w_ref[j:j + 1, cols] * shd[o, pl.ds(r0 + SUBLANES * q, CONV_ROWS), cols]
                dg_ref[pl.ds(r0, CONV_ROWS), cols] = acc
            return carry

        lax.fori_loop(0, ts // CONV_ROWS, dg_block, 0)

        for c in range(ch // LANES):
            cols = slice(c * LANES, (c + 1) * LANES)

            def dw_rows(k, accs):
                r0 = pl.multiple_of(k * SUBLANES, SUBLANES)
                cur = shd[0, pl.ds(r0, SUBLANES), cols]
                out = []
                for j in range(CONV_KERNEL):
                    q, o = divmod(lead + j, SUBLANES)
                    out.append(accs[j] + cur * shg[o, pl.ds(r0 + SUBLANES * q, SUBLANES), cols])
                return tuple(out)

            init = tuple(wacc[j, :, cols] for j in range(CONV_KERNEL))
            accs = lax.fori_loop(0, ts // SUBLANES, dw_rows, init, unroll=2)
            for j in range(CONV_KERNEL):
                wacc[j, :, cols] = accs[j]

        @pl.when(i == n - 1)
        def _():
            dw_ref[...] = jnp.zeros_like(dw_ref)
            for j in range(CONV_KERNEL):
                dw_ref[j:j + 1, :] = jnp.sum(wacc[j], axis=0, keepdims=True)

    return pl.pallas_call(
        body, name="conv_bwd", grid=(n,),
        in_specs=[_rows(ts, ch),
                  pl.BlockSpec((CONV_HALO, ch), lambda i: (jnp.minimum((i + 1) * per, s // CONV_HALO - 1), 0)),
                  pl.BlockSpec((CONV_HALO, ch), lambda i: (jnp.maximum(i * per - 1, 0), 0)),
                  _rows(ts, ch), _full(w.shape)],
        out_specs=[_rows(ts, ch), _full((CONV_HALO, ch))],
        out_shape=[SDS((s, ch), F32), SDS((CONV_HALO, ch), F32)],
        scratch_shapes=[pltpu.VMEM((SUBLANES, ts + CONV_HALO, ch), F32),
                        pltpu.VMEM((SUBLANES, ts + CONV_HALO, ch), F32),
                        pltpu.VMEM((CONV_HALO, SUBLANES, ch), F32)],
        compiler_params=_seq_params(56),
    )(dc, dc, g, g, w)


def _conv_act(c, z1, cb, lg, lb):
    cc = c + cb
    mu = jnp.mean(cc, axis=-1, keepdims=True)
    dev = cc - mu
    var = jnp.mean(dev * dev, axis=-1, keepdims=True)
    cn = dev * lax.rsqrt(var + LN_EPS) * lg + lb
    return _silu(cn) * _silu(z1)


def _odd_out(c, proj1, x1, tgt, cb, lg, lb, w_out, fg):
    s, d = c.shape
    ts = 256

    def body(c_ref, z_ref, x1_ref, t_ref, cb_ref, lg_ref, lb_ref, w_ref, fg_ref,
             dc_ref, dz_ref, dx2_ref, y1_ref, loss_ref, dcb_ref, dlg_ref, dlb_ref, dfg_ref):
        @pl.when(pl.program_id(0) == 0)
        def _():
            for ref in (loss_ref, dcb_ref, dlg_ref, dlb_ref, dfg_ref):
                ref[...] = jnp.zeros_like(ref)

        y1, act_vjp = jax.vjp(_conv_act, c_ref[...], z_ref[...], cb_ref[...], lg_ref[...], lb_ref[...])
        y1b = y1.astype(MM)
        y1_ref[...] = y1b
        x2 = x1_ref[...] + jnp.dot(y1b, w_ref[...], preferred_element_type=F32)
        tgt_tile = t_ref[...]

        def head(x2, fg):
            err = jnp.square(_rms(x2, fg) - tgt_tile)
            return 0.5 * jnp.sum(jnp.mean(err, axis=-1))

        loss, (dx2, dfg) = jax.value_and_grad(head, argnums=(0, 1))(x2, fg_ref[...])
        loss_ref[...] += loss
        dfg_ref[...] += dfg
        dx2_ref[...] = dx2
        dy1 = _dot_nt(dx2, w_ref[...])
        dc, dz, dcb, dlg, dlb = act_vjp(dy1)
        dc_ref[...] = dc
        dz_ref[...] = dz
        dcb_ref[...] += dcb
        dlg_ref[...] += dlg
        dlb_ref[...] += dlb

    vec = SDS((1, d), F32)
    return pl.pallas_call(
        body, name="odd_out", grid=(s // ts,),
        in_specs=[_rows(ts, d), _rows(ts, d, col=2), _rows(ts, d), _rows(ts, d), _full((1, d)), _full((1, d)),
                  _full((1, d)), _full((d, d)), _full((1, d))],
        out_specs=[_rows(ts, d), _rows(ts, d), _rows(ts, d), _rows(ts, d), _full((SUBLANES, LANES)),
                   _full((1, d)), _full((1, d)), _full((1, d)), _full((1, d))],
        out_shape=[SDS((s, d), F32), SDS((s, d), F32), SDS((s, d), F32), SDS((s, d), MM),
                   SDS((SUBLANES, LANES), F32), vec, vec, vec, vec],
        compiler_params=_seq_params(56),
    )(c, proj1, x1, tgt, cb, lg, lb, w_out, fg)


def _odd_in_bwd(x1, proj1, dglu, dz1, dx2, g1, w_in):
    s, d = x1.shape
    e = w_in.shape[1]
    ts = 256

    def body(x_ref, vg_ref, dglu_ref, dz_ref, dx2_ref, g_ref, w_ref, dx1_ref, dp_ref, dg_ref):
        @pl.when(pl.program_id(0) == 0)
        def _():
            dg_ref[...] = jnp.zeros_like(dg_ref)

        _, glu_vjp = jax.vjp(_glu, vg_ref[:, :d], vg_ref[:, d:])
        dval, dgt = glu_vjp(dglu_ref[...])
        dp = jnp.concatenate([dval, dgt, dz_ref[...]], axis=1).astype(MM)
        dp_ref[...] = dp
        dh = _dot_nt(dp, w_ref[...])
        _, rms_vjp = jax.vjp(_rms, x_ref[...], g_ref[...])
        dxa, dg = rms_vjp(dh)
        dx1_ref[...] = dxa + dx2_ref[...]
        dg_ref[...] += dg

    return pl.pallas_call(
        body, name="odd_in_bwd", grid=(s // ts,),
        in_specs=[_rows(ts, d), _rows(ts, 2 * d), _rows(ts, d), _rows(ts, d), _rows(ts, d), _full((1, d)),
                  _full((d, e))],
        out_specs=[_rows(ts, d), _rows(ts, e), _full((1, d))],
        out_shape=[SDS((s, d), F32), SDS((s, e), MM), SDS((1, d), F32)],
        compiler_params=_seq_params(56),
    )(x1, proj1, dglu, dz1, dx2, g1, w_in)


def _even_mix_bwd(pooled, proj, mixed, gv, dx1, pw, scale, cc_re, cc_im, d_skip, w_glu, w_out):
    s, d = dx1.shape
    half = pooled.shape[1]
    n = cc_re.shape[0] * cc_re.shape[1]
    ts = 256
    groups = len(POOL_WINDOWS)

    def body(pooled_ref, u_ref, z_ref, mixed_ref, gv_ref, dx1_ref, pw_ref, scale_ref, ccre_ref, ccim_ref, d_ref,
             wglu_ref, wout_ref, dpooled_ref, du_ref, dz_ref, dxre_ref, dxim_ref, dyin_ref, dgv_ref,
             dpw_ref, dscale_ref, dd_ref):
        @pl.when(pl.program_id(0) == 0)
        def _():
            for ref in (dpw_ref, dscale_ref, dd_ref):
                ref[...] = jnp.zeros_like(ref)

        dymix = _dot_nt(dx1_ref[...], wout_ref[...])
        _, a_vjp = jax.vjp(_mix_a, mixed_ref[...], scale_ref[...], z_ref[:, :half])
        dmixed, dscale, dza = a_vjp(dymix[:, :half])
        _, b_vjp = jax.vjp(_mix_b, gv_ref[:, :half], gv_ref[:, half:], z_ref[:, half:])
        dval, dgate, dzb = b_vjp(dymix[:, half:])
        dz_ref[:, :half] = dza
        dz_ref[:, half:] = dzb
        dscale_ref[...] += dscale
        dgv = jnp.concatenate([dval, dgate], axis=1).astype(MM)
        dgv_ref[...] = dgv
        dyin = _dot_nt(dgv, wglu_ref[...])
        dd_ref[...] += jnp.sum(dyin * u_ref[...], axis=0, keepdims=True)
        du_ref[...] = d_ref[...] * dyin
        dyb = dyin.astype(MM)
        dyin_ref[...] = dyb
        sb, cb = ccre_ref.shape[1:]
        for b in range(SSM_BLOCKS):
            states, chans = slice(b * sb, (b + 1) * sb), slice(b * cb, (b + 1) * cb)
            dxre_ref[:, states] = _dot_nt(dyb[:, chans], ccre_ref[b])
            dxim_ref[:, states] = -_dot_nt(dyb[:, chans], ccim_ref[b])
        for g in range(groups):
            cols = slice(g * LANES, (g + 1) * LANES)
            dm = dmixed[:, cols].astype(MM)
            dpooled_ref[:, cols] = _dot_nt(dm, pw_ref[g])
            dpw_ref[g] += _dot_tn(pooled_ref[:, cols], dm)

    return pl.pallas_call(
        body, name="even_mix_bwd", grid=(s // ts,),
        in_specs=[_rows(ts, half), _rows(ts, half, col=1), _rows(ts, d, col=1), _rows(ts, half), _rows(ts, d),
                  _rows(ts, d), _full(pw.shape), _full((1, half)), _full(cc_re.shape), _full(cc_im.shape),
                  _full((1, half)), _full((half, d)), _full((d, d))],
        out_specs=[_rows(ts, half), _rows(ts, half), _rows(ts, d), _rows(ts, n), _rows(ts, n), _rows(ts, half),
                   _rows(ts, d), _full(pw.shape), _full((1, half)), _full((1, half))],
        out_shape=[SDS((s, half), F32), SDS((s, half), F32), SDS((s, d), F32), SDS((s, n), F32), SDS((s, n), F32),
                   SDS((s, half), MM), SDS((s, d), MM), SDS(pw.shape, F32), SDS((1, half), F32),
                   SDS((1, half), F32)],
        compiler_params=_seq_params(56),
    )(pooled, proj, proj, mixed, gv, dx1, pw, scale, cc_re, cc_im, d_skip, w_glu, w_out)


def _ssm_bu_bwd(g_re, g_im, du_skip, bb):
    s, n = g_re.shape
    nb, cb, two_nb = bb.shape
    sb = two_nb // 2
    cin = nb * cb
    ts = 512

    def body(gre_ref, gim_ref, du_ref, bb_ref, out_ref):
        for b in range(nb):
            states, chans = slice(b * sb, (b + 1) * sb), slice(b * cb, (b + 1) * cb)
            out_ref[:, chans] = (du_ref[:, chans] + _dot_nt(gre_ref[:, states], bb_ref[b, :, :sb])
                                 + _dot_nt(gim_ref[:, states], bb_ref[b, :, sb:]))

    return pl.pallas_call(
        body, name="ssm_bu_bwd", grid=(s // ts,),
        in_specs=[_rows(ts, n), _rows(ts, n), _rows(ts, cin), _full(bb.shape)],
        out_specs=_rows(ts, cin),
        out_shape=SDS((s, cin), F32),
        compiler_params=_seq_params(48),
    )(g_re, g_im, du_skip, bb)


def _even_in_bwd(x, du_pool, du_ssm, dz, dx1, g0, w_in):
    s, d = x.shape
    half = du_pool.shape[1]
    e = w_in.shape[1]
    ts = 256

    def body(x_ref, dup_ref, dus_ref, dz_ref, dx1_ref, g_ref, w_ref, gx_ref, dg_ref):
        @pl.when(pl.program_id(0) == 0)
        def _():
            dg_ref[...] = jnp.zeros_like(dg_ref)

        dp = jnp.concatenate([dup_ref[...], dus_ref[...], dz_ref[...]], axis=1)
        dh = _dot_nt(dp, w_ref[...])
        _, rms_vjp = jax.vjp(_rms, x_ref[...], g_ref[...])
        dxa, dg = rms_vjp(dh)
        gx_ref[...] = dxa + dx1_ref[...]
        dg_ref[...] += dg

    return pl.pallas_call(
        body, name="even_in_bwd", grid=(s // ts,),
        in_specs=[_rows(ts, d), _rows(ts, half), _rows(ts, half), _rows(ts, d), _rows(ts, d), _full((1, d)),
                  _full((d, e))],
        out_specs=[_rows(ts, d), _full((1, d))],
        out_shape=[SDS((s, d), F32), SDS((1, d), F32)],
        compiler_params=_seq_params(48),
    )(x, du_pool, du_ssm, dz, dx1, g0, w_in)


def _discretise(log_dt, ar, ai, br, bi):
    dt = jnp.exp(log_dt)
    mag = jnp.exp(ar * dt)
    ang = ai * dt
    abr = mag * jnp.cos(ang)
    abi = mag * jnp.sin(ang)
    den = ar * ar + ai * ai
    nr = abr - 1.0
    ni = abi
    kr = (nr * ar + ni * ai) / den
    ki = (ni * ar - nr * ai) / den
    bbr = kr[None] * br - ki[None] * bi
    bbi = kr[None] * bi + ki[None] * br
    return abr, abi, bbr, bbi


def _whole(n):
    return [pl.BlockSpec(memory_space=pltpu.VMEM)] * n


def _disc_fwd(log_dt, ar, ai, br, bi):
    def body(ld_ref, ar_ref, ai_ref, br_ref, bi_ref, abr_ref, abi_ref, bbr_ref, bbi_ref):
        out = _discretise(ld_ref[...], ar_ref[...], ai_ref[...], br_ref[...], bi_ref[...])
        for ref, val in zip((abr_ref, abi_ref, bbr_ref, bbi_ref), out):
            ref[...] = val

    return pl.pallas_call(
        body, name="ssm_discretise", in_specs=_whole(5), out_specs=_whole(4),
        out_shape=[SDS(ar.shape, F32), SDS(ar.shape, F32), SDS(br.shape, F32), SDS(br.shape, F32)],
    )(log_dt, ar, ai, br, bi)


def _disc_bwd(log_dt, ar, ai, br, bi, dabr, dabi, dbbr, dbbi):
    def body(ld_ref, ar_ref, ai_ref, br_ref, bi_ref, dabr_ref, dabi_ref, dbbr_ref, dbbi_ref,
             dld_ref, dar_ref, dai_ref, dbr_ref, dbi_ref):
        _, vjp = jax.vjp(_discretise, ld_ref[...], ar_ref[...], ai_ref[...], br_ref[...], bi_ref[...])
        grads = vjp((dabr_ref[...], dabi_ref[...], dbbr_ref[...], dbbi_ref[...]))
        for ref, val in zip((dld_ref, dar_ref, dai_ref, dbr_ref, dbi_ref), grads):
            ref[...] = val

    return pl.pallas_call(
        body, name="ssm_discretise_bwd", in_specs=_whole(9), out_specs=_whole(5),
        out_shape=[SDS(log_dt.shape, F32), SDS(ar.shape, F32), SDS(ar.shape, F32), SDS(br.shape, F32),
                   SDS(br.shape, F32)],
    )(log_dt, ar, ai, br, bi, dabr, dabi, dbbr, dbbi)


def _block_diag(t):
    g, a, b = t.shape
    per = g // SSM_BLOCKS
    t = t.reshape(SSM_BLOCKS, per, a, b)
    same = jnp.eye(per, dtype=bool)[None, :, None, :, None]
    return jnp.where(same, t[:, :, :, None, :], 0.0).reshape(SSM_BLOCKS, per * a, per * b)


def _diag_blocks(m, g):
    per = g // SSM_BLOCKS
    a, b = m.shape[1] // per, m.shape[2] // per
    d = jnp.diagonal(m.reshape(SSM_BLOCKS, per, a, per, b), axis1=1, axis2=3)
    return jnp.moveaxis(d, -1, 1).reshape(g, a, b)


def _adamw(parts, w, m, v):
    rows = w.shape[0]
    tr = PACK_ROW_BLOCK

    def body(p_ref, w_ref, m_ref, v_ref, g_ref, d_ref, nm_ref, nv_ref):
        g = p_ref[0]
        for dev in range(1, N_DEV):
            g = g + p_ref[dev]
        g_ref[...] = g
        nm = ADAM_B1 * m_ref[...] + (1.0 - ADAM_B1) * g
        nv = ADAM_B2 * v_ref[...] + (1.0 - ADAM_B2) * jnp.square(g)
        nm_ref[...] = nm
        nv_ref[...] = nv
        m_hat = nm / (1.0 - ADAM_B1 ** ADAM_STEP)
        v_hat = nv / (1.0 - ADAM_B2 ** ADAM_STEP)
        d_ref[...] = -ADAM_LR * (m_hat / (jnp.sqrt(v_hat) + ADAM_EPS) + ADAM_WD * w_ref[...])

    tile = _rows(tr, LANES)
    out = SDS((rows, LANES), F32)
    return pl.pallas_call(
        body, name="adamw", grid=(rows // tr,),
        in_specs=[pl.BlockSpec((N_DEV, tr, LANES), lambda i: (0, i, 0)), tile, tile, tile],
        out_specs=[tile, tile, tile, tile],
        out_shape=[out, out, out, out],
        compiler_params=_seq_params(32),
    )(parts, w, m, v)


def _as_rows(flat):
    n = flat.shape[-1]
    padded = _round_up(n, PACK_TILE)
    if padded != n:
        flat = jnp.pad(flat, [(0, 0)] * (flat.ndim - 1) + [(0, padded - n)])
    return flat.reshape(flat.shape[:-1] + (padded // LANES, LANES))


def _shard_major(full, axis):
    shape = full.shape
    split = shape[:axis] + (N_DEV, shape[axis] // N_DEV) + shape[axis + 1:]
    return jnp.moveaxis(full.reshape(split), axis, 0).reshape(N_DEV, -1)


def _from_shard_major(rows8, shape, axis):
    local = shape[:axis] + (shape[axis] // N_DEV,) + shape[axis + 1:]
    return jnp.moveaxis(rows8.reshape((N_DEV,) + local), 0, axis).reshape(shape)


def _pack_local(tensors):
    return jnp.concatenate([_as_rows(t.reshape(-1)) for t in tensors], axis=0)


def _unpack_local(packed, shapes):
    out, r = [], 0
    for shape in shapes:
        n = 1
        for dim in shape:
            n *= dim
        rows = _round_up(n, PACK_TILE) // LANES
        out.append(packed[r:r + rows].reshape(-1)[:n].reshape(shape))
        r += rows
    return out


def kernel(x, even_norm, even_w_in, pool_w, pool_scale, ssm_log_dt, ssm_a_re, ssm_a_im, ssm_b_re, ssm_b_im, ssm_c_re, ssm_c_im, ssm_d, ssm_w_glu, even_w_out, odd_norm, odd_w_in, conv_w, conv_b, conv_ln_g, conv_ln_b, odd_w_out, final_norm, loss_target, m_even_norm, m_even_w_in, m_pool_w, m_pool_scale, m_ssm_log_dt, m_ssm_a_re, m_ssm_a_im, m_ssm_b_re, m_ssm_b_im, m_ssm_c_re, m_ssm_c_im, m_ssm_d, m_ssm_w_glu, m_even_w_out, m_odd_norm, m_odd_w_in, m_conv_w, m_conv_b, m_conv_ln_g, m_conv_ln_b, m_odd_w_out, m_final_norm, v_even_norm, v_even_w_in, v_pool_w, v_pool_scale, v_ssm_log_dt, v_ssm_a_re, v_ssm_a_im, v_ssm_b_re, v_ssm_b_im, v_ssm_c_re, v_ssm_c_im, v_ssm_d, v_ssm_w_glu, v_even_w_out, v_odd_norm, v_odd_w_in, v_conv_w, v_conv_b, v_conv_ln_g, v_conv_ln_b, v_odd_w_out, v_final_norm):
    given = dict(locals())
    xs = x[0]
    tgt = loss_target[0]
    d_model = xs.shape[1]

    def local(prefix, name):
        t = given[prefix + name]
        return t if name == 'final_norm' else t[0]

    local_shapes = {n: local('', n).shape for n in WEIGHTS}
    full_shapes = {n: tuple(dim * N_DEV if i == SHARDED[n] else dim for i, dim in enumerate(local_shapes[n]))
                   for n in SHARDED_ORDER}

    gathered = _all_gather(_pack_local([local('', n) for n in SHARDED_ORDER]))
    full = {}
    r = 0
    for n in SHARDED_ORDER:
        count = 1
        for dim in local_shapes[n]:
            count *= dim
        rows = _round_up(count, PACK_TILE) // LANES
        rows8 = gathered[:, r:r + rows].reshape(N_DEV, -1)[:, :count]
        full[n] = _from_shard_major(rows8, full_shapes[n], SHARDED[n])
        r += rows
    for n in REPLICATED_ORDER:
        full[n] = local('', n)

    groups, state = full['ssm_a_re'].shape
    log_dt = full['ssm_log_dt'].reshape(groups, 1)
    b_re_t = jnp.transpose(full['ssm_b_re'], (2, 0, 1))
    b_im_t = jnp.transpose(full['ssm_b_im'], (2, 0, 1))
    abr, abi, bbr, bbi = _disc_fwd(log_dt, full['ssm_a_re'], full['ssm_a_im'], b_re_t, b_im_t)
    a_re_row = abr.reshape(1, groups * state)
    a_im_row = abi.reshape(1, groups * state)
    bb = jnp.concatenate([_block_diag(jnp.transpose(bbr, (1, 0, 2))), _block_diag(jnp.transpose(bbi, (1, 0, 2)))],
                         axis=2).astype(MM)
    cc_re = _block_diag(jnp.transpose(full['ssm_c_re'], (0, 2, 1))).astype(MM)
    cc_im = _block_diag(jnp.transpose(full['ssm_c_im'], (0, 2, 1))).astype(MM)

    w_in_e = full['even_w_in'].astype(MM)
    w_glu = full['ssm_w_glu'].astype(MM)
    w_out_e = full['even_w_out'].astype(MM)
    w_in_o = full['odd_w_in'].astype(MM)
    w_out_o = full['odd_w_out'].astype(MM)
    pw = full['pool_w'].astype(MM)
    g0 = full['even_norm'].reshape(1, d_model)
    g1 = full['odd_norm'].reshape(1, d_model)
    fg = full['final_norm'].reshape(1, d_model)
    scale = full['pool_scale'].reshape(1, -1)
    d_skip = full['ssm_d'].reshape(1, -1)
    cb = full['conv_b'].reshape(1, d_model)
    lg = full['conv_ln_g'].reshape(1, d_model)
    lb = full['conv_ln_b'].reshape(1, d_model)
    cw = full['conv_w']

    proj0, h0 = _even_in(xs, g0, w_in_e)
    pooled = _pool_fwd(proj0)
    bu_re, bu_im = _ssm_bu(proj0, bb)
    x_re, x_im = _scan_fwd(a_re_row, a_im_row, bu_re, bu_im)
    x1, ymix, mixed, yin, gv = _even_mix(pooled, proj0, x_re, x_im, xs, pw, scale, cc_re, cc_im, d_skip, w_glu,
                                         w_out_e)
    proj1, glu, h1 = _odd_in(x1, g1, w_in_o)
    conv = _conv_fwd(glu, cw)

    dc, dz1, dx2, y1, loss_tile, dcb, dlg, dlb, dfg = _odd_out(conv, proj1, x1, tgt, cb, lg, lb, w_out_o, fg)
    dglu, dcw = _conv_bwd(dc, glu, cw)
    dx1, dproj1, dg1 = _odd_in_bwd(x1, proj1, dglu, dz1, dx2, g1, w_in_o)
    (dpooled, du_skip, dz0, dx_re, dx_im, dyin, dgv, dpw, dscale, dd) = _even_mix_bwd(
        pooled, proj0, mixed, gv, dx1, pw, scale, cc_re, cc_im, d_skip, w_glu, w_out_e)
    g_re, g_im, dabr, dabi = _scan_bwd(a_re_row, a_im_row, dx_re, dx_im, x_re, x_im)
    du_ssm = _ssm_bu_bwd(g_re, g_im, du_skip, bb)
    du_pool = _pool_bwd(dpooled)
    grad_x, dg0 = _even_in_bwd(xs, du_pool, du_ssm, dz0, dx1, g0, w_in_e)

    half = du_pool.shape[1]
    n_state = groups * state
    grads = {}
    grads['odd_w_out'] = _mm_tn("dw_odd_out", y1, dx2, d_model, d_model)
    grads['odd_w_in'] = _mm_tn("dw_odd_in", h1, dproj1, d_model, 3 * d_model)
    grads['even_w_out'] = _mm_tn("dw_even_out", ymix, dx1, d_model, d_model)
    grads['ssm_w_glu'] = _mm_tn("dw_glu", yin, dgv, half, d_model)
    grads['even_w_in'] = jnp.concatenate(
        [_mm_tn("dw_even_in_pool", h0, du_pool, d_model, half), _mm_tn("dw_even_in_ssm", h0, du_ssm, d_model, half),
         _mm_tn("dw_even_in_gate", h0, dz0, d_model, d_model)], axis=1)
    sblk, cblk = n_state // SSM_BLOCKS, half // SSM_BLOCKS
    dcc_re = _mm_tn_blocks("dw_c_re", x_re, dyin, sblk, cblk, SSM_BLOCKS)
    dcc_im = _mm_tn_blocks("dw_c_im", x_im, dyin, sblk, cblk, SSM_BLOCKS, sign=-1.0)
    dbb_re = _mm_tn_blocks("dw_b_re", proj0, g_re, cblk, sblk, SSM_BLOCKS, a_col=SSM_BLOCKS)
    dbb_im = _mm_tn_blocks("dw_b_im", proj0, g_im, cblk, sblk, SSM_BLOCKS, a_col=SSM_BLOCKS)

    grads['ssm_c_re'] = jnp.transpose(_diag_blocks(dcc_re, groups), (0, 2, 1))
    grads['ssm_c_im'] = jnp.transpose(_diag_blocks(dcc_im, groups), (0, 2, 1))
    dbbr = jnp.transpose(_diag_blocks(dbb_re, groups), (1, 0, 2))
    dbbi = jnp.transpose(_diag_blocks(dbb_im, groups), (1, 0, 2))
    dld, dar, dai, dbr, dbi = _disc_bwd(log_dt, full['ssm_a_re'], full['ssm_a_im'], b_re_t, b_im_t,
                                        dabr.reshape(groups, state), dabi.reshape(groups, state), dbbr, dbbi)
    grads['ssm_log_dt'] = dld.reshape(groups)
    grads['ssm_a_re'] = dar
    grads['ssm_a_im'] = dai
    grads['ssm_b_re'] = jnp.transpose(dbr, (1, 2, 0))
    grads['ssm_b_im'] = jnp.transpose(dbi, (1, 2, 0))
    grads['even_norm'] = dg0
    grads['odd_norm'] = dg1
    grads['final_norm'] = dfg
    grads['pool_w'] = dpw
    grads['pool_scale'] = dscale
    grads['ssm_d'] = dd
    grads['conv_w'] = dcw[:CONV_KERNEL]
    grads['conv_b'] = dcb
    grads['conv_ln_g'] = dlg
    grads['conv_ln_b'] = dlb

    pieces = [_as_rows(_shard_major(grads[n].reshape(full_shapes[n]), SHARDED[n])) for n in SHARDED_ORDER]
    for n in REPLICATED_ORDER:
        rows = _as_rows(grads[n].reshape(-1))
        pieces.append(jnp.broadcast_to(rows[None], (N_DEV,) + rows.shape))
    used = sum(p.shape[1] for p in pieces)
    total = _round_up(used, PACK_ROW_BLOCK)
    if total != used:
        pieces.append(jnp.zeros((N_DEV, total - used, LANES), F32))
    received = _exchange(jnp.concatenate(pieces, axis=1))

    order = SHARDED_ORDER + REPLICATED_ORDER

    def packed(prefix):
        rows = _pack_local([local(prefix, n) for n in order])
        return jnp.pad(rows, ((0, total - used), (0, 0)))

    g_p, delta_p, m_p, v_p = _adamw(received, packed(''), packed('m_'), packed('v_'))

    out_shapes = [given[n].shape for n in order]
    unpacked = {}
    for kind, arr in (('grad', g_p), ('delta', delta_p), ('new_m', m_p), ('new_v', v_p)):
        for n, t in zip(order, _unpack_local(arr, out_shapes)):
            unpacked[kind, n] = t

    loss = lax.psum(loss_tile[0, 0], ("x", "y", "c"))
    outs = [loss, grad_x[None]]
    for kind in ('grad', 'delta', 'new_m', 'new_v'):
        outs.extend(unpacked[kind, n] for n in WEIGHTS)
    return tuple(outs)
```

```python
import functools

import jax
import jax.numpy as jnp
from jax import lax
from jax.experimental import pallas as pl
from jax.experimental.pallas import tpu as pltpu

F32 = jnp.float32
MM = jnp.bfloat16
SDS = jax.ShapeDtypeStruct

RMS_EPS = 1e-6
LN_EPS = 1e-5
POOL_WINDOWS = (2, 4, 8, 16)
POOL_HALO = 16
CONV_KERNEL = 31
CONV_HALO = 32
N_DEV = 8
LANES = 128
SUBLANES = 8
PACK_TILE = SUBLANES * LANES
ADAM_ROWS = 128
MIB = 1024 * 1024

ADAM_LR = 0.001
ADAM_B1 = 0.9
ADAM_B2 = 0.999
ADAM_EPS = 1e-08
ADAM_WD = 0.01
ADAM_STEP = 10

WEIGHTS = ['even_norm', 'even_w_in', 'pool_w', 'pool_scale', 'ssm_log_dt', 'ssm_a_re', 'ssm_a_im', 'ssm_b_re',
           'ssm_b_im', 'ssm_c_re', 'ssm_c_im', 'ssm_d', 'ssm_w_glu', 'even_w_out', 'odd_norm', 'odd_w_in', 'conv_w',
           'conv_b', 'conv_ln_g', 'conv_ln_b', 'odd_w_out', 'final_norm']
SMALL_VECTORS = ('odd_norm', 'conv_b', 'conv_ln_g', 'conv_ln_b')
SMALL_AT = {'odd_norm': 32, 'conv_b': 40, 'conv_ln_g': 48, 'conv_ln_b': 56}
EARLY_REPLICATED = ['pool_w', 'pool_scale', 'ssm_c_re', 'ssm_c_im', 'ssm_d', 'final_norm']
LATE_REPLICATED = ['even_norm', 'ssm_log_dt', 'ssm_a_re', 'ssm_a_im', 'ssm_b_re', 'ssm_b_im']


def _round_up(n, m):
    return (n + m - 1) // m * m


def _sigmoid(x):
    return jax.nn.sigmoid(x)


def _silu(x):
    return x * jax.nn.sigmoid(x)


def _rms(x, g):
    return x * lax.rsqrt(jnp.mean(x * x, axis=-1, keepdims=True) + RMS_EPS) * g


def _dot(a, b):
    return jnp.dot(a.astype(MM), b.astype(MM), preferred_element_type=F32)


def _dot_nt(a, b):
    return lax.dot_general(a.astype(MM), b.astype(MM), (((1,), (1,)), ((), ())), preferred_element_type=F32)


def _dot_tn(a, b):
    return lax.dot_general(a.astype(MM), b.astype(MM), (((0,), (0,)), ((), ())), preferred_element_type=F32)


def _rows(ts, width, col=0):
    return pl.BlockSpec((ts, width), lambda i: (i, col))


def _rows_rev(ts, width, n, col=0):
    return pl.BlockSpec((ts, width), lambda i: (n - 1 - i, col))


def _full(shape):
    zeros = (0,) * len(shape)
    return pl.BlockSpec(shape, lambda i: zeros)


def _seq_params(vmem_mib=48, dims=1):
    return pltpu.CompilerParams(dimension_semantics=("arbitrary",) * dims, vmem_limit_bytes=vmem_mib * MIB)


def _mesh_position():
    x, y, c = lax.axis_index("x"), lax.axis_index("y"), lax.axis_index("c")
    return x, y, c


def _peer(pos, relation):
    x, y, c = pos
    px = 1 - x if relation & 4 else x
    py = 1 - y if relation & 2 else y
    pc = 1 - c if relation & 1 else c
    return px, py, pc


ANY = pl.BlockSpec(memory_space=pl.ANY)


class _Push:
    def __init__(self, array, scatter):
        self.array = array
        self.scatter = scatter
        self.out = SDS(array.shape if scatter else (N_DEV,) + array.shape, array.dtype)


def _push_scratch(n):
    return [pltpu.SemaphoreType.DMA((n * (N_DEV - 1),)), pltpu.SemaphoreType.DMA((n * (N_DEV - 1),)),
            pltpu.SemaphoreType.DMA((n,))]


def _push_copies(pushes, srcs, dsts, send_sems, recv_sems, local_sems):
    pos = _mesh_position()
    me = 4 * pos[0] + 2 * pos[1] + pos[2]
    copies = []
    for a, (push, src, dst) in enumerate(zip(pushes, srcs, dsts)):
        copies.append(pltpu.make_async_copy(src.at[me] if push.scatter else src, dst.at[me], local_sems.at[a]))
        for relation in range(1, N_DEV):
            peer = _peer(pos, relation)
            peer_id = 4 * peer[0] + 2 * peer[1] + peer[2]
            k = a * (N_DEV - 1) + relation - 1
            copies.append(pltpu.make_async_remote_copy(
                src_ref=src.at[peer_id] if push.scatter else src, dst_ref=dst.at[me],
                send_sem=send_sems.at[k], recv_sem=recv_sems.at[k], device_id=peer,
                device_id_type=pl.DeviceIdType.MESH))
    return copies


def _push_alone(name, pushes):
    n = len(pushes)

    def body(*refs):
        copies = _push_copies(pushes, refs[:n], refs[n:2 * n], *refs[2 * n:])
        for cp in copies:
            cp.start()
        for cp in copies:
            cp.wait()

    return pl.pallas_call(
        body, name=name, out_shape=[p.out for p in pushes], in_specs=[ANY] * n, out_specs=[ANY] * n,
        scratch_shapes=_push_scratch(n),
    )(*[p.array for p in pushes])


def _riding_start(pushes, first, refs):
    n = len(pushes)

    @pl.when(first)
    def _():
        for cp in _push_copies(pushes, refs[:n], refs[n:2 * n], *refs[2 * n:]):
            cp.start()


def _riding_wait(pushes, last, refs):
    n = len(pushes)

    @pl.when(last)
    def _():
        for cp in _push_copies(pushes, refs[:n], refs[n:2 * n], *refs[2 * n:]):
            cp.wait()


def _split_refs(refs, n_in, n_out, n_push):
    ins = refs[:n_in]
    srcs = refs[n_in:n_in + n_push]
    o0 = n_in + n_push
    outs = refs[o0:o0 + n_out]
    dsts = refs[o0 + n_out:o0 + n_out + n_push]
    s0 = o0 + n_out + n_push
    sems = refs[s0:s0 + 3]
    scratch = refs[s0 + 3:]
    return ins, outs, scratch, tuple(srcs) + tuple(dsts) + tuple(sems)


def _mm_tn(name, a, b, m, n, a_col=0, b_col=0, sign=1.0, ts=512):
    s = a.shape[0]
    tn = min(n, 512)
    nj = n // tn

    def body(a_ref, b_ref, o_ref):
        @pl.when(pl.program_id(1) == 0)
        def _():
            o_ref[...] = jnp.zeros_like(o_ref)

        o_ref[...] += sign * _dot_tn(a_ref[...], b_ref[...])

    return pl.pallas_call(
        body, name=name, grid=(nj, s // ts),
        in_specs=[pl.BlockSpec((ts, m), lambda j, k: (k, a_col)),
                  pl.BlockSpec((ts, tn), lambda j, k: (k, b_col * nj + j))],
        out_specs=pl.BlockSpec((m, tn), lambda j, k: (0, j)),
        out_shape=SDS((m, n), F32),
        compiler_params=_seq_params(48, dims=2),
    )(a, b)


def _mm_tn_blocks(name, a, b, m, n, nb, a_col=0, a_step=1, b_col=0, sign=1.0, ts=512):
    s = a.shape[0]

    def body(a_ref, b_ref, o_ref):
        @pl.when(pl.program_id(1) == 0)
        def _():
            o_ref[...] = jnp.zeros_like(o_ref)

        o_ref[...] += sign * _dot_tn(a_ref[...], b_ref[...])

    return pl.pallas_call(
        body, name=name, grid=(nb, s // ts),
        in_specs=[pl.BlockSpec((ts, m), lambda j, k: (k, a_col + a_step * j)),
                  pl.BlockSpec((ts, n), lambda j, k: (k, b_col + j))],
        out_specs=pl.BlockSpec((None, m, n), lambda j, k: (j, 0, 0)),
        out_shape=SDS((nb, m, n), F32),
        compiler_params=_seq_params(48, dims=2),
    )(a, b)


def _even_in(x, g0, w_in):
    s, d = x.shape
    shards, _, wc = w_in.shape
    e = shards * wc
    ts = 512

    def body(x_ref, g_ref, w_ref, proj_ref, h_ref):
        hb = _rms(x_ref[...], g_ref[...]).astype(MM)
        h_ref[...] = hb
        for k in range(shards):
            proj_ref[:, k * wc:(k + 1) * wc] = jnp.dot(hb, w_ref[k], preferred_element_type=F32)

    return pl.pallas_call(
        body, name="even_in", grid=(s // ts,),
        in_specs=[_rows(ts, d), _full((1, d)), _full(w_in.shape)],
        out_specs=[_rows(ts, e), _rows(ts, d)],
        out_shape=[SDS((s, e), F32), SDS((s, d), MM)],
        compiler_params=_seq_params(48),
    )(x, g0, w_in)


def _pool_counts(t0, ts, w):
    pos = (t0 + lax.broadcasted_iota(jnp.int32, (ts, LANES), 0) + 1).astype(F32)
    return jnp.minimum(pos, float(w))


def _pool_fwd(proj):
    s = proj.shape[0]
    width = LANES * len(POOL_WINDOWS)
    ts = 512
    per = ts // POOL_HALO

    def body(prev_ref, u_ref, out_ref, ext):
        i = pl.program_id(0)
        ext[0:POOL_HALO, :] = jnp.where(i == 0, 0.0, prev_ref[...])
        ext[POOL_HALO:, :] = u_ref[...]
        for g, w in enumerate(POOL_WINDOWS):
            cols = slice(g * LANES, (g + 1) * LANES)
            tok = ext[pl.ds(POOL_HALO, ts), cols]
            acc = tok
            for k in range(1, w):
                acc = acc + ext[pl.ds(POOL_HALO - k, ts), cols]
            out_ref[:, cols] = acc / _pool_counts(i * ts, ts, w) - tok

    return pl.pallas_call(
        body, name="pool_fwd", grid=(s // ts,),
        in_specs=[pl.BlockSpec((POOL_HALO, width), lambda i: (jnp.maximum(i * per - 1, 0), 0)),
                  _rows(ts, width)],
        out_specs=_rows(ts, width),
        out_shape=SDS((s, width), F32),
        scratch_shapes=[pltpu.VMEM((ts + POOL_HALO, width), F32)],
        compiler_params=_seq_params(32),
    )(proj, proj)


def _pool_bwd(dp):
    s, width = dp.shape
    ts = 512
    per = ts // POOL_HALO
    n = s // ts

    def body(dp_ref, next_ref, out_ref, ext):
        i = pl.program_id(0)
        nxt = jnp.where(i == n - 1, 0.0, next_ref[...])
        for g, w in enumerate(POOL_WINDOWS):
            cols = slice(g * LANES, (g + 1) * LANES)
            cur = dp_ref[:, cols]
            ext[0:ts, cols] = cur / _pool_counts(i * ts, ts, w)
            ext[ts:, cols] = nxt[:, cols] / _pool_counts((i + 1) * ts, POOL_HALO, w)
            acc = -cur
            for k in range(w):
                acc = acc + ext[pl.ds(k, ts), cols]
            out_ref[:, cols] = acc

    return pl.pallas_call(
        body, name="pool_bwd", grid=(n,),
        in_specs=[_rows(ts, width),
                  pl.BlockSpec((POOL_HALO, width), lambda i: (jnp.minimum((i + 1) * per, s // POOL_HALO - 1), 0))],
        out_specs=_rows(ts, width),
        out_shape=SDS((s, width), F32),
        scratch_shapes=[pltpu.VMEM((ts + POOL_HALO, width), F32)],
        compiler_params=_seq_params(32),
    )(dp, dp)


SSM_BLOCKS = 4


def _ssm_bu(proj, bb):
    s = proj.shape[0]
    nb, cb, two_nb = bb.shape
    sb = two_nb // 2
    cin, n = nb * cb, nb * sb
    ts = 512

    def body(u_ref, bb_ref, re_ref, im_ref):
        for b in range(nb):
            bu = _dot(u_ref[:, b * cb:(b + 1) * cb], bb_ref[b])
            re_ref[:, b * sb:(b + 1) * sb] = bu[:, :sb]
            im_ref[:, b * sb:(b + 1) * sb] = bu[:, sb:]

    return pl.pallas_call(
        body, name="ssm_bu", grid=(s // ts,),
        in_specs=[_rows(ts, cin, col=1), _full(bb.shape)],
        out_specs=[_rows(ts, n), _rows(ts, n)],
        out_shape=[SDS((s, n), F32), SDS((s, n), F32)],
        compiler_params=_seq_params(48),
    )(proj, bb)


SCAN_LANES = 256


def _cmul(a, b):
    return a[0] * b[0] - a[1] * b[1], a[0] * b[1] + a[1] * b[0]


def _scan_constants(ar_row, ai_row, reverse):
    shape = (SUBLANES, ar_row.shape[1])
    a1 = (jnp.broadcast_to(ar_row, shape), jnp.broadcast_to(ai_row, shape))
    a2 = _cmul(a1, a1)
    a4 = _cmul(a2, a2)
    a8 = _cmul(a4, a4)
    row = lax.broadcasted_iota(jnp.int32, shape, 0)
    expo = (SUBLANES - row) if reverse else (row + 1)
    pr, pi = jnp.ones(shape, F32), jnp.zeros(shape, F32)
    for bit, q in enumerate((a1, a2, a4, a8)):
        take = ((expo >> bit) & 1) == 1
        nr, ni = _cmul((pr, pi), q)
        pr, pi = jnp.where(take, nr, pr), jnp.where(take, ni, pi)
    return row, (a1, a2, a4), (pr, pi)


def _group_scan(v, row, powers, reverse):
    vr, vi = v
    for k, q in zip((1, 2, 4), powers):
        if reverse:
            keep = row < SUBLANES - k
            sr, si = pltpu.roll(vr, SUBLANES - k, 0), pltpu.roll(vi, SUBLANES - k, 0)
        else:
            keep = row >= k
            sr, si = pltpu.roll(vr, k, 0), pltpu.roll(vi, k, 0)
        sr, si = jnp.where(keep, sr, 0.0), jnp.where(keep, si, 0.0)
        tr, ti = _cmul(q, (sr, si))
        vr, vi = vr + tr, vi + ti
    return vr, vi


def _scan_fwd(a_re, a_im, bu_re, bu_im, pushes):
    s, n = bu_re.shape
    ts = 512
    nt = s // ts
    groups = ts // SUBLANES
    np_ = len(pushes)

    def body(*refs):
        ins, outs, scratch, push_refs = _split_refs(refs, 4, 2, np_)
        ar_ref, ai_ref, bre_ref, bim_ref = ins
        xre_ref, xim_ref = outs
        cre, cim = scratch
        i = pl.program_id(0)
        _riding_start(pushes, i == 0, push_refs)

        @pl.when(i == 0)
        def _():
            cre[...] = jnp.zeros_like(cre)
            cim[...] = jnp.zeros_like(cim)

        for c in range(n // SCAN_LANES):
            cols = slice(c * SCAN_LANES, (c + 1) * SCAN_LANES)
            row, powers, carry_pow = _scan_constants(ar_ref[:, cols], ai_ref[:, cols], False)

            def group(gi, carry):
                r0 = pl.multiple_of(gi * SUBLANES, SUBLANES)
                v = (bre_ref[pl.ds(r0, SUBLANES), cols], bim_ref[pl.ds(r0, SUBLANES), cols])
                vr, vi = _group_scan(v, row, powers, False)
                tr, ti = _cmul(carry_pow, carry)
                vr, vi = vr + tr, vi + ti
                xre_ref[pl.ds(r0, SUBLANES), cols] = vr
                xim_ref[pl.ds(r0, SUBLANES), cols] = vi
                shape = vr.shape
                return (jnp.broadcast_to(vr[SUBLANES - 1:SUBLANES, :], shape),
                        jnp.broadcast_to(vi[SUBLANES - 1:SUBLANES, :], shape))

            cr, ci = lax.fori_loop(0, groups, group, (cre[:, cols], cim[:, cols]), unroll=2)
            cre[:, cols] = cr
            cim[:, cols] = ci

        _riding_wait(pushes, i == nt - 1, push_refs)

    out = pl.pallas_call(
        body, name="ssm_scan", grid=(nt,),
        in_specs=[_full((1, n)), _full((1, n)), _rows(ts, n), _rows(ts, n)] + [ANY] * np_,
        out_specs=[_rows(ts, n), _rows(ts, n)] + [ANY] * np_,
        out_shape=[SDS((s, n), F32), SDS((s, n), F32)] + [p.out for p in pushes],
        scratch_shapes=_push_scratch(np_) + [pltpu.VMEM((SUBLANES, n), F32), pltpu.VMEM((SUBLANES, n), F32)],
        compiler_params=_seq_params(48),
    )(a_re, a_im, bu_re, bu_im, *[p.array for p in pushes])
    return out[0], out[1], out[2:]


def _scan_bwd(a_re, a_im, dx_re, dx_im, x_re, x_im, pushes):
    s, n = dx_re.shape
    ts = 256
    nt = s // ts
    groups = ts // SUBLANES
    np_ = len(pushes)

    def body(*refs):
        ins, outs, scratch, push_refs = _split_refs(refs, 6, 4, np_)
        ar_ref, ai_ref, dre_ref, dim_ref, xre_ref, xim_ref = ins
        gre_ref, gim_ref, dar_ref, dai_ref = outs
        cre, cim, accr, acci = scratch
        i = pl.program_id(0)
        _riding_start(pushes, i == 0, push_refs)

        @pl.when(i == 0)
        def _():
            for ref in (cre, cim, accr, acci):
                ref[...] = jnp.zeros_like(ref)

        for c in range(n // SCAN_LANES):
            cols = slice(c * SCAN_LANES, (c + 1) * SCAN_LANES)
            row, powers, carry_pow = _scan_constants(ar_ref[:, cols], -ai_ref[:, cols], True)

            def group(k, state):
                carry, acc = state[:2], state[2:]
                r0 = pl.multiple_of((groups - 1 - k) * SUBLANES, SUBLANES)
                v = (dre_ref[pl.ds(r0, SUBLANES), cols], dim_ref[pl.ds(r0, SUBLANES), cols])
                vr, vi = _group_scan(v, row, powers, True)
                tr, ti = _cmul(carry_pow, carry)
                vr, vi = vr + tr, vi + ti
                gre_ref[pl.ds(r0, SUBLANES), cols] = vr
                gim_ref[pl.ds(r0, SUBLANES), cols] = vi
                nr = jnp.where(row < SUBLANES - 1, pltpu.roll(vr, SUBLANES - 1, 0), carry[0])
                ni = jnp.where(row < SUBLANES - 1, pltpu.roll(vi, SUBLANES - 1, 0), carry[1])
                xr, xi = xre_ref[pl.ds(r0, SUBLANES), cols], xim_ref[pl.ds(r0, SUBLANES), cols]
                shape = vr.shape
                return (jnp.broadcast_to(vr[0:1, :], shape), jnp.broadcast_to(vi[0:1, :], shape),
                        acc[0] + nr * xr + ni * xi, acc[1] + ni * xr - nr * xi)

            out = lax.fori_loop(0, groups, group, (cre[:, cols], cim[:, cols], accr[:, cols], acci[:, cols]),
                                unroll=2)
            cre[:, cols], cim[:, cols], accr[:, cols], acci[:, cols] = out

        @pl.when(i == nt - 1)
        def _():
            dar_ref[...] = jnp.sum(accr[...], axis=0, keepdims=True)
            dai_ref[...] = jnp.sum(acci[...], axis=0, keepdims=True)

        _riding_wait(pushes, i == nt - 1, push_refs)

    small = pltpu.VMEM((SUBLANES, n), F32)
    out = pl.pallas_call(
        body, name="ssm_scan_bwd", grid=(nt,),
        in_specs=[_full((1, n)), _full((1, n))] + [_rows_rev(ts, n, nt)] * 4 + [ANY] * np_,
        out_specs=[_rows_rev(ts, n, nt), _rows_rev(ts, n, nt), _full((1, n)), _full((1, n))] + [ANY] * np_,
        out_shape=[SDS((s, n), F32), SDS((s, n), F32), SDS((1, n), F32), SDS((1, n), F32)]
        + [p.out for p in pushes],
        scratch_shapes=_push_scratch(np_) + [small, small, small, small],
        compiler_params=_seq_params(48),
    )(a_re, a_im, dx_re, dx_im, x_re, x_im, *[p.array for p in pushes])
    return out[0], out[1], out[2], out[3], out[4:]


def _mix_a(mixed, scale, za):
    return mixed * scale * _silu(za)


def _mix_b(val, gate, zb):
    return val * _sigmoid(gate) * _silu(zb)


def _even_mix(pooled, proj, x_re, x_im, x, pw, scale, cc_re, cc_im, d_skip, w_glu, w_out):
    s, d = x.shape
    half = pooled.shape[1]
    n = x_re.shape[1]
    ts = 256

    def body(pooled_ref, u_ref, z_ref, xre_ref, xim_ref, x_ref, pw_ref, scale_ref, ccre_ref, ccim_ref, d_ref,
             wglu_ref, wout_ref, x1_ref, ymix_ref, mixed_ref, yin_ref, gv_ref):
        for g in range(len(POOL_WINDOWS)):
            cols = slice(g * LANES, (g + 1) * LANES)
            mixed_ref[:, cols] = _dot(pooled_ref[:, cols], pw_ref[g])
        sb, cb = ccre_ref.shape[1:]
        for b in range(SSM_BLOCKS):
            states, chans = slice(b * sb, (b + 1) * sb), slice(b * cb, (b + 1) * cb)
            yin_ref[:, chans] = (_dot(xre_ref[:, states], ccre_ref[b]) - _dot(xim_ref[:, states], ccim_ref[b])
                                 + d_ref[:, chans] * u_ref[:, chans])
        yin = yin_ref[...]
        gv = _dot(yin, wglu_ref[...])
        gv_ref[...] = gv
        ya = _mix_a(mixed_ref[...], scale_ref[...], z_ref[:, :half])
        yb = _mix_b(gv[:, :half], gv[:, half:], z_ref[:, half:])
        ymix = jnp.concatenate([ya, yb], axis=1).astype(MM)
        ymix_ref[...] = ymix
        x1_ref[...] = x_ref[...] + jnp.dot(ymix, wout_ref[...], preferred_element_type=F32)

    return pl.pallas_call(
        body, name="even_mix", grid=(s // ts,),
        in_specs=[_rows(ts, half), _rows(ts, half, col=1), _rows(ts, d, col=1), _rows(ts, n), _rows(ts, n),
                  _rows(ts, d), _full(pw.shape), _full((1, half)), _full(cc_re.shape), _full(cc_im.shape),
                  _full((1, half)), _full((half, d)), _full((d, d))],
        out_specs=[_rows(ts, d), _rows(ts, d), _rows(ts, half), _rows(ts, half), _rows(ts, d)],
        out_shape=[SDS((s, d), F32), SDS((s, d), MM), SDS((s, half), F32), SDS((s, half), F32), SDS((s, d), F32)],
        compiler_params=_seq_params(56),
    )(pooled, proj, proj, x_re, x_im, x, pw, scale, cc_re, cc_im, d_skip, w_glu, w_out)


def _glu(val, gt):
    return val * _sigmoid(gt)


def _odd_in(x1, g1, w_in):
    s, d = x1.shape
    e = w_in.shape[1]
    ts = 256

    def body(x_ref, g_ref, w_ref, proj_ref, glu_ref, h_ref):
        hb = _rms(x_ref[...], g_ref[...]).astype(MM)
        h_ref[...] = hb
        proj = jnp.dot(hb, w_ref[...], preferred_element_type=F32)
        proj_ref[...] = proj
        glu_ref[...] = _glu(proj[:, :d], proj[:, d:2 * d])

    return pl.pallas_call(
        body, name="odd_in", grid=(s // ts,),
        in_specs=[_rows(ts, d), _full((1, d)), _full((d, e))],
        out_specs=[_rows(ts, e), _rows(ts, d), _rows(ts, d)],
        out_shape=[SDS((s, e), F32), SDS((s, d), F32), SDS((s, d), MM)],
        compiler_params=_seq_params(48),
    )(x1, g1, w_in)


CONV_ROWS = 32
CONV_COLS = 256


def _phase_copies(sh, rows):
    for o in range(1, SUBLANES):
        sh[o, 0:rows, :] = sh[0, pl.ds(o, rows), :]


def _conv_fwd(g, w):
    s, ch = g.shape
    ts = 256
    per = ts // CONV_HALO
    lead = CONV_HALO - (CONV_KERNEL - 1)
    span = ts + CONV_HALO - SUBLANES

    def body(prev_ref, g_ref, w_ref, out_ref, sh):
        i = pl.program_id(0)
        sh[0, 0:CONV_HALO, :] = jnp.where(i == 0, 0.0, prev_ref[...])
        sh[0, CONV_HALO:, :] = g_ref[...]
        _phase_copies(sh, span)

        def block(rb, carry):
            r0 = pl.multiple_of(rb * CONV_ROWS, CONV_ROWS)
            for c in range(ch // CONV_COLS):
                cols = slice(c * CONV_COLS, (c + 1) * CONV_COLS)
                acc = jnp.zeros((CONV_ROWS, CONV_COLS), F32)
                for j in range(CONV_KERNEL):
                    q, o = divmod(lead + j, SUBLANES)
                    acc = acc + w_ref[j:j + 1, cols] * sh[o, pl.ds(r0 + SUBLANES * q, CONV_ROWS), cols]
                out_ref[pl.ds(r0, CONV_ROWS), cols] = acc
            return carry

        lax.fori_loop(0, ts // CONV_ROWS, block, 0)

    return pl.pallas_call(
        body, name="conv_fwd", grid=(s // ts,),
        in_specs=[pl.BlockSpec((CONV_HALO, ch), lambda i: (jnp.maximum(i * per - 1, 0), 0)),
                  _rows(ts, ch), _full(w.shape)],
        out_specs=_rows(ts, ch),
        out_shape=SDS((s, ch), F32),
        scratch_shapes=[pltpu.VMEM((SUBLANES, ts + CONV_HALO, ch), F32)],
        compiler_params=_seq_params(40),
    )(g, g, w)


def _conv_bwd(dc, g, w):
    s, ch = dc.shape
    ts = 256
    per = ts // CONV_HALO
    n = s // ts
    lead = CONV_HALO - (CONV_KERNEL - 1)
    span = ts + CONV_HALO - SUBLANES

    def body(dc_ref, next_ref, prev_ref, g_ref, w_ref, dg_ref, dw_ref, shd, shg, wacc):
        i = pl.program_id(0)

        @pl.when(i == 0)
        def _():
            wacc[...] = jnp.zeros_like(wacc)

        shd[0, 0:ts, :] = dc_ref[...]
        shd[0, ts:, :] = jnp.where(i == n - 1, 0.0, next_ref[...])
        shg[0, 0:CONV_HALO, :] = jnp.where(i == 0, 0.0, prev_ref[...])
        shg[0, CONV_HALO:, :] = g_ref[...]
        _phase_copies(shd, span)
        _phase_copies(shg, span)

        def dg_block(rb, carry):
            r0 = pl.multiple_of(rb * CONV_ROWS, CONV_ROWS)
            for c in range(ch // CONV_COLS):
                cols = slice(c * CONV_COLS, (c + 1) * CONV_COLS)
                acc = jnp.zeros((CONV_ROWS, CONV_COLS), F32)
                for j in range(CONV_KERNEL):
                    q, o = divmod(CONV_KERNEL - 1 - j, SUBLANES)
                    acc = acc + w_ref[j:j + 1, cols] * shd[o, pl.ds(r0 + SUBLANES * q, CONV_ROWS), cols]
                dg_ref[pl.ds(r0, CONV_ROWS), cols] = acc
            return carry

        lax.fori_loop(0, ts // CONV_ROWS, dg_block, 0)

        for c in range(ch // LANES):
            cols = slice(c * LANES, (c + 1) * LANES)

            def dw_rows(k, accs):
                r0 = pl.multiple_of(k * SUBLANES, SUBLANES)
                cur = shd[0, pl.ds(r0, SUBLANES), cols]
                out = []
                for j in range(CONV_KERNEL):
                    q, o = divmod(lead + j, SUBLANES)
                    out.append(accs[j] + cur * shg[o, pl.ds(r0 + SUBLANES * q, SUBLANES), cols])
                return tuple(out)

            init = tuple(wacc[j, :, cols] for j in range(CONV_KERNEL))
            accs = lax.fori_loop(0, ts // SUBLANES, dw_rows, init, unroll=2)
            for j in range(CONV_KERNEL):
                wacc[j, :, cols] = accs[j]

        @pl.when(i == n - 1)
        def _():
            dw_ref[...] = jnp.zeros_like(dw_ref)
            for j in range(CONV_KERNEL):
                dw_ref[j:j + 1, :] = jnp.sum(wacc[j], axis=0, keepdims=True)

    return pl.pallas_call(
        body, name="conv_bwd", grid=(n,),
        in_specs=[_rows(ts, ch),
                  pl.BlockSpec((CONV_HALO, ch), lambda i: (jnp.minimum((i + 1) * per, s // CONV_HALO - 1), 0)),
                  pl.BlockSpec((CONV_HALO, ch), lambda i: (jnp.maximum(i * per - 1, 0), 0)),
                  _rows(ts, ch), _full(w.shape)],
        out_specs=[_rows(ts, ch), _full((CONV_HALO, ch))],
        out_shape=[SDS((s, ch), F32), SDS((CONV_HALO, ch), F32)],
        scratch_shapes=[pltpu.VMEM((SUBLANES, ts + CONV_HALO, ch), F32),
                        pltpu.VMEM((SUBLANES, ts + CONV_HALO, ch), F32),
                        pltpu.VMEM((CONV_HALO, SUBLANES, ch), F32)],
        compiler_params=_seq_params(56),
    )(dc, dc, g, g, w)


def _conv_act(c, z1, cb, lg, lb):
    cc = c + cb
    mu = jnp.mean(cc, axis=-1, keepdims=True)
    dev = cc - mu
    var = jnp.mean(dev * dev, axis=-1, keepdims=True)
    cn = dev * lax.rsqrt(var + LN_EPS) * lg + lb
    return _silu(cn) * _silu(z1)


def _odd_out(c, proj1, x1, tgt, cb, lg, lb, w_out, fg):
    s, d = c.shape
    ts = 256

    def body(c_ref, z_ref, x1_ref, t_ref, cb_ref, lg_ref, lb_ref, w_ref, fg_ref,
             dc_ref, dz_ref, dx2_ref, y1_ref, loss_ref, dcb_ref, dlg_ref, dlb_ref, dfg_ref):
        @pl.when(pl.program_id(0) == 0)
        def _():
            for ref in (loss_ref, dcb_ref, dlg_ref, dlb_ref, dfg_ref):
                ref[...] = jnp.zeros_like(ref)

        y1, act_vjp = jax.vjp(_conv_act, c_ref[...], z_ref[...], cb_ref[...], lg_ref[...], lb_ref[...])
        y1b = y1.astype(MM)
        y1_ref[...] = y1b
        x2 = x1_ref[...] + jnp.dot(y1b, w_ref[...], preferred_element_type=F32)
        tgt_tile = t_ref[...]

        def head(x2, fg):
            err = jnp.square(_rms(x2, fg) - tgt_tile)
            return 0.5 * jnp.sum(jnp.mean(err, axis=-1))

        loss, (dx2, dfg) = jax.value_and_grad(head, argnums=(0, 1))(x2, fg_ref[...])
        loss_ref[...] += loss
        dfg_ref[...] += dfg
        dx2_ref[...] = dx2
        dy1 = _dot_nt(dx2, w_ref[...])
        dc, dz, dcb, dlg, dlb = act_vjp(dy1)
        dc_ref[...] = dc
        dz_ref[...] = dz
        dcb_ref[...] += dcb
        dlg_ref[...] += dlg
        dlb_ref[...] += dlb

    vec = SDS((1, d), F32)
    return pl.pallas_call(
        body, name="odd_out", grid=(s // ts,),
        in_specs=[_rows(ts, d), _rows(ts, d, col=2), _rows(ts, d), _rows(ts, d), _full((1, d)), _full((1, d)),
                  _full((1, d)), _full((d, d)), _full((1, d))],
        out_specs=[_rows(ts, d), _rows(ts, d), _rows(ts, d), _rows(ts, d), _full((SUBLANES, LANES)),
                   _full((1, d)), _full((1, d)), _full((1, d)), _full((1, d))],
        out_shape=[SDS((s, d), F32), SDS((s, d), F32), SDS((s, d), F32), SDS((s, d), MM),
                   SDS((SUBLANES, LANES), F32), vec, vec, vec, vec],
        compiler_params=_seq_params(56),
    )(c, proj1, x1, tgt, cb, lg, lb, w_out, fg)


def _odd_in_bwd(x1, proj1, dglu, dz1, dx2, g1, w_in):
    s, d = x1.shape
    e = w_in.shape[1]
    ts = 256

    def body(x_ref, vg_ref, dglu_ref, dz_ref, dx2_ref, g_ref, w_ref, dx1_ref, dp_ref, dg_ref):
        @pl.when(pl.program_id(0) == 0)
        def _():
            dg_ref[...] = jnp.zeros_like(dg_ref)

        _, glu_vjp = jax.vjp(_glu, vg_ref[:, :d], vg_ref[:, d:])
        dval, dgt = glu_vjp(dglu_ref[...])
        dp = jnp.concatenate([dval, dgt, dz_ref[...]], axis=1).astype(MM)
        dp_ref[...] = dp
        dh = _dot_nt(dp, w_ref[...])
        _, rms_vjp = jax.vjp(_rms, x_ref[...], g_ref[...])
        dxa, dg = rms_vjp(dh)
        dx1_ref[...] = dxa + dx2_ref[...]
        dg_ref[...] += dg

    return pl.pallas_call(
        body, name="odd_in_bwd", grid=(s // ts,),
        in_specs=[_rows(ts, d), _rows(ts, 2 * d), _rows(ts, d), _rows(ts, d), _rows(ts, d), _full((1, d)),
                  _full((d, e))],
        out_specs=[_rows(ts, d), _rows(ts, e), _full((1, d))],
        out_shape=[SDS((s, d), F32), SDS((s, e), MM), SDS((1, d), F32)],
        compiler_params=_seq_params(56),
    )(x1, proj1, dglu, dz1, dx2, g1, w_in)


def _even_mix_bwd(pooled, proj, mixed, gv, dx1, pw, scale, cc_re, cc_im, d_skip, w_glu, w_out, pushes):
    s, d = dx1.shape
    half = pooled.shape[1]
    n = cc_re.shape[0] * cc_re.shape[1]
    ts = 256
    nt = s // ts
    groups = len(POOL_WINDOWS)
    np_ = len(pushes)

    def body(*refs):
        ins, outs, _, push_refs = _split_refs(refs, 13, 10, np_)
        (pooled_ref, u_ref, z_ref, mixed_ref, gv_ref, dx1_ref, pw_ref, scale_ref, ccre_ref, ccim_ref, d_ref,
         wglu_ref, wout_ref) = ins
        (dpooled_ref, du_ref, dz_ref, dxre_ref, dxim_ref, dyin_ref, dgv_ref, dpw_ref, dscale_ref, dd_ref) = outs
        i = pl.program_id(0)
        _riding_start(pushes, i == 0, push_refs)

        @pl.when(i == 0)
        def _():
            for ref in (dpw_ref, dscale_ref, dd_ref):
                ref[...] = jnp.zeros_like(ref)

        dymix = _dot_nt(dx1_ref[...], wout_ref[...])
        _, a_vjp = jax.vjp(_mix_a, mixed_ref[...], scale_ref[...], z_ref[:, :half])
        dmixed, dscale, dza = a_vjp(dymix[:, :half])
        _, b_vjp = jax.vjp(_mix_b, gv_ref[:, :half], gv_ref[:, half:], z_ref[:, half:])
        dval, dgate, dzb = b_vjp(dymix[:, half:])
        dz_ref[:, :half] = dza
        dz_ref[:, half:] = dzb
        dscale_ref[...] += dscale
        dgv = jnp.concatenate([dval, dgate], axis=1).astype(MM)
        dgv_ref[...] = dgv
        dyin = _dot_nt(dgv, wglu_ref[...])
        dd_ref[...] += jnp.sum(dyin * u_ref[...], axis=0, keepdims=True)
        du_ref[...] = d_ref[...] * dyin
        dyb = dyin.astype(MM)
        dyin_ref[...] = dyb
        sb, cb = ccre_ref.shape[1:]
        for b in range(SSM_BLOCKS):
            states, chans = slice(b * sb, (b + 1) * sb), slice(b * cb, (b + 1) * cb)
            dxre_ref[:, states] = _dot_nt(dyb[:, chans], ccre_ref[b])
            dxim_ref[:, states] = -_dot_nt(dyb[:, chans], ccim_ref[b])
        for g in range(groups):
            cols = slice(g * LANES, (g + 1) * LANES)
            dm = dmixed[:, cols].astype(MM)
            dpooled_ref[:, cols] = _dot_nt(dm, pw_ref[g])
            dpw_ref[g] += _dot_tn(pooled_ref[:, cols], dm)

        _riding_wait(pushes, i == nt - 1, push_refs)

    out = pl.pallas_call(
        body, name="even_mix_bwd", grid=(nt,),
        in_specs=[_rows(ts, half), _rows(ts, half, col=1), _rows(ts, d, col=1), _rows(ts, half), _rows(ts, d),
                  _rows(ts, d), _full(pw.shape), _full((1, half)), _full(cc_re.shape), _full(cc_im.shape),
                  _full((1, half)), _full((half, d)), _full((d, d))] + [ANY] * np_,
        out_specs=[_rows(ts, half), _rows(ts, half), _rows(ts, d), _rows(ts, n), _rows(ts, n), _rows(ts, half),
                   _rows(ts, d), _full(pw.shape), _full((1, half)), _full((1, half))] + [ANY] * np_,
        out_shape=[SDS((s, half), F32), SDS((s, half), F32), SDS((s, d), F32), SDS((s, n), F32), SDS((s, n), F32),
                   SDS((s, half), MM), SDS((s, d), MM), SDS(pw.shape, F32), SDS((1, half), F32),
                   SDS((1, half), F32)] + [p.out for p in pushes],
        scratch_shapes=_push_scratch(np_),
        compiler_params=_seq_params(56),
    )(pooled, proj, proj, mixed, gv, dx1, pw, scale, cc_re, cc_im, d_skip, w_glu, w_out,
      *[p.array for p in pushes])
    return out[:10], out[10:]


def _ssm_bu_bwd(g_re, g_im, du_skip, bb):
    s, n = g_re.shape
    nb, cb, two_nb = bb.shape
    sb = two_nb // 2
    cin = nb * cb
    ts = 512

    def body(gre_ref, gim_ref, du_ref, bb_ref, out_ref):
        for b in range(nb):
            states, chans = slice(b * sb, (b + 1) * sb), slice(b * cb, (b + 1) * cb)
            out_ref[:, chans] = (du_ref[:, chans] + _dot_nt(gre_ref[:, states], bb_ref[b, :, :sb])
                                 + _dot_nt(gim_ref[:, states], bb_ref[b, :, sb:]))

    return pl.pallas_call(
        body, name="ssm_bu_bwd", grid=(s // ts,),
        in_specs=[_rows(ts, n), _rows(ts, n), _rows(ts, cin), _full(bb.shape)],
        out_specs=_rows(ts, cin),
        out_shape=SDS((s, cin), F32),
        compiler_params=_seq_params(48),
    )(g_re, g_im, du_skip, bb)


def _even_in_bwd(x, du_pool, du_ssm, dz, dx1, g0, w_in):
    s, d = x.shape
    half = du_pool.shape[1]
    shards, _, wc = w_in.shape
    ts = 256

    def body(x_ref, dup_ref, dus_ref, dz_ref, dx1_ref, g_ref, w_ref, gx_ref, dg_ref):
        @pl.when(pl.program_id(0) == 0)
        def _():
            dg_ref[...] = jnp.zeros_like(dg_ref)

        dp = jnp.concatenate([dup_ref[...], dus_ref[...], dz_ref[...]], axis=1).astype(MM)
        dh = _dot_nt(dp[:, :wc], w_ref[0])
        for k in range(1, shards):
            dh = dh + _dot_nt(dp[:, k * wc:(k + 1) * wc], w_ref[k])
        _, rms_vjp = jax.vjp(_rms, x_ref[...], g_ref[...])
        dxa, dg = rms_vjp(dh)
        gx_ref[...] = dxa + dx1_ref[...]
        dg_ref[...] += dg

    return pl.pallas_call(
        body, name="even_in_bwd", grid=(s // ts,),
        in_specs=[_rows(ts, d), _rows(ts, half), _rows(ts, half), _rows(ts, d), _rows(ts, d), _full((1, d)),
                  _full(w_in.shape)],
        out_specs=[_rows(ts, d), _full((1, d))],
        out_shape=[SDS((s, d), F32), SDS((1, d), F32)],
        compiler_params=_seq_params(48),
    )(x, du_pool, du_ssm, dz, dx1, g0, w_in)


def _discretise(log_dt, ar, ai, br, bi):
    dt = jnp.exp(log_dt)
    mag = jnp.exp(ar * dt)
    ang = ai * dt
    abr = mag * jnp.cos(ang)
    abi = mag * jnp.sin(ang)
    den = ar * ar + ai * ai
    nr = abr - 1.0
    ni = abi
    kr = (nr * ar + ni * ai) / den
    ki = (ni * ar - nr * ai) / den
    bbr = kr[None] * br - ki[None] * bi
    bbi = kr[None] * bi + ki[None] * br
    return abr, abi, bbr, bbi


def _whole(n):
    return [pl.BlockSpec(memory_space=pltpu.VMEM)] * n


def _disc_fwd(log_dt, ar, ai, br, bi):
    def body(ld_ref, ar_ref, ai_ref, br_ref, bi_ref, abr_ref, abi_ref, bbr_ref, bbi_ref):
        out = _discretise(ld_ref[...], ar_ref[...], ai_ref[...], br_ref[...], bi_ref[...])
        for ref, val in zip((abr_ref, abi_ref, bbr_ref, bbi_ref), out):
            ref[...] = val

    return pl.pallas_call(
        body, name="ssm_discretise", in_specs=_whole(5), out_specs=_whole(4),
        out_shape=[SDS(ar.shape, F32), SDS(ar.shape, F32), SDS(br.shape, F32), SDS(br.shape, F32)],
    )(log_dt, ar, ai, br, bi)


def _disc_bwd(log_dt, ar, ai, br, bi, dabr, dabi, dbbr, dbbi):
    def body(ld_ref, ar_ref, ai_ref, br_ref, bi_ref, dabr_ref, dabi_ref, dbbr_ref, dbbi_ref,
             dld_ref, dar_ref, dai_ref, dbr_ref, dbi_ref):
        _, vjp = jax.vjp(_discretise, ld_ref[...], ar_ref[...], ai_ref[...], br_ref[...], bi_ref[...])
        grads = vjp((dabr_ref[...], dabi_ref[...], dbbr_ref[...], dbbi_ref[...]))
        for ref, val in zip((dld_ref, dar_ref, dai_ref, dbr_ref, dbi_ref), grads):
            ref[...] = val

    return pl.pallas_call(
        body, name="ssm_discretise_bwd", in_specs=_whole(9), out_specs=_whole(5),
        out_shape=[SDS(log_dt.shape, F32), SDS(ar.shape, F32), SDS(ar.shape, F32), SDS(br.shape, F32),
                   SDS(br.shape, F32)],
    )(log_dt, ar, ai, br, bi, dabr, dabi, dbbr, dbbi)


def _block_diag(t):
    g, a, b = t.shape
    per = g // SSM_BLOCKS
    t = t.reshape(SSM_BLOCKS, per, a, b)
    same = jnp.eye(per, dtype=bool)[None, :, None, :, None]
    return jnp.where(same, t[:, :, :, None, :], 0.0).reshape(SSM_BLOCKS, per * a, per * b)


def _diag_blocks(m, g):
    per = g // SSM_BLOCKS
    a, b = m.shape[1] // per, m.shape[2] // per
    d = jnp.diagonal(m.reshape(SSM_BLOCKS, per, a, per, b), axis1=1, axis2=3)
    return jnp.moveaxis(d, -1, 1).reshape(g, a, b)


def _adamw(name, parts, w, m, v):
    rows, cols = w.shape
    tr = ADAM_ROWS if rows % ADAM_ROWS == 0 else rows

    def body(p_ref, w_ref, m_ref, v_ref, g_ref, d_ref, nm_ref, nv_ref):
        g = p_ref[0]
        for dev in range(1, N_DEV):
            g = g + p_ref[dev]
        g_ref[...] = g
        nm = ADAM_B1 * m_ref[...] + (1.0 - ADAM_B1) * g
        nv = ADAM_B2 * v_ref[...] + (1.0 - ADAM_B2) * jnp.square(g)
        nm_ref[...] = nm
        nv_ref[...] = nv
        m_hat = nm / (1.0 - ADAM_B1 ** ADAM_STEP)
        v_hat = nv / (1.0 - ADAM_B2 ** ADAM_STEP)
        d_ref[...] = -ADAM_LR * (m_hat / (jnp.sqrt(v_hat) + ADAM_EPS) + ADAM_WD * w_ref[...])

    tile = _rows(tr, cols)
    out = SDS((rows, cols), F32)
    return pl.pallas_call(
        body, name=name, grid=(rows // tr,),
        in_specs=[pl.BlockSpec((N_DEV, tr, cols), lambda i: (0, i, 0)), tile, tile, tile],
        out_specs=[tile, tile, tile, tile],
        out_shape=[out, out, out, out],
        compiler_params=_seq_params(32),
    )(parts, w, m, v)


def _as_rows(flat):
    n = flat.shape[-1]
    padded = _round_up(n, PACK_TILE)
    if padded != n:
        flat = jnp.pad(flat, [(0, 0)] * (flat.ndim - 1) + [(0, padded - n)])
    return flat.reshape(flat.shape[:-1] + (padded // LANES, LANES))


def _columns_to_shards(full):
    r, c8 = full.shape
    return jnp.transpose(full.reshape(r, N_DEV, c8 // N_DEV), (1, 0, 2))


def _shards_to_columns(shards):
    n, r, c = shards.shape
    return jnp.transpose(shards, (1, 0, 2)).reshape(r, n * c)


def _pack_local(tensors):
    return jnp.concatenate([_as_rows(t.reshape(-1)) for t in tensors], axis=0)


def _unpack_local(packed, shapes):
    out, r = [], 0
    for shape in shapes:
        n = 1
        for dim in shape:
            n *= dim
        rows = _round_up(n, PACK_TILE) // LANES
        out.append(packed[r:r + rows].reshape(-1)[:n].reshape(shape))
        r += rows
    return out


def kernel(x, even_norm, even_w_in, pool_w, pool_scale, ssm_log_dt, ssm_a_re, ssm_a_im, ssm_b_re, ssm_b_im, ssm_c_re, ssm_c_im, ssm_d, ssm_w_glu, even_w_out, odd_norm, odd_w_in, conv_w, conv_b, conv_ln_g, conv_ln_b, odd_w_out, final_norm, loss_target, m_even_norm, m_even_w_in, m_pool_w, m_pool_scale, m_ssm_log_dt, m_ssm_a_re, m_ssm_a_im, m_ssm_b_re, m_ssm_b_im, m_ssm_c_re, m_ssm_c_im, m_ssm_d, m_ssm_w_glu, m_even_w_out, m_odd_norm, m_odd_w_in, m_conv_w, m_conv_b, m_conv_ln_g, m_conv_ln_b, m_odd_w_out, m_final_norm, v_even_norm, v_even_w_in, v_pool_w, v_pool_scale, v_ssm_log_dt, v_ssm_a_re, v_ssm_a_im, v_ssm_b_re, v_ssm_b_im, v_ssm_c_re, v_ssm_c_im, v_ssm_d, v_ssm_w_glu, v_even_w_out, v_odd_norm, v_odd_w_in, v_conv_w, v_conv_b, v_conv_ln_g, v_conv_ln_b, v_odd_w_out, v_final_norm):
    given = dict(locals())
    xs = x[0]
    tgt = loss_target[0]
    d_model = xs.shape[1]

    def local(prefix, name):
        t = given[prefix + name]
        return t if name == 'final_norm' else t[0]

    def small_block(prefix):
        parts = [jnp.pad(local(prefix, 'conv_w'), ((0, SMALL_AT['odd_norm'] - CONV_KERNEL), (0, 0)))]
        for n in SMALL_VECTORS:
            parts.append(jnp.pad(local(prefix, n).reshape(1, LANES), ((0, SUBLANES - 1), (0, 0))))
        return jnp.concatenate(parts, axis=0)

    def as_tile(row):
        return jnp.pad(row, ((0, SUBLANES - row.shape[0]), (0, 0)))

    (w_in_e,) = _push_alone("gather_first", [_Push(local('', 'even_w_in').astype(MM), False)])
    later = [_Push(local('', 'ssm_w_glu').astype(MM), False), _Push(local('', 'even_w_out').astype(MM), False),
             _Push(local('', 'odd_w_in').astype(MM), False), _Push(local('', 'odd_w_out').astype(MM), False),
             _Push(small_block(''), False)]

    a_re, a_im = local('', 'ssm_a_re'), local('', 'ssm_a_im')
    groups, state = a_re.shape
    log_dt = local('', 'ssm_log_dt').reshape(groups, 1)
    b_re_t = jnp.transpose(local('', 'ssm_b_re'), (2, 0, 1))
    b_im_t = jnp.transpose(local('', 'ssm_b_im'), (2, 0, 1))
    abr, abi, bbr, bbi = _disc_fwd(log_dt, a_re, a_im, b_re_t, b_im_t)
    a_re_row = abr.reshape(1, groups * state)
    a_im_row = abi.reshape(1, groups * state)
    bb = jnp.concatenate([_block_diag(jnp.transpose(bbr, (1, 0, 2))), _block_diag(jnp.transpose(bbi, (1, 0, 2)))],
                         axis=2).astype(MM)
    cc_re = _block_diag(jnp.transpose(local('', 'ssm_c_re'), (0, 2, 1))).astype(MM)
    cc_im = _block_diag(jnp.transpose(local('', 'ssm_c_im'), (0, 2, 1))).astype(MM)
    pw = local('', 'pool_w').astype(MM)
    g0 = local('', 'even_norm').reshape(1, d_model)
    fg = local('', 'final_norm').reshape(1, d_model)
    scale = local('', 'pool_scale').reshape(1, -1)
    d_skip = local('', 'ssm_d').reshape(1, -1)

    proj0, h0 = _even_in(xs, g0, w_in_e)
    pooled = _pool_fwd(proj0)
    bu_re, bu_im = _ssm_bu(proj0, bb)
    x_re, x_im, (wg3, wo_e3, wi_o3, wo_o3, small8) = _scan_fwd(a_re_row, a_im_row, bu_re, bu_im, later)
    w_glu = _shards_to_columns(wg3)
    w_out_e = wo_e3.reshape(d_model, d_model)
    w_in_o = _shards_to_columns(wi_o3)
    w_out_o = wo_o3.reshape(d_model, d_model)
    small = _shards_to_columns(small8)
    cw = small[:CONV_KERNEL]
    g1, cb, lg, lb = (small[SMALL_AT[n]:SMALL_AT[n] + 1] for n in SMALL_VECTORS)
    x1, ymix, mixed, yin, gv = _even_mix(pooled, proj0, x_re, x_im, xs, pw, scale, cc_re, cc_im, d_skip, w_glu,
                                         w_out_e)
    proj1, glu, h1 = _odd_in(x1, g1, w_in_o)
    conv = _conv_fwd(glu, cw)

    half = pooled.shape[1]
    n_state = groups * state
    rows_per = d_model // N_DEV
    dc, dz1, dx2, y1, loss_tile, dcb, dlg, dlb, dfg = _odd_out(conv, proj1, x1, tgt, cb, lg, lb, w_out_o, fg)
    dglu, dcw = _conv_bwd(dc, glu, cw)
    dx1, dproj1, dg1 = _odd_in_bwd(x1, proj1, dglu, dz1, dx2, g1, w_in_o)
    g_odd_out = _mm_tn("dw_odd_out", y1, dx2, d_model, d_model).reshape(N_DEV, rows_per, d_model)
    g_odd_in = _columns_to_shards(_mm_tn("dw_odd_in", h1, dproj1, d_model, 3 * d_model))
    g_even_out = _mm_tn("dw_even_out", ymix, dx1, d_model, d_model).reshape(N_DEV, rows_per, d_model)
    mix_grads, (r_odd_in,) = _even_mix_bwd(pooled, proj0, mixed, gv, dx1, pw, scale, cc_re, cc_im, d_skip, w_glu,
                                           w_out_e, [_Push(g_odd_in, True)])
    dpooled, du_skip, dz0, dx_re, dx_im, dyin, dgv, dpw, dscale, dd = mix_grads
    g_glu = _columns_to_shards(_mm_tn("dw_glu", yin, dgv, half, d_model))
    sblk, cblk = n_state // SSM_BLOCKS, half // SSM_BLOCKS
    dcc_re = _mm_tn_blocks("dw_c_re", x_re, dyin, sblk, cblk, SSM_BLOCKS)
    dcc_im = _mm_tn_blocks("dw_c_im", x_im, dyin, sblk, cblk, SSM_BLOCKS, sign=-1.0)
    dc_re = jnp.transpose(_diag_blocks(dcc_re, groups), (0, 2, 1))
    dc_im = jnp.transpose(_diag_blocks(dcc_im, groups), (0, 2, 1))
    g_small = _columns_to_shards(jnp.concatenate(
        [dcw, as_tile(dg1), as_tile(dcb), as_tile(dlg), as_tile(dlb)], axis=0))
    early = {'pool_w': dpw, 'pool_scale': dscale, 'ssm_c_re': dc_re, 'ssm_c_im': dc_im, 'ssm_d': dd,
             'final_norm': dfg}
    g_early = _pack_local([early[n] for n in EARLY_REPLICATED])
    g_re, g_im, dabr, dabi, (r_odd_out, r_even_out, r_glu, r_small, r_early) = _scan_bwd(
        a_re_row, a_im_row, dx_re, dx_im, x_re, x_im,
        [_Push(g_odd_out, True), _Push(g_even_out, True), _Push(g_glu, True), _Push(g_small, True),
         _Push(g_early, False)])
    du_ssm = _ssm_bu_bwd(g_re, g_im, du_skip, bb)
    du_pool = _pool_bwd(dpooled)
    grad_x, dg0 = _even_in_bwd(xs, du_pool, du_ssm, dz0, dx1, g0, w_in_e)
    wc = w_in_e.shape[2]
    g_even_in = jnp.concatenate(
        [_mm_tn_blocks("dw_even_in_pool", h0, du_pool, d_model, wc, half // wc, a_step=0),
         _mm_tn_blocks("dw_even_in_ssm", h0, du_ssm, d_model, wc, half // wc, a_step=0),
         _mm_tn_blocks("dw_even_in_gate", h0, dz0, d_model, wc, d_model // wc, a_step=0)], axis=0)
    dbb_re = _mm_tn_blocks("dw_b_re", proj0, g_re, cblk, sblk, SSM_BLOCKS, a_col=SSM_BLOCKS)
    dbb_im = _mm_tn_blocks("dw_b_im", proj0, g_im, cblk, sblk, SSM_BLOCKS, a_col=SSM_BLOCKS)
    dbbr = jnp.transpose(_diag_blocks(dbb_re, groups), (1, 0, 2))
    dbbi = jnp.transpose(_diag_blocks(dbb_im, groups), (1, 0, 2))
    dld, dar, dai, dbr, dbi = _disc_bwd(log_dt, a_re, a_im, b_re_t, b_im_t,
                                        dabr.reshape(groups, state), dabi.reshape(groups, state), dbbr, dbbi)
    late = {'even_norm': dg0, 'ssm_log_dt': dld, 'ssm_a_re': dar, 'ssm_a_im': dai,
            'ssm_b_re': jnp.transpose(dbr, (1, 2, 0)), 'ssm_b_im': jnp.transpose(dbi, (1, 2, 0))}
    g_late = _pack_local([late[n] for n in LATE_REPLICATED])
    r_even_in, r_late = _push_alone("exchange_tail", [_Push(g_even_in, True), _Push(g_late, False)])

    results = {}
    for n, parts in (('even_w_in', r_even_in), ('ssm_w_glu', r_glu), ('even_w_out', r_even_out),
                     ('odd_w_in', r_odd_in), ('odd_w_out', r_odd_out)):
        results[n] = _adamw("adamw_" + n, parts, local('', n), local('m_', n), local('v_', n))
    small_out = _adamw("adamw_small", r_small, small_block(''), small_block('m_'), small_block('v_'))
    results['conv_w'] = [o[:CONV_KERNEL] for o in small_out]
    for n in SMALL_VECTORS:
        results[n] = [o[SMALL_AT[n]] for o in small_out]
    for names, parts in ((EARLY_REPLICATED, r_early), (LATE_REPLICATED, r_late)):
        packed = [_pack_local([local(p, n) for n in names]) for p in ('', 'm_', 'v_')]
        out = _adamw("adamw_" + names[0], parts, *packed)
        unpacked = [_unpack_local(o, [given[n].shape for n in names]) for o in out]
        for k, n in enumerate(names):
            results[n] = [u[k] for u in unpacked]

    loss = lax.psum(loss_tile[0, 0], ("x", "y", "c"))
    outs = [loss, grad_x[None]]
    for kind in range(4):
        outs.extend(results[n][kind].reshape(given[n].shape) for n in WEIGHTS)
    return tuple(outs)
```

```python
import functools

import jax
import jax.numpy as jnp
from jax import lax
from jax.experimental import pallas as pl
from jax.experimental.pallas import tpu as pltpu

F32 = jnp.float32
MM = jnp.bfloat16
SDS = jax.ShapeDtypeStruct

RMS_EPS = 1e-6
LN_EPS = 1e-5
POOL_WINDOWS = (2, 4, 8, 16)
POOL_HALO = 16
CONV_KERNEL = 31
CONV_HALO = 32
N_DEV = 8
LANES = 128
SUBLANES = 8
PACK_TILE = SUBLANES * LANES
ADAM_ROWS = 128
MIB = 1024 * 1024

ADAM_LR = 0.001
ADAM_B1 = 0.9
ADAM_B2 = 0.999
ADAM_EPS = 1e-08
ADAM_WD = 0.01
ADAM_STEP = 10

WEIGHTS = ['even_norm', 'even_w_in', 'pool_w', 'pool_scale', 'ssm_log_dt', 'ssm_a_re', 'ssm_a_im', 'ssm_b_re',
           'ssm_b_im', 'ssm_c_re', 'ssm_c_im', 'ssm_d', 'ssm_w_glu', 'even_w_out', 'odd_norm', 'odd_w_in', 'conv_w',
           'conv_b', 'conv_ln_g', 'conv_ln_b', 'odd_w_out', 'final_norm']
SMALL_VECTORS = ('odd_norm', 'conv_b', 'conv_ln_g', 'conv_ln_b')
SMALL_AT = {'odd_norm': 32, 'conv_b': 40, 'conv_ln_g': 48, 'conv_ln_b': 56}
EARLY_REPLICATED = ['pool_w', 'pool_scale', 'ssm_c_re', 'ssm_c_im', 'ssm_d', 'final_norm']
MID_REPLICATED = ['even_norm']
LATE_REPLICATED = ['ssm_log_dt', 'ssm_a_re', 'ssm_a_im', 'ssm_b_re', 'ssm_b_im']


def _round_up(n, m):
    return (n + m - 1) // m * m


def _sigmoid(x):
    return jax.nn.sigmoid(x)


def _silu(x):
    return x * jax.nn.sigmoid(x)


def _rms(x, g):
    return x * lax.rsqrt(jnp.mean(x * x, axis=-1, keepdims=True) + RMS_EPS) * g


def _dot(a, b):
    return jnp.dot(a.astype(MM), b.astype(MM), preferred_element_type=F32)


def _dot_nt(a, b):
    return lax.dot_general(a.astype(MM), b.astype(MM), (((1,), (1,)), ((), ())), preferred_element_type=F32)


def _dot_tn(a, b):
    return lax.dot_general(a.astype(MM), b.astype(MM), (((0,), (0,)), ((), ())), preferred_element_type=F32)


def _rows(ts, width, col=0):
    return pl.BlockSpec((ts, width), lambda i: (i, col))


def _rows_rev(ts, width, n, col=0):
    return pl.BlockSpec((ts, width), lambda i: (n - 1 - i, col))


def _full(shape):
    zeros = (0,) * len(shape)
    return pl.BlockSpec(shape, lambda i: zeros)


def _seq_params(vmem_mib=48, dims=1):
    return pltpu.CompilerParams(dimension_semantics=("arbitrary",) * dims, vmem_limit_bytes=vmem_mib * MIB)


def _mesh_position():
    x, y, c = lax.axis_index("x"), lax.axis_index("y"), lax.axis_index("c")
    return x, y, c


def _peer(pos, relation):
    x, y, c = pos
    px = 1 - x if relation & 4 else x
    py = 1 - y if relation & 2 else y
    pc = 1 - c if relation & 1 else c
    return px, py, pc


ANY = pl.BlockSpec(memory_space=pl.ANY)


class _Push:
    def __init__(self, array, scatter):
        self.array = array
        self.scatter = scatter
        self.out = SDS(array.shape if scatter else (N_DEV,) + array.shape, array.dtype)


def _push_scratch(n):
    return [pltpu.SemaphoreType.DMA((n * (N_DEV - 1),)), pltpu.SemaphoreType.DMA((n * (N_DEV - 1),)),
            pltpu.SemaphoreType.DMA((n,))]


def _push_copies(pushes, srcs, dsts, send_sems, recv_sems, local_sems):
    pos = _mesh_position()
    me = 4 * pos[0] + 2 * pos[1] + pos[2]
    copies = []
    for a, (push, src, dst) in enumerate(zip(pushes, srcs, dsts)):
        copies.append(pltpu.make_async_copy(src.at[me] if push.scatter else src, dst.at[me], local_sems.at[a]))
        for relation in range(1, N_DEV):
            peer = _peer(pos, relation)
            peer_id = 4 * peer[0] + 2 * peer[1] + peer[2]
            k = a * (N_DEV - 1) + relation - 1
            copies.append(pltpu.make_async_remote_copy(
                src_ref=src.at[peer_id] if push.scatter else src, dst_ref=dst.at[me],
                send_sem=send_sems.at[k], recv_sem=recv_sems.at[k], device_id=peer,
                device_id_type=pl.DeviceIdType.MESH))
    return copies


def _push_alone(name, pushes):
    n = len(pushes)

    def body(*refs):
        copies = _push_copies(pushes, refs[:n], refs[n:2 * n], *refs[2 * n:])
        for cp in copies:
            cp.start()
        for cp in copies:
            cp.wait()

    return pl.pallas_call(
        body, name=name, out_shape=[p.out for p in pushes], in_specs=[ANY] * n, out_specs=[ANY] * n,
        scratch_shapes=_push_scratch(n),
    )(*[p.array for p in pushes])


def _riding_start(pushes, first, refs):
    n = len(pushes)

    @pl.when(first)
    def _():
        for cp in _push_copies(pushes, refs[:n], refs[n:2 * n], *refs[2 * n:]):
            cp.start()


def _riding_wait(pushes, last, refs):
    n = len(pushes)

    @pl.when(last)
    def _():
        for cp in _push_copies(pushes, refs[:n], refs[n:2 * n], *refs[2 * n:]):
            cp.wait()


def _split_refs(refs, n_in, n_out, n_push):
    ins = refs[:n_in]
    srcs = refs[n_in:n_in + n_push]
    o0 = n_in + n_push
    outs = refs[o0:o0 + n_out]
    dsts = refs[o0 + n_out:o0 + n_out + n_push]
    s0 = o0 + n_out + n_push
    sems = refs[s0:s0 + 3]
    scratch = refs[s0 + 3:]
    return ins, outs, scratch, tuple(srcs) + tuple(dsts) + tuple(sems)


def _mm_tn(name, a, b, m, n, a_col=0, b_col=0, sign=1.0, ts=512):
    s = a.shape[0]
    tn = min(n, 512)
    nj = n // tn

    def body(a_ref, b_ref, o_ref):
        @pl.when(pl.program_id(1) == 0)
        def _():
            o_ref[...] = jnp.zeros_like(o_ref)

        o_ref[...] += sign * _dot_tn(a_ref[...], b_ref[...])

    return pl.pallas_call(
        body, name=name, grid=(nj, s // ts),
        in_specs=[pl.BlockSpec((ts, m), lambda j, k: (k, a_col)),
                  pl.BlockSpec((ts, tn), lambda j, k: (k, b_col * nj + j))],
        out_specs=pl.BlockSpec((m, tn), lambda j, k: (0, j)),
        out_shape=SDS((m, n), F32),
        compiler_params=_seq_params(48, dims=2),
    )(a, b)


def _mm_tn_blocks(name, a, b, m, n, nb, a_col=0, a_step=1, b_col=0, sign=1.0, ts=512):
    s = a.shape[0]

    def body(a_ref, b_ref, o_ref):
        @pl.when(pl.program_id(1) == 0)
        def _():
            o_ref[...] = jnp.zeros_like(o_ref)

        o_ref[...] += sign * _dot_tn(a_ref[...], b_ref[...])

    return pl.pallas_call(
        body, name=name, grid=(nb, s // ts),
        in_specs=[pl.BlockSpec((ts, m), lambda j, k: (k, a_col + a_step * j)),
                  pl.BlockSpec((ts, n), lambda j, k: (k, b_col + j))],
        out_specs=pl.BlockSpec((None, m, n), lambda j, k: (j, 0, 0)),
        out_shape=SDS((nb, m, n), F32),
        compiler_params=_seq_params(48, dims=2),
    )(a, b)


def _dw_bbar(proj, g_re, g_im, cb, sb, pushes, ts=2048):
    s = proj.shape[0]
    nb, nk = SSM_BLOCKS, s // ts
    np_ = len(pushes)

    def body(*refs):
        ins, outs, _, push_refs = _split_refs(refs, 3, 2, np_)
        u_ref, gre_ref, gim_ref = ins
        ore_ref, oim_ref = outs
        j, k = pl.program_id(0), pl.program_id(1)
        _riding_start(pushes, jnp.logical_and(j == 0, k == 0), push_refs)

        @pl.when(k == 0)
        def _():
            ore_ref[...] = jnp.zeros_like(ore_ref)
            oim_ref[...] = jnp.zeros_like(oim_ref)

        u = u_ref[...].astype(MM)
        ore_ref[...] += _dot_tn(u, gre_ref[...])
        oim_ref[...] += _dot_tn(u, gim_ref[...])
        _riding_wait(pushes, jnp.logical_and(j == nb - 1, k == nk - 1), push_refs)

    wide = pl.BlockSpec((ts, sb), lambda j, k: (k, j))
    out = pl.BlockSpec((None, cb, sb), lambda j, k: (j, 0, 0))
    res = pl.pallas_call(
        body, name="dw_bbar", grid=(nb, nk),
        in_specs=[pl.BlockSpec((ts, cb), lambda j, k: (k, nb + j)), wide, wide] + [ANY] * np_,
        out_specs=[out, out] + [ANY] * np_,
        out_shape=[SDS((nb, cb, sb), F32), SDS((nb, cb, sb), F32)] + [p.out for p in pushes],
        scratch_shapes=_push_scratch(np_),
        compiler_params=_seq_params(48, dims=2),
    )(proj, g_re, g_im, *[p.array for p in pushes])
    return res[0], res[1], res[2:]


def _even_in(x, g0, w_in):
    s, d = x.shape
    shards, _, wc = w_in.shape
    e = shards * wc
    ts = 512

    def body(x_ref, g_ref, w_ref, proj_ref, h_ref):
        hb = _rms(x_ref[...], g_ref[...]).astype(MM)
        h_ref[...] = hb
        for k in range(shards):
            proj_ref[:, k * wc:(k + 1) * wc] = jnp.dot(hb, w_ref[k], preferred_element_type=F32)

    return pl.pallas_call(
        body, name="even_in", grid=(s // ts,),
        in_specs=[_rows(ts, d), _full((1, d)), _full(w_in.shape)],
        out_specs=[_rows(ts, e), _rows(ts, d)],
        out_shape=[SDS((s, e), F32), SDS((s, d), MM)],
        compiler_params=_seq_params(48),
    )(x, g0, w_in)


def _pool_counts(t0, ts, w):
    pos = (t0 + lax.broadcasted_iota(jnp.int32, (ts, LANES), 0) + 1).astype(F32)
    return jnp.minimum(pos, float(w))


def _pool_fwd(proj):
    s = proj.shape[0]
    width = LANES * len(POOL_WINDOWS)
    ts = 512
    per = ts // POOL_HALO

    def body(prev_ref, u_ref, out_ref, ext):
        i = pl.program_id(0)
        ext[0:POOL_HALO, :] = jnp.where(i == 0, 0.0, prev_ref[...])
        ext[POOL_HALO:, :] = u_ref[...]
        for g, w in enumerate(POOL_WINDOWS):
            cols = slice(g * LANES, (g + 1) * LANES)
            tok = ext[pl.ds(POOL_HALO, ts), cols]
            acc = tok
            for k in range(1, w):
                acc = acc + ext[pl.ds(POOL_HALO - k, ts), cols]
            out_ref[:, cols] = acc / _pool_counts(i * ts, ts, w) - tok

    return pl.pallas_call(
        body, name="pool_fwd", grid=(s // ts,),
        in_specs=[pl.BlockSpec((POOL_HALO, width), lambda i: (jnp.maximum(i * per - 1, 0), 0)),
                  _rows(ts, width)],
        out_specs=_rows(ts, width),
        out_shape=SDS((s, width), F32),
        scratch_shapes=[pltpu.VMEM((ts + POOL_HALO, width), F32)],
        compiler_params=_seq_params(32),
    )(proj, proj)


def _pool_bwd(dp):
    s, width = dp.shape
    ts = 512
    per = ts // POOL_HALO
    n = s // ts

    def body(dp_ref, next_ref, out_ref, ext):
        i = pl.program_id(0)
        nxt = jnp.where(i == n - 1, 0.0, next_ref[...])
        for g, w in enumerate(POOL_WINDOWS):
            cols = slice(g * LANES, (g + 1) * LANES)
            cur = dp_ref[:, cols]
            ext[0:ts, cols] = cur / _pool_counts(i * ts, ts, w)
            ext[ts:, cols] = nxt[:, cols] / _pool_counts((i + 1) * ts, POOL_HALO, w)
            acc = -cur
            for k in range(w):
                acc = acc + ext[pl.ds(k, ts), cols]
            out_ref[:, cols] = acc

    return pl.pallas_call(
        body, name="pool_bwd", grid=(n,),
        in_specs=[_rows(ts, width),
                  pl.BlockSpec((POOL_HALO, width), lambda i: (jnp.minimum((i + 1) * per, s // POOL_HALO - 1), 0))],
        out_specs=_rows(ts, width),
        out_shape=SDS((s, width), F32),
        scratch_shapes=[pltpu.VMEM((ts + POOL_HALO, width), F32)],
        compiler_params=_seq_params(32),
    )(dp, dp)


SSM_BLOCKS = 4


def _ssm_bu(proj, bb):
    s = proj.shape[0]
    nb, cb, two_nb = bb.shape
    sb = two_nb // 2
    cin, n = nb * cb, nb * sb
    ts = 512

    def body(u_ref, bb_ref, re_ref, im_ref):
        for b in range(nb):
            bu = _dot(u_ref[:, b * cb:(b + 1) * cb], bb_ref[b])
            re_ref[:, b * sb:(b + 1) * sb] = bu[:, :sb]
            im_ref[:, b * sb:(b + 1) * sb] = bu[:, sb:]

    return pl.pallas_call(
        body, name="ssm_bu", grid=(s // ts,),
        in_specs=[_rows(ts, cin, col=1), _full(bb.shape)],
        out_specs=[_rows(ts, n), _rows(ts, n)],
        out_shape=[SDS((s, n), F32), SDS((s, n), F32)],
        compiler_params=_seq_params(48),
    )(proj, bb)


SCAN_LANES = 256


def _cmul(a, b):
    return a[0] * b[0] - a[1] * b[1], a[0] * b[1] + a[1] * b[0]


SCAN_STEPS = (1, 2, 4)


def _fill_scan_tables(tab, ar_row, ai_row, reverse):
    shape = (SUBLANES, ar_row.shape[1])
    a1 = (jnp.broadcast_to(ar_row, shape), jnp.broadcast_to(ai_row, shape))
    a2 = _cmul(a1, a1)
    a4 = _cmul(a2, a2)
    a8 = _cmul(a4, a4)
    row = lax.broadcasted_iota(jnp.int32, shape, 0)
    for idx, (k, q) in enumerate(zip(SCAN_STEPS, (a1, a2, a4))):
        keep = (row < SUBLANES - k) if reverse else (row >= k)
        tab[2 * idx] = jnp.where(keep, q[0], 0.0)
        tab[2 * idx + 1] = jnp.where(keep, q[1], 0.0)
    expo = (SUBLANES - row) if reverse else (row + 1)
    pr, pi = jnp.ones(shape, F32), jnp.zeros(shape, F32)
    for bit, q in enumerate((a1, a2, a4, a8)):
        take = ((expo >> bit) & 1) == 1
        nr, ni = _cmul((pr, pi), q)
        pr, pi = jnp.where(take, nr, pr), jnp.where(take, ni, pi)
    tab[2 * len(SCAN_STEPS)] = pr
    tab[2 * len(SCAN_STEPS) + 1] = pi


def _group_scan(v, tab, cols, carry, reverse):
    vr, vi = v
    for idx, k in enumerate(SCAN_STEPS):
        shift = SUBLANES - k if reverse else k
        tr, ti = _cmul((tab[2 * idx, :, cols], tab[2 * idx + 1, :, cols]),
                       (pltpu.roll(vr, shift, 0), pltpu.roll(vi, shift, 0)))
        vr, vi = vr + tr, vi + ti
    last = 2 * len(SCAN_STEPS)
    tr, ti = _cmul((tab[last, :, cols], tab[last + 1, :, cols]), carry)
    return vr + tr, vi + ti


def _scan_fwd(a_re, a_im, bu_re, bu_im, pushes):
    s, n = bu_re.shape
    ts = 512
    nt = s // ts
    groups = ts // SUBLANES
    np_ = len(pushes)

    def body(*refs):
        ins, outs, scratch, push_refs = _split_refs(refs, 4, 2, np_)
        ar_ref, ai_ref, bre_ref, bim_ref = ins
        xre_ref, xim_ref = outs
        tab, cre, cim = scratch
        i = pl.program_id(0)
        _riding_start(pushes, i == 0, push_refs)

        @pl.when(i == 0)
        def _():
            _fill_scan_tables(tab, ar_ref[...], ai_ref[...], False)
            cre[...] = jnp.zeros_like(cre)
            cim[...] = jnp.zeros_like(cim)

        def group(gi, carry):
            r0 = pl.multiple_of(gi * SUBLANES, SUBLANES)
            for c in range(n // SCAN_LANES):
                cols = slice(c * SCAN_LANES, (c + 1) * SCAN_LANES)
                v = (bre_ref[pl.ds(r0, SUBLANES), cols], bim_ref[pl.ds(r0, SUBLANES), cols])
                vr, vi = _group_scan(v, tab, cols, (cre[:, cols], cim[:, cols]), False)
                xre_ref[pl.ds(r0, SUBLANES), cols] = vr
                xim_ref[pl.ds(r0, SUBLANES), cols] = vi
                cre[:, cols] = jnp.broadcast_to(vr[SUBLANES - 1:SUBLANES, :], vr.shape)
                cim[:, cols] = jnp.broadcast_to(vi[SUBLANES - 1:SUBLANES, :], vi.shape)
            return carry

        lax.fori_loop(0, groups, group, 0, unroll=2)
        _riding_wait(pushes, i == nt - 1, push_refs)

    out = pl.pallas_call(
        body, name="ssm_scan", grid=(nt,),
        in_specs=[_full((1, n)), _full((1, n)), _rows(ts, n), _rows(ts, n)] + [ANY] * np_,
        out_specs=[_rows(ts, n), _rows(ts, n)] + [ANY] * np_,
        out_shape=[SDS((s, n), F32), SDS((s, n), F32)] + [p.out for p in pushes],
        scratch_shapes=_push_scratch(np_) + [pltpu.VMEM((2 * len(SCAN_STEPS) + 2, SUBLANES, n), F32),
                                             pltpu.VMEM((SUBLANES, n), F32), pltpu.VMEM((SUBLANES, n), F32)],
        compiler_params=_seq_params(48),
    )(a_re, a_im, bu_re, bu_im, *[p.array for p in pushes])
    return out[0], out[1], out[2:]


def _scan_bwd(a_re, a_im, dx_re, dx_im, x_re, x_im, pushes):
    s, n = dx_re.shape
    ts = 256
    nt = s // ts
    groups = ts // SUBLANES
    np_ = len(pushes)

    def body(*refs):
        ins, outs, scratch, push_refs = _split_refs(refs, 6, 4, np_)
        ar_ref, ai_ref, dre_ref, dim_ref, xre_ref, xim_ref = ins
        gre_ref, gim_ref, dar_ref, dai_ref = outs
        tab, cre, cim, accr, acci = scratch
        i = pl.program_id(0)
        _riding_start(pushes, i == 0, push_refs)

        @pl.when(i == 0)
        def _():
            _fill_scan_tables(tab, ar_ref[...], -ai_ref[...], True)
            for ref in (cre, cim, accr, acci):
                ref[...] = jnp.zeros_like(ref)

        inner = lax.broadcasted_iota(jnp.int32, (SUBLANES, SCAN_LANES), 0) < SUBLANES - 1

        def group(k, carry):
            r0 = pl.multiple_of((groups - 1 - k) * SUBLANES, SUBLANES)
            for c in range(n // SCAN_LANES):
                cols = slice(c * SCAN_LANES, (c + 1) * SCAN_LANES)
                after = (cre[:, cols], cim[:, cols])
                v = (dre_ref[pl.ds(r0, SUBLANES), cols], dim_ref[pl.ds(r0, SUBLANES), cols])
                vr, vi = _group_scan(v, tab, cols, after, True)
                gre_ref[pl.ds(r0, SUBLANES), cols] = vr
                gim_ref[pl.ds(r0, SUBLANES), cols] = vi
                nr = jnp.where(inner, pltpu.roll(vr, SUBLANES - 1, 0), after[0])
                ni = jnp.where(inner, pltpu.roll(vi, SUBLANES - 1, 0), after[1])
                xr, xi = xre_ref[pl.ds(r0, SUBLANES), cols], xim_ref[pl.ds(r0, SUBLANES), cols]
                accr[:, cols] += nr * xr + ni * xi
                acci[:, cols] += ni * xr - nr * xi
                cre[:, cols] = jnp.broadcast_to(vr[0:1, :], vr.shape)
                cim[:, cols] = jnp.broadcast_to(vi[0:1, :], vi.shape)
            return carry

        lax.fori_loop(0, groups, group, 0, unroll=2)

        @pl.when(i == nt - 1)
        def _():
            dar_ref[...] = jnp.sum(accr[...], axis=0, keepdims=True)
            dai_ref[...] = jnp.sum(acci[...], axis=0, keepdims=True)

        _riding_wait(pushes, i == nt - 1, push_refs)

    small = pltpu.VMEM((SUBLANES, n), F32)
    out = pl.pallas_call(
        body, name="ssm_scan_bwd", grid=(nt,),
        in_specs=[_full((1, n)), _full((1, n))] + [_rows_rev(ts, n, nt)] * 4 + [ANY] * np_,
        out_specs=[_rows_rev(ts, n, nt), _rows_rev(ts, n, nt), _full((1, n)), _full((1, n))] + [ANY] * np_,
        out_shape=[SDS((s, n), F32), SDS((s, n), F32), SDS((1, n), F32), SDS((1, n), F32)]
        + [p.out for p in pushes],
        scratch_shapes=_push_scratch(np_) + [pltpu.VMEM((2 * len(SCAN_STEPS) + 2, SUBLANES, n), F32),
                                             small, small, small, small],
        compiler_params=_seq_params(48),
    )(a_re, a_im, dx_re, dx_im, x_re, x_im, *[p.array for p in pushes])
    return out[0], out[1], out[2], out[3], out[4:]


def _mix_a(mixed, scale, za):
    return mixed * scale * _silu(za)


def _mix_b(val, gate, zb):
    return val * _sigmoid(gate) * _silu(zb)


def _even_mix(pooled, proj, x_re, x_im, x, pw, scale, cc_re, cc_im, d_skip, w_glu, w_out):
    s, d = x.shape
    half = pooled.shape[1]
    n = x_re.shape[1]
    ts = 256

    def body(pooled_ref, u_ref, z_ref, xre_ref, xim_ref, x_ref, pw_ref, scale_ref, ccre_ref, ccim_ref, d_ref,
             wglu_ref, wout_ref, x1_ref, ymix_ref, mixed_ref, yin_ref, gv_ref):
        for g in range(len(POOL_WINDOWS)):
            cols = slice(g * LANES, (g + 1) * LANES)
            mixed_ref[:, cols] = _dot(pooled_ref[:, cols], pw_ref[g])
        sb, cb = ccre_ref.shape[1:]
        for b in range(SSM_BLOCKS):
            states, chans = slice(b * sb, (b + 1) * sb), slice(b * cb, (b + 1) * cb)
            yin_ref[:, chans] = (_dot(xre_ref[:, states], ccre_ref[b]) - _dot(xim_ref[:, states], ccim_ref[b])
                                 + d_ref[:, chans] * u_ref[:, chans])
        yin = yin_ref[...]
        gv = _dot(yin, wglu_ref[...])
        gv_ref[...] = gv
        ya = _mix_a(mixed_ref[...], scale_ref[...], z_ref[:, :half])
        yb = _mix_b(gv[:, :half], gv[:, half:], z_ref[:, half:])
        ymix = jnp.concatenate([ya, yb], axis=1).astype(MM)
        ymix_ref[...] = ymix
        x1_ref[...] = x_ref[...] + jnp.dot(ymix, wout_ref[...], preferred_element_type=F32)

    return pl.pallas_call(
        body, name="even_mix", grid=(s // ts,),
        in_specs=[_rows(ts, half), _rows(ts, half, col=1), _rows(ts, d, col=1), _rows(ts, n), _rows(ts, n),
                  _rows(ts, d), _full(pw.shape), _full((1, half)), _full(cc_re.shape), _full(cc_im.shape),
                  _full((1, half)), _full((half, d)), _full((d, d))],
        out_specs=[_rows(ts, d), _rows(ts, d), _rows(ts, half), _rows(ts, half), _rows(ts, d)],
        out_shape=[SDS((s, d), F32), SDS((s, d), MM), SDS((s, half), F32), SDS((s, half), F32), SDS((s, d), F32)],
        compiler_params=_seq_params(56),
    )(pooled, proj, proj, x_re, x_im, x, pw, scale, cc_re, cc_im, d_skip, w_glu, w_out)


def _glu(val, gt):
    return val * _sigmoid(gt)


def _odd_in(x1, g1, w_in):
    s, d = x1.shape
    e = w_in.shape[1]
    ts = 256

    def body(x_ref, g_ref, w_ref, proj_ref, glu_ref, h_ref):
        hb = _rms(x_ref[...], g_ref[...]).astype(MM)
        h_ref[...] = hb
        proj = jnp.dot(hb, w_ref[...], preferred_element_type=F32)
        proj_ref[...] = proj
        glu_ref[...] = _glu(proj[:, :d], proj[:, d:2 * d])

    return pl.pallas_call(
        body, name="odd_in", grid=(s // ts,),
        in_specs=[_rows(ts, d), _full((1, d)), _full((d, e))],
        out_specs=[_rows(ts, e), _rows(ts, d), _rows(ts, d)],
        out_shape=[SDS((s, e), F32), SDS((s, d), F32), SDS((s, d), MM)],
        compiler_params=_seq_params(48),
    )(x1, g1, w_in)


CONV_ROWS = 32
CONV_COLS = 256


def _phase_copies(sh, rows):
    for o in range(1, SUBLANES):
        sh[o, 0:rows, :] = sh[0, pl.ds(o, rows), :]


def _conv_taps(sh, w_ref, offsets, r0, cols):
    acc = jnp.zeros((CONV_ROWS, cols.stop - cols.start), F32)
    for o in range(SUBLANES):
        taps = [(j, e // SUBLANES) for j, e in offsets if e % SUBLANES == o]
        if not taps:
            continue
        q0 = min(q for _, q in taps)
        q1 = max(q for _, q in taps)
        win = sh[o, pl.ds(r0 + SUBLANES * q0, CONV_ROWS + SUBLANES * (q1 - q0)), cols]
        for j, q in taps:
            lo = SUBLANES * (q - q0)
            acc = acc + w_ref[j:j + 1, cols] * win[lo:lo + CONV_ROWS]
    return acc


def _conv_fwd(g, w):
    s, ch = g.shape
    ts = 256
    per = ts // CONV_HALO
    lead = CONV_HALO - (CONV_KERNEL - 1)
    span = ts + CONV_HALO - SUBLANES

    def body(prev_ref, g_ref, w_ref, out_ref, sh):
        i = pl.program_id(0)
        sh[0, 0:CONV_HALO, :] = jnp.where(i == 0, 0.0, prev_ref[...])
        sh[0, CONV_HALO:, :] = g_ref[...]
        _phase_copies(sh, span)

        offsets = [(j, lead + j) for j in range(CONV_KERNEL)]

        def block(rb, carry):
            r0 = pl.multiple_of(rb * CONV_ROWS, CONV_ROWS)
            for c in range(ch // CONV_COLS):
                cols = slice(c * CONV_COLS, (c + 1) * CONV_COLS)
                out_ref[pl.ds(r0, CONV_ROWS), cols] = _conv_taps(sh, w_ref, offsets, r0, cols)
            return carry

        lax.fori_loop(0, ts // CONV_ROWS, block, 0)

    return pl.pallas_call(
        body, name="conv_fwd", grid=(s // ts,),
        in_specs=[pl.BlockSpec((CONV_HALO, ch), lambda i: (jnp.maximum(i * per - 1, 0), 0)),
                  _rows(ts, ch), _full(w.shape)],
        out_specs=_rows(ts, ch),
        out_shape=SDS((s, ch), F32),
        scratch_shapes=[pltpu.VMEM((SUBLANES, ts + CONV_HALO, ch), F32)],
        compiler_params=_seq_params(40),
    )(g, g, w)


def _conv_bwd(dc, g, w):
    s, ch = dc.shape
    ts = 256
    per = ts // CONV_HALO
    n = s // ts
    lead = CONV_HALO - (CONV_KERNEL - 1)
    span = ts + CONV_HALO - SUBLANES

    def body(dc_ref, next_ref, prev_ref, g_ref, w_ref, dg_ref, dw_ref, shd, shg, wacc):
        i = pl.program_id(0)

        @pl.when(i == 0)
        def _():
            wacc[...] = jnp.zeros_like(wacc)

        shd[0, 0:ts, :] = dc_ref[...]
        shd[0, ts:, :] = jnp.where(i == n - 1, 0.0, next_ref[...])
        shg[0, 0:CONV_HALO, :] = jnp.where(i == 0, 0.0, prev_ref[...])
        shg[0, CONV_HALO:, :] = g_ref[...]
        _phase_copies(shd, span)
        _phase_copies(shg, span)

        offsets = [(j, CONV_KERNEL - 1 - j) for j in range(CONV_KERNEL)]

        def dg_block(rb, carry):
            r0 = pl.multiple_of(rb * CONV_ROWS, CONV_ROWS)
            for c in range(ch // CONV_COLS):
                cols = slice(c * CONV_COLS, (c + 1) * CONV_COLS)
                dg_ref[pl.ds(r0, CONV_ROWS), cols] = _conv_taps(shd, w_ref, offsets, r0, cols)
            return carry

        lax.fori_loop(0, ts // CONV_ROWS, dg_block, 0)

        tiles = ts // SUBLANES
        for c in range(ch // LANES):
            cols = slice(c * LANES, (c + 1) * LANES)
            cur = [shd[0, k * SUBLANES:(k + 1) * SUBLANES, cols] for k in range(tiles)]
            for j in range(CONV_KERNEL):
                q, o = divmod(lead + j, SUBLANES)
                parts = [None] * 4
                for k in range(tiles):
                    term = cur[k] * shg[o, (k + q) * SUBLANES:(k + q + 1) * SUBLANES, cols]
                    parts[k % 4] = term if parts[k % 4] is None else parts[k % 4] + term
                wacc[j, :, cols] += (parts[0] + parts[1]) + (parts[2] + parts[3])

        @pl.when(i == n - 1)
        def _():
            dw_ref[...] = jnp.zeros_like(dw_ref)
            for j in range(CONV_KERNEL):
                dw_ref[j:j + 1, :] = jnp.sum(wacc[j], axis=0, keepdims=True)

    return pl.pallas_call(
        body, name="conv_bwd", grid=(n,),
        in_specs=[_rows(ts, ch),
                  pl.BlockSpec((CONV_HALO, ch), lambda i: (jnp.minimum((i + 1) * per, s // CONV_HALO - 1), 0)),
                  pl.BlockSpec((CONV_HALO, ch), lambda i: (jnp.maximum(i * per - 1, 0), 0)),
                  _rows(ts, ch), _full(w.shape)],
        out_specs=[_rows(ts, ch), _full((CONV_HALO, ch))],
        out_shape=[SDS((s, ch), F32), SDS((CONV_HALO, ch), F32)],
        scratch_shapes=[pltpu.VMEM((SUBLANES, ts + CONV_HALO, ch), F32),
                        pltpu.VMEM((SUBLANES, ts + CONV_HALO, ch), F32),
                        pltpu.VMEM((CONV_HALO, SUBLANES, ch), F32)],
        compiler_params=_seq_params(56),
    )(dc, dc, g, g, w)


def _conv_act(c, z1, cb, lg, lb):
    cc = c + cb
    mu = jnp.mean(cc, axis=-1, keepdims=True)
    dev = cc - mu
    var = jnp.mean(dev * dev, axis=-1, keepdims=True)
    cn = dev * lax.rsqrt(var + LN_EPS) * lg + lb
    return _silu(cn) * _silu(z1)


def _odd_out(c, proj1, x1, tgt, cb, lg, lb, w_out, fg):
    s, d = c.shape
    ts = 256

    def body(c_ref, z_ref, x1_ref, t_ref, cb_ref, lg_ref, lb_ref, w_ref, fg_ref,
             dc_ref, dz_ref, dx2_ref, y1_ref, loss_ref, dcb_ref, dlg_ref, dlb_ref, dfg_ref):
        @pl.when(pl.program_id(0) == 0)
        def _():
            for ref in (loss_ref, dcb_ref, dlg_ref, dlb_ref, dfg_ref):
                ref[...] = jnp.zeros_like(ref)

        y1, act_vjp = jax.vjp(_conv_act, c_ref[...], z_ref[...], cb_ref[...], lg_ref[...], lb_ref[...])
        y1b = y1.astype(MM)
        y1_ref[...] = y1b
        x2 = x1_ref[...] + jnp.dot(y1b, w_ref[...], preferred_element_type=F32)
        tgt_tile = t_ref[...]

        def head(x2, fg):
            err = jnp.square(_rms(x2, fg) - tgt_tile)
            return 0.5 * jnp.sum(jnp.mean(err, axis=-1))

        loss, (dx2, dfg) = jax.value_and_grad(head, argnums=(0, 1))(x2, fg_ref[...])
        loss_ref[...] += loss
        dfg_ref[...] += dfg
        dx2_ref[...] = dx2
        dy1 = _dot_nt(dx2, w_ref[...])
        dc, dz, dcb, dlg, dlb = act_vjp(dy1)
        dc_ref[...] = dc
        dz_ref[...] = dz
        dcb_ref[...] += dcb
        dlg_ref[...] += dlg
        dlb_ref[...] += dlb

    vec = SDS((1, d), F32)
    return pl.pallas_call(
        body, name="odd_out", grid=(s // ts,),
        in_specs=[_rows(ts, d), _rows(ts, d, col=2), _rows(ts, d), _rows(ts, d), _full((1, d)), _full((1, d)),
                  _full((1, d)), _full((d, d)), _full((1, d))],
        out_specs=[_rows(ts, d), _rows(ts, d), _rows(ts, d), _rows(ts, d), _full((SUBLANES, LANES)),
                   _full((1, d)), _full((1, d)), _full((1, d)), _full((1, d))],
        out_shape=[SDS((s, d), F32), SDS((s, d), F32), SDS((s, d), F32), SDS((s, d), MM),
                   SDS((SUBLANES, LANES), F32), vec, vec, vec, vec],
        compiler_params=_seq_params(56),
    )(c, proj1, x1, tgt, cb, lg, lb, w_out, fg)


def _odd_in_bwd(x1, proj1, dglu, dz1, dx2, g1, w_in):
    s, d = x1.shape
    e = w_in.shape[1]
    ts = 256

    def body(x_ref, vg_ref, dglu_ref, dz_ref, dx2_ref, g_ref, w_ref, dx1_ref, dp_ref, dg_ref):
        @pl.when(pl.program_id(0) == 0)
        def _():
            dg_ref[...] = jnp.zeros_like(dg_ref)

        _, glu_vjp = jax.vjp(_glu, vg_ref[:, :d], vg_ref[:, d:])
        dval, dgt = glu_vjp(dglu_ref[...])
        dp = jnp.concatenate([dval, dgt, dz_ref[...]], axis=1).astype(MM)
        dp_ref[...] = dp
        dh = _dot_nt(dp, w_ref[...])
        _, rms_vjp = jax.vjp(_rms, x_ref[...], g_ref[...])
        dxa, dg = rms_vjp(dh)
        dx1_ref[...] = dxa + dx2_ref[...]
        dg_ref[...] += dg

    return pl.pallas_call(
        body, name="odd_in_bwd", grid=(s // ts,),
        in_specs=[_rows(ts, d), _rows(ts, 2 * d), _rows(ts, d), _rows(ts, d), _rows(ts, d), _full((1, d)),
                  _full((d, e))],
        out_specs=[_rows(ts, d), _rows(ts, e), _full((1, d))],
        out_shape=[SDS((s, d), F32), SDS((s, e), MM), SDS((1, d), F32)],
        compiler_params=_seq_params(56),
    )(x1, proj1, dglu, dz1, dx2, g1, w_in)


def _even_mix_bwd(pooled, proj, mixed, gv, dx1, pw, scale, cc_re, cc_im, d_skip, w_glu, w_out, pushes):
    s, d = dx1.shape
    half = pooled.shape[1]
    n = cc_re.shape[0] * cc_re.shape[1]
    ts = 256
    nt = s // ts
    groups = len(POOL_WINDOWS)
    np_ = len(pushes)

    def body(*refs):
        ins, outs, _, push_refs = _split_refs(refs, 13, 10, np_)
        (pooled_ref, u_ref, z_ref, mixed_ref, gv_ref, dx1_ref, pw_ref, scale_ref, ccre_ref, ccim_ref, d_ref,
         wglu_ref, wout_ref) = ins
        (dpooled_ref, du_ref, dz_ref, dxre_ref, dxim_ref, dyin_ref, dgv_ref, dpw_ref, dscale_ref, dd_ref) = outs
        i = pl.program_id(0)
        _riding_start(pushes, i == 0, push_refs)

        @pl.when(i == 0)
        def _():
            for ref in (dpw_ref, dscale_ref, dd_ref):
                ref[...] = jnp.zeros_like(ref)

        dymix = _dot_nt(dx1_ref[...], wout_ref[...])
        _, a_vjp = jax.vjp(_mix_a, mixed_ref[...], scale_ref[...], z_ref[:, :half])
        dmixed, dscale, dza = a_vjp(dymix[:, :half])
        _, b_vjp = jax.vjp(_mix_b, gv_ref[:, :half], gv_ref[:, half:], z_ref[:, half:])
        dval, dgate, dzb = b_vjp(dymix[:, half:])
        dz_ref[:, :half] = dza
        dz_ref[:, half:] = dzb
        dscale_ref[...] += dscale
        dgv = jnp.concatenate([dval, dgate], axis=1).astype(MM)
        dgv_ref[...] = dgv
        dyin = _dot_nt(dgv, wglu_ref[...])
        dd_ref[...] += jnp.sum(dyin * u_ref[...], axis=0, keepdims=True)
        du_ref[...] = d_ref[...] * dyin
        dyb = dyin.astype(MM)
        dyin_ref[...] = dyb
        sb, cb = ccre_ref.shape[1:]
        for b in range(SSM_BLOCKS):
            states, chans = slice(b * sb, (b + 1) * sb), slice(b * cb, (b + 1) * cb)
            dxre_ref[:, states] = _dot_nt(dyb[:, chans], ccre_ref[b])
            dxim_ref[:, states] = -_dot_nt(dyb[:, chans], ccim_ref[b])
        for g in range(groups):
            cols = slice(g * LANES, (g + 1) * LANES)
            dm = dmixed[:, cols].astype(MM)
            dpooled_ref[:, cols] = _dot_nt(dm, pw_ref[g])
            dpw_ref[g] += _dot_tn(pooled_ref[:, cols], dm)

        _riding_wait(pushes, i == nt - 1, push_refs)

    out = pl.pallas_call(
        body, name="even_mix_bwd", grid=(nt,),
        in_specs=[_rows(ts, half), _rows(ts, half, col=1), _rows(ts, d, col=1), _rows(ts, half), _rows(ts, d),
                  _rows(ts, d), _full(pw.shape), _full((1, half)), _full(cc_re.shape), _full(cc_im.shape),
                  _full((1, half)), _full((half, d)), _full((d, d))] + [ANY] * np_,
        out_specs=[_rows(ts, half), _rows(ts, half), _rows(ts, d), _rows(ts, n), _rows(ts, n), _rows(ts, half),
                   _rows(ts, d), _full(pw.shape), _full((1, half)), _full((1, half))] + [ANY] * np_,
        out_shape=[SDS((s, half), F32), SDS((s, half), F32), SDS((s, d), F32), SDS((s, n), F32), SDS((s, n), F32),
                   SDS((s, half), MM), SDS((s, d), MM), SDS(pw.shape, F32), SDS((1, half), F32),
                   SDS((1, half), F32)] + [p.out for p in pushes],
        scratch_shapes=_push_scratch(np_),
        compiler_params=_seq_params(56),
    )(pooled, proj, proj, mixed, gv, dx1, pw, scale, cc_re, cc_im, d_skip, w_glu, w_out,
      *[p.array for p in pushes])
    return out[:10], out[10:]


def _ssm_bu_bwd(g_re, g_im, du_skip, bb):
    s, n = g_re.shape
    nb, cb, two_nb = bb.shape
    sb = two_nb // 2
    cin = nb * cb
    ts = 512

    def body(gre_ref, gim_ref, du_ref, bb_ref, out_ref):
        for b in range(nb):
            states, chans = slice(b * sb, (b + 1) * sb), slice(b * cb, (b + 1) * cb)
            out_ref[:, chans] = (du_ref[:, chans] + _dot_nt(gre_ref[:, states], bb_ref[b, :, :sb])
                                 + _dot_nt(gim_ref[:, states], bb_ref[b, :, sb:]))

    return pl.pallas_call(
        body, name="ssm_bu_bwd", grid=(s // ts,),
        in_specs=[_rows(ts, n), _rows(ts, n), _rows(ts, cin), _full(bb.shape)],
        out_specs=_rows(ts, cin),
        out_shape=SDS((s, cin), F32),
        compiler_params=_seq_params(48),
    )(g_re, g_im, du_skip, bb)


def _even_in_bwd(x, du_pool, du_ssm, dz, dx1, g0, w_in):
    s, d = x.shape
    half = du_pool.shape[1]
    shards, _, wc = w_in.shape
    ts = 256

    def body(x_ref, dup_ref, dus_ref, dz_ref, dx1_ref, g_ref, w_ref, gx_ref, dg_ref):
        @pl.when(pl.program_id(0) == 0)
        def _():
            dg_ref[...] = jnp.zeros_like(dg_ref)

        dp = jnp.concatenate([dup_ref[...], dus_ref[...], dz_ref[...]], axis=1).astype(MM)
        dh = _dot_nt(dp[:, :wc], w_ref[0])
        for k in range(1, shards):
            dh = dh + _dot_nt(dp[:, k * wc:(k + 1) * wc], w_ref[k])
        _, rms_vjp = jax.vjp(_rms, x_ref[...], g_ref[...])
        dxa, dg = rms_vjp(dh)
        gx_ref[...] = dxa + dx1_ref[...]
        dg_ref[...] += dg

    return pl.pallas_call(
        body, name="even_in_bwd", grid=(s // ts,),
        in_specs=[_rows(ts, d), _rows(ts, half), _rows(ts, half), _rows(ts, d), _rows(ts, d), _full((1, d)),
                  _full(w_in.shape)],
        out_specs=[_rows(ts, d), _full((1, d))],
        out_shape=[SDS((s, d), F32), SDS((1, d), F32)],
        compiler_params=_seq_params(48),
    )(x, du_pool, du_ssm, dz, dx1, g0, w_in)


def _discretise(log_dt, ar, ai, br, bi):
    dt = jnp.exp(log_dt)
    mag = jnp.exp(ar * dt)
    ang = ai * dt
    abr = mag * jnp.cos(ang)
    abi = mag * jnp.sin(ang)
    den = ar * ar + ai * ai
    nr = abr - 1.0
    ni = abi
    kr = (nr * ar + ni * ai) / den
    ki = (ni * ar - nr * ai) / den
    bbr = kr[None] * br - ki[None] * bi
    bbi = kr[None] * bi + ki[None] * br
    return abr, abi, bbr, bbi


def _whole(n):
    return [pl.BlockSpec(memory_space=pltpu.VMEM)] * n


def _disc_fwd(log_dt, ar, ai, br, bi):
    def body(ld_ref, ar_ref, ai_ref, br_ref, bi_ref, abr_ref, abi_ref, bbr_ref, bbi_ref):
        out = _discretise(ld_ref[...], ar_ref[...], ai_ref[...], br_ref[...], bi_ref[...])
        for ref, val in zip((abr_ref, abi_ref, bbr_ref, bbi_ref), out):
            ref[...] = val

    return pl.pallas_call(
        body, name="ssm_discretise", in_specs=_whole(5), out_specs=_whole(4),
        out_shape=[SDS(ar.shape, F32), SDS(ar.shape, F32), SDS(br.shape, F32), SDS(br.shape, F32)],
    )(log_dt, ar, ai, br, bi)


def _disc_bwd(log_dt, ar, ai, br, bi, dabr, dabi, dbbr, dbbi):
    def body(ld_ref, ar_ref, ai_ref, br_ref, bi_ref, dabr_ref, dabi_ref, dbbr_ref, dbbi_ref,
             dld_ref, dar_ref, dai_ref, dbr_ref, dbi_ref):
        _, vjp = jax.vjp(_discretise, ld_ref[...], ar_ref[...], ai_ref[...], br_ref[...], bi_ref[...])
        grads = vjp((dabr_ref[...], dabi_ref[...], dbbr_ref[...], dbbi_ref[...]))
        for ref, val in zip((dld_ref, dar_ref, dai_ref, dbr_ref, dbi_ref), grads):
            ref[...] = val

    return pl.pallas_call(
        body, name="ssm_discretise_bwd", in_specs=_whole(9), out_specs=_whole(5),
        out_shape=[SDS(log_dt.shape, F32), SDS(ar.shape, F32), SDS(ar.shape, F32), SDS(br.shape, F32),
                   SDS(br.shape, F32)],
    )(log_dt, ar, ai, br, bi, dabr, dabi, dbbr, dbbi)


def _block_diag(t):
    g, a, b = t.shape
    per = g // SSM_BLOCKS
    t = t.reshape(SSM_BLOCKS, per, a, b)
    same = jnp.eye(per, dtype=bool)[None, :, None, :, None]
    return jnp.where(same, t[:, :, :, None, :], 0.0).reshape(SSM_BLOCKS, per * a, per * b)


def _diag_blocks(m, g):
    per = g // SSM_BLOCKS
    a, b = m.shape[1] // per, m.shape[2] // per
    d = jnp.diagonal(m.reshape(SSM_BLOCKS, per, a, per, b), axis1=1, axis2=3)
    return jnp.moveaxis(d, -1, 1).reshape(g, a, b)


def _adamw(name, parts, w, m, v):
    rows, cols = w.shape
    tr = ADAM_ROWS if rows % ADAM_ROWS == 0 else rows

    def body(p_ref, w_ref, m_ref, v_ref, g_ref, d_ref, nm_ref, nv_ref):
        g = p_ref[0]
        for dev in range(1, N_DEV):
            g = g + p_ref[dev]
        g_ref[...] = g
        nm = ADAM_B1 * m_ref[...] + (1.0 - ADAM_B1) * g
        nv = ADAM_B2 * v_ref[...] + (1.0 - ADAM_B2) * jnp.square(g)
        nm_ref[...] = nm
        nv_ref[...] = nv
        m_hat = nm / (1.0 - ADAM_B1 ** ADAM_STEP)
        v_hat = nv / (1.0 - ADAM_B2 ** ADAM_STEP)
        d_ref[...] = -ADAM_LR * (m_hat / (jnp.sqrt(v_hat) + ADAM_EPS) + ADAM_WD * w_ref[...])

    tile = _rows(tr, cols)
    out = SDS((rows, cols), F32)
    return pl.pallas_call(
        body, name=name, grid=(rows // tr,),
        in_specs=[pl.BlockSpec((N_DEV, tr, cols), lambda i: (0, i, 0)), tile, tile, tile],
        out_specs=[tile, tile, tile, tile],
        out_shape=[out, out, out, out],
        compiler_params=_seq_params(32),
    )(parts, w, m, v)


def _as_rows(flat):
    n = flat.shape[-1]
    padded = _round_up(n, PACK_TILE)
    if padded != n:
        flat = jnp.pad(flat, [(0, 0)] * (flat.ndim - 1) + [(0, padded - n)])
    return flat.reshape(flat.shape[:-1] + (padded // LANES, LANES))


def _columns_to_shards(full):
    r, c8 = full.shape
    return jnp.transpose(full.reshape(r, N_DEV, c8 // N_DEV), (1, 0, 2))


def _shards_to_columns(shards):
    n, r, c = shards.shape
    return jnp.transpose(shards, (1, 0, 2)).reshape(r, n * c)


def _pack_local(tensors):
    return jnp.concatenate([_as_rows(t.reshape(-1)) for t in tensors], axis=0)


def _unpack_local(packed, shapes):
    out, r = [], 0
    for shape in shapes:
        n = 1
        for dim in shape:
            n *= dim
        rows = _round_up(n, PACK_TILE) // LANES
        out.append(packed[r:r + rows].reshape(-1)[:n].reshape(shape))
        r += rows
    return out


def kernel(x, even_norm, even_w_in, pool_w, pool_scale, ssm_log_dt, ssm_a_re, ssm_a_im, ssm_b_re, ssm_b_im, ssm_c_re, ssm_c_im, ssm_d, ssm_w_glu, even_w_out, odd_norm, odd_w_in, conv_w, conv_b, conv_ln_g, conv_ln_b, odd_w_out, final_norm, loss_target, m_even_norm, m_even_w_in, m_pool_w, m_pool_scale, m_ssm_log_dt, m_ssm_a_re, m_ssm_a_im, m_ssm_b_re, m_ssm_b_im, m_ssm_c_re, m_ssm_c_im, m_ssm_d, m_ssm_w_glu, m_even_w_out, m_odd_norm, m_odd_w_in, m_conv_w, m_conv_b, m_conv_ln_g, m_conv_ln_b, m_odd_w_out, m_final_norm, v_even_norm, v_even_w_in, v_pool_w, v_pool_scale, v_ssm_log_dt, v_ssm_a_re, v_ssm_a_im, v_ssm_b_re, v_ssm_b_im, v_ssm_c_re, v_ssm_c_im, v_ssm_d, v_ssm_w_glu, v_even_w_out, v_odd_norm, v_odd_w_in, v_conv_w, v_conv_b, v_conv_ln_g, v_conv_ln_b, v_odd_w_out, v_final_norm):
    given = dict(locals())
    xs = x[0]
    tgt = loss_target[0]
    d_model = xs.shape[1]

    def local(prefix, name):
        t = given[prefix + name]
        return t if name == 'final_norm' else t[0]

    def small_block(prefix):
        parts = [jnp.pad(local(prefix, 'conv_w'), ((0, SMALL_AT['odd_norm'] - CONV_KERNEL), (0, 0)))]
        for n in SMALL_VECTORS:
            parts.append(jnp.pad(local(prefix, n).reshape(1, LANES), ((0, SUBLANES - 1), (0, 0))))
        return jnp.concatenate(parts, axis=0)

    def as_tile(row):
        return jnp.pad(row, ((0, SUBLANES - row.shape[0]), (0, 0)))

    (w_in_e,) = _push_alone("gather_first", [_Push(local('', 'even_w_in').astype(MM), False)])
    later = [_Push(local('', 'ssm_w_glu').astype(MM), False), _Push(local('', 'even_w_out').astype(MM), False),
             _Push(local('', 'odd_w_in').astype(MM), False), _Push(local('', 'odd_w_out').astype(MM), False),
             _Push(small_block(''), False)]

    a_re, a_im = local('', 'ssm_a_re'), local('', 'ssm_a_im')
    groups, state = a_re.shape
    log_dt = local('', 'ssm_log_dt').reshape(groups, 1)
    b_re_t = jnp.transpose(local('', 'ssm_b_re'), (2, 0, 1))
    b_im_t = jnp.transpose(local('', 'ssm_b_im'), (2, 0, 1))
    abr, abi, bbr, bbi = _disc_fwd(log_dt, a_re, a_im, b_re_t, b_im_t)
    a_re_row = abr.reshape(1, groups * state)
    a_im_row = abi.reshape(1, groups * state)
    bb = jnp.concatenate([_block_diag(jnp.transpose(bbr, (1, 0, 2))), _block_diag(jnp.transpose(bbi, (1, 0, 2)))],
                         axis=2).astype(MM)
    cc_re = _block_diag(jnp.transpose(local('', 'ssm_c_re'), (0, 2, 1))).astype(MM)
    cc_im = _block_diag(jnp.transpose(local('', 'ssm_c_im'), (0, 2, 1))).astype(MM)
    pw = local('', 'pool_w').astype(MM)
    g0 = local('', 'even_norm').reshape(1, d_model)
    fg = local('', 'final_norm').reshape(1, d_model)
    scale = local('', 'pool_scale').reshape(1, -1)
    d_skip = local('', 'ssm_d').reshape(1, -1)

    proj0, h0 = _even_in(xs, g0, w_in_e)
    pooled = _pool_fwd(proj0)
    bu_re, bu_im = _ssm_bu(proj0, bb)
    x_re, x_im, (wg3, wo_e3, wi_o3, wo_o3, small8) = _scan_fwd(a_re_row, a_im_row, bu_re, bu_im, later)
    w_glu = _shards_to_columns(wg3)
    w_out_e = wo_e3.reshape(d_model, d_model)
    w_in_o = _shards_to_columns(wi_o3)
    w_out_o = wo_o3.reshape(d_model, d_model)
    small = _shards_to_columns(small8)
    cw = small[:CONV_KERNEL]
    g1, cb, lg, lb = (small[SMALL_AT[n]:SMALL_AT[n] + 1] for n in SMALL_VECTORS)
    x1, ymix, mixed, yin, gv = _even_mix(pooled, proj0, x_re, x_im, xs, pw, scale, cc_re, cc_im, d_skip, w_glu,
                                         w_out_e)
    proj1, glu, h1 = _odd_in(x1, g1, w_in_o)
    conv = _conv_fwd(glu, cw)

    half = pooled.shape[1]
    n_state = groups * state
    rows_per = d_model // N_DEV
    dc, dz1, dx2, y1, loss_tile, dcb, dlg, dlb, dfg = _odd_out(conv, proj1, x1, tgt, cb, lg, lb, w_out_o, fg)
    dglu, dcw = _conv_bwd(dc, glu, cw)
    dx1, dproj1, dg1 = _odd_in_bwd(x1, proj1, dglu, dz1, dx2, g1, w_in_o)
    g_odd_out = _mm_tn("dw_odd_out", y1, dx2, d_model, d_model).reshape(N_DEV, rows_per, d_model)
    g_odd_in = _columns_to_shards(_mm_tn("dw_odd_in", h1, dproj1, d_model, 3 * d_model))
    g_even_out = _mm_tn("dw_even_out", ymix, dx1, d_model, d_model).reshape(N_DEV, rows_per, d_model)
    mix_grads, (r_odd_in,) = _even_mix_bwd(pooled, proj0, mixed, gv, dx1, pw, scale, cc_re, cc_im, d_skip, w_glu,
                                           w_out_e, [_Push(g_odd_in, True)])
    dpooled, du_skip, dz0, dx_re, dx_im, dyin, dgv, dpw, dscale, dd = mix_grads
    g_glu = _columns_to_shards(_mm_tn("dw_glu", yin, dgv, half, d_model))
    sblk, cblk = n_state // SSM_BLOCKS, half // SSM_BLOCKS
    dcc_re = _mm_tn_blocks("dw_c_re", x_re, dyin, sblk, cblk, SSM_BLOCKS, ts=2048)
    dcc_im = _mm_tn_blocks("dw_c_im", x_im, dyin, sblk, cblk, SSM_BLOCKS, sign=-1.0, ts=2048)
    dc_re = jnp.transpose(_diag_blocks(dcc_re, groups), (0, 2, 1))
    dc_im = jnp.transpose(_diag_blocks(dcc_im, groups), (0, 2, 1))
    g_small = _columns_to_shards(jnp.concatenate(
        [dcw, as_tile(dg1), as_tile(dcb), as_tile(dlg), as_tile(dlb)], axis=0))
    early = {'pool_w': dpw, 'pool_scale': dscale, 'ssm_c_re': dc_re, 'ssm_c_im': dc_im, 'ssm_d': dd,
             'final_norm': dfg}
    g_early = _pack_local([early[n] for n in EARLY_REPLICATED])
    g_re, g_im, dabr, dabi, (r_odd_out, r_even_out, r_glu, r_small, r_early) = _scan_bwd(
        a_re_row, a_im_row, dx_re, dx_im, x_re, x_im,
        [_Push(g_odd_out, True), _Push(g_even_out, True), _Push(g_glu, True), _Push(g_small, True),
         _Push(g_early, False)])
    du_ssm = _ssm_bu_bwd(g_re, g_im, du_skip, bb)
    du_pool = _pool_bwd(dpooled)
    grad_x, dg0 = _even_in_bwd(xs, du_pool, du_ssm, dz0, dx1, g0, w_in_e)
    g_even_in = _columns_to_shards(jnp.concatenate(
        [_mm_tn("dw_even_in_pool", h0, du_pool, d_model, half), _mm_tn("dw_even_in_ssm", h0, du_ssm, d_model, half),
         _mm_tn("dw_even_in_gate", h0, dz0, d_model, d_model)], axis=1))
    g_mid = _pack_local([dg0])
    dbb_re, dbb_im, (r_even_in, r_mid) = _dw_bbar(proj0, g_re, g_im, cblk, sblk,
                                                  [_Push(g_even_in, True), _Push(g_mid, False)])
    dbbr = jnp.transpose(_diag_blocks(dbb_re, groups), (1, 0, 2))
    dbbi = jnp.transpose(_diag_blocks(dbb_im, groups), (1, 0, 2))
    dld, dar, dai, dbr, dbi = _disc_bwd(log_dt, a_re, a_im, b_re_t, b_im_t,
                                        dabr.reshape(groups, state), dabi.reshape(groups, state), dbbr, dbbi)
    late = {'ssm_log_dt': dld, 'ssm_a_re': dar, 'ssm_a_im': dai,
            'ssm_b_re': jnp.transpose(dbr, (1, 2, 0)), 'ssm_b_im': jnp.transpose(dbi, (1, 2, 0))}
    g_late = _pack_local([late[n] for n in LATE_REPLICATED])
    (r_late,) = _push_alone("exchange_tail", [_Push(g_late, False)])

    results = {}
    for n, parts in (('even_w_in', r_even_in), ('ssm_w_glu', r_glu), ('even_w_out', r_even_out),
                     ('odd_w_in', r_odd_in), ('odd_w_out', r_odd_out)):
        results[n] = _adamw("adamw_" + n, parts, local('', n), local('m_', n), local('v_', n))
    small_out = _adamw("adamw_small", r_small, small_block(''), small_block('m_'), small_block('v_'))
    results['conv_w'] = [o[:CONV_KERNEL] for o in small_out]
    for n in SMALL_VECTORS:
        results[n] = [o[SMALL_AT[n]] for o in small_out]
    for names, parts in ((EARLY_REPLICATED, r_early), (MID_REPLICATED, r_mid), (LATE_REPLICATED, r_late)):
        packed = [_pack_local([local(p, n) for n in names]) for p in ('', 'm_', 'v_')]
        out = _adamw("adamw_" + names[0], parts, *packed)
        unpacked = [_unpack_local(o, [given[n].shape for n in names]) for o in out]
        for k, n in enumerate(names):
            results[n] = [u[k] for u in unpacked]

    loss = lax.psum(loss_tile[0, 0], ("x", "y", "c"))
    outs = [loss, grad_x[None]]
    for kind in range(4):
        outs.extend(results[n][kind].reshape(given[n].shape) for n in WEIGHTS)
    return tuple(outs)
```

```python
import functools

import jax
import jax.numpy as jnp
from jax import lax
from jax.experimental import pallas as pl
from jax.experimental.pallas import tpu as pltpu

F32 = jnp.float32
MM = jnp.bfloat16
SDS = jax.ShapeDtypeStruct

RMS_EPS = 1e-6
LN_EPS = 1e-5
POOL_WINDOWS = (2, 4, 8, 16)
POOL_HALO = 16
CONV_KERNEL = 31
CONV_HALO = 32
N_DEV = 8
LANES = 128
SUBLANES = 8
PACK_TILE = SUBLANES * LANES
ADAM_ROWS = 128
MIB = 1024 * 1024

ADAM_LR = 0.001
ADAM_B1 = 0.9
ADAM_B2 = 0.999
ADAM_EPS = 1e-08
ADAM_WD = 0.01
ADAM_STEP = 10

WEIGHTS = ['even_norm', 'even_w_in', 'pool_w', 'pool_scale', 'ssm_log_dt', 'ssm_a_re', 'ssm_a_im', 'ssm_b_re',
           'ssm_b_im', 'ssm_c_re', 'ssm_c_im', 'ssm_d', 'ssm_w_glu', 'even_w_out', 'odd_norm', 'odd_w_in', 'conv_w',
           'conv_b', 'conv_ln_g', 'conv_ln_b', 'odd_w_out', 'final_norm']
SMALL_VECTORS = ('odd_norm', 'conv_b', 'conv_ln_g', 'conv_ln_b')
SMALL_AT = {'odd_norm': 32, 'conv_b': 40, 'conv_ln_g': 48, 'conv_ln_b': 56}
EARLY_REPLICATED = ['pool_w', 'pool_scale', 'ssm_c_re', 'ssm_c_im', 'ssm_d', 'final_norm']
MID_REPLICATED = ['even_norm']
LATE_REPLICATED = ['ssm_log_dt', 'ssm_a_re', 'ssm_a_im', 'ssm_b_re', 'ssm_b_im']


def _round_up(n, m):
    return (n + m - 1) // m * m


def _sigmoid(x):
    return jax.nn.sigmoid(x)


def _silu(x):
    return x * jax.nn.sigmoid(x)


def _rms(x, g):
    return x * lax.rsqrt(jnp.mean(x * x, axis=-1, keepdims=True) + RMS_EPS) * g


def _dot(a, b):
    return jnp.dot(a.astype(MM), b.astype(MM), preferred_element_type=F32)


def _dot_nt(a, b):
    return lax.dot_general(a.astype(MM), b.astype(MM), (((1,), (1,)), ((), ())), preferred_element_type=F32)


def _dot_tn(a, b):
    return lax.dot_general(a.astype(MM), b.astype(MM), (((0,), (0,)), ((), ())), preferred_element_type=F32)


def _rows(ts, width, col=0):
    return pl.BlockSpec((ts, width), lambda i: (i, col))


def _rows_rev(ts, width, n, col=0):
    return pl.BlockSpec((ts, width), lambda i: (n - 1 - i, col))


def _full(shape):
    zeros = (0,) * len(shape)
    return pl.BlockSpec(shape, lambda i: zeros)


def _seq_params(vmem_mib=48, dims=1):
    return pltpu.CompilerParams(dimension_semantics=("arbitrary",) * dims, vmem_limit_bytes=vmem_mib * MIB)


def _mesh_position():
    x, y, c = lax.axis_index("x"), lax.axis_index("y"), lax.axis_index("c")
    return x, y, c


def _peer(pos, relation):
    x, y, c = pos
    px = 1 - x if relation & 4 else x
    py = 1 - y if relation & 2 else y
    pc = 1 - c if relation & 1 else c
    return px, py, pc


ANY = pl.BlockSpec(memory_space=pl.ANY)


class _Push:
    def __init__(self, array, scatter):
        self.array = array
        self.scatter = scatter
        self.out = SDS(array.shape if scatter else (N_DEV,) + array.shape, array.dtype)


def _push_scratch(n):
    return [pltpu.SemaphoreType.DMA((n * (N_DEV - 1),)), pltpu.SemaphoreType.DMA((n * (N_DEV - 1),)),
            pltpu.SemaphoreType.DMA((n,))]


def _push_copies(pushes, srcs, dsts, send_sems, recv_sems, local_sems):
    pos = _mesh_position()
    me = 4 * pos[0] + 2 * pos[1] + pos[2]
    copies = []
    for a, (push, src, dst) in enumerate(zip(pushes, srcs, dsts)):
        copies.append(pltpu.make_async_copy(src.at[me] if push.scatter else src, dst.at[me], local_sems.at[a]))
        for relation in range(1, N_DEV):
            peer = _peer(pos, relation)
            peer_id = 4 * peer[0] + 2 * peer[1] + peer[2]
            k = a * (N_DEV - 1) + relation - 1
            copies.append(pltpu.make_async_remote_copy(
                src_ref=src.at[peer_id] if push.scatter else src, dst_ref=dst.at[me],
                send_sem=send_sems.at[k], recv_sem=recv_sems.at[k], device_id=peer,
                device_id_type=pl.DeviceIdType.MESH))
    return copies


def _push_alone(name, pushes):
    n = len(pushes)

    def body(*refs):
        copies = _push_copies(pushes, refs[:n], refs[n:2 * n], *refs[2 * n:])
        for cp in copies:
            cp.start()
        for cp in copies:
            cp.wait()

    return pl.pallas_call(
        body, name=name, out_shape=[p.out for p in pushes], in_specs=[ANY] * n, out_specs=[ANY] * n,
        scratch_shapes=_push_scratch(n),
    )(*[p.array for p in pushes])


def _riding_start(pushes, first, refs):
    n = len(pushes)

    @pl.when(first)
    def _():
        for cp in _push_copies(pushes, refs[:n], refs[n:2 * n], *refs[2 * n:]):
            cp.start()


def _riding_wait(pushes, last, refs):
    n = len(pushes)

    @pl.when(last)
    def _():
        for cp in _push_copies(pushes, refs[:n], refs[n:2 * n], *refs[2 * n:]):
            cp.wait()


def _split_refs(refs, n_in, n_out, n_push):
    ins = refs[:n_in]
    srcs = refs[n_in:n_in + n_push]
    o0 = n_in + n_push
    outs = refs[o0:o0 + n_out]
    dsts = refs[o0 + n_out:o0 + n_out + n_push]
    s0 = o0 + n_out + n_push
    sems = refs[s0:s0 + 3]
    scratch = refs[s0 + 3:]
    return ins, outs, scratch, tuple(srcs) + tuple(dsts) + tuple(sems)


def _mm_tn(name, a, b, m, n, a_col=0, b_col=0, ts=512):
    s = a.shape[0]
    tn = n
    nj = n // tn

    def body(a_ref, b_ref, o_ref):
        @pl.when(pl.program_id(1) == 0)
        def _():
            o_ref[...] = jnp.zeros_like(o_ref)

        o_ref[...] += _dot_tn(a_ref[...], b_ref[...])

    return pl.pallas_call(
        body, name=name, grid=(nj, s // ts),
        in_specs=[pl.BlockSpec((ts, m), lambda j, k: (k, a_col)),
                  pl.BlockSpec((ts, tn), lambda j, k: (k, b_col * nj + j))],
        out_specs=pl.BlockSpec((m, tn), lambda j, k: (0, j)),
        out_shape=SDS((m, n), F32),
        compiler_params=_seq_params(56, dims=2),
    )(a, b)


def _mm_tn_blocks(name, a, b, m, n, nb, a_col=0, a_step=1, b_col=0, sign=1.0, ts=512):
    s = a.shape[0]

    def body(a_ref, b_ref, o_ref):
        @pl.when(pl.program_id(1) == 0)
        def _():
            o_ref[...] = jnp.zeros_like(o_ref)

        o_ref[...] += sign * _dot_tn(a_ref[...], b_ref[...])

    return pl.pallas_call(
        body, name=name, grid=(nb, s // ts),
        in_specs=[pl.BlockSpec((ts, m), lambda j, k: (k, a_col + a_step * j)),
                  pl.BlockSpec((ts, n), lambda j, k: (k, b_col + j))],
        out_specs=pl.BlockSpec((None, m, n), lambda j, k: (j, 0, 0)),
        out_shape=SDS((nb, m, n), F32),
        compiler_params=_seq_params(48, dims=2),
    )(a, b)


def _dw_bbar(proj, g_re, g_im, cb, sb, pushes, ts=2048):
    s = proj.shape[0]
    nb, nk = SSM_BLOCKS, s // ts
    np_ = len(pushes)

    def body(*refs):
        ins, outs, _, push_refs = _split_refs(refs, 3, 2, np_)
        u_ref, gre_ref, gim_ref = ins
        ore_ref, oim_ref = outs
        j, k = pl.program_id(0), pl.program_id(1)
        _riding_start(pushes, jnp.logical_and(j == 0, k == 0), push_refs)

        @pl.when(k == 0)
        def _():
            ore_ref[...] = jnp.zeros_like(ore_ref)
            oim_ref[...] = jnp.zeros_like(oim_ref)

        u = u_ref[...].astype(MM)
        ore_ref[...] += _dot_tn(u, gre_ref[...])
        oim_ref[...] += _dot_tn(u, gim_ref[...])
        _riding_wait(pushes, jnp.logical_and(j == nb - 1, k == nk - 1), push_refs)

    wide = pl.BlockSpec((ts, sb), lambda j, k: (k, j))
    out = pl.BlockSpec((None, cb, sb), lambda j, k: (j, 0, 0))
    res = pl.pallas_call(
        body, name="dw_bbar", grid=(nb, nk),
        in_specs=[pl.BlockSpec((ts, cb), lambda j, k: (k, nb + j)), wide, wide] + [ANY] * np_,
        out_specs=[out, out] + [ANY] * np_,
        out_shape=[SDS((nb, cb, sb), F32), SDS((nb, cb, sb), F32)] + [p.out for p in pushes],
        scratch_shapes=_push_scratch(np_),
        compiler_params=_seq_params(48, dims=2),
    )(proj, g_re, g_im, *[p.array for p in pushes])
    return res[0], res[1], res[2:]


def _even_in(x, g0, w_in):
    s, d = x.shape
    shards, _, wc = w_in.shape
    e = shards * wc
    ts = 512

    def body(x_ref, g_ref, w_ref, proj_ref, h_ref):
        hb = _rms(x_ref[...], g_ref[...]).astype(MM)
        h_ref[...] = hb
        for k in range(shards):
            proj_ref[:, k * wc:(k + 1) * wc] = jnp.dot(hb, w_ref[k], preferred_element_type=F32)

    return pl.pallas_call(
        body, name="even_in", grid=(s // ts,),
        in_specs=[_rows(ts, d), _full((1, d)), _full(w_in.shape)],
        out_specs=[_rows(ts, e), _rows(ts, d)],
        out_shape=[SDS((s, e), F32), SDS((s, d), MM)],
        compiler_params=_seq_params(48),
    )(x, g0, w_in)


def _pool_counts(t0, ts, w):
    pos = (t0 + lax.broadcasted_iota(jnp.int32, (ts, LANES), 0) + 1).astype(F32)
    return jnp.minimum(pos, float(w))


def _pool_fwd(proj):
    s = proj.shape[0]
    width = LANES * len(POOL_WINDOWS)
    ts = 512
    per = ts // POOL_HALO

    def body(prev_ref, u_ref, out_ref, ext):
        i = pl.program_id(0)
        ext[0:POOL_HALO, :] = jnp.where(i == 0, 0.0, prev_ref[...])
        ext[POOL_HALO:, :] = u_ref[...]
        for g, w in enumerate(POOL_WINDOWS):
            cols = slice(g * LANES, (g + 1) * LANES)
            tok = ext[pl.ds(POOL_HALO, ts), cols]
            acc = tok
            for k in range(1, w):
                acc = acc + ext[pl.ds(POOL_HALO - k, ts), cols]
            out_ref[:, cols] = acc / _pool_counts(i * ts, ts, w) - tok

    return pl.pallas_call(
        body, name="pool_fwd", grid=(s // ts,),
        in_specs=[pl.BlockSpec((POOL_HALO, width), lambda i: (jnp.maximum(i * per - 1, 0), 0)),
                  _rows(ts, width)],
        out_specs=_rows(ts, width),
        out_shape=SDS((s, width), F32),
        scratch_shapes=[pltpu.VMEM((ts + POOL_HALO, width), F32)],
        compiler_params=_seq_params(32),
    )(proj, proj)


def _pool_bwd(dp):
    s, width = dp.shape
    ts = 512
    per = ts // POOL_HALO
    n = s // ts

    def body(dp_ref, next_ref, out_ref, ext):
        i = pl.program_id(0)
        nxt = jnp.where(i == n - 1, 0.0, next_ref[...])
        for g, w in enumerate(POOL_WINDOWS):
            cols = slice(g * LANES, (g + 1) * LANES)
            cur = dp_ref[:, cols]
            ext[0:ts, cols] = cur / _pool_counts(i * ts, ts, w)
            ext[ts:, cols] = nxt[:, cols] / _pool_counts((i + 1) * ts, POOL_HALO, w)
            acc = -cur
            for k in range(w):
                acc = acc + ext[pl.ds(k, ts), cols]
            out_ref[:, cols] = acc

    return pl.pallas_call(
        body, name="pool_bwd", grid=(n,),
        in_specs=[_rows(ts, width),
                  pl.BlockSpec((POOL_HALO, width), lambda i: (jnp.minimum((i + 1) * per, s // POOL_HALO - 1), 0))],
        out_specs=_rows(ts, width),
        out_shape=SDS((s, width), F32),
        scratch_shapes=[pltpu.VMEM((ts + POOL_HALO, width), F32)],
        compiler_params=_seq_params(32),
    )(dp, dp)


SSM_BLOCKS = 4


def _ssm_bu(proj, bb):
    s = proj.shape[0]
    nb, cb, two_nb = bb.shape
    sb = two_nb // 2
    cin, n = nb * cb, nb * sb
    ts = 512

    def body(u_ref, bb_ref, re_ref, im_ref):
        for b in range(nb):
            bu = _dot(u_ref[:, b * cb:(b + 1) * cb], bb_ref[b])
            re_ref[:, b * sb:(b + 1) * sb] = bu[:, :sb]
            im_ref[:, b * sb:(b + 1) * sb] = bu[:, sb:]

    return pl.pallas_call(
        body, name="ssm_bu", grid=(s // ts,),
        in_specs=[_rows(ts, cin, col=1), _full(bb.shape)],
        out_specs=[_rows(ts, n), _rows(ts, n)],
        out_shape=[SDS((s, n), F32), SDS((s, n), F32)],
        compiler_params=_seq_params(48),
    )(proj, bb)


SCAN_LANES = 256


def _cmul(a, b):
    return a[0] * b[0] - a[1] * b[1], a[0] * b[1] + a[1] * b[0]


SCAN_STEPS = (1, 2, 4)


def _fill_scan_tables(tab, ar_row, ai_row, reverse):
    shape = (SUBLANES, ar_row.shape[1])
    a1 = (jnp.broadcast_to(ar_row, shape), jnp.broadcast_to(ai_row, shape))
    a2 = _cmul(a1, a1)
    a4 = _cmul(a2, a2)
    a8 = _cmul(a4, a4)
    row = lax.broadcasted_iota(jnp.int32, shape, 0)
    for idx, (k, q) in enumerate(zip(SCAN_STEPS, (a1, a2, a4))):
        keep = (row < SUBLANES - k) if reverse else (row >= k)
        tab[2 * idx] = jnp.where(keep, q[0], 0.0)
        tab[2 * idx + 1] = jnp.where(keep, q[1], 0.0)
    expo = (SUBLANES - row) if reverse else (row + 1)
    pr, pi = jnp.ones(shape, F32), jnp.zeros(shape, F32)
    for bit, q in enumerate((a1, a2, a4, a8)):
        take = ((expo >> bit) & 1) == 1
        nr, ni = _cmul((pr, pi), q)
        pr, pi = jnp.where(take, nr, pr), jnp.where(take, ni, pi)
    tab[2 * len(SCAN_STEPS)] = pr
    tab[2 * len(SCAN_STEPS) + 1] = pi


def _group_scan(v, tab, cols, carry, reverse):
    vr, vi = v
    for idx, k in enumerate(SCAN_STEPS):
        shift = SUBLANES - k if reverse else k
        tr, ti = _cmul((tab[2 * idx, :, cols], tab[2 * idx + 1, :, cols]),
                       (pltpu.roll(vr, shift, 0), pltpu.roll(vi, shift, 0)))
        vr, vi = vr + tr, vi + ti
    last = 2 * len(SCAN_STEPS)
    tr, ti = _cmul((tab[last, :, cols], tab[last + 1, :, cols]), carry)
    return vr + tr, vi + ti


def _scan_fwd(a_re, a_im, bu_re, bu_im, pushes):
    s, n = bu_re.shape
    ts = 512
    nt = s // ts
    groups = ts // SUBLANES
    np_ = len(pushes)

    def body(*refs):
        ins, outs, scratch, push_refs = _split_refs(refs, 4, 2, np_)
        ar_ref, ai_ref, bre_ref, bim_ref = ins
        xre_ref, xim_ref = outs
        tab, cre, cim = scratch
        i = pl.program_id(0)
        _riding_start(pushes, i == 0, push_refs)

        @pl.when(i == 0)
        def _():
            _fill_scan_tables(tab, ar_ref[...], ai_ref[...], False)
            cre[...] = jnp.zeros_like(cre)
            cim[...] = jnp.zeros_like(cim)

        def group(gi, carry):
            r0 = pl.multiple_of(gi * SUBLANES, SUBLANES)
            for c in range(n // SCAN_LANES):
                cols = slice(c * SCAN_LANES, (c + 1) * SCAN_LANES)
                v = (bre_ref[pl.ds(r0, SUBLANES), cols], bim_ref[pl.ds(r0, SUBLANES), cols])
                vr, vi = _group_scan(v, tab, cols, (cre[:, cols], cim[:, cols]), False)
                xre_ref[pl.ds(r0, SUBLANES), cols] = vr
                xim_ref[pl.ds(r0, SUBLANES), cols] = vi
                cre[:, cols] = jnp.broadcast_to(vr[SUBLANES - 1:SUBLANES, :], vr.shape)
                cim[:, cols] = jnp.broadcast_to(vi[SUBLANES - 1:SUBLANES, :], vi.shape)
            return carry

        lax.fori_loop(0, groups, group, 0, unroll=2)
        _riding_wait(pushes, i == nt - 1, push_refs)

    out = pl.pallas_call(
        body, name="ssm_scan", grid=(nt,),
        in_specs=[_full((1, n)), _full((1, n)), _rows(ts, n), _rows(ts, n)] + [ANY] * np_,
        out_specs=[_rows(ts, n), _rows(ts, n)] + [ANY] * np_,
        out_shape=[SDS((s, n), F32), SDS((s, n), F32)] + [p.out for p in pushes],
        scratch_shapes=_push_scratch(np_) + [pltpu.VMEM((2 * len(SCAN_STEPS) + 2, SUBLANES, n), F32),
                                             pltpu.VMEM((SUBLANES, n), F32), pltpu.VMEM((SUBLANES, n), F32)],
        compiler_params=_seq_params(48),
    )(a_re, a_im, bu_re, bu_im, *[p.array for p in pushes])
    return out[0], out[1], out[2:]


def _scan_bwd(a_re, a_im, dx_re, dx_im, x_re, x_im, pushes):
    s, n = dx_re.shape
    ts = 256
    nt = s // ts
    groups = ts // SUBLANES
    np_ = len(pushes)

    def body(*refs):
        ins, outs, scratch, push_refs = _split_refs(refs, 6, 4, np_)
        ar_ref, ai_ref, dre_ref, dim_ref, xre_ref, xim_ref = ins
        gre_ref, gim_ref, dar_ref, dai_ref = outs
        tab, cre, cim, accr, acci = scratch
        i = pl.program_id(0)
        _riding_start(pushes, i == 0, push_refs)

        @pl.when(i == 0)
        def _():
            _fill_scan_tables(tab, ar_ref[...], -ai_ref[...], True)
            for ref in (cre, cim, accr, acci):
                ref[...] = jnp.zeros_like(ref)

        inner = lax.broadcasted_iota(jnp.int32, (SUBLANES, SCAN_LANES), 0) < SUBLANES - 1

        def group(k, carry):
            r0 = pl.multiple_of((groups - 1 - k) * SUBLANES, SUBLANES)
            for c in range(n // SCAN_LANES):
                cols = slice(c * SCAN_LANES, (c + 1) * SCAN_LANES)
                after = (cre[:, cols], cim[:, cols])
                v = (dre_ref[pl.ds(r0, SUBLANES), cols], dim_ref[pl.ds(r0, SUBLANES), cols])
                vr, vi = _group_scan(v, tab, cols, after, True)
                gre_ref[pl.ds(r0, SUBLANES), cols] = vr
                gim_ref[pl.ds(r0, SUBLANES), cols] = vi
                nr = jnp.where(inner, pltpu.roll(vr, SUBLANES - 1, 0), after[0])
                ni = jnp.where(inner, pltpu.roll(vi, SUBLANES - 1, 0), after[1])
                xr, xi = xre_ref[pl.ds(r0, SUBLANES), cols], xim_ref[pl.ds(r0, SUBLANES), cols]
                accr[:, cols] += nr * xr + ni * xi
                acci[:, cols] += ni * xr - nr * xi
                cre[:, cols] = jnp.broadcast_to(vr[0:1, :], vr.shape)
                cim[:, cols] = jnp.broadcast_to(vi[0:1, :], vi.shape)
            return carry

        lax.fori_loop(0, groups, group, 0, unroll=2)

        @pl.when(i == nt - 1)
        def _():
            dar_ref[...] = jnp.sum(accr[...], axis=0, keepdims=True)
            dai_ref[...] = jnp.sum(acci[...], axis=0, keepdims=True)

        _riding_wait(pushes, i == nt - 1, push_refs)

    small = pltpu.VMEM((SUBLANES, n), F32)
    out = pl.pallas_call(
        body, name="ssm_scan_bwd", grid=(nt,),
        in_specs=[_full((1, n)), _full((1, n))] + [_rows_rev(ts, n, nt)] * 4 + [ANY] * np_,
        out_specs=[_rows_rev(ts, n, nt), _rows_rev(ts, n, nt), _full((1, n)), _full((1, n))] + [ANY] * np_,
        out_shape=[SDS((s, n), F32), SDS((s, n), F32), SDS((1, n), F32), SDS((1, n), F32)]
        + [p.out for p in pushes],
        scratch_shapes=_push_scratch(np_) + [pltpu.VMEM((2 * len(SCAN_STEPS) + 2, SUBLANES, n), F32),
                                             small, small, small, small],
        compiler_params=_seq_params(48),
    )(a_re, a_im, dx_re, dx_im, x_re, x_im, *[p.array for p in pushes])
    return out[0], out[1], out[2], out[3], out[4:]


def _mix_a(mixed, scale, za):
    return mixed * scale * _silu(za)


def _mix_b(val, gate, zb):
    return val * _sigmoid(gate) * _silu(zb)


def _even_mix(pooled, proj, x_re, x_im, x, pw, scale, cc_re, cc_im, d_skip, w_glu, w_out):
    s, d = x.shape
    half = pooled.shape[1]
    n = x_re.shape[1]
    ts = 256

    def body(pooled_ref, u_ref, z_ref, xre_ref, xim_ref, x_ref, pw_ref, scale_ref, ccre_ref, ccim_ref, d_ref,
             wglu_ref, wout_ref, x1_ref, ymix_ref, mixed_ref, yin_ref, gv_ref):
        for g in range(len(POOL_WINDOWS)):
            cols = slice(g * LANES, (g + 1) * LANES)
            mixed_ref[:, cols] = _dot(pooled_ref[:, cols], pw_ref[g])
        sb, cb = ccre_ref.shape[1:]
        for b in range(SSM_BLOCKS):
            states, chans = slice(b * sb, (b + 1) * sb), slice(b * cb, (b + 1) * cb)
            yin_ref[:, chans] = (_dot(xre_ref[:, states], ccre_ref[b]) - _dot(xim_ref[:, states], ccim_ref[b])
                                 + d_ref[:, chans] * u_ref[:, chans])
        yin = yin_ref[...]
        gv = _dot(yin, wglu_ref[...])
        gv_ref[...] = gv
        ya = _mix_a(mixed_ref[...], scale_ref[...], z_ref[:, :half])
        yb = _mix_b(gv[:, :half], gv[:, half:], z_ref[:, half:])
        ymix = jnp.concatenate([ya, yb], axis=1).astype(MM)
        ymix_ref[...] = ymix
        x1_ref[...] = x_ref[...] + jnp.dot(ymix, wout_ref[...], preferred_element_type=F32)

    return pl.pallas_call(
        body, name="even_mix", grid=(s // ts,),
        in_specs=[_rows(ts, half), _rows(ts, half, col=1), _rows(ts, d, col=1), _rows(ts, n), _rows(ts, n),
                  _rows(ts, d), _full(pw.shape), _full((1, half)), _full(cc_re.shape), _full(cc_im.shape),
                  _full((1, half)), _full((half, d)), _full((d, d))],
        out_specs=[_rows(ts, d), _rows(ts, d), _rows(ts, half), _rows(ts, half), _rows(ts, d)],
        out_shape=[SDS((s, d), F32), SDS((s, d), MM), SDS((s, half), F32), SDS((s, half), F32), SDS((s, d), F32)],
        compiler_params=_seq_params(56),
    )(pooled, proj, proj, x_re, x_im, x, pw, scale, cc_re, cc_im, d_skip, w_glu, w_out)


def _glu(val, gt):
    return val * _sigmoid(gt)


def _odd_in(x1, g1, w_in):
    s, d = x1.shape
    e = w_in.shape[1]
    ts = 256

    def body(x_ref, g_ref, w_ref, proj_ref, glu_ref, h_ref):
        hb = _rms(x_ref[...], g_ref[...]).astype(MM)
        h_ref[...] = hb
        proj = jnp.dot(hb, w_ref[...], preferred_element_type=F32)
        proj_ref[...] = proj
        glu_ref[...] = _glu(proj[:, :d], proj[:, d:2 * d])

    return pl.pallas_call(
        body, name="odd_in", grid=(s // ts,),
        in_specs=[_rows(ts, d), _full((1, d)), _full((d, e))],
        out_specs=[_rows(ts, e), _rows(ts, d), _rows(ts, d)],
        out_shape=[SDS((s, e), F32), SDS((s, d), F32), SDS((s, d), MM)],
        compiler_params=_seq_params(48),
    )(x1, g1, w_in)


CONV_ROWS = 32
CONV_COLS = 256


def _conv_scratch(ts, ch):
    return pltpu.VMEM((SUBLANES, ts + CONV_HALO, ch + LANES), F32)


def _phase_copies(sh, rows, ch):
    for o in range(1, SUBLANES):
        sh[o, 0:rows, 0:ch] = sh[0, pl.ds(o, rows), 0:ch]


def _conv_taps(sh, w_ref, offsets, r0, cols):
    acc = jnp.zeros((CONV_ROWS, cols.stop - cols.start), F32)
    for o in range(SUBLANES):
        taps = [(j, e // SUBLANES) for j, e in offsets if e % SUBLANES == o]
        if not taps:
            continue
        q0 = min(q for _, q in taps)
        q1 = max(q for _, q in taps)
        win = sh[o, pl.ds(r0 + SUBLANES * q0, CONV_ROWS + SUBLANES * (q1 - q0)), cols]
        for j, q in taps:
            lo = SUBLANES * (q - q0)
            acc = acc + w_ref[j:j + 1, cols] * win[lo:lo + CONV_ROWS]
    return acc


def _conv_fwd(g, w):
    s, ch = g.shape
    ts = 256
    per = ts // CONV_HALO
    lead = CONV_HALO - (CONV_KERNEL - 1)
    span = ts + CONV_HALO - SUBLANES

    def body(prev_ref, g_ref, w_ref, out_ref, sh):
        i = pl.program_id(0)
        sh[0, 0:CONV_HALO, 0:ch] = jnp.where(i == 0, 0.0, prev_ref[...])
        sh[0, CONV_HALO:, 0:ch] = g_ref[...]
        _phase_copies(sh, span, ch)

        offsets = [(j, lead + j) for j in range(CONV_KERNEL)]

        def block(rb, carry):
            r0 = pl.multiple_of(rb * CONV_ROWS, CONV_ROWS)
            for c in range(ch // CONV_COLS):
                cols = slice(c * CONV_COLS, (c + 1) * CONV_COLS)
                out_ref[pl.ds(r0, CONV_ROWS), cols] = _conv_taps(sh, w_ref, offsets, r0, cols)
            return carry

        lax.fori_loop(0, ts // CONV_ROWS, block, 0)

    return pl.pallas_call(
        body, name="conv_fwd", grid=(s // ts,),
        in_specs=[pl.BlockSpec((CONV_HALO, ch), lambda i: (jnp.maximum(i * per - 1, 0), 0)),
                  _rows(ts, ch), _full(w.shape)],
        out_specs=_rows(ts, ch),
        out_shape=SDS((s, ch), F32),
        scratch_shapes=[_conv_scratch(ts, ch)],
        compiler_params=_seq_params(40),
    )(g, g, w)


def _conv_bwd(dc, g, w):
    s, ch = dc.shape
    ts = 256
    per = ts // CONV_HALO
    n = s // ts
    lead = CONV_HALO - (CONV_KERNEL - 1)
    span = ts + CONV_HALO - SUBLANES

    def body(dc_ref, next_ref, prev_ref, g_ref, w_ref, dg_ref, dw_ref, shd, shg, wacc):
        i = pl.program_id(0)

        @pl.when(i == 0)
        def _():
            wacc[...] = jnp.zeros_like(wacc)

        shd[0, 0:ts, 0:ch] = dc_ref[...]
        shd[0, ts:, 0:ch] = jnp.where(i == n - 1, 0.0, next_ref[...])
        shg[0, 0:CONV_HALO, 0:ch] = jnp.where(i == 0, 0.0, prev_ref[...])
        shg[0, CONV_HALO:, 0:ch] = g_ref[...]
        _phase_copies(shd, span, ch)
        _phase_copies(shg, span, ch)

        offsets = [(j, CONV_KERNEL - 1 - j) for j in range(CONV_KERNEL)]

        def dg_block(rb, carry):
            r0 = pl.multiple_of(rb * CONV_ROWS, CONV_ROWS)
            for c in range(ch // CONV_COLS):
                cols = slice(c * CONV_COLS, (c + 1) * CONV_COLS)
                dg_ref[pl.ds(r0, CONV_ROWS), cols] = _conv_taps(shd, w_ref, offsets, r0, cols)
            return carry

        lax.fori_loop(0, ts // CONV_ROWS, dg_block, 0)

        tiles = ts // SUBLANES
        for c in range(ch // LANES):
            cols = slice(c * LANES, (c + 1) * LANES)
            cur = [shd[0, k * SUBLANES:(k + 1) * SUBLANES, cols] for k in range(tiles)]
            for j in range(CONV_KERNEL):
                q, o = divmod(lead + j, SUBLANES)
                parts = [None] * 4
                for k in range(tiles):
                    term = cur[k] * shg[o, (k + q) * SUBLANES:(k + q + 1) * SUBLANES, cols]
                    parts[k % 4] = term if parts[k % 4] is None else parts[k % 4] + term
                wacc[j, :, cols] += (parts[0] + parts[1]) + (parts[2] + parts[3])

        @pl.when(i == n - 1)
        def _():
            dw_ref[...] = jnp.zeros_like(dw_ref)
            for j in range(CONV_KERNEL):
                dw_ref[j:j + 1, :] = jnp.sum(wacc[j], axis=0, keepdims=True)

    return pl.pallas_call(
        body, name="conv_bwd", grid=(n,),
        in_specs=[_rows(ts, ch),
                  pl.BlockSpec((CONV_HALO, ch), lambda i: (jnp.minimum((i + 1) * per, s // CONV_HALO - 1), 0)),
                  pl.BlockSpec((CONV_HALO, ch), lambda i: (jnp.maximum(i * per - 1, 0), 0)),
                  _rows(ts, ch), _full(w.shape)],
        out_specs=[_rows(ts, ch), _full((CONV_HALO, ch))],
        out_shape=[SDS((s, ch), F32), SDS((CONV_HALO, ch), F32)],
        scratch_shapes=[_conv_scratch(ts, ch), _conv_scratch(ts, ch),
                        pltpu.VMEM((CONV_HALO, SUBLANES, ch), F32)],
        compiler_params=_seq_params(56),
    )(dc, dc, g, g, w)


def _conv_act(c, z1, cb, lg, lb):
    cc = c + cb
    mu = jnp.mean(cc, axis=-1, keepdims=True)
    dev = cc - mu
    var = jnp.mean(dev * dev, axis=-1, keepdims=True)
    cn = dev * lax.rsqrt(var + LN_EPS) * lg + lb
    return _silu(cn) * _silu(z1)


def _odd_out(c, proj1, x1, tgt, cb, lg, lb, w_out, fg):
    s, d = c.shape
    ts = 256

    def body(c_ref, z_ref, x1_ref, t_ref, cb_ref, lg_ref, lb_ref, w_ref, fg_ref,
             dc_ref, dz_ref, dx2_ref, y1_ref, loss_ref, dcb_ref, dlg_ref, dlb_ref, dfg_ref):
        @pl.when(pl.program_id(0) == 0)
        def _():
            for ref in (loss_ref, dcb_ref, dlg_ref, dlb_ref, dfg_ref):
                ref[...] = jnp.zeros_like(ref)

        y1, act_vjp = jax.vjp(_conv_act, c_ref[...], z_ref[...], cb_ref[...], lg_ref[...], lb_ref[...])
        y1b = y1.astype(MM)
        y1_ref[...] = y1b
        x2 = x1_ref[...] + jnp.dot(y1b, w_ref[...], preferred_element_type=F32)
        tgt_tile = t_ref[...]

        def head(x2, fg):
            err = jnp.square(_rms(x2, fg) - tgt_tile)
            return 0.5 * jnp.sum(jnp.mean(err, axis=-1))

        loss, (dx2, dfg) = jax.value_and_grad(head, argnums=(0, 1))(x2, fg_ref[...])
        loss_ref[...] += loss
        dfg_ref[...] += dfg
        dx2_ref[...] = dx2
        dy1 = _dot_nt(dx2, w_ref[...])
        dc, dz, dcb, dlg, dlb = act_vjp(dy1)
        dc_ref[...] = dc
        dz_ref[...] = dz
        dcb_ref[...] += dcb
        dlg_ref[...] += dlg
        dlb_ref[...] += dlb

    vec = SDS((1, d), F32)
    return pl.pallas_call(
        body, name="odd_out", grid=(s // ts,),
        in_specs=[_rows(ts, d), _rows(ts, d, col=2), _rows(ts, d), _rows(ts, d), _full((1, d)), _full((1, d)),
                  _full((1, d)), _full((d, d)), _full((1, d))],
        out_specs=[_rows(ts, d), _rows(ts, d), _rows(ts, d), _rows(ts, d), _full((SUBLANES, LANES)),
                   _full((1, d)), _full((1, d)), _full((1, d)), _full((1, d))],
        out_shape=[SDS((s, d), F32), SDS((s, d), F32), SDS((s, d), F32), SDS((s, d), MM),
                   SDS((SUBLANES, LANES), F32), vec, vec, vec, vec],
        compiler_params=_seq_params(56),
    )(c, proj1, x1, tgt, cb, lg, lb, w_out, fg)


def _odd_in_bwd(x1, proj1, dglu, dz1, dx2, g1, w_in):
    s, d = x1.shape
    e = w_in.shape[1]
    ts = 256

    def body(x_ref, vg_ref, dglu_ref, dz_ref, dx2_ref, g_ref, w_ref, dx1_ref, dp_ref, dg_ref):
        @pl.when(pl.program_id(0) == 0)
        def _():
            dg_ref[...] = jnp.zeros_like(dg_ref)

        _, glu_vjp = jax.vjp(_glu, vg_ref[:, :d], vg_ref[:, d:])
        dval, dgt = glu_vjp(dglu_ref[...])
        dp = jnp.concatenate([dval, dgt, dz_ref[...]], axis=1).astype(MM)
        dp_ref[...] = dp
        dh = _dot_nt(dp, w_ref[...])
        _, rms_vjp = jax.vjp(_rms, x_ref[...], g_ref[...])
        dxa, dg = rms_vjp(dh)
        dx1_ref[...] = dxa + dx2_ref[...]
        dg_ref[...] += dg

    return pl.pallas_call(
        body, name="odd_in_bwd", grid=(s // ts,),
        in_specs=[_rows(ts, d), _rows(ts, 2 * d), _rows(ts, d), _rows(ts, d), _rows(ts, d), _full((1, d)),
                  _full((d, e))],
        out_specs=[_rows(ts, d), _rows(ts, e), _full((1, d))],
        out_shape=[SDS((s, d), F32), SDS((s, e), MM), SDS((1, d), F32)],
        compiler_params=_seq_params(56),
    )(x1, proj1, dglu, dz1, dx2, g1, w_in)


def _even_mix_bwd(pooled, proj, mixed, gv, dx1, pw, scale, cc_re, cc_im, d_skip, w_glu, w_out, pushes):
    s, d = dx1.shape
    half = pooled.shape[1]
    n = cc_re.shape[0] * cc_re.shape[1]
    ts = 256
    nt = s // ts
    groups = len(POOL_WINDOWS)
    np_ = len(pushes)

    def body(*refs):
        ins, outs, _, push_refs = _split_refs(refs, 13, 10, np_)
        (pooled_ref, u_ref, z_ref, mixed_ref, gv_ref, dx1_ref, pw_ref, scale_ref, ccre_ref, ccim_ref, d_ref,
         wglu_ref, wout_ref) = ins
        (dpooled_ref, du_ref, dz_ref, dxre_ref, dxim_ref, dyin_ref, dgv_ref, dpw_ref, dscale_ref, dd_ref) = outs
        i = pl.program_id(0)
        _riding_start(pushes, i == 0, push_refs)

        @pl.when(i == 0)
        def _():
            for ref in (dpw_ref, dscale_ref, dd_ref):
                ref[...] = jnp.zeros_like(ref)

        dymix = _dot_nt(dx1_ref[...], wout_ref[...])
        _, a_vjp = jax.vjp(_mix_a, mixed_ref[...], scale_ref[...], z_ref[:, :half])
        dmixed, dscale, dza = a_vjp(dymix[:, :half])
        _, b_vjp = jax.vjp(_mix_b, gv_ref[:, :half], gv_ref[:, half:], z_ref[:, half:])
        dval, dgate, dzb = b_vjp(dymix[:, half:])
        dz_ref[:, :half] = dza
        dz_ref[:, half:] = dzb
        dscale_ref[...] += dscale
        dgv = jnp.concatenate([dval, dgate], axis=1).astype(MM)
        dgv_ref[...] = dgv
        dyin = _dot_nt(dgv, wglu_ref[...])
        dd_ref[...] += jnp.sum(dyin * u_ref[...], axis=0, keepdims=True)
        du_ref[...] = d_ref[...] * dyin
        dyb = dyin.astype(MM)
        dyin_ref[...] = dyb
        sb, cb = ccre_ref.shape[1:]
        for b in range(SSM_BLOCKS):
            states, chans = slice(b * sb, (b + 1) * sb), slice(b * cb, (b + 1) * cb)
            dxre_ref[:, states] = _dot_nt(dyb[:, chans], ccre_ref[b])
            dxim_ref[:, states] = -_dot_nt(dyb[:, chans], ccim_ref[b])
        for g in range(groups):
            cols = slice(g * LANES, (g + 1) * LANES)
            dm = dmixed[:, cols].astype(MM)
            dpooled_ref[:, cols] = _dot_nt(dm, pw_ref[g])
            dpw_ref[g] += _dot_tn(pooled_ref[:, cols], dm)

        _riding_wait(pushes, i == nt - 1, push_refs)

    out = pl.pallas_call(
        body, name="even_mix_bwd", grid=(nt,),
        in_specs=[_rows(ts, half), _rows(ts, half, col=1), _rows(ts, d, col=1), _rows(ts, half), _rows(ts, d),
                  _rows(ts, d), _full(pw.shape), _full((1, half)), _full(cc_re.shape), _full(cc_im.shape),
                  _full((1, half)), _full((half, d)), _full((d, d))] + [ANY] * np_,
        out_specs=[_rows(ts, half), _rows(ts, half), _rows(ts, d), _rows(ts, n), _rows(ts, n), _rows(ts, half),
                   _rows(ts, d), _full(pw.shape), _full((1, half)), _full((1, half))] + [ANY] * np_,
        out_shape=[SDS((s, half), F32), SDS((s, half), F32), SDS((s, d), F32), SDS((s, n), F32), SDS((s, n), F32),
                   SDS((s, half), MM), SDS((s, d), MM), SDS(pw.shape, F32), SDS((1, half), F32),
                   SDS((1, half), F32)] + [p.out for p in pushes],
        scratch_shapes=_push_scratch(np_),
        compiler_params=_seq_params(56),
    )(pooled, proj, proj, mixed, gv, dx1, pw, scale, cc_re, cc_im, d_skip, w_glu, w_out,
      *[p.array for p in pushes])
    return out[:10], out[10:]


def _ssm_bu_bwd(g_re, g_im, du_skip, bb):
    s, n = g_re.shape
    nb, cb, two_nb = bb.shape
    sb = two_nb // 2
    cin = nb * cb
    ts = 512

    def body(gre_ref, gim_ref, du_ref, bb_ref, out_ref):
        for b in range(nb):
            states, chans = slice(b * sb, (b + 1) * sb), slice(b * cb, (b + 1) * cb)
            out_ref[:, chans] = (du_ref[:, chans] + _dot_nt(gre_ref[:, states], bb_ref[b, :, :sb])
                                 + _dot_nt(gim_ref[:, states], bb_ref[b, :, sb:]))

    return pl.pallas_call(
        body, name="ssm_bu_bwd", grid=(s // ts,),
        in_specs=[_rows(ts, n), _rows(ts, n), _rows(ts, cin), _full(bb.shape)],
        out_specs=_rows(ts, cin),
        out_shape=SDS((s, cin), F32),
        compiler_params=_seq_params(48),
    )(g_re, g_im, du_skip, bb)


def _even_in_bwd(x, du_pool, du_ssm, dz, dx1, g0, w_in):
    s, d = x.shape
    half = du_pool.shape[1]
    shards, _, wc = w_in.shape
    ts = 256

    def body(x_ref, dup_ref, dus_ref, dz_ref, dx1_ref, g_ref, w_ref, gx_ref, dg_ref):
        @pl.when(pl.program_id(0) == 0)
        def _():
            dg_ref[...] = jnp.zeros_like(dg_ref)

        dp = jnp.concatenate([dup_ref[...], dus_ref[...], dz_ref[...]], axis=1).astype(MM)
        dh = _dot_nt(dp[:, :wc], w_ref[0])
        for k in range(1, shards):
            dh = dh + _dot_nt(dp[:, k * wc:(k + 1) * wc], w_ref[k])
        _, rms_vjp = jax.vjp(_rms, x_ref[...], g_ref[...])
        dxa, dg = rms_vjp(dh)
        gx_ref[...] = dxa + dx1_ref[...]
        dg_ref[...] += dg

    return pl.pallas_call(
        body, name="even_in_bwd", grid=(s // ts,),
        in_specs=[_rows(ts, d), _rows(ts, half), _rows(ts, half), _rows(ts, d), _rows(ts, d), _full((1, d)),
                  _full(w_in.shape)],
        out_specs=[_rows(ts, d), _full((1, d))],
        out_shape=[SDS((s, d), F32), SDS((1, d), F32)],
        compiler_params=_seq_params(48),
    )(x, du_pool, du_ssm, dz, dx1, g0, w_in)


def _discretise(log_dt, ar, ai, br, bi):
    dt = jnp.exp(log_dt)
    mag = jnp.exp(ar * dt)
    ang = ai * dt
    abr = mag * jnp.cos(ang)
    abi = mag * jnp.sin(ang)
    den = ar * ar + ai * ai
    nr = abr - 1.0
    ni = abi
    kr = (nr * ar + ni * ai) / den
    ki = (ni * ar - nr * ai) / den
    bbr = kr[None] * br - ki[None] * bi
    bbi = kr[None] * bi + ki[None] * br
    return abr, abi, bbr, bbi


def _whole(n):
    return [pl.BlockSpec(memory_space=pltpu.VMEM)] * n


def _disc_fwd(log_dt, ar, ai, br, bi):
    def body(ld_ref, ar_ref, ai_ref, br_ref, bi_ref, abr_ref, abi_ref, bbr_ref, bbi_ref):
        out = _discretise(ld_ref[...], ar_ref[...], ai_ref[...], br_ref[...], bi_ref[...])
        for ref, val in zip((abr_ref, abi_ref, bbr_ref, bbi_ref), out):
            ref[...] = val

    return pl.pallas_call(
        body, name="ssm_discretise", in_specs=_whole(5), out_specs=_whole(4),
        out_shape=[SDS(ar.shape, F32), SDS(ar.shape, F32), SDS(br.shape, F32), SDS(br.shape, F32)],
    )(log_dt, ar, ai, br, bi)


def _disc_bwd(log_dt, ar, ai, br, bi, dabr, dabi, dbbr, dbbi):
    def body(ld_ref, ar_ref, ai_ref, br_ref, bi_ref, dabr_ref, dabi_ref, dbbr_ref, dbbi_ref,
             dld_ref, dar_ref, dai_ref, dbr_ref, dbi_ref):
        _, vjp = jax.vjp(_discretise, ld_ref[...], ar_ref[...], ai_ref[...], br_ref[...], bi_ref[...])
        grads = vjp((dabr_ref[...], dabi_ref[...], dbbr_ref[...], dbbi_ref[...]))
        for ref, val in zip((dld_ref, dar_ref, dai_ref, dbr_ref, dbi_ref), grads):
            ref[...] = val

    return pl.pallas_call(
        body, name="ssm_discretise_bwd", in_specs=_whole(9), out_specs=_whole(5),
        out_shape=[SDS(log_dt.shape, F32), SDS(ar.shape, F32), SDS(ar.shape, F32), SDS(br.shape, F32),
                   SDS(br.shape, F32)],
    )(log_dt, ar, ai, br, bi, dabr, dabi, dbbr, dbbi)


def _block_diag(t):
    g, a, b = t.shape
    per = g // SSM_BLOCKS
    t = t.reshape(SSM_BLOCKS, per, a, b)
    same = jnp.eye(per, dtype=bool)[None, :, None, :, None]
    return jnp.where(same, t[:, :, :, None, :], 0.0).reshape(SSM_BLOCKS, per * a, per * b)


def _diag_blocks(m, g):
    per = g // SSM_BLOCKS
    a, b = m.shape[1] // per, m.shape[2] // per
    d = jnp.diagonal(m.reshape(SSM_BLOCKS, per, a, per, b), axis1=1, axis2=3)
    return jnp.moveaxis(d, -1, 1).reshape(g, a, b)


def _adamw(name, parts, w, m, v):
    rows, cols = w.shape
    tr = ADAM_ROWS if rows % ADAM_ROWS == 0 else rows

    def body(p_ref, w_ref, m_ref, v_ref, g_ref, d_ref, nm_ref, nv_ref):
        g = p_ref[0]
        for dev in range(1, N_DEV):
            g = g + p_ref[dev]
        g_ref[...] = g
        nm = ADAM_B1 * m_ref[...] + (1.0 - ADAM_B1) * g
        nv = ADAM_B2 * v_ref[...] + (1.0 - ADAM_B2) * jnp.square(g)
        nm_ref[...] = nm
        nv_ref[...] = nv
        m_hat = nm / (1.0 - ADAM_B1 ** ADAM_STEP)
        v_hat = nv / (1.0 - ADAM_B2 ** ADAM_STEP)
        d_ref[...] = -ADAM_LR * (m_hat / (jnp.sqrt(v_hat) + ADAM_EPS) + ADAM_WD * w_ref[...])

    tile = _rows(tr, cols)
    out = SDS((rows, cols), F32)
    return pl.pallas_call(
        body, name=name, grid=(rows // tr,),
        in_specs=[pl.BlockSpec((N_DEV, tr, cols), lambda i: (0, i, 0)), tile, tile, tile],
        out_specs=[tile, tile, tile, tile],
        out_shape=[out, out, out, out],
        compiler_params=_seq_params(32),
    )(parts, w, m, v)


def _as_rows(flat):
    n = flat.shape[-1]
    padded = _round_up(n, PACK_TILE)
    if padded != n:
        flat = jnp.pad(flat, [(0, 0)] * (flat.ndim - 1) + [(0, padded - n)])
    return flat.reshape(flat.shape[:-1] + (padded // LANES, LANES))


def _columns_to_shards(full):
    r, c8 = full.shape
    return jnp.transpose(full.reshape(r, N_DEV, c8 // N_DEV), (1, 0, 2))


def _shards_to_columns(shards):
    n, r, c = shards.shape
    return jnp.transpose(shards, (1, 0, 2)).reshape(r, n * c)


def _pack_local(tensors):
    return jnp.concatenate([_as_rows(t.reshape(-1)) for t in tensors], axis=0)


def _unpack_local(packed, shapes):
    out, r = [], 0
    for shape in shapes:
        n = 1
        for dim in shape:
            n *= dim
        rows = _round_up(n, PACK_TILE) // LANES
        out.append(packed[r:r + rows].reshape(-1)[:n].reshape(shape))
        r += rows
    return out


def kernel(x, even_norm, even_w_in, pool_w, pool_scale, ssm_log_dt, ssm_a_re, ssm_a_im, ssm_b_re, ssm_b_im, ssm_c_re, ssm_c_im, ssm_d, ssm_w_glu, even_w_out, odd_norm, odd_w_in, conv_w, conv_b, conv_ln_g, conv_ln_b, odd_w_out, final_norm, loss_target, m_even_norm, m_even_w_in, m_pool_w, m_pool_scale, m_ssm_log_dt, m_ssm_a_re, m_ssm_a_im, m_ssm_b_re, m_ssm_b_im, m_ssm_c_re, m_ssm_c_im, m_ssm_d, m_ssm_w_glu, m_even_w_out, m_odd_norm, m_odd_w_in, m_conv_w, m_conv_b, m_conv_ln_g, m_conv_ln_b, m_odd_w_out, m_final_norm, v_even_norm, v_even_w_in, v_pool_w, v_pool_scale, v_ssm_log_dt, v_ssm_a_re, v_ssm_a_im, v_ssm_b_re, v_ssm_b_im, v_ssm_c_re, v_ssm_c_im, v_ssm_d, v_ssm_w_glu, v_even_w_out, v_odd_norm, v_odd_w_in, v_conv_w, v_conv_b, v_conv_ln_g, v_conv_ln_b, v_odd_w_out, v_final_norm):
    given = dict(locals())
    xs = x[0]
    tgt = loss_target[0]
    d_model = xs.shape[1]

    def local(prefix, name):
        t = given[prefix + name]
        return t if name == 'final_norm' else t[0]

    def small_block(prefix):
        parts = [jnp.pad(local(prefix, 'conv_w'), ((0, SMALL_AT['odd_norm'] - CONV_KERNEL), (0, 0)))]
        for n in SMALL_VECTORS:
            parts.append(jnp.pad(local(prefix, n).reshape(1, LANES), ((0, SUBLANES - 1), (0, 0))))
        return jnp.concatenate(parts, axis=0)

    def as_tile(row):
        return jnp.pad(row, ((0, SUBLANES - row.shape[0]), (0, 0)))

    (w_in_e,) = _push_alone("gather_first", [_Push(local('', 'even_w_in').astype(MM), False)])
    later = [_Push(local('', 'ssm_w_glu').astype(MM), False), _Push(local('', 'even_w_out').astype(MM), False),
             _Push(local('', 'odd_w_in').astype(MM), False), _Push(local('', 'odd_w_out').astype(MM), False),
             _Push(small_block(''), False)]

    a_re, a_im = local('', 'ssm_a_re'), local('', 'ssm_a_im')
    groups, state = a_re.shape
    log_dt = local('', 'ssm_log_dt').reshape(groups, 1)
    b_re_t = jnp.transpose(local('', 'ssm_b_re'), (2, 0, 1))
    b_im_t = jnp.transpose(local('', 'ssm_b_im'), (2, 0, 1))
    abr, abi, bbr, bbi = _disc_fwd(log_dt, a_re, a_im, b_re_t, b_im_t)
    a_re_row = abr.reshape(1, groups * state)
    a_im_row = abi.reshape(1, groups * state)
    bb = jnp.concatenate([_block_diag(jnp.transpose(bbr, (1, 0, 2))), _block_diag(jnp.transpose(bbi, (1, 0, 2)))],
                         axis=2).astype(MM)
    cc_re = _block_diag(jnp.transpose(local('', 'ssm_c_re'), (0, 2, 1))).astype(MM)
    cc_im = _block_diag(jnp.transpose(local('', 'ssm_c_im'), (0, 2, 1))).astype(MM)
    pw = local('', 'pool_w').astype(MM)
    g0 = local('', 'even_norm').reshape(1, d_model)
    fg = local('', 'final_norm').reshape(1, d_model)
    scale = local('', 'pool_scale').reshape(1, -1)
    d_skip = local('', 'ssm_d').reshape(1, -1)

    proj0, h0 = _even_in(xs, g0, w_in_e)
    pooled = _pool_fwd(proj0)
    bu_re, bu_im = _ssm_bu(proj0, bb)
    x_re, x_im, (wg3, wo_e3, wi_o3, wo_o3, small8) = _scan_fwd(a_re_row, a_im_row, bu_re, bu_im, later)
    w_glu = _shards_to_columns(wg3)
    w_out_e = wo_e3.reshape(d_model, d_model)
    w_in_o = _shards_to_columns(wi_o3)
    w_out_o = wo_o3.reshape(d_model, d_model)
    small = _shards_to_columns(small8)
    cw = small[:CONV_KERNEL]
    g1, cb, lg, lb = (small[SMALL_AT[n]:SMALL_AT[n] + 1] for n in SMALL_VECTORS)
    x1, ymix, mixed, yin, gv = _even_mix(pooled, proj0, x_re, x_im, xs, pw, scale, cc_re, cc_im, d_skip, w_glu,
                                         w_out_e)
    proj1, glu, h1 = _odd_in(x1, g1, w_in_o)
    conv = _conv_fwd(glu, cw)

    half = pooled.shape[1]
    n_state = groups * state
    rows_per = d_model // N_DEV
    dc, dz1, dx2, y1, loss_tile, dcb, dlg, dlb, dfg = _odd_out(conv, proj1, x1, tgt, cb, lg, lb, w_out_o, fg)
    dglu, dcw = _conv_bwd(dc, glu, cw)
    dx1, dproj1, dg1 = _odd_in_bwd(x1, proj1, dglu, dz1, dx2, g1, w_in_o)
    g_odd_out = _mm_tn("dw_odd_out", y1, dx2, d_model, d_model).reshape(N_DEV, rows_per, d_model)
    g_odd_in = _columns_to_shards(_mm_tn("dw_odd_in", h1, dproj1, d_model, 3 * d_model))
    g_even_out = _mm_tn("dw_even_out", ymix, dx1, d_model, d_model).reshape(N_DEV, rows_per, d_model)
    mix_grads, (r_odd_in,) = _even_mix_bwd(pooled, proj0, mixed, gv, dx1, pw, scale, cc_re, cc_im, d_skip, w_glu,
                                           w_out_e, [_Push(g_odd_in, True)])
    dpooled, du_skip, dz0, dx_re, dx_im, dyin, dgv, dpw, dscale, dd = mix_grads
    g_glu = _columns_to_shards(_mm_tn("dw_glu", yin, dgv, half, d_model))
    sblk, cblk = n_state // SSM_BLOCKS, half // SSM_BLOCKS
    dcc_re = _mm_tn_blocks("dw_c_re", x_re, dyin, sblk, cblk, SSM_BLOCKS, ts=2048)
    dcc_im = _mm_tn_blocks("dw_c_im", x_im, dyin, sblk, cblk, SSM_BLOCKS, sign=-1.0, ts=2048)
    dc_re = jnp.transpose(_diag_blocks(dcc_re, groups), (0, 2, 1))
    dc_im = jnp.transpose(_diag_blocks(dcc_im, groups), (0, 2, 1))
    g_small = _columns_to_shards(jnp.concatenate(
        [dcw, as_tile(dg1), as_tile(dcb), as_tile(dlg), as_tile(dlb)], axis=0))
    early = {'pool_w': dpw, 'pool_scale': dscale, 'ssm_c_re': dc_re, 'ssm_c_im': dc_im, 'ssm_d': dd,
             'final_norm': dfg}
    g_early = _pack_local([early[n] for n in EARLY_REPLICATED] + [loss_tile])
    g_re, g_im, dabr, dabi, (r_odd_out, r_even_out, r_glu, r_small, r_early) = _scan_bwd(
        a_re_row, a_im_row, dx_re, dx_im, x_re, x_im,
        [_Push(g_odd_out, True), _Push(g_even_out, True), _Push(g_glu, True), _Push(g_small, True),
         _Push(g_early, False)])
    du_ssm = _ssm_bu_bwd(g_re, g_im, du_skip, bb)
    du_pool = _pool_bwd(dpooled)
    grad_x, dg0 = _even_in_bwd(xs, du_pool, du_ssm, dz0, dx1, g0, w_in_e)
    g_even_in = _columns_to_shards(jnp.concatenate(
        [_mm_tn("dw_even_in_pool", h0, du_pool, d_model, half), _mm_tn("dw_even_in_ssm", h0, du_ssm, d_model, half),
         _mm_tn("dw_even_in_gate", h0, dz0, d_model, d_model)], axis=1))
    g_mid = _pack_local([dg0])
    dbb_re, dbb_im, (r_even_in, r_mid) = _dw_bbar(proj0, g_re, g_im, cblk, sblk,
                                                  [_Push(g_even_in, True), _Push(g_mid, False)])
    dbbr = jnp.transpose(_diag_blocks(dbb_re, groups), (1, 0, 2))
    dbbi = jnp.transpose(_diag_blocks(dbb_im, groups), (1, 0, 2))
    dld, dar, dai, dbr, dbi = _disc_bwd(log_dt, a_re, a_im, b_re_t, b_im_t,
                                        dabr.reshape(groups, state), dabi.reshape(groups, state), dbbr, dbbi)
    late = {'ssm_log_dt': dld, 'ssm_a_re': dar, 'ssm_a_im': dai,
            'ssm_b_re': jnp.transpose(dbr, (1, 2, 0)), 'ssm_b_im': jnp.transpose(dbi, (1, 2, 0))}
    g_late = _pack_local([late[n] for n in LATE_REPLICATED])
    (r_late,) = _push_alone("exchange_tail", [_Push(g_late, False)])

    results = {}
    for n, parts in (('even_w_in', r_even_in), ('ssm_w_glu', r_glu), ('even_w_out', r_even_out),
                     ('odd_w_in', r_odd_in), ('odd_w_out', r_odd_out)):
        results[n] = _adamw("adamw_" + n, parts, local('', n), local('m_', n), local('v_', n))
    small_out = _adamw("adamw_small", r_small, small_block(''), small_block('m_'), small_block('v_'))
    results['conv_w'] = [o[:CONV_KERNEL] for o in small_out]
    for n in SMALL_VECTORS:
        results[n] = [o[SMALL_AT[n]] for o in small_out]
    loss_rows = [jnp.zeros_like(loss_tile)]
    for names, parts, extra in ((EARLY_REPLICATED, r_early, loss_rows), (MID_REPLICATED, r_mid, []),
                               (LATE_REPLICATED, r_late, [])):
        packed = [_pack_local([local(p, n) for n in names] + extra) for p in ('', 'm_', 'v_')]
        out = _adamw("adamw_" + names[0], parts, *packed)
        unpacked = [_unpack_local(o, [given[n].shape for n in names]) for o in out]
        for k, n in enumerate(names):
            results[n] = [u[k] for u in unpacked]
        if extra:
            loss = out[0][-SUBLANES, 0]

    outs = [loss, grad_x[None]]
    for kind in range(4):
        outs.extend(results[n][kind].reshape(given[n].shape) for n in WEIGHTS)
    return tuple(outs)
```

```python
import functools

import jax
import jax.numpy as jnp
from jax import lax
from jax.experimental import pallas as pl
from jax.experimental.pallas import tpu as pltpu

F32 = jnp.float32
MM = jnp.bfloat16
SDS = jax.ShapeDtypeStruct

RMS_EPS = 1e-6
LN_EPS = 1e-5
POOL_WINDOWS = (2, 4, 8, 16)
POOL_HALO = 16
CONV_KERNEL = 31
CONV_HALO = 32
N_DEV = 8
LANES = 128
SUBLANES = 8
PACK_TILE = SUBLANES * LANES
ADAM_ROWS = 128
MIB = 1024 * 1024

ADAM_LR = 0.001
ADAM_B1 = 0.9
ADAM_B2 = 0.999
ADAM_EPS = 1e-08
ADAM_WD = 0.01
ADAM_STEP = 10

WEIGHTS = ['even_norm', 'even_w_in', 'pool_w', 'pool_scale', 'ssm_log_dt', 'ssm_a_re', 'ssm_a_im', 'ssm_b_re',
           'ssm_b_im', 'ssm_c_re', 'ssm_c_im', 'ssm_d', 'ssm_w_glu', 'even_w_out', 'odd_norm', 'odd_w_in', 'conv_w',
           'conv_b', 'conv_ln_g', 'conv_ln_b', 'odd_w_out', 'final_norm']
SMALL_VECTORS = ('odd_norm', 'conv_b', 'conv_ln_g', 'conv_ln_b')
SMALL_AT = {'odd_norm': 32, 'conv_b': 40, 'conv_ln_g': 48, 'conv_ln_b': 56}
EARLY_REPLICATED = ['pool_w', 'pool_scale', 'ssm_c_re', 'ssm_c_im', 'ssm_d', 'final_norm']
MID_REPLICATED = ['even_norm']
LATE_REPLICATED = ['ssm_log_dt', 'ssm_a_re', 'ssm_a_im', 'ssm_b_re', 'ssm_b_im']


def _round_up(n, m):
    return (n + m - 1) // m * m


def _sigmoid(x):
    return jax.nn.sigmoid(x)


def _silu(x):
    return x * jax.nn.sigmoid(x)


def _rms(x, g):
    return x * lax.rsqrt(jnp.mean(x * x, axis=-1, keepdims=True) + RMS_EPS) * g


def _dot(a, b):
    return jnp.dot(a.astype(MM), b.astype(MM), preferred_element_type=F32)


def _dot_nt(a, b):
    return lax.dot_general(a.astype(MM), b.astype(MM), (((1,), (1,)), ((), ())), preferred_element_type=F32)


def _dot_tn(a, b):
    return lax.dot_general(a.astype(MM), b.astype(MM), (((0,), (0,)), ((), ())), preferred_element_type=F32)


def _rows(ts, width, col=0):
    return pl.BlockSpec((ts, width), lambda i: (i, col))


def _rows_rev(ts, width, n, col=0):
    return pl.BlockSpec((ts, width), lambda i: (n - 1 - i, col))


def _full(shape):
    zeros = (0,) * len(shape)
    return pl.BlockSpec(shape, lambda i: zeros)


def _seq_params(vmem_mib=48, dims=1):
    return pltpu.CompilerParams(dimension_semantics=("arbitrary",) * dims, vmem_limit_bytes=vmem_mib * MIB)


def _mesh_position():
    x, y, c = lax.axis_index("x"), lax.axis_index("y"), lax.axis_index("c")
    return x, y, c


def _peer(pos, relation):
    x, y, c = pos
    px = 1 - x if relation & 4 else x
    py = 1 - y if relation & 2 else y
    pc = 1 - c if relation & 1 else c
    return px, py, pc


ANY = pl.BlockSpec(memory_space=pl.ANY)


class _Push:
    def __init__(self, array, scatter):
        self.array = array
        self.scatter = scatter
        self.out = SDS(array.shape if scatter else (N_DEV,) + array.shape, array.dtype)


def _push_scratch(n):
    return [pltpu.SemaphoreType.DMA((n * (N_DEV - 1),)), pltpu.SemaphoreType.DMA((n * (N_DEV - 1),)),
            pltpu.SemaphoreType.DMA((n,))]


def _push_copies(pushes, srcs, dsts, send_sems, recv_sems, local_sems):
    pos = _mesh_position()
    me = 4 * pos[0] + 2 * pos[1] + pos[2]
    copies = []
    for a, (push, src, dst) in enumerate(zip(pushes, srcs, dsts)):
        copies.append(pltpu.make_async_copy(src.at[me] if push.scatter else src, dst.at[me], local_sems.at[a]))
        for relation in range(1, N_DEV):
            peer = _peer(pos, relation)
            peer_id = 4 * peer[0] + 2 * peer[1] + peer[2]
            k = a * (N_DEV - 1) + relation - 1
            copies.append(pltpu.make_async_remote_copy(
                src_ref=src.at[peer_id] if push.scatter else src, dst_ref=dst.at[me],
                send_sem=send_sems.at[k], recv_sem=recv_sems.at[k], device_id=peer,
                device_id_type=pl.DeviceIdType.MESH))
    return copies


def _push_alone(name, pushes):
    n = len(pushes)

    def body(*refs):
        copies = _push_copies(pushes, refs[:n], refs[n:2 * n], *refs[2 * n:])
        for cp in copies:
            cp.start()
        for cp in copies:
            cp.wait()

    return pl.pallas_call(
        body, name=name, out_shape=[p.out for p in pushes], in_specs=[ANY] * n, out_specs=[ANY] * n,
        scratch_shapes=_push_scratch(n),
    )(*[p.array for p in pushes])


def _riding_start(pushes, first, refs):
    n = len(pushes)

    @pl.when(first)
    def _():
        for cp in _push_copies(pushes, refs[:n], refs[n:2 * n], *refs[2 * n:]):
            cp.start()


def _riding_wait(pushes, last, refs):
    n = len(pushes)

    @pl.when(last)
    def _():
        for cp in _push_copies(pushes, refs[:n], refs[n:2 * n], *refs[2 * n:]):
            cp.wait()


def _split_refs(refs, n_in, n_out, n_push):
    ins = refs[:n_in]
    srcs = refs[n_in:n_in + n_push]
    o0 = n_in + n_push
    outs = refs[o0:o0 + n_out]
    dsts = refs[o0 + n_out:o0 + n_out + n_push]
    s0 = o0 + n_out + n_push
    sems = refs[s0:s0 + 3]
    scratch = refs[s0 + 3:]
    return ins, outs, scratch, tuple(srcs) + tuple(dsts) + tuple(sems)


def _mm_tn(name, a, b, m, n, a_col=0, b_col=0, ts=512):
    s = a.shape[0]
    tn = n
    nj = n // tn

    def body(a_ref, b_ref, o_ref):
        @pl.when(pl.program_id(1) == 0)
        def _():
            o_ref[...] = jnp.zeros_like(o_ref)

        o_ref[...] += _dot_tn(a_ref[...], b_ref[...])

    return pl.pallas_call(
        body, name=name, grid=(nj, s // ts),
        in_specs=[pl.BlockSpec((ts, m), lambda j, k: (k, a_col)),
                  pl.BlockSpec((ts, tn), lambda j, k: (k, b_col * nj + j))],
        out_specs=pl.BlockSpec((m, tn), lambda j, k: (0, j)),
        out_shape=SDS((m, n), F32),
        compiler_params=_seq_params(56, dims=2),
    )(a, b)


def _mm_tn_blocks(name, a, b, m, n, nb, a_col=0, a_step=1, b_col=0, sign=1.0, ts=512):
    s = a.shape[0]

    def body(a_ref, b_ref, o_ref):
        @pl.when(pl.program_id(1) == 0)
        def _():
            o_ref[...] = jnp.zeros_like(o_ref)

        o_ref[...] += sign * _dot_tn(a_ref[...], b_ref[...])

    return pl.pallas_call(
        body, name=name, grid=(nb, s // ts),
        in_specs=[pl.BlockSpec((ts, m), lambda j, k: (k, a_col + a_step * j)),
                  pl.BlockSpec((ts, n), lambda j, k: (k, b_col + j))],
        out_specs=pl.BlockSpec((None, m, n), lambda j, k: (j, 0, 0)),
        out_shape=SDS((nb, m, n), F32),
        compiler_params=_seq_params(48, dims=2),
    )(a, b)


def _dw_bbar(proj, g_re, g_im, cb, sb, pushes, ts=2048):
    s = proj.shape[0]
    nb, nk = SSM_BLOCKS, s // ts
    np_ = len(pushes)

    def body(*refs):
        ins, outs, _, push_refs = _split_refs(refs, 3, 2, np_)
        u_ref, gre_ref, gim_ref = ins
        ore_ref, oim_ref = outs
        j, k = pl.program_id(0), pl.program_id(1)
        _riding_start(pushes, jnp.logical_and(j == 0, k == 0), push_refs)

        @pl.when(k == 0)
        def _():
            ore_ref[...] = jnp.zeros_like(ore_ref)
            oim_ref[...] = jnp.zeros_like(oim_ref)

        u = u_ref[...].astype(MM)
        ore_ref[...] += _dot_tn(gre_ref[...], u)
        oim_ref[...] += _dot_tn(gim_ref[...], u)
        _riding_wait(pushes, jnp.logical_and(j == nb - 1, k == nk - 1), push_refs)

    wide = pl.BlockSpec((ts, sb), lambda j, k: (k, j))
    out = pl.BlockSpec((None, sb, cb), lambda j, k: (j, 0, 0))
    res = pl.pallas_call(
        body, name="dw_bbar", grid=(nb, nk),
        in_specs=[pl.BlockSpec((ts, cb), lambda j, k: (k, nb + j)), wide, wide] + [ANY] * np_,
        out_specs=[out, out] + [ANY] * np_,
        out_shape=[SDS((nb, sb, cb), F32), SDS((nb, sb, cb), F32)] + [p.out for p in pushes],
        scratch_shapes=_push_scratch(np_),
        compiler_params=_seq_params(48, dims=2),
    )(proj, g_re, g_im, *[p.array for p in pushes])
    return res[0], res[1], res[2:]


def _even_in(x, g0, w_in):
    s, d = x.shape
    shards, _, wc = w_in.shape
    e = shards * wc
    ts = 512

    def body(x_ref, g_ref, w_ref, proj_ref, h_ref):
        hb = _rms(x_ref[...], g_ref[...]).astype(MM)
        h_ref[...] = hb
        for k in range(shards):
            proj_ref[:, k * wc:(k + 1) * wc] = jnp.dot(hb, w_ref[k], preferred_element_type=F32)

    return pl.pallas_call(
        body, name="even_in", grid=(s // ts,),
        in_specs=[_rows(ts, d), _full((1, d)), _full(w_in.shape)],
        out_specs=[_rows(ts, e), _rows(ts, d)],
        out_shape=[SDS((s, e), F32), SDS((s, d), MM)],
        compiler_params=_seq_params(48),
    )(x, g0, w_in)


def _pool_counts(t0, ts, w):
    pos = (t0 + lax.broadcasted_iota(jnp.int32, (ts, LANES), 0) + 1).astype(F32)
    return jnp.minimum(pos, float(w))


def _pool_fwd(proj):
    s = proj.shape[0]
    width = LANES * len(POOL_WINDOWS)
    ts = 512
    per = ts // POOL_HALO

    def body(prev_ref, u_ref, out_ref, ext):
        i = pl.program_id(0)
        ext[0:POOL_HALO, :] = jnp.where(i == 0, 0.0, prev_ref[...])
        ext[POOL_HALO:, :] = u_ref[...]
        for g, w in enumerate(POOL_WINDOWS):
            cols = slice(g * LANES, (g + 1) * LANES)
            tok = ext[pl.ds(POOL_HALO, ts), cols]
            acc = tok
            for k in range(1, w):
                acc = acc + ext[pl.ds(POOL_HALO - k, ts), cols]
            out_ref[:, cols] = acc / _pool_counts(i * ts, ts, w) - tok

    return pl.pallas_call(
        body, name="pool_fwd", grid=(s // ts,),
        in_specs=[pl.BlockSpec((POOL_HALO, width), lambda i: (jnp.maximum(i * per - 1, 0), 0)),
                  _rows(ts, width)],
        out_specs=_rows(ts, width),
        out_shape=SDS((s, width), F32),
        scratch_shapes=[pltpu.VMEM((ts + POOL_HALO, width), F32)],
        compiler_params=_seq_params(32),
    )(proj, proj)


def _pool_bwd(dp):
    s, width = dp.shape
    ts = 512
    per = ts // POOL_HALO
    n = s // ts

    def body(dp_ref, next_ref, out_ref, ext):
        i = pl.program_id(0)
        nxt = jnp.where(i == n - 1, 0.0, next_ref[...])
        for g, w in enumerate(POOL_WINDOWS):
            cols = slice(g * LANES, (g + 1) * LANES)
            cur = dp_ref[:, cols]
            ext[0:ts, cols] = cur / _pool_counts(i * ts, ts, w)
            ext[ts:, cols] = nxt[:, cols] / _pool_counts((i + 1) * ts, POOL_HALO, w)
            acc = -cur
            for k in range(w):
                acc = acc + ext[pl.ds(k, ts), cols]
            out_ref[:, cols] = acc

    return pl.pallas_call(
        body, name="pool_bwd", grid=(n,),
        in_specs=[_rows(ts, width),
                  pl.BlockSpec((POOL_HALO, width), lambda i: (jnp.minimum((i + 1) * per, s // POOL_HALO - 1), 0))],
        out_specs=_rows(ts, width),
        out_shape=SDS((s, width), F32),
        scratch_shapes=[pltpu.VMEM((ts + POOL_HALO, width), F32)],
        compiler_params=_seq_params(32),
    )(dp, dp)


SSM_BLOCKS = 4


def _ssm_bu(proj, bb):
    s = proj.shape[0]
    nb, cb, two_nb = bb.shape
    sb = two_nb // 2
    cin, n = nb * cb, nb * sb
    ts = 512

    def body(u_ref, bb_ref, re_ref, im_ref):
        for b in range(nb):
            bu = _dot(u_ref[:, b * cb:(b + 1) * cb], bb_ref[b])
            re_ref[:, b * sb:(b + 1) * sb] = bu[:, :sb]
            im_ref[:, b * sb:(b + 1) * sb] = bu[:, sb:]

    return pl.pallas_call(
        body, name="ssm_bu", grid=(s // ts,),
        in_specs=[_rows(ts, cin, col=1), _full(bb.shape)],
        out_specs=[_rows(ts, n), _rows(ts, n)],
        out_shape=[SDS((s, n), F32), SDS((s, n), F32)],
        compiler_params=_seq_params(48),
    )(proj, bb)


SCAN_LANES = 256


def _cmul(a, b):
    return a[0] * b[0] - a[1] * b[1], a[0] * b[1] + a[1] * b[0]


SCAN_STEPS = (1, 2, 4)


def _fill_scan_tables(tab, ar_row, ai_row, reverse):
    shape = (SUBLANES, ar_row.shape[1])
    a1 = (jnp.broadcast_to(ar_row, shape), jnp.broadcast_to(ai_row, shape))
    a2 = _cmul(a1, a1)
    a4 = _cmul(a2, a2)
    a8 = _cmul(a4, a4)
    row = lax.broadcasted_iota(jnp.int32, shape, 0)
    for idx, (k, q) in enumerate(zip(SCAN_STEPS, (a1, a2, a4))):
        keep = (row < SUBLANES - k) if reverse else (row >= k)
        tab[2 * idx] = jnp.where(keep, q[0], 0.0)
        tab[2 * idx + 1] = jnp.where(keep, q[1], 0.0)
    expo = (SUBLANES - row) if reverse else (row + 1)
    pr, pi = jnp.ones(shape, F32), jnp.zeros(shape, F32)
    for bit, q in enumerate((a1, a2, a4, a8)):
        take = ((expo >> bit) & 1) == 1
        nr, ni = _cmul((pr, pi), q)
        pr, pi = jnp.where(take, nr, pr), jnp.where(take, ni, pi)
    tab[2 * len(SCAN_STEPS)] = pr
    tab[2 * len(SCAN_STEPS) + 1] = pi


def _group_scan(v, tab, cols, carry, reverse):
    vr, vi = v
    for idx, k in enumerate(SCAN_STEPS):
        shift = SUBLANES - k if reverse else k
        tr, ti = _cmul((tab[2 * idx, :, cols], tab[2 * idx + 1, :, cols]),
                       (pltpu.roll(vr, shift, 0), pltpu.roll(vi, shift, 0)))
        vr, vi = vr + tr, vi + ti
    last = 2 * len(SCAN_STEPS)
    tr, ti = _cmul((tab[last, :, cols], tab[last + 1, :, cols]), carry)
    return vr + tr, vi + ti


def _scan_fwd(a_re, a_im, bu_re, bu_im, pushes):
    s, n = bu_re.shape
    ts = 512
    nt = s // ts
    groups = ts // SUBLANES
    np_ = len(pushes)

    def body(*refs):
        ins, outs, scratch, push_refs = _split_refs(refs, 4, 2, np_)
        ar_ref, ai_ref, bre_ref, bim_ref = ins
        xre_ref, xim_ref = outs
        tab, cre, cim = scratch
        i = pl.program_id(0)
        _riding_start(pushes, i == 0, push_refs)

        @pl.when(i == 0)
        def _():
            _fill_scan_tables(tab, ar_ref[...], ai_ref[...], False)
            cre[...] = jnp.zeros_like(cre)
            cim[...] = jnp.zeros_like(cim)

        def group(gi, carry):
            r0 = pl.multiple_of(gi * SUBLANES, SUBLANES)
            for c in range(n // SCAN_LANES):
                cols = slice(c * SCAN_LANES, (c + 1) * SCAN_LANES)
                v = (bre_ref[pl.ds(r0, SUBLANES), cols], bim_ref[pl.ds(r0, SUBLANES), cols])
                vr, vi = _group_scan(v, tab, cols, (cre[:, cols], cim[:, cols]), False)
                xre_ref[pl.ds(r0, SUBLANES), cols] = vr
                xim_ref[pl.ds(r0, SUBLANES), cols] = vi
                cre[:, cols] = jnp.broadcast_to(vr[SUBLANES - 1:SUBLANES, :], vr.shape)
                cim[:, cols] = jnp.broadcast_to(vi[SUBLANES - 1:SUBLANES, :], vi.shape)
            return carry

        lax.fori_loop(0, groups, group, 0, unroll=2)
        _riding_wait(pushes, i == nt - 1, push_refs)

    out = pl.pallas_call(
        body, name="ssm_scan", grid=(nt,),
        in_specs=[_full((1, n)), _full((1, n)), _rows(ts, n), _rows(ts, n)] + [ANY] * np_,
        out_specs=[_rows(ts, n), _rows(ts, n)] + [ANY] * np_,
        out_shape=[SDS((s, n), F32), SDS((s, n), F32)] + [p.out for p in pushes],
        scratch_shapes=_push_scratch(np_) + [pltpu.VMEM((2 * len(SCAN_STEPS) + 2, SUBLANES, n), F32),
                                             pltpu.VMEM((SUBLANES, n), F32), pltpu.VMEM((SUBLANES, n), F32)],
        compiler_params=_seq_params(48),
    )(a_re, a_im, bu_re, bu_im, *[p.array for p in pushes])
    return out[0], out[1], out[2:]


def _scan_bwd(a_re, a_im, dx_re, dx_im, x_re, x_im, pushes):
    s, n = dx_re.shape
    ts = 256
    nt = s // ts
    groups = ts // SUBLANES
    np_ = len(pushes)

    def body(*refs):
        ins, outs, scratch, push_refs = _split_refs(refs, 6, 4, np_)
        ar_ref, ai_ref, dre_ref, dim_ref, xre_ref, xim_ref = ins
        gre_ref, gim_ref, dar_ref, dai_ref = outs
        tab, cre, cim, accr, acci = scratch
        i = pl.program_id(0)
        _riding_start(pushes, i == 0, push_refs)

        @pl.when(i == 0)
        def _():
            _fill_scan_tables(tab, ar_ref[...], -ai_ref[...], True)
            for ref in (cre, cim, accr, acci):
                ref[...] = jnp.zeros_like(ref)

        inner = lax.broadcasted_iota(jnp.int32, (SUBLANES, SCAN_LANES), 0) < SUBLANES - 1

        def group(k, carry):
            r0 = pl.multiple_of((groups - 1 - k) * SUBLANES, SUBLANES)
            for c in range(n // SCAN_LANES):
                cols = slice(c * SCAN_LANES, (c + 1) * SCAN_LANES)
                after = (cre[:, cols], cim[:, cols])
                v = (dre_ref[pl.ds(r0, SUBLANES), cols], dim_ref[pl.ds(r0, SUBLANES), cols])
                vr, vi = _group_scan(v, tab, cols, after, True)
                gre_ref[pl.ds(r0, SUBLANES), cols] = vr
                gim_ref[pl.ds(r0, SUBLANES), cols] = vi
                nr = jnp.where(inner, pltpu.roll(vr, SUBLANES - 1, 0), after[0])
                ni = jnp.where(inner, pltpu.roll(vi, SUBLANES - 1, 0), after[1])
                xr, xi = xre_ref[pl.ds(r0, SUBLANES), cols], xim_ref[pl.ds(r0, SUBLANES), cols]
                accr[:, cols] += nr * xr + ni * xi
                acci[:, cols] += ni * xr - nr * xi
                cre[:, cols] = jnp.broadcast_to(vr[0:1, :], vr.shape)
                cim[:, cols] = jnp.broadcast_to(vi[0:1, :], vi.shape)
            return carry

        lax.fori_loop(0, groups, group, 0, unroll=2)

        @pl.when(i == nt - 1)
        def _():
            dar_ref[...] = jnp.sum(accr[...], axis=0, keepdims=True)
            dai_ref[...] = jnp.sum(acci[...], axis=0, keepdims=True)

        _riding_wait(pushes, i == nt - 1, push_refs)

    small = pltpu.VMEM((SUBLANES, n), F32)
    out = pl.pallas_call(
        body, name="ssm_scan_bwd", grid=(nt,),
        in_specs=[_full((1, n)), _full((1, n))] + [_rows_rev(ts, n, nt)] * 4 + [ANY] * np_,
        out_specs=[_rows_rev(ts, n, nt), _rows_rev(ts, n, nt), _full((1, n)), _full((1, n))] + [ANY] * np_,
        out_shape=[SDS((s, n), F32), SDS((s, n), F32), SDS((1, n), F32), SDS((1, n), F32)]
        + [p.out for p in pushes],
        scratch_shapes=_push_scratch(np_) + [pltpu.VMEM((2 * len(SCAN_STEPS) + 2, SUBLANES, n), F32),
                                             small, small, small, small],
        compiler_params=_seq_params(48),
    )(a_re, a_im, dx_re, dx_im, x_re, x_im, *[p.array for p in pushes])
    return out[0], out[1], out[2], out[3], out[4:]


def _mix_a(mixed, scale, za):
    return mixed * scale * _silu(za)


def _mix_b(val, gate, zb):
    return val * _sigmoid(gate) * _silu(zb)


def _even_mix(pooled, proj, x_re, x_im, x, pw, scale, cc_re, cc_im, d_skip, w_glu, w_out, pushes):
    s, d = x.shape
    half = pooled.shape[1]
    n = x_re.shape[1]
    ts = 256
    nt = s // ts
    np_ = len(pushes)

    def body(*refs):
        ins, outs, _, push_refs = _split_refs(refs, 13, 5, np_)
        (pooled_ref, u_ref, z_ref, xre_ref, xim_ref, x_ref, pw_ref, scale_ref, ccre_ref, ccim_ref, d_ref,
         wglu_ref, wout_ref) = ins
        x1_ref, ymix_ref, mixed_ref, yin_ref, gv_ref = outs
        i = pl.program_id(0)
        _riding_start(pushes, i == 0, push_refs)
        for g in range(len(POOL_WINDOWS)):
            cols = slice(g * LANES, (g + 1) * LANES)
            mixed_ref[:, cols] = _dot(pooled_ref[:, cols], pw_ref[g])
        sb, cb = ccre_ref.shape[1:]
        for b in range(SSM_BLOCKS):
            states, chans = slice(b * sb, (b + 1) * sb), slice(b * cb, (b + 1) * cb)
            yin_ref[:, chans] = (_dot(xre_ref[:, states], ccre_ref[b]) - _dot(xim_ref[:, states], ccim_ref[b])
                                 + d_ref[:, chans] * u_ref[:, chans])
        yin = yin_ref[...]
        gv = _dot(yin, wglu_ref[...])
        gv_ref[...] = gv
        ya = _mix_a(mixed_ref[...], scale_ref[...], z_ref[:, :half])
        yb = _mix_b(gv[:, :half], gv[:, half:], z_ref[:, half:])
        ymix = jnp.concatenate([ya, yb], axis=1).astype(MM)
        ymix_ref[...] = ymix
        x1_ref[...] = x_ref[...] + jnp.dot(ymix, wout_ref[...], preferred_element_type=F32)
        _riding_wait(pushes, i == nt - 1, push_refs)

    out = pl.pallas_call(
        body, name="even_mix", grid=(nt,),
        in_specs=[_rows(ts, half), _rows(ts, half, col=1), _rows(ts, d, col=1), _rows(ts, n), _rows(ts, n),
                  _rows(ts, d), _full(pw.shape), _full((1, half)), _full(cc_re.shape), _full(cc_im.shape),
                  _full((1, half)), _full((half, d)), _full((d, d))] + [ANY] * np_,
        out_specs=[_rows(ts, d), _rows(ts, d), _rows(ts, half), _rows(ts, half), _rows(ts, d)] + [ANY] * np_,
        out_shape=[SDS((s, d), F32), SDS((s, d), MM), SDS((s, half), F32), SDS((s, half), F32), SDS((s, d), F32)]
        + [p.out for p in pushes],
        scratch_shapes=_push_scratch(np_),
        compiler_params=_seq_params(56),
    )(pooled, proj, proj, x_re, x_im, x, pw, scale, cc_re, cc_im, d_skip, w_glu, w_out,
      *[p.array for p in pushes])
    return out[:5], out[5:]


def _glu(val, gt):
    return val * _sigmoid(gt)


def _odd_in(x1, g1, w_in):
    s, d = x1.shape
    e = w_in.shape[1]
    ts = 512

    def body(x_ref, g_ref, w_ref, proj_ref, glu_ref, h_ref):
        hb = _rms(x_ref[...], g_ref[...]).astype(MM)
        h_ref[...] = hb
        proj_ref[...] = jnp.dot(hb, w_ref[...], preferred_element_type=F32)
        glu_ref[...] = _glu(proj_ref[:, :d], proj_ref[:, d:2 * d])

    return pl.pallas_call(
        body, name="odd_in", grid=(s // ts,),
        in_specs=[_rows(ts, d), _full((1, d)), _full((d, e))],
        out_specs=[_rows(ts, e), _rows(ts, d), _rows(ts, d)],
        out_shape=[SDS((s, e), F32), SDS((s, d), F32), SDS((s, d), MM)],
        compiler_params=_seq_params(56),
    )(x1, g1, w_in)


CONV_ROWS = 32
CONV_COLS = 256


def _conv_scratch(ts, ch):
    return pltpu.VMEM((SUBLANES, ts + CONV_HALO, ch + LANES), F32)


def _phase_copies(sh, rows, ch):
    for o in range(1, SUBLANES):
        sh[o, 0:rows, 0:ch] = sh[0, pl.ds(o, rows), 0:ch]


def _conv_taps(sh, w_ref, offsets, r0, cols):
    acc = jnp.zeros((CONV_ROWS, cols.stop - cols.start), F32)
    for o in range(SUBLANES):
        taps = [(j, e // SUBLANES) for j, e in offsets if e % SUBLANES == o]
        if not taps:
            continue
        q0 = min(q for _, q in taps)
        q1 = max(q for _, q in taps)
        win = sh[o, pl.ds(r0 + SUBLANES * q0, CONV_ROWS + SUBLANES * (q1 - q0)), cols]
        for j, q in taps:
            lo = SUBLANES * (q - q0)
            acc = acc + w_ref[j:j + 1, cols] * win[lo:lo + CONV_ROWS]
    return acc


def _conv_fwd(g, w):
    s, ch = g.shape
    ts = 256
    per = ts // CONV_HALO
    lead = CONV_HALO - (CONV_KERNEL - 1)
    span = ts + CONV_HALO - SUBLANES

    def body(prev_ref, g_ref, w_ref, out_ref, sh):
        i = pl.program_id(0)
        sh[0, 0:CONV_HALO, 0:ch] = jnp.where(i == 0, 0.0, prev_ref[...])
        sh[0, CONV_HALO:, 0:ch] = g_ref[...]
        _phase_copies(sh, span, ch)

        offsets = [(j, lead + j) for j in range(CONV_KERNEL)]

        def block(rb, carry):
            r0 = pl.multiple_of(rb * CONV_ROWS, CONV_ROWS)
            for c in range(ch // CONV_COLS):
                cols = slice(c * CONV_COLS, (c + 1) * CONV_COLS)
                out_ref[pl.ds(r0, CONV_ROWS), cols] = _conv_taps(sh, w_ref, offsets, r0, cols)
            return carry

        lax.fori_loop(0, ts // CONV_ROWS, block, 0)

    return pl.pallas_call(
        body, name="conv_fwd", grid=(s // ts,),
        in_specs=[pl.BlockSpec((CONV_HALO, ch), lambda i: (jnp.maximum(i * per - 1, 0), 0)),
                  _rows(ts, ch), _full(w.shape)],
        out_specs=_rows(ts, ch),
        out_shape=SDS((s, ch), F32),
        scratch_shapes=[_conv_scratch(ts, ch)],
        compiler_params=_seq_params(40),
    )(g, g, w)


def _conv_bwd(dc, g, w, pushes):
    s, ch = dc.shape
    ts = 256
    per = ts // CONV_HALO
    n = s // ts
    lead = CONV_HALO - (CONV_KERNEL - 1)
    span = ts + CONV_HALO - SUBLANES
    np_ = len(pushes)

    def body(*refs):
        ins, outs, scratch, push_refs = _split_refs(refs, 5, 2, np_)
        dc_ref, next_ref, prev_ref, g_ref, w_ref = ins
        dg_ref, dw_ref = outs
        shd, shg, wacc = scratch
        i = pl.program_id(0)
        _riding_start(pushes, i == 0, push_refs)

        @pl.when(i == 0)
        def _():
            wacc[...] = jnp.zeros_like(wacc)

        shd[0, 0:ts, 0:ch] = dc_ref[...]
        shd[0, ts:, 0:ch] = jnp.where(i == n - 1, 0.0, next_ref[...])
        shg[0, 0:CONV_HALO, 0:ch] = jnp.where(i == 0, 0.0, prev_ref[...])
        shg[0, CONV_HALO:, 0:ch] = g_ref[...]
        _phase_copies(shd, span, ch)
        _phase_copies(shg, span, ch)

        offsets = [(j, CONV_KERNEL - 1 - j) for j in range(CONV_KERNEL)]

        def dg_block(rb, carry):
            r0 = pl.multiple_of(rb * CONV_ROWS, CONV_ROWS)
            for c in range(ch // CONV_COLS):
                cols = slice(c * CONV_COLS, (c + 1) * CONV_COLS)
                dg_ref[pl.ds(r0, CONV_ROWS), cols] = _conv_taps(shd, w_ref, offsets, r0, cols)
            return carry

        lax.fori_loop(0, ts // CONV_ROWS, dg_block, 0)

        tiles = ts // SUBLANES
        for c in range(ch // LANES):
            cols = slice(c * LANES, (c + 1) * LANES)
            cur = [shd[0, k * SUBLANES:(k + 1) * SUBLANES, cols] for k in range(tiles)]
            for j in range(CONV_KERNEL):
                q, o = divmod(lead + j, SUBLANES)
                parts = [None] * 4
                for k in range(tiles):
                    term = cur[k] * shg[o, (k + q) * SUBLANES:(k + q + 1) * SUBLANES, cols]
                    parts[k % 4] = term if parts[k % 4] is None else parts[k % 4] + term
                wacc[j, :, cols] += (parts[0] + parts[1]) + (parts[2] + parts[3])

        @pl.when(i == n - 1)
        def _():
            dw_ref[...] = jnp.zeros_like(dw_ref)
            for j in range(CONV_KERNEL):
                dw_ref[j:j + 1, :] = jnp.sum(wacc[j], axis=0, keepdims=True)

        _riding_wait(pushes, i == n - 1, push_refs)

    out = pl.pallas_call(
        body, name="conv_bwd", grid=(n,),
        in_specs=[_rows(ts, ch),
                  pl.BlockSpec((CONV_HALO, ch), lambda i: (jnp.minimum((i + 1) * per, s // CONV_HALO - 1), 0)),
                  pl.BlockSpec((CONV_HALO, ch), lambda i: (jnp.maximum(i * per - 1, 0), 0)),
                  _rows(ts, ch), _full(w.shape)] + [ANY] * np_,
        out_specs=[_rows(ts, ch), _full((CONV_HALO, ch))] + [ANY] * np_,
        out_shape=[SDS((s, ch), F32), SDS((CONV_HALO, ch), F32)] + [p.out for p in pushes],
        scratch_shapes=_push_scratch(np_) + [_conv_scratch(ts, ch), _conv_scratch(ts, ch),
                                             pltpu.VMEM((CONV_HALO, SUBLANES, ch), F32)],
        compiler_params=_seq_params(56),
    )(dc, dc, g, g, w, *[p.array for p in pushes])
    return out[0], out[1], out[2:]


def _conv_act(c, z1, cb, lg, lb):
    cc = c + cb
    mu = jnp.mean(cc, axis=-1, keepdims=True)
    dev = cc - mu
    var = jnp.mean(dev * dev, axis=-1, keepdims=True)
    cn = dev * lax.rsqrt(var + LN_EPS) * lg + lb
    return _silu(cn) * _silu(z1)


def _odd_out(c, proj1, x1, tgt, cb, lg, lb, w_out, fg):
    s, d = c.shape
    ts = 256

    def body(c_ref, z_ref, x1_ref, t_ref, cb_ref, lg_ref, lb_ref, w_ref, fg_ref,
             dc_ref, dz_ref, dx2_ref, y1_ref, loss_ref, dcb_ref, dlg_ref, dlb_ref, dfg_ref):
        @pl.when(pl.program_id(0) == 0)
        def _():
            for ref in (loss_ref, dcb_ref, dlg_ref, dlb_ref, dfg_ref):
                ref[...] = jnp.zeros_like(ref)

        y1, act_vjp = jax.vjp(_conv_act, c_ref[...], z_ref[...], cb_ref[...], lg_ref[...], lb_ref[...])
        y1b = y1.astype(MM)
        y1_ref[...] = y1b
        x2 = x1_ref[...] + jnp.dot(y1b, w_ref[...], preferred_element_type=F32)
        tgt_tile = t_ref[...]

        def head(x2, fg):
            err = jnp.square(_rms(x2, fg) - tgt_tile)
            return 0.5 * jnp.sum(jnp.mean(err, axis=-1))

        loss, (dx2, dfg) = jax.value_and_grad(head, argnums=(0, 1))(x2, fg_ref[...])
        loss_ref[...] += loss
        dfg_ref[...] += dfg
        dx2_ref[...] = dx2
        dy1 = _dot_nt(dx2, w_ref[...])
        dc, dz, dcb, dlg, dlb = act_vjp(dy1)
        dc_ref[...] = dc
        dz_ref[...] = dz
        dcb_ref[...] += dcb
        dlg_ref[...] += dlg
        dlb_ref[...] += dlb

    vec = SDS((1, d), F32)
    return pl.pallas_call(
        body, name="odd_out", grid=(s // ts,),
        in_specs=[_rows(ts, d), _rows(ts, d, col=2), _rows(ts, d), _rows(ts, d), _full((1, d)), _full((1, d)),
                  _full((1, d)), _full((d, d)), _full((1, d))],
        out_specs=[_rows(ts, d), _rows(ts, d), _rows(ts, d), _rows(ts, d), _full((SUBLANES, LANES)),
                   _full((1, d)), _full((1, d)), _full((1, d)), _full((1, d))],
        out_shape=[SDS((s, d), F32), SDS((s, d), F32), SDS((s, d), F32), SDS((s, d), MM),
                   SDS((SUBLANES, LANES), F32), vec, vec, vec, vec],
        compiler_params=_seq_params(56),
    )(c, proj1, x1, tgt, cb, lg, lb, w_out, fg)


def _odd_in_bwd(x1, proj1, dglu, dz1, dx2, g1, w_in):
    s, d = x1.shape
    e = w_in.shape[1]
    ts = 256

    def body(x_ref, vg_ref, dglu_ref, dz_ref, dx2_ref, g_ref, w_ref, dx1_ref, dp_ref, dg_ref):
        @pl.when(pl.program_id(0) == 0)
        def _():
            dg_ref[...] = jnp.zeros_like(dg_ref)

        _, glu_vjp = jax.vjp(_glu, vg_ref[:, :d], vg_ref[:, d:])
        dval, dgt = glu_vjp(dglu_ref[...])
        dp = jnp.concatenate([dval, dgt, dz_ref[...]], axis=1).astype(MM)
        dp_ref[...] = dp
        dh = _dot_nt(dp, w_ref[...])
        _, rms_vjp = jax.vjp(_rms, x_ref[...], g_ref[...])
        dxa, dg = rms_vjp(dh)
        dx1_ref[...] = dxa + dx2_ref[...]
        dg_ref[...] += dg

    return pl.pallas_call(
        body, name="odd_in_bwd", grid=(s // ts,),
        in_specs=[_rows(ts, d), _rows(ts, 2 * d), _rows(ts, d), _rows(ts, d), _rows(ts, d), _full((1, d)),
                  _full((d, e))],
        out_specs=[_rows(ts, d), _rows(ts, e), _full((1, d))],
        out_shape=[SDS((s, d), F32), SDS((s, e), MM), SDS((1, d), F32)],
        compiler_params=_seq_params(56),
    )(x1, proj1, dglu, dz1, dx2, g1, w_in)


def _even_mix_bwd(pooled, proj, mixed, gv, dx1, pw, scale, cc_re, cc_im, d_skip, w_glu, w_out, pushes):
    s, d = dx1.shape
    half = pooled.shape[1]
    n = cc_re.shape[0] * cc_re.shape[1]
    ts = 256
    nt = s // ts
    groups = len(POOL_WINDOWS)
    np_ = len(pushes)

    def body(*refs):
        ins, outs, _, push_refs = _split_refs(refs, 13, 10, np_)
        (pooled_ref, u_ref, z_ref, mixed_ref, gv_ref, dx1_ref, pw_ref, scale_ref, ccre_ref, ccim_ref, d_ref,
         wglu_ref, wout_ref) = ins
        (dpooled_ref, du_ref, dz_ref, dxre_ref, dxim_ref, dyin_ref, dgv_ref, dpw_ref, dscale_ref, dd_ref) = outs
        i = pl.program_id(0)
        _riding_start(pushes, i == 0, push_refs)

        @pl.when(i == 0)
        def _():
            for ref in (dpw_ref, dscale_ref, dd_ref):
                ref[...] = jnp.zeros_like(ref)

        dymix = _dot_nt(dx1_ref[...], wout_ref[...])
        _, a_vjp = jax.vjp(_mix_a, mixed_ref[...], scale_ref[...], z_ref[:, :half])
        dmixed, dscale, dza = a_vjp(dymix[:, :half])
        _, b_vjp = jax.vjp(_mix_b, gv_ref[:, :half], gv_ref[:, half:], z_ref[:, half:])
        dval, dgate, dzb = b_vjp(dymix[:, half:])
        dz_ref[:, :half] = dza
        dz_ref[:, half:] = dzb
        dscale_ref[...] += dscale
        dgv = jnp.concatenate([dval, dgate], axis=1).astype(MM)
        dgv_ref[...] = dgv
        dyin = _dot_nt(dgv, wglu_ref[...])
        dd_ref[...] += jnp.sum(dyin * u_ref[...], axis=0, keepdims=True)
        du_ref[...] = d_ref[...] * dyin
        dyb = dyin.astype(MM)
        dyin_ref[...] = dyb
        sb, cb = ccre_ref.shape[1:]
        for b in range(SSM_BLOCKS):
            states, chans = slice(b * sb, (b + 1) * sb), slice(b * cb, (b + 1) * cb)
            dxre_ref[:, states] = _dot_nt(dyb[:, chans], ccre_ref[b])
            dxim_ref[:, states] = -_dot_nt(dyb[:, chans], ccim_ref[b])
        for g in range(groups):
            cols = slice(g * LANES, (g + 1) * LANES)
            dm = dmixed[:, cols].astype(MM)
            dpooled_ref[:, cols] = _dot_nt(dm, pw_ref[g])
            dpw_ref[g] += _dot_tn(pooled_ref[:, cols], dm)

        _riding_wait(pushes, i == nt - 1, push_refs)

    out = pl.pallas_call(
        body, name="even_mix_bwd", grid=(nt,),
        in_specs=[_rows(ts, half), _rows(ts, half, col=1), _rows(ts, d, col=1), _rows(ts, half), _rows(ts, d),
                  _rows(ts, d), _full(pw.shape), _full((1, half)), _full(cc_re.shape), _full(cc_im.shape),
                  _full((1, half)), _full((half, d)), _full((d, d))] + [ANY] * np_,
        out_specs=[_rows(ts, half), _rows(ts, half), _rows(ts, d), _rows(ts, n), _rows(ts, n), _rows(ts, half),
                   _rows(ts, d), _full(pw.shape), _full((1, half)), _full((1, half))] + [ANY] * np_,
        out_shape=[SDS((s, half), F32), SDS((s, half), F32), SDS((s, d), F32), SDS((s, n), F32), SDS((s, n), F32),
                   SDS((s, half), MM), SDS((s, d), MM), SDS(pw.shape, F32), SDS((1, half), F32),
                   SDS((1, half), F32)] + [p.out for p in pushes],
        scratch_shapes=_push_scratch(np_),
        compiler_params=_seq_params(56),
    )(pooled, proj, proj, mixed, gv, dx1, pw, scale, cc_re, cc_im, d_skip, w_glu, w_out,
      *[p.array for p in pushes])
    return out[:10], out[10:]


def _ssm_bu_bwd(g_re, g_im, du_skip, bb):
    s, n = g_re.shape
    nb, cb, two_nb = bb.shape
    sb = two_nb // 2
    cin = nb * cb
    ts = 512

    def body(gre_ref, gim_ref, du_ref, bb_ref, out_ref):
        for b in range(nb):
            states, chans = slice(b * sb, (b + 1) * sb), slice(b * cb, (b + 1) * cb)
            out_ref[:, chans] = (du_ref[:, chans] + _dot_nt(gre_ref[:, states], bb_ref[b, :, :sb])
                                 + _dot_nt(gim_ref[:, states], bb_ref[b, :, sb:]))

    return pl.pallas_call(
        body, name="ssm_bu_bwd", grid=(s // ts,),
        in_specs=[_rows(ts, n), _rows(ts, n), _rows(ts, cin), _full(bb.shape)],
        out_specs=_rows(ts, cin),
        out_shape=SDS((s, cin), F32),
        compiler_params=_seq_params(48),
    )(g_re, g_im, du_skip, bb)


def _even_in_bwd(x, du_pool, du_ssm, dz, dx1, g0, w_in):
    s, d = x.shape
    half = du_pool.shape[1]
    shards, _, wc = w_in.shape
    ts = 256

    def body(x_ref, dup_ref, dus_ref, dz_ref, dx1_ref, g_ref, w_ref, gx_ref, dg_ref):
        @pl.when(pl.program_id(0) == 0)
        def _():
            dg_ref[...] = jnp.zeros_like(dg_ref)

        dp = jnp.concatenate([dup_ref[...], dus_ref[...], dz_ref[...]], axis=1).astype(MM)
        dh = _dot_nt(dp[:, :wc], w_ref[0])
        for k in range(1, shards):
            dh = dh + _dot_nt(dp[:, k * wc:(k + 1) * wc], w_ref[k])
        _, rms_vjp = jax.vjp(_rms, x_ref[...], g_ref[...])
        dxa, dg = rms_vjp(dh)
        gx_ref[...] = dxa + dx1_ref[...]
        dg_ref[...] += dg

    return pl.pallas_call(
        body, name="even_in_bwd", grid=(s // ts,),
        in_specs=[_rows(ts, d), _rows(ts, half), _rows(ts, half), _rows(ts, d), _rows(ts, d), _full((1, d)),
                  _full(w_in.shape)],
        out_specs=[_rows(ts, d), _full((1, d))],
        out_shape=[SDS((s, d), F32), SDS((1, d), F32)],
        compiler_params=_seq_params(48),
    )(x, du_pool, du_ssm, dz, dx1, g0, w_in)


def _discretise(log_dt, ar, ai, br, bi):
    dt = jnp.exp(log_dt)
    mag = jnp.exp(ar * dt)
    ang = ai * dt
    abr = mag * jnp.cos(ang)
    abi = mag * jnp.sin(ang)
    den = ar * ar + ai * ai
    nr = abr - 1.0
    ni = abi
    kr = (nr * ar + ni * ai) / den
    ki = (ni * ar - nr * ai) / den
    bbr = kr[None] * br - ki[None] * bi
    bbi = kr[None] * bi + ki[None] * br
    return abr, abi, bbr, bbi


def _whole(n):
    return [pl.BlockSpec(memory_space=pltpu.VMEM)] * n


def _disc_fwd(log_dt, ar, ai, br, bi):
    def body(ld_ref, ar_ref, ai_ref, br_ref, bi_ref, abr_ref, abi_ref, bbr_ref, bbi_ref):
        out = _discretise(ld_ref[...], ar_ref[...], ai_ref[...], br_ref[...], bi_ref[...])
        for ref, val in zip((abr_ref, abi_ref, bbr_ref, bbi_ref), out):
            ref[...] = val

    return pl.pallas_call(
        body, name="ssm_discretise", in_specs=_whole(5), out_specs=_whole(4),
        out_shape=[SDS(ar.shape, F32), SDS(ar.shape, F32), SDS(br.shape, F32), SDS(br.shape, F32)],
    )(log_dt, ar, ai, br, bi)


def _disc_bwd(log_dt, ar, ai, br, bi, dabr, dabi, dbbr, dbbi):
    def body(ld_ref, ar_ref, ai_ref, br_ref, bi_ref, dabr_ref, dabi_ref, dbbr_ref, dbbi_ref,
             dld_ref, dar_ref, dai_ref, dbr_ref, dbi_ref):
        _, vjp = jax.vjp(_discretise, ld_ref[...], ar_ref[...], ai_ref[...], br_ref[...], bi_ref[...])
        grads = vjp((dabr_ref[...], dabi_ref[...], dbbr_ref[...], dbbi_ref[...]))
        for ref, val in zip((dld_ref, dar_ref, dai_ref, dbr_ref, dbi_ref), grads):
            ref[...] = val

    return pl.pallas_call(
        body, name="ssm_discretise_bwd", in_specs=_whole(9), out_specs=_whole(5),
        out_shape=[SDS(log_dt.shape, F32), SDS(ar.shape, F32), SDS(ar.shape, F32), SDS(br.shape, F32),
                   SDS(br.shape, F32)],
    )(log_dt, ar, ai, br, bi, dabr, dabi, dbbr, dbbi)


def _block_diag(t):
    g, a, b = t.shape
    per = g // SSM_BLOCKS
    t = t.reshape(SSM_BLOCKS, per, a, b)
    same = jnp.eye(per, dtype=bool)[None, :, None, :, None]
    return jnp.where(same, t[:, :, :, None, :], 0.0).reshape(SSM_BLOCKS, per * a, per * b)


def _diag_blocks(m, g):
    per = g // SSM_BLOCKS
    a, b = m.shape[1] // per, m.shape[2] // per
    d = jnp.diagonal(m.reshape(SSM_BLOCKS, per, a, per, b), axis1=1, axis2=3)
    return jnp.moveaxis(d, -1, 1).reshape(g, a, b)


def _adamw(name, parts, w, m, v):
    rows, cols = w.shape
    tr = ADAM_ROWS if rows % ADAM_ROWS == 0 else rows

    def body(p_ref, w_ref, m_ref, v_ref, g_ref, d_ref, nm_ref, nv_ref):
        g = p_ref[0]
        for dev in range(1, N_DEV):
            g = g + p_ref[dev]
        g_ref[...] = g
        nm = ADAM_B1 * m_ref[...] + (1.0 - ADAM_B1) * g
        nv = ADAM_B2 * v_ref[...] + (1.0 - ADAM_B2) * jnp.square(g)
        nm_ref[...] = nm
        nv_ref[...] = nv
        m_hat = nm / (1.0 - ADAM_B1 ** ADAM_STEP)
        v_hat = nv / (1.0 - ADAM_B2 ** ADAM_STEP)
        d_ref[...] = -ADAM_LR * (m_hat / (jnp.sqrt(v_hat) + ADAM_EPS) + ADAM_WD * w_ref[...])

    tile = _rows(tr, cols)
    out = SDS((rows, cols), F32)
    return pl.pallas_call(
        body, name=name, grid=(rows // tr,),
        in_specs=[pl.BlockSpec((N_DEV, tr, cols), lambda i: (0, i, 0)), tile, tile, tile],
        out_specs=[tile, tile, tile, tile],
        out_shape=[out, out, out, out],
        compiler_params=_seq_params(32),
    )(parts, w, m, v)


def _as_rows(flat):
    n = flat.shape[-1]
    padded = _round_up(n, PACK_TILE)
    if padded != n:
        flat = jnp.pad(flat, [(0, 0)] * (flat.ndim - 1) + [(0, padded - n)])
    return flat.reshape(flat.shape[:-1] + (padded // LANES, LANES))


def _columns_to_shards(full):
    r, c8 = full.shape
    return jnp.transpose(full.reshape(r, N_DEV, c8 // N_DEV), (1, 0, 2))


def _shards_to_columns(shards):
    n, r, c = shards.shape
    return jnp.transpose(shards, (1, 0, 2)).reshape(r, n * c)


def _pack_local(tensors):
    return jnp.concatenate([_as_rows(t.reshape(-1)) for t in tensors], axis=0)


def _unpack_local(packed, shapes):
    out, r = [], 0
    for shape in shapes:
        n = 1
        for dim in shape:
            n *= dim
        rows = _round_up(n, PACK_TILE) // LANES
        out.append(packed[r:r + rows].reshape(-1)[:n].reshape(shape))
        r += rows
    return out


def kernel(x, even_norm, even_w_in, pool_w, pool_scale, ssm_log_dt, ssm_a_re, ssm_a_im, ssm_b_re, ssm_b_im, ssm_c_re, ssm_c_im, ssm_d, ssm_w_glu, even_w_out, odd_norm, odd_w_in, conv_w, conv_b, conv_ln_g, conv_ln_b, odd_w_out, final_norm, loss_target, m_even_norm, m_even_w_in, m_pool_w, m_pool_scale, m_ssm_log_dt, m_ssm_a_re, m_ssm_a_im, m_ssm_b_re, m_ssm_b_im, m_ssm_c_re, m_ssm_c_im, m_ssm_d, m_ssm_w_glu, m_even_w_out, m_odd_norm, m_odd_w_in, m_conv_w, m_conv_b, m_conv_ln_g, m_conv_ln_b, m_odd_w_out, m_final_norm, v_even_norm, v_even_w_in, v_pool_w, v_pool_scale, v_ssm_log_dt, v_ssm_a_re, v_ssm_a_im, v_ssm_b_re, v_ssm_b_im, v_ssm_c_re, v_ssm_c_im, v_ssm_d, v_ssm_w_glu, v_even_w_out, v_odd_norm, v_odd_w_in, v_conv_w, v_conv_b, v_conv_ln_g, v_conv_ln_b, v_odd_w_out, v_final_norm):
    given = dict(locals())
    xs = x[0]
    tgt = loss_target[0]
    d_model = xs.shape[1]

    def local(prefix, name):
        t = given[prefix + name]
        return t if name == 'final_norm' else t[0]

    def small_block(prefix):
        parts = [jnp.pad(local(prefix, 'conv_w'), ((0, SMALL_AT['odd_norm'] - CONV_KERNEL), (0, 0)))]
        for n in SMALL_VECTORS:
            parts.append(jnp.pad(local(prefix, n).reshape(1, LANES), ((0, SUBLANES - 1), (0, 0))))
        return jnp.concatenate(parts, axis=0)

    def as_tile(row):
        return jnp.pad(row, ((0, SUBLANES - row.shape[0]), (0, 0)))

    (w_in_e,) = _push_alone("gather_first", [_Push(local('', 'even_w_in').astype(MM), False)])
    later = [_Push(local('', 'ssm_w_glu').astype(MM), False), _Push(local('', 'even_w_out').astype(MM), False),
             _Push(local('', 'odd_w_in').astype(MM), False), _Push(local('', 'odd_w_out').astype(MM), False),
             _Push(small_block(''), False)]

    a_re, a_im = local('', 'ssm_a_re'), local('', 'ssm_a_im')
    groups, state = a_re.shape
    log_dt = local('', 'ssm_log_dt').reshape(groups, 1)
    b_re_t = jnp.transpose(local('', 'ssm_b_re'), (2, 0, 1))
    b_im_t = jnp.transpose(local('', 'ssm_b_im'), (2, 0, 1))
    abr, abi, bbr, bbi = _disc_fwd(log_dt, a_re, a_im, b_re_t, b_im_t)
    a_re_row = abr.reshape(1, groups * state)
    a_im_row = abi.reshape(1, groups * state)
    bb = jnp.concatenate([_block_diag(jnp.transpose(bbr, (1, 0, 2))), _block_diag(jnp.transpose(bbi, (1, 0, 2)))],
                         axis=2).astype(MM)
    cc_re = _block_diag(jnp.transpose(local('', 'ssm_c_re'), (0, 2, 1))).astype(MM)
    cc_im = _block_diag(jnp.transpose(local('', 'ssm_c_im'), (0, 2, 1))).astype(MM)
    pw = local('', 'pool_w').astype(MM)
    g0 = local('', 'even_norm').reshape(1, d_model)
    fg = local('', 'final_norm').reshape(1, d_model)
    scale = local('', 'pool_scale').reshape(1, -1)
    d_skip = local('', 'ssm_d').reshape(1, -1)

    proj0, h0 = _even_in(xs, g0, w_in_e)
    pooled = _pool_fwd(proj0)
    bu_re, bu_im = _ssm_bu(proj0, bb)
    x_re, x_im, (wg3, wo_e3) = _scan_fwd(a_re_row, a_im_row, bu_re, bu_im, later[:2])
    w_glu = _shards_to_columns(wg3)
    w_out_e = wo_e3.reshape(d_model, d_model)
    (x1, ymix, mixed, yin, gv), (wi_o3, wo_o3, small8) = _even_mix(
        pooled, proj0, x_re, x_im, xs, pw, scale, cc_re, cc_im, d_skip, w_glu, w_out_e, later[2:])
    w_in_o = _shards_to_columns(wi_o3)
    w_out_o = wo_o3.reshape(d_model, d_model)
    small = _shards_to_columns(small8)
    cw = small[:CONV_KERNEL]
    g1, cb, lg, lb = (small[SMALL_AT[n]:SMALL_AT[n] + 1] for n in SMALL_VECTORS)
    proj1, glu, h1 = _odd_in(x1, g1, w_in_o)
    conv = _conv_fwd(glu, cw)

    half = pooled.shape[1]
    n_state = groups * state
    rows_per = d_model // N_DEV
    dc, dz1, dx2, y1, loss_tile, dcb, dlg, dlb, dfg = _odd_out(conv, proj1, x1, tgt, cb, lg, lb, w_out_o, fg)
    g_odd_out = _mm_tn("dw_odd_out", y1, dx2, d_model, d_model).reshape(N_DEV, rows_per, d_model)
    dglu, dcw, (r_odd_out,) = _conv_bwd(dc, glu, cw, [_Push(g_odd_out, True)])
    dx1, dproj1, dg1 = _odd_in_bwd(x1, proj1, dglu, dz1, dx2, g1, w_in_o)
    g_odd_in = _columns_to_shards(_mm_tn("dw_odd_in", h1, dproj1, d_model, 3 * d_model))
    g_even_out = _mm_tn("dw_even_out", ymix, dx1, d_model, d_model).reshape(N_DEV, rows_per, d_model)
    mix_grads, (r_odd_in,) = _even_mix_bwd(pooled, proj0, mixed, gv, dx1, pw, scale, cc_re, cc_im, d_skip, w_glu,
                                           w_out_e, [_Push(g_odd_in, True)])
    dpooled, du_skip, dz0, dx_re, dx_im, dyin, dgv, dpw, dscale, dd = mix_grads
    g_glu = _columns_to_shards(_mm_tn("dw_glu", yin, dgv, half, d_model))
    sblk, cblk = n_state // SSM_BLOCKS, half // SSM_BLOCKS
    dcc_re = _mm_tn_blocks("dw_c_re", x_re, dyin, sblk, cblk, SSM_BLOCKS, ts=2048)
    dcc_im = _mm_tn_blocks("dw_c_im", x_im, dyin, sblk, cblk, SSM_BLOCKS, sign=-1.0, ts=2048)
    dc_re = jnp.transpose(_diag_blocks(dcc_re, groups), (0, 2, 1))
    dc_im = jnp.transpose(_diag_blocks(dcc_im, groups), (0, 2, 1))
    g_small = _columns_to_shards(jnp.concatenate(
        [dcw, as_tile(dg1), as_tile(dcb), as_tile(dlg), as_tile(dlb)], axis=0))
    early = {'pool_w': dpw, 'pool_scale': dscale, 'ssm_c_re': dc_re, 'ssm_c_im': dc_im, 'ssm_d': dd,
             'final_norm': dfg}
    g_early = _pack_local([early[n] for n in EARLY_REPLICATED] + [loss_tile])
    g_re, g_im, dabr, dabi, (r_even_out, r_glu, r_small, r_early) = _scan_bwd(
        a_re_row, a_im_row, dx_re, dx_im, x_re, x_im,
        [_Push(g_even_out, True), _Push(g_glu, True), _Push(g_small, True), _Push(g_early, False)])
    du_ssm = _ssm_bu_bwd(g_re, g_im, du_skip, bb)
    du_pool = _pool_bwd(dpooled)
    grad_x, dg0 = _even_in_bwd(xs, du_pool, du_ssm, dz0, dx1, g0, w_in_e)
    g_even_in = _columns_to_shards(jnp.concatenate(
        [_mm_tn("dw_even_in_pool", h0, du_pool, d_model, half), _mm_tn("dw_even_in_ssm", h0, du_ssm, d_model, half),
         _mm_tn("dw_even_in_gate", h0, dz0, d_model, d_model)], axis=1))
    g_mid = _pack_local([dg0])
    dbb_re, dbb_im, (r_even_in, r_mid) = _dw_bbar(proj0, g_re, g_im, cblk, sblk,
                                                  [_Push(g_even_in, True), _Push(g_mid, False)])
    dbbr = jnp.transpose(_diag_blocks(dbb_re, groups), (2, 0, 1))
    dbbi = jnp.transpose(_diag_blocks(dbb_im, groups), (2, 0, 1))
    dld, dar, dai, dbr, dbi = _disc_bwd(log_dt, a_re, a_im, b_re_t, b_im_t,
                                        dabr.reshape(groups, state), dabi.reshape(groups, state), dbbr, dbbi)
    late = {'ssm_log_dt': dld, 'ssm_a_re': dar, 'ssm_a_im': dai,
            'ssm_b_re': jnp.transpose(dbr, (1, 2, 0)), 'ssm_b_im': jnp.transpose(dbi, (1, 2, 0))}
    g_late = _pack_local([late[n] for n in LATE_REPLICATED])
    (r_late,) = _push_alone("exchange_tail", [_Push(g_late, False)])

    results = {}
    for n, parts in (('even_w_in', r_even_in), ('ssm_w_glu', r_glu), ('even_w_out', r_even_out),
                     ('odd_w_in', r_odd_in), ('odd_w_out', r_odd_out)):
        results[n] = _adamw("adamw_" + n, parts, local('', n), local('m_', n), local('v_', n))
    small_out = _adamw("adamw_small", r_small, small_block(''), small_block('m_'), small_block('v_'))
    results['conv_w'] = [o[:CONV_KERNEL] for o in small_out]
    for n in SMALL_VECTORS:
        results[n] = [o[SMALL_AT[n]] for o in small_out]
    loss_rows = [jnp.zeros_like(loss_tile)]
    for names, parts, extra in ((EARLY_REPLICATED, r_early, loss_rows), (MID_REPLICATED, r_mid, []),
                               (LATE_REPLICATED, r_late, [])):
        packed = [_pack_local([local(p, n) for n in names] + extra) for p in ('', 'm_', 'v_')]
        out = _adamw("adamw_" + names[0], parts, *packed)
        unpacked = [_unpack_local(o, [given[n].shape for n in names]) for o in out]
        for k, n in enumerate(names):
            results[n] = [u[k] for u in unpacked]
        if extra:
            loss = out[0][-SUBLANES, 0]

    outs = [loss, grad_x[None]]
    for kind in range(4):
        outs.extend(results[n][kind].reshape(given[n].shape) for n in WEIGHTS)
    return tuple(outs)
```

```python
import functools

import jax
import jax.numpy as jnp
from jax import lax
from jax.experimental import pallas as pl
from jax.experimental.pallas import tpu as pltpu

F32 = jnp.float32
MM = jnp.bfloat16
WIRE = jnp.bfloat16
SDS = jax.ShapeDtypeStruct

RMS_EPS = 1e-6
LN_EPS = 1e-5
POOL_WINDOWS = (2, 4, 8, 16)
POOL_HALO = 16
CONV_KERNEL = 31
CONV_HALO = 32
N_DEV = 8
LANES = 128
SUBLANES = 8
PACK_TILE = SUBLANES * LANES
ADAM_ROWS = 128
MIB = 1024 * 1024

ADAM_LR = 0.001
ADAM_B1 = 0.9
ADAM_B2 = 0.999
ADAM_EPS = 1e-08
ADAM_WD = 0.01
ADAM_STEP = 10

WEIGHTS = ['even_norm', 'even_w_in', 'pool_w', 'pool_scale', 'ssm_log_dt', 'ssm_a_re', 'ssm_a_im', 'ssm_b_re',
           'ssm_b_im', 'ssm_c_re', 'ssm_c_im', 'ssm_d', 'ssm_w_glu', 'even_w_out', 'odd_norm', 'odd_w_in', 'conv_w',
           'conv_b', 'conv_ln_g', 'conv_ln_b', 'odd_w_out', 'final_norm']
SMALL_VECTORS = ('odd_norm', 'conv_b', 'conv_ln_g', 'conv_ln_b')
SMALL_AT = {'odd_norm': 32, 'conv_b': 40, 'conv_ln_g': 48, 'conv_ln_b': 56}
EARLY_REPLICATED = ['pool_w', 'pool_scale', 'ssm_c_re', 'ssm_c_im', 'ssm_d', 'final_norm']
MID_REPLICATED = ['even_norm']
LATE_REPLICATED = ['ssm_log_dt', 'ssm_a_re', 'ssm_a_im', 'ssm_b_re', 'ssm_b_im']


def _round_up(n, m):
    return (n + m - 1) // m * m


def _sigmoid(x):
    return jax.nn.sigmoid(x)


def _silu(x):
    return x * jax.nn.sigmoid(x)


def _rms(x, g):
    return x * lax.rsqrt(jnp.mean(x * x, axis=-1, keepdims=True) + RMS_EPS) * g


def _dot(a, b):
    return jnp.dot(a.astype(MM), b.astype(MM), preferred_element_type=F32)


def _dot_nt(a, b):
    return lax.dot_general(a.astype(MM), b.astype(MM), (((1,), (1,)), ((), ())), preferred_element_type=F32)


def _dot_tn(a, b):
    return lax.dot_general(a.astype(MM), b.astype(MM), (((0,), (0,)), ((), ())), preferred_element_type=F32)


def _rows(ts, width, col=0):
    return pl.BlockSpec((ts, width), lambda i: (i, col))


def _rows_rev(ts, width, n, col=0):
    return pl.BlockSpec((ts, width), lambda i: (n - 1 - i, col))


def _full(shape):
    zeros = (0,) * len(shape)
    return pl.BlockSpec(shape, lambda i: zeros)


def _seq_params(vmem_mib=48, dims=1):
    return pltpu.CompilerParams(dimension_semantics=("arbitrary",) * dims, vmem_limit_bytes=vmem_mib * MIB)


def _mesh_position():
    x, y, c = lax.axis_index("x"), lax.axis_index("y"), lax.axis_index("c")
    return x, y, c


def _peer(pos, relation):
    x, y, c = pos
    px = 1 - x if relation & 4 else x
    py = 1 - y if relation & 2 else y
    pc = 1 - c if relation & 1 else c
    return px, py, pc


ANY = pl.BlockSpec(memory_space=pl.ANY)


class _Push:
    def __init__(self, array, scatter):
        self.array = array
        self.scatter = scatter
        self.out = SDS(array.shape if scatter else (N_DEV,) + array.shape, array.dtype)


def _push_scratch(n):
    if n == 0:
        return []
    return [pltpu.SemaphoreType.DMA((n * (N_DEV - 1),)), pltpu.SemaphoreType.DMA((n * (N_DEV - 1),)),
            pltpu.SemaphoreType.DMA((n,))]


def _push_copies(pushes, srcs, dsts, send_sems, recv_sems, local_sems):
    pos = _mesh_position()
    me = 4 * pos[0] + 2 * pos[1] + pos[2]
    copies = []
    for a, (push, src, dst) in enumerate(zip(pushes, srcs, dsts)):
        copies.append(pltpu.make_async_copy(src.at[me] if push.scatter else src, dst.at[me], local_sems.at[a]))
        for relation in range(1, N_DEV):
            peer = _peer(pos, relation)
            peer_id = 4 * peer[0] + 2 * peer[1] + peer[2]
            k = a * (N_DEV - 1) + relation - 1
            copies.append(pltpu.make_async_remote_copy(
                src_ref=src.at[peer_id] if push.scatter else src, dst_ref=dst.at[me],
                send_sem=send_sems.at[k], recv_sem=recv_sems.at[k], device_id=peer,
                device_id_type=pl.DeviceIdType.MESH))
    return copies


def _push_alone(name, pushes):
    n = len(pushes)

    def body(*refs):
        copies = _push_copies(pushes, refs[:n], refs[n:2 * n], *refs[2 * n:])
        for cp in copies:
            cp.start()
        for cp in copies:
            cp.wait()

    return pl.pallas_call(
        body, name=name, out_shape=[p.out for p in pushes], in_specs=[ANY] * n, out_specs=[ANY] * n,
        scratch_shapes=_push_scratch(n),
    )(*[p.array for p in pushes])


def _riding_start(pushes, first, refs):
    n = len(pushes)
    if n == 0:
        return

    @pl.when(first)
    def _():
        for cp in _push_copies(pushes, refs[:n], refs[n:2 * n], *refs[2 * n:]):
            cp.start()


def _riding_wait(pushes, last, refs):
    n = len(pushes)
    if n == 0:
        return

    @pl.when(last)
    def _():
        for cp in _push_copies(pushes, refs[:n], refs[n:2 * n], *refs[2 * n:]):
            cp.wait()


def _split_refs(refs, n_in, n_out, n_push):
    ins = refs[:n_in]
    srcs = refs[n_in:n_in + n_push]
    o0 = n_in + n_push
    outs = refs[o0:o0 + n_out]
    dsts = refs[o0 + n_out:o0 + n_out + n_push]
    s0 = o0 + n_out + n_push
    n_sem = 3 if n_push else 0
    sems = refs[s0:s0 + n_sem]
    scratch = refs[s0 + n_sem:]
    return ins, outs, scratch, tuple(srcs) + tuple(dsts) + tuple(sems)


def _mm_tn(name, a, b, m, n, a_col=0, b_col=0, ts=512):
    s = a.shape[0]
    tn = n
    nj = n // tn

    def body(a_ref, b_ref, o_ref):
        @pl.when(pl.program_id(1) == 0)
        def _():
            o_ref[...] = jnp.zeros_like(o_ref)

        o_ref[...] += _dot_tn(a_ref[...], b_ref[...])

    return pl.pallas_call(
        body, name=name, grid=(nj, s // ts),
        in_specs=[pl.BlockSpec((ts, m), lambda j, k: (k, a_col)),
                  pl.BlockSpec((ts, tn), lambda j, k: (k, b_col * nj + j))],
        out_specs=pl.BlockSpec((m, tn), lambda j, k: (0, j)),
        out_shape=SDS((m, n), F32),
        compiler_params=_seq_params(56, dims=2),
    )(a, b)


def _mm_tn_blocks(name, a, b, m, n, nb, a_col=0, a_step=1, b_col=0, sign=1.0, ts=512):
    s = a.shape[0]

    def body(a_ref, b_ref, o_ref):
        @pl.when(pl.program_id(1) == 0)
        def _():
            o_ref[...] = jnp.zeros_like(o_ref)

        o_ref[...] += sign * _dot_tn(a_ref[...], b_ref[...])

    return pl.pallas_call(
        body, name=name, grid=(nb, s // ts),
        in_specs=[pl.BlockSpec((ts, m), lambda j, k: (k, a_col + a_step * j)),
                  pl.BlockSpec((ts, n), lambda j, k: (k, b_col + j))],
        out_specs=pl.BlockSpec((None, m, n), lambda j, k: (j, 0, 0)),
        out_shape=SDS((nb, m, n), F32),
        compiler_params=_seq_params(48, dims=2),
    )(a, b)


def _dw_bbar(proj, g_re, g_im, cb, sb, pushes, ts=2048):
    s = proj.shape[0]
    nb, nk = SSM_BLOCKS, s // ts
    np_ = len(pushes)

    def body(*refs):
        ins, outs, _, push_refs = _split_refs(refs, 3, 2, np_)
        u_ref, gre_ref, gim_ref = ins
        ore_ref, oim_ref = outs
        j, k = pl.program_id(0), pl.program_id(1)
        _riding_start(pushes, jnp.logical_and(j == 0, k == 0), push_refs)

        @pl.when(k == 0)
        def _():
            ore_ref[...] = jnp.zeros_like(ore_ref)
            oim_ref[...] = jnp.zeros_like(oim_ref)

        u = u_ref[...].astype(MM)
        ore_ref[...] += _dot_tn(gre_ref[...], u)
        oim_ref[...] += _dot_tn(gim_ref[...], u)
        _riding_wait(pushes, jnp.logical_and(j == nb - 1, k == nk - 1), push_refs)

    wide = pl.BlockSpec((ts, sb), lambda j, k: (k, j))
    out = pl.BlockSpec((None, sb, cb), lambda j, k: (j, 0, 0))
    res = pl.pallas_call(
        body, name="dw_bbar", grid=(nb, nk),
        in_specs=[pl.BlockSpec((ts, cb), lambda j, k: (k, nb + j)), wide, wide] + [ANY] * np_,
        out_specs=[out, out] + [ANY] * np_,
        out_shape=[SDS((nb, sb, cb), F32), SDS((nb, sb, cb), F32)] + [p.out for p in pushes],
        scratch_shapes=_push_scratch(np_),
        compiler_params=_seq_params(48, dims=2),
    )(proj, g_re, g_im, *[p.array for p in pushes])
    return res[0], res[1], res[2:]


def _even_in(x, g0, w_in, pushes):
    s, d = x.shape
    shards, _, wc = w_in.shape
    e = shards * wc
    ts = 512
    nt = s // ts
    np_ = len(pushes)

    def body(*refs):
        (x_ref, g_ref, w_ref), (proj_ref, h_ref), _, push_refs = _split_refs(refs, 3, 2, np_)
        i = pl.program_id(0)
        _riding_start(pushes, i == 0, push_refs)
        hb = _rms(x_ref[...], g_ref[...]).astype(MM)
        h_ref[...] = hb
        for k in range(shards):
            proj_ref[:, k * wc:(k + 1) * wc] = jnp.dot(hb, w_ref[k], preferred_element_type=F32)
        _riding_wait(pushes, i == nt - 1, push_refs)

    out = pl.pallas_call(
        body, name="even_in", grid=(nt,),
        in_specs=[_rows(ts, d), _full((1, d)), _full(w_in.shape)] + [ANY] * np_,
        out_specs=[_rows(ts, e), _rows(ts, d)] + [ANY] * np_,
        out_shape=[SDS((s, e), F32), SDS((s, d), MM)] + [p.out for p in pushes],
        scratch_shapes=_push_scratch(np_),
        compiler_params=_seq_params(48),
    )(x, g0, w_in, *[p.array for p in pushes])
    return out[0], out[1], out[2:]


def _pool_counts(t0, ts, w):
    pos = (t0 + lax.broadcasted_iota(jnp.int32, (ts, LANES), 0) + 1).astype(F32)
    return jnp.minimum(pos, float(w))


def _pool_fwd(proj):
    s = proj.shape[0]
    width = LANES * len(POOL_WINDOWS)
    ts = 512
    per = ts // POOL_HALO

    def body(prev_ref, u_ref, out_ref, ext):
        i = pl.program_id(0)
        ext[0:POOL_HALO, :] = jnp.where(i == 0, 0.0, prev_ref[...])
        ext[POOL_HALO:, :] = u_ref[...]
        for g, w in enumerate(POOL_WINDOWS):
            cols = slice(g * LANES, (g + 1) * LANES)
            tok = ext[pl.ds(POOL_HALO, ts), cols]
            acc = tok
            for k in range(1, w):
                acc = acc + ext[pl.ds(POOL_HALO - k, ts), cols]
            out_ref[:, cols] = acc / _pool_counts(i * ts, ts, w) - tok

    return pl.pallas_call(
        body, name="pool_fwd", grid=(s // ts,),
        in_specs=[pl.BlockSpec((POOL_HALO, width), lambda i: (jnp.maximum(i * per - 1, 0), 0)),
                  _rows(ts, width)],
        out_specs=_rows(ts, width),
        out_shape=SDS((s, width), F32),
        scratch_shapes=[pltpu.VMEM((ts + POOL_HALO, width), F32)],
        compiler_params=_seq_params(32),
    )(proj, proj)


def _pool_bwd(dp):
    s, width = dp.shape
    ts = 512
    per = ts // POOL_HALO
    n = s // ts

    def body(dp_ref, next_ref, out_ref, ext):
        i = pl.program_id(0)
        nxt = jnp.where(i == n - 1, 0.0, next_ref[...])
        for g, w in enumerate(POOL_WINDOWS):
            cols = slice(g * LANES, (g + 1) * LANES)
            cur = dp_ref[:, cols]
            ext[0:ts, cols] = cur / _pool_counts(i * ts, ts, w)
            ext[ts:, cols] = nxt[:, cols] / _pool_counts((i + 1) * ts, POOL_HALO, w)
            acc = -cur
            for k in range(w):
                acc = acc + ext[pl.ds(k, ts), cols]
            out_ref[:, cols] = acc

    return pl.pallas_call(
        body, name="pool_bwd", grid=(n,),
        in_specs=[_rows(ts, width),
                  pl.BlockSpec((POOL_HALO, width), lambda i: (jnp.minimum((i + 1) * per, s // POOL_HALO - 1), 0))],
        out_specs=_rows(ts, width),
        out_shape=SDS((s, width), F32),
        scratch_shapes=[pltpu.VMEM((ts + POOL_HALO, width), F32)],
        compiler_params=_seq_params(32),
    )(dp, dp)


SSM_BLOCKS = 4


def _ssm_bu(proj, bb, pushes):
    s = proj.shape[0]
    nb, cb, two_nb = bb.shape
    sb = two_nb // 2
    cin, n = nb * cb, nb * sb
    ts = 512
    nt = s // ts
    np_ = len(pushes)

    def body(*refs):
        (u_ref, bb_ref), (re_ref, im_ref), _, push_refs = _split_refs(refs, 2, 2, np_)
        i = pl.program_id(0)
        _riding_start(pushes, i == 0, push_refs)
        for b in range(nb):
            bu = _dot(u_ref[:, b * cb:(b + 1) * cb], bb_ref[b])
            re_ref[:, b * sb:(b + 1) * sb] = bu[:, :sb]
            im_ref[:, b * sb:(b + 1) * sb] = bu[:, sb:]
        _riding_wait(pushes, i == nt - 1, push_refs)

    out = pl.pallas_call(
        body, name="ssm_bu", grid=(nt,),
        in_specs=[_rows(ts, cin, col=1), _full(bb.shape)] + [ANY] * np_,
        out_specs=[_rows(ts, n), _rows(ts, n)] + [ANY] * np_,
        out_shape=[SDS((s, n), F32), SDS((s, n), F32)] + [p.out for p in pushes],
        scratch_shapes=_push_scratch(np_),
        compiler_params=_seq_params(48),
    )(proj, bb, *[p.array for p in pushes])
    return out[0], out[1], out[2:]


SCAN_LANES = 256


def _cmul(a, b):
    return a[0] * b[0] - a[1] * b[1], a[0] * b[1] + a[1] * b[0]


SCAN_STEPS = (1, 2, 4)


def _fill_scan_tables(tab, ar_row, ai_row, reverse):
    shape = (SUBLANES, ar_row.shape[1])
    a1 = (jnp.broadcast_to(ar_row, shape), jnp.broadcast_to(ai_row, shape))
    a2 = _cmul(a1, a1)
    a4 = _cmul(a2, a2)
    a8 = _cmul(a4, a4)
    row = lax.broadcasted_iota(jnp.int32, shape, 0)
    for idx, (k, q) in enumerate(zip(SCAN_STEPS, (a1, a2, a4))):
        keep = (row < SUBLANES - k) if reverse else (row >= k)
        tab[2 * idx] = jnp.where(keep, q[0], 0.0)
        tab[2 * idx + 1] = jnp.where(keep, q[1], 0.0)
    expo = (SUBLANES - row) if reverse else (row + 1)
    pr, pi = jnp.ones(shape, F32), jnp.zeros(shape, F32)
    for bit, q in enumerate((a1, a2, a4, a8)):
        take = ((expo >> bit) & 1) == 1
        nr, ni = _cmul((pr, pi), q)
        pr, pi = jnp.where(take, nr, pr), jnp.where(take, ni, pi)
    tab[2 * len(SCAN_STEPS)] = pr
    tab[2 * len(SCAN_STEPS) + 1] = pi


def _group_scan(v, tab, cols, carry, reverse):
    vr, vi = v
    for idx, k in enumerate(SCAN_STEPS):
        shift = SUBLANES - k if reverse else k
        tr, ti = _cmul((tab[2 * idx, :, cols], tab[2 * idx + 1, :, cols]),
                       (pltpu.roll(vr, shift, 0), pltpu.roll(vi, shift, 0)))
        vr, vi = vr + tr, vi + ti
    last = 2 * len(SCAN_STEPS)
    tr, ti = _cmul((tab[last, :, cols], tab[last + 1, :, cols]), carry)
    return vr + tr, vi + ti


def _scan_fwd(a_re, a_im, bu_re, bu_im, pushes):
    s, n = bu_re.shape
    ts = 512
    nt = s // ts
    groups = ts // SUBLANES
    np_ = len(pushes)

    def body(*refs):
        ins, outs, scratch, push_refs = _split_refs(refs, 4, 2, np_)
        ar_ref, ai_ref, bre_ref, bim_ref = ins
        xre_ref, xim_ref = outs
        tab, cre, cim = scratch
        i = pl.program_id(0)
        _riding_start(pushes, i == 0, push_refs)

        @pl.when(i == 0)
        def _():
            _fill_scan_tables(tab, ar_ref[...], ai_ref[...], False)
            cre[...] = jnp.zeros_like(cre)
            cim[...] = jnp.zeros_like(cim)

        def group(gi, carry):
            r0 = pl.multiple_of(gi * SUBLANES, SUBLANES)
            for c in range(n // SCAN_LANES):
                cols = slice(c * SCAN_LANES, (c + 1) * SCAN_LANES)
                v = (bre_ref[pl.ds(r0, SUBLANES), cols], bim_ref[pl.ds(r0, SUBLANES), cols])
                vr, vi = _group_scan(v, tab, cols, (cre[:, cols], cim[:, cols]), False)
                xre_ref[pl.ds(r0, SUBLANES), cols] = vr
                xim_ref[pl.ds(r0, SUBLANES), cols] = vi
                cre[:, cols] = jnp.broadcast_to(vr[SUBLANES - 1:SUBLANES, :], vr.shape)
                cim[:, cols] = jnp.broadcast_to(vi[SUBLANES - 1:SUBLANES, :], vi.shape)
            return carry

        lax.fori_loop(0, groups, group, 0, unroll=2)
        _riding_wait(pushes, i == nt - 1, push_refs)

    out = pl.pallas_call(
        body, name="ssm_scan", grid=(nt,),
        in_specs=[_full((1, n)), _full((1, n)), _rows(ts, n), _rows(ts, n)] + [ANY] * np_,
        out_specs=[_rows(ts, n), _rows(ts, n)] + [ANY] * np_,
        out_shape=[SDS((s, n), F32), SDS((s, n), F32)] + [p.out for p in pushes],
        scratch_shapes=_push_scratch(np_) + [pltpu.VMEM((2 * len(SCAN_STEPS) + 2, SUBLANES, n), F32),
                                             pltpu.VMEM((SUBLANES, n), F32), pltpu.VMEM((SUBLANES, n), F32)],
        compiler_params=_seq_params(48),
    )(a_re, a_im, bu_re, bu_im, *[p.array for p in pushes])
    return out[0], out[1], out[2:]


def _scan_bwd(a_re, a_im, dx_re, dx_im, x_re, x_im, pushes):
    s, n = dx_re.shape
    ts = 256
    nt = s // ts
    groups = ts // SUBLANES
    np_ = len(pushes)

    def body(*refs):
        ins, outs, scratch, push_refs = _split_refs(refs, 6, 4, np_)
        ar_ref, ai_ref, dre_ref, dim_ref, xre_ref, xim_ref = ins
        gre_ref, gim_ref, dar_ref, dai_ref = outs
        tab, cre, cim, accr, acci = scratch
        i = pl.program_id(0)
        _riding_start(pushes, i == 0, push_refs)

        @pl.when(i == 0)
        def _():
            _fill_scan_tables(tab, ar_ref[...], -ai_ref[...], True)
            for ref in (cre, cim, accr, acci):
                ref[...] = jnp.zeros_like(ref)

        inner = lax.broadcasted_iota(jnp.int32, (SUBLANES, SCAN_LANES), 0) < SUBLANES - 1

        def group(k, carry):
            r0 = pl.multiple_of((groups - 1 - k) * SUBLANES, SUBLANES)
            for c in range(n // SCAN_LANES):
                cols = slice(c * SCAN_LANES, (c + 1) * SCAN_LANES)
                after = (cre[:, cols], cim[:, cols])
                v = (dre_ref[pl.ds(r0, SUBLANES), cols], dim_ref[pl.ds(r0, SUBLANES), cols])
                vr, vi = _group_scan(v, tab, cols, after, True)
                gre_ref[pl.ds(r0, SUBLANES), cols] = vr
                gim_ref[pl.ds(r0, SUBLANES), cols] = vi
                nr = jnp.where(inner, pltpu.roll(vr, SUBLANES - 1, 0), after[0])
                ni = jnp.where(inner, pltpu.roll(vi, SUBLANES - 1, 0), after[1])
                xr, xi = xre_ref[pl.ds(r0, SUBLANES), cols], xim_ref[pl.ds(r0, SUBLANES), cols]
                accr[:, cols] += nr * xr + ni * xi
                acci[:, cols] += ni * xr - nr * xi
                cre[:, cols] = jnp.broadcast_to(vr[0:1, :], vr.shape)
                cim[:, cols] = jnp.broadcast_to(vi[0:1, :], vi.shape)
            return carry

        lax.fori_loop(0, groups, group, 0, unroll=2)

        @pl.when(i == nt - 1)
        def _():
            dar_ref[...] = jnp.sum(accr[...], axis=0, keepdims=True)
            dai_ref[...] = jnp.sum(acci[...], axis=0, keepdims=True)

        _riding_wait(pushes, i == nt - 1, push_refs)

    small = pltpu.VMEM((SUBLANES, n), F32)
    out = pl.pallas_call(
        body, name="ssm_scan_bwd", grid=(nt,),
        in_specs=[_full((1, n)), _full((1, n))] + [_rows_rev(ts, n, nt)] * 4 + [ANY] * np_,
        out_specs=[_rows_rev(ts, n, nt), _rows_rev(ts, n, nt), _full((1, n)), _full((1, n))] + [ANY] * np_,
        out_shape=[SDS((s, n), F32), SDS((s, n), F32), SDS((1, n), F32), SDS((1, n), F32)]
        + [p.out for p in pushes],
        scratch_shapes=_push_scratch(np_) + [pltpu.VMEM((2 * len(SCAN_STEPS) + 2, SUBLANES, n), F32),
                                             small, small, small, small],
        compiler_params=_seq_params(48),
    )(a_re, a_im, dx_re, dx_im, x_re, x_im, *[p.array for p in pushes])
    return out[0], out[1], out[2], out[3], out[4:]


def _mix_a(mixed, scale, za):
    return mixed * scale * _silu(za)


def _mix_b(val, gate, zb):
    return val * _sigmoid(gate) * _silu(zb)


def _even_mix(pooled, proj, x_re, x_im, x, pw, scale, cc_re, cc_im, d_skip, w_glu, w_out, pushes):
    s, d = x.shape
    half = pooled.shape[1]
    n = x_re.shape[1]
    ts = 256
    nt = s // ts
    np_ = len(pushes)

    def body(*refs):
        ins, outs, _, push_refs = _split_refs(refs, 13, 5, np_)
        (pooled_ref, u_ref, z_ref, xre_ref, xim_ref, x_ref, pw_ref, scale_ref, ccre_ref, ccim_ref, d_ref,
         wglu_ref, wout_ref) = ins
        x1_ref, ymix_ref, mixed_ref, yin_ref, gv_ref = outs
        i = pl.program_id(0)
        _riding_start(pushes, i == 0, push_refs)
        for g in range(len(POOL_WINDOWS)):
            cols = slice(g * LANES, (g + 1) * LANES)
            mixed_ref[:, cols] = _dot(pooled_ref[:, cols], pw_ref[g])
        sb, cb = ccre_ref.shape[1:]
        for b in range(SSM_BLOCKS):
            states, chans = slice(b * sb, (b + 1) * sb), slice(b * cb, (b + 1) * cb)
            yin_ref[:, chans] = (_dot(xre_ref[:, states], ccre_ref[b]) - _dot(xim_ref[:, states], ccim_ref[b])
                                 + d_ref[:, chans] * u_ref[:, chans])
        yin = yin_ref[...]
        gv = _dot(yin, wglu_ref[...])
        gv_ref[...] = gv
        ya = _mix_a(mixed_ref[...], scale_ref[...], z_ref[:, :half])
        yb = _mix_b(gv[:, :half], gv[:, half:], z_ref[:, half:])
        ymix = jnp.concatenate([ya, yb], axis=1).astype(MM)
        ymix_ref[...] = ymix
        x1_ref[...] = x_ref[...] + jnp.dot(ymix, wout_ref[...], preferred_element_type=F32)
        _riding_wait(pushes, i == nt - 1, push_refs)

    out = pl.pallas_call(
        body, name="even_mix", grid=(nt,),
        in_specs=[_rows(ts, half), _rows(ts, half, col=1), _rows(ts, d, col=1), _rows(ts, n), _rows(ts, n),
                  _rows(ts, d), _full(pw.shape), _full((1, half)), _full(cc_re.shape), _full(cc_im.shape),
                  _full((1, half)), _full((half, d)), _full((d, d))] + [ANY] * np_,
        out_specs=[_rows(ts, d), _rows(ts, d), _rows(ts, half), _rows(ts, half), _rows(ts, d)] + [ANY] * np_,
        out_shape=[SDS((s, d), F32), SDS((s, d), MM), SDS((s, half), F32), SDS((s, half), F32), SDS((s, d), F32)]
        + [p.out for p in pushes],
        scratch_shapes=_push_scratch(np_),
        compiler_params=_seq_params(56),
    )(pooled, proj, proj, x_re, x_im, x, pw, scale, cc_re, cc_im, d_skip, w_glu, w_out,
      *[p.array for p in pushes])
    return out[:5], out[5:]


def _glu(val, gt):
    return val * _sigmoid(gt)


def _odd_in(x1, g1, w_in):
    s, d = x1.shape
    e = w_in.shape[1]
    ts = 512

    def body(x_ref, g_ref, w_ref, proj_ref, glu_ref, h_ref):
        hb = _rms(x_ref[...], g_ref[...]).astype(MM)
        h_ref[...] = hb
        proj_ref[...] = jnp.dot(hb, w_ref[...], preferred_element_type=F32)
        glu_ref[...] = _glu(proj_ref[:, :d], proj_ref[:, d:2 * d])

    return pl.pallas_call(
        body, name="odd_in", grid=(s // ts,),
        in_specs=[_rows(ts, d), _full((1, d)), _full((d, e))],
        out_specs=[_rows(ts, e), _rows(ts, d), _rows(ts, d)],
        out_shape=[SDS((s, e), F32), SDS((s, d), F32), SDS((s, d), MM)],
        compiler_params=_seq_params(56),
    )(x1, g1, w_in)


CONV_ROWS = 32
CONV_COLS = 256


def _conv_scratch(ts, ch):
    return pltpu.VMEM((SUBLANES, ts + CONV_HALO, ch + LANES), F32)


def _phase_copies(sh, rows, ch):
    for o in range(1, SUBLANES):
        sh[o, 0:rows, 0:ch] = sh[0, pl.ds(o, rows), 0:ch]


def _conv_taps(sh, w_ref, offsets, r0, cols):
    acc = jnp.zeros((CONV_ROWS, cols.stop - cols.start), F32)
    for o in range(SUBLANES):
        taps = [(j, e // SUBLANES) for j, e in offsets if e % SUBLANES == o]
        if not taps:
            continue
        q0 = min(q for _, q in taps)
        q1 = max(q for _, q in taps)
        win = sh[o, pl.ds(r0 + SUBLANES * q0, CONV_ROWS + SUBLANES * (q1 - q0)), cols]
        for j, q in taps:
            lo = SUBLANES * (q - q0)
            acc = acc + w_ref[j:j + 1, cols] * win[lo:lo + CONV_ROWS]
    return acc


def _conv_fwd(g, w):
    s, ch = g.shape
    ts = 256
    per = ts // CONV_HALO
    lead = CONV_HALO - (CONV_KERNEL - 1)
    span = ts + CONV_HALO - SUBLANES

    def body(prev_ref, g_ref, w_ref, out_ref, sh):
        i = pl.program_id(0)
        sh[0, 0:CONV_HALO, 0:ch] = jnp.where(i == 0, 0.0, prev_ref[...])
        sh[0, CONV_HALO:, 0:ch] = g_ref[...]
        _phase_copies(sh, span, ch)

        offsets = [(j, lead + j) for j in range(CONV_KERNEL)]

        def block(rb, carry):
            r0 = pl.multiple_of(rb * CONV_ROWS, CONV_ROWS)
            for c in range(ch // CONV_COLS):
                cols = slice(c * CONV_COLS, (c + 1) * CONV_COLS)
                out_ref[pl.ds(r0, CONV_ROWS), cols] = _conv_taps(sh, w_ref, offsets, r0, cols)
            return carry

        lax.fori_loop(0, ts // CONV_ROWS, block, 0)

    return pl.pallas_call(
        body, name="conv_fwd", grid=(s // ts,),
        in_specs=[pl.BlockSpec((CONV_HALO, ch), lambda i: (jnp.maximum(i * per - 1, 0), 0)),
                  _rows(ts, ch), _full(w.shape)],
        out_specs=_rows(ts, ch),
        out_shape=SDS((s, ch), F32),
        scratch_shapes=[_conv_scratch(ts, ch)],
        compiler_params=_seq_params(40),
    )(g, g, w)


def _conv_bwd(dc, g, w, pushes):
    s, ch = dc.shape
    ts = 256
    per = ts // CONV_HALO
    n = s // ts
    lead = CONV_HALO - (CONV_KERNEL - 1)
    span = ts + CONV_HALO - SUBLANES
    np_ = len(pushes)

    def body(*refs):
        ins, outs, scratch, push_refs = _split_refs(refs, 5, 2, np_)
        dc_ref, next_ref, prev_ref, g_ref, w_ref = ins
        dg_ref, dw_ref = outs
        shd, shg, wacc = scratch
        i = pl.program_id(0)
        _riding_start(pushes, i == 0, push_refs)

        @pl.when(i == 0)
        def _():
            wacc[...] = jnp.zeros_like(wacc)

        shd[0, 0:ts, 0:ch] = dc_ref[...]
        shd[0, ts:, 0:ch] = jnp.where(i == n - 1, 0.0, next_ref[...])
        shg[0, 0:CONV_HALO, 0:ch] = jnp.where(i == 0, 0.0, prev_ref[...])
        shg[0, CONV_HALO:, 0:ch] = g_ref[...]
        _phase_copies(shd, span, ch)
        _phase_copies(shg, span, ch)

        offsets = [(j, CONV_KERNEL - 1 - j) for j in range(CONV_KERNEL)]

        def dg_block(rb, carry):
            r0 = pl.multiple_of(rb * CONV_ROWS, CONV_ROWS)
            for c in range(ch // CONV_COLS):
                cols = slice(c * CONV_COLS, (c + 1) * CONV_COLS)
                dg_ref[pl.ds(r0, CONV_ROWS), cols] = _conv_taps(shd, w_ref, offsets, r0, cols)
            return carry

        lax.fori_loop(0, ts // CONV_ROWS, dg_block, 0)

        tiles = ts // SUBLANES
        for c in range(ch // LANES):
            cols = slice(c * LANES, (c + 1) * LANES)
            cur = [shd[0, k * SUBLANES:(k + 1) * SUBLANES, cols] for k in range(tiles)]
            for j in range(CONV_KERNEL):
                q, o = divmod(lead + j, SUBLANES)
                parts = [None] * 4
                for k in range(tiles):
                    term = cur[k] * shg[o, (k + q) * SUBLANES:(k + q + 1) * SUBLANES, cols]
                    parts[k % 4] = term if parts[k % 4] is None else parts[k % 4] + term
                wacc[j, :, cols] += (parts[0] + parts[1]) + (parts[2] + parts[3])

        @pl.when(i == n - 1)
        def _():
            dw_ref[...] = jnp.zeros_like(dw_ref)
            for j in range(CONV_KERNEL):
                dw_ref[j:j + 1, :] = jnp.sum(wacc[j], axis=0, keepdims=True)

        _riding_wait(pushes, i == n - 1, push_refs)

    out = pl.pallas_call(
        body, name="conv_bwd", grid=(n,),
        in_specs=[_rows(ts, ch),
                  pl.BlockSpec((CONV_HALO, ch), lambda i: (jnp.minimum((i + 1) * per, s // CONV_HALO - 1), 0)),
                  pl.BlockSpec((CONV_HALO, ch), lambda i: (jnp.maximum(i * per - 1, 0), 0)),
                  _rows(ts, ch), _full(w.shape)] + [ANY] * np_,
        out_specs=[_rows(ts, ch), _full((CONV_HALO, ch))] + [ANY] * np_,
        out_shape=[SDS((s, ch), F32), SDS((CONV_HALO, ch), F32)] + [p.out for p in pushes],
        scratch_shapes=_push_scratch(np_) + [_conv_scratch(ts, ch), _conv_scratch(ts, ch),
                                             pltpu.VMEM((CONV_HALO, SUBLANES, ch), F32)],
        compiler_params=_seq_params(56),
    )(dc, dc, g, g, w, *[p.array for p in pushes])
    return out[0], out[1], out[2:]


def _conv_act(c, z1, cb, lg, lb):
    cc = c + cb
    mu = jnp.mean(cc, axis=-1, keepdims=True)
    dev = cc - mu
    var = jnp.mean(dev * dev, axis=-1, keepdims=True)
    cn = dev * lax.rsqrt(var + LN_EPS) * lg + lb
    return _silu(cn) * _silu(z1)


def _odd_out(c, proj1, x1, tgt, cb, lg, lb, w_out, fg):
    s, d = c.shape
    ts = 256

    def body(c_ref, z_ref, x1_ref, t_ref, cb_ref, lg_ref, lb_ref, w_ref, fg_ref,
             dc_ref, dz_ref, dx2_ref, y1_ref, loss_ref, dcb_ref, dlg_ref, dlb_ref, dfg_ref):
        @pl.when(pl.program_id(0) == 0)
        def _():
            for ref in (loss_ref, dcb_ref, dlg_ref, dlb_ref, dfg_ref):
                ref[...] = jnp.zeros_like(ref)

        y1, act_vjp = jax.vjp(_conv_act, c_ref[...], z_ref[...], cb_ref[...], lg_ref[...], lb_ref[...])
        y1b = y1.astype(MM)
        y1_ref[...] = y1b
        x2 = x1_ref[...] + jnp.dot(y1b, w_ref[...], preferred_element_type=F32)
        tgt_tile = t_ref[...]

        def head(x2, fg):
            err = jnp.square(_rms(x2, fg) - tgt_tile)
            return 0.5 * jnp.sum(jnp.mean(err, axis=-1))

        loss, (dx2, dfg) = jax.value_and_grad(head, argnums=(0, 1))(x2, fg_ref[...])
        loss_ref[...] += loss
        dfg_ref[...] += dfg
        dx2_ref[...] = dx2
        dy1 = _dot_nt(dx2, w_ref[...])
        dc, dz, dcb, dlg, dlb = act_vjp(dy1)
        dc_ref[...] = dc
        dz_ref[...] = dz
        dcb_ref[...] += dcb
        dlg_ref[...] += dlg
        dlb_ref[...] += dlb

    vec = SDS((1, d), F32)
    return pl.pallas_call(
        body, name="odd_out", grid=(s // ts,),
        in_specs=[_rows(ts, d), _rows(ts, d, col=2), _rows(ts, d), _rows(ts, d), _full((1, d)), _full((1, d)),
                  _full((1, d)), _full((d, d)), _full((1, d))],
        out_specs=[_rows(ts, d), _rows(ts, d), _rows(ts, d), _rows(ts, d), _full((SUBLANES, LANES)),
                   _full((1, d)), _full((1, d)), _full((1, d)), _full((1, d))],
        out_shape=[SDS((s, d), F32), SDS((s, d), F32), SDS((s, d), F32), SDS((s, d), MM),
                   SDS((SUBLANES, LANES), F32), vec, vec, vec, vec],
        compiler_params=_seq_params(56),
    )(c, proj1, x1, tgt, cb, lg, lb, w_out, fg)


def _odd_in_bwd(x1, proj1, dglu, dz1, dx2, g1, w_in):
    s, d = x1.shape
    e = w_in.shape[1]
    ts = 256

    def body(x_ref, vg_ref, dglu_ref, dz_ref, dx2_ref, g_ref, w_ref, dx1_ref, dp_ref, dg_ref):
        @pl.when(pl.program_id(0) == 0)
        def _():
            dg_ref[...] = jnp.zeros_like(dg_ref)

        _, glu_vjp = jax.vjp(_glu, vg_ref[:, :d], vg_ref[:, d:])
        dval, dgt = glu_vjp(dglu_ref[...])
        dp = jnp.concatenate([dval, dgt, dz_ref[...]], axis=1).astype(MM)
        dp_ref[...] = dp
        dh = _dot_nt(dp, w_ref[...])
        _, rms_vjp = jax.vjp(_rms, x_ref[...], g_ref[...])
        dxa, dg = rms_vjp(dh)
        dx1_ref[...] = dxa + dx2_ref[...]
        dg_ref[...] += dg

    return pl.pallas_call(
        body, name="odd_in_bwd", grid=(s // ts,),
        in_specs=[_rows(ts, d), _rows(ts, 2 * d), _rows(ts, d), _rows(ts, d), _rows(ts, d), _full((1, d)),
                  _full((d, e))],
        out_specs=[_rows(ts, d), _rows(ts, e), _full((1, d))],
        out_shape=[SDS((s, d), F32), SDS((s, e), MM), SDS((1, d), F32)],
        compiler_params=_seq_params(56),
    )(x1, proj1, dglu, dz1, dx2, g1, w_in)


def _even_mix_bwd(pooled, proj, mixed, gv, dx1, pw, scale, cc_re, cc_im, d_skip, w_glu, w_out, pushes):
    s, d = dx1.shape
    half = pooled.shape[1]
    n = cc_re.shape[0] * cc_re.shape[1]
    ts = 256
    nt = s // ts
    groups = len(POOL_WINDOWS)
    np_ = len(pushes)

    def body(*refs):
        ins, outs, _, push_refs = _split_refs(refs, 13, 10, np_)
        (pooled_ref, u_ref, z_ref, mixed_ref, gv_ref, dx1_ref, pw_ref, scale_ref, ccre_ref, ccim_ref, d_ref,
         wglu_ref, wout_ref) = ins
        (dpooled_ref, du_ref, dz_ref, dxre_ref, dxim_ref, dyin_ref, dgv_ref, dpw_ref, dscale_ref, dd_ref) = outs
        i = pl.program_id(0)
        _riding_start(pushes, i == 0, push_refs)

        @pl.when(i == 0)
        def _():
            for ref in (dpw_ref, dscale_ref, dd_ref):
                ref[...] = jnp.zeros_like(ref)

        dymix = _dot_nt(dx1_ref[...], wout_ref[...])
        _, a_vjp = jax.vjp(_mix_a, mixed_ref[...], scale_ref[...], z_ref[:, :half])
        dmixed, dscale, dza = a_vjp(dymix[:, :half])
        _, b_vjp = jax.vjp(_mix_b, gv_ref[:, :half], gv_ref[:, half:], z_ref[:, half:])
        dval, dgate, dzb = b_vjp(dymix[:, half:])
        dz_ref[:, :half] = dza
        dz_ref[:, half:] = dzb
        dscale_ref[...] += dscale
        dgv = jnp.concatenate([dval, dgate], axis=1).astype(MM)
        dgv_ref[...] = dgv
        dyin = _dot_nt(dgv, wglu_ref[...])
        dd_ref[...] += jnp.sum(dyin * u_ref[...], axis=0, keepdims=True)
        du_ref[...] = d_ref[...] * dyin
        dyb = dyin.astype(MM)
        dyin_ref[...] = dyb
        sb, cb = ccre_ref.shape[1:]
        for b in range(SSM_BLOCKS):
            states, chans = slice(b * sb, (b + 1) * sb), slice(b * cb, (b + 1) * cb)
            dxre_ref[:, states] = _dot_nt(dyb[:, chans], ccre_ref[b])
            dxim_ref[:, states] = -_dot_nt(dyb[:, chans], ccim_ref[b])
        for g in range(groups):
            cols = slice(g * LANES, (g + 1) * LANES)
            dm = dmixed[:, cols].astype(MM)
            dpooled_ref[:, cols] = _dot_nt(dm, pw_ref[g])
            dpw_ref[g] += _dot_tn(pooled_ref[:, cols], dm)

        _riding_wait(pushes, i == nt - 1, push_refs)

    out = pl.pallas_call(
        body, name="even_mix_bwd", grid=(nt,),
        in_specs=[_rows(ts, half), _rows(ts, half, col=1), _rows(ts, d, col=1), _rows(ts, half), _rows(ts, d),
                  _rows(ts, d), _full(pw.shape), _full((1, half)), _full(cc_re.shape), _full(cc_im.shape),
                  _full((1, half)), _full((half, d)), _full((d, d))] + [ANY] * np_,
        out_specs=[_rows(ts, half), _rows(ts, half), _rows(ts, d), _rows(ts, n), _rows(ts, n), _rows(ts, half),
                   _rows(ts, d), _full(pw.shape), _full((1, half)), _full((1, half))] + [ANY] * np_,
        out_shape=[SDS((s, half), F32), SDS((s, half), F32), SDS((s, d), F32), SDS((s, n), F32), SDS((s, n), F32),
                   SDS((s, half), MM), SDS((s, d), MM), SDS(pw.shape, F32), SDS((1, half), F32),
                   SDS((1, half), F32)] + [p.out for p in pushes],
        scratch_shapes=_push_scratch(np_),
        compiler_params=_seq_params(56),
    )(pooled, proj, proj, mixed, gv, dx1, pw, scale, cc_re, cc_im, d_skip, w_glu, w_out,
      *[p.array for p in pushes])
    return out[:10], out[10:]


def _ssm_bu_bwd(g_re, g_im, du_skip, bb):
    s, n = g_re.shape
    nb, cb, two_nb = bb.shape
    sb = two_nb // 2
    cin = nb * cb
    ts = 512

    def body(gre_ref, gim_ref, du_ref, bb_ref, out_ref):
        for b in range(nb):
            states, chans = slice(b * sb, (b + 1) * sb), slice(b * cb, (b + 1) * cb)
            out_ref[:, chans] = (du_ref[:, chans] + _dot_nt(gre_ref[:, states], bb_ref[b, :, :sb])
                                 + _dot_nt(gim_ref[:, states], bb_ref[b, :, sb:]))

    return pl.pallas_call(
        body, name="ssm_bu_bwd", grid=(s // ts,),
        in_specs=[_rows(ts, n), _rows(ts, n), _rows(ts, cin), _full(bb.shape)],
        out_specs=_rows(ts, cin),
        out_shape=SDS((s, cin), F32),
        compiler_params=_seq_params(48),
    )(g_re, g_im, du_skip, bb)


def _even_in_bwd(x, du_pool, du_ssm, dz, dx1, g0, w_in):
    s, d = x.shape
    half = du_pool.shape[1]
    shards, _, wc = w_in.shape
    ts = 256

    def body(x_ref, dup_ref, dus_ref, dz_ref, dx1_ref, g_ref, w_ref, gx_ref, dg_ref):
        @pl.when(pl.program_id(0) == 0)
        def _():
            dg_ref[...] = jnp.zeros_like(dg_ref)

        dp = jnp.concatenate([dup_ref[...], dus_ref[...], dz_ref[...]], axis=1).astype(MM)
        dh = _dot_nt(dp[:, :wc], w_ref[0])
        for k in range(1, shards):
            dh = dh + _dot_nt(dp[:, k * wc:(k + 1) * wc], w_ref[k])
        _, rms_vjp = jax.vjp(_rms, x_ref[...], g_ref[...])
        dxa, dg = rms_vjp(dh)
        gx_ref[...] = dxa + dx1_ref[...]
        dg_ref[...] += dg

    return pl.pallas_call(
        body, name="even_in_bwd", grid=(s // ts,),
        in_specs=[_rows(ts, d), _rows(ts, half), _rows(ts, half), _rows(ts, d), _rows(ts, d), _full((1, d)),
                  _full(w_in.shape)],
        out_specs=[_rows(ts, d), _full((1, d))],
        out_shape=[SDS((s, d), F32), SDS((1, d), F32)],
        compiler_params=_seq_params(48),
    )(x, du_pool, du_ssm, dz, dx1, g0, w_in)


def _discretise(log_dt, ar, ai, br, bi):
    dt = jnp.exp(log_dt)
    mag = jnp.exp(ar * dt)
    ang = ai * dt
    abr = mag * jnp.cos(ang)
    abi = mag * jnp.sin(ang)
    den = ar * ar + ai * ai
    nr = abr - 1.0
    ni = abi
    kr = (nr * ar + ni * ai) / den
    ki = (ni * ar - nr * ai) / den
    bbr = kr[None] * br - ki[None] * bi
    bbi = kr[None] * bi + ki[None] * br
    return abr, abi, bbr, bbi


def _whole(n):
    return [pl.BlockSpec(memory_space=pltpu.VMEM)] * n


def _disc_fwd(log_dt, ar, ai, br, bi):
    def body(ld_ref, ar_ref, ai_ref, br_ref, bi_ref, abr_ref, abi_ref, bbr_ref, bbi_ref):
        out = _discretise(ld_ref[...], ar_ref[...], ai_ref[...], br_ref[...], bi_ref[...])
        for ref, val in zip((abr_ref, abi_ref, bbr_ref, bbi_ref), out):
            ref[...] = val

    return pl.pallas_call(
        body, name="ssm_discretise", in_specs=_whole(5), out_specs=_whole(4),
        out_shape=[SDS(ar.shape, F32), SDS(ar.shape, F32), SDS(br.shape, F32), SDS(br.shape, F32)],
    )(log_dt, ar, ai, br, bi)


def _disc_bwd(log_dt, ar, ai, br, bi, dabr, dabi, dbbr, dbbi):
    def body(ld_ref, ar_ref, ai_ref, br_ref, bi_ref, dabr_ref, dabi_ref, dbbr_ref, dbbi_ref,
             dld_ref, dar_ref, dai_ref, dbr_ref, dbi_ref):
        _, vjp = jax.vjp(_discretise, ld_ref[...], ar_ref[...], ai_ref[...], br_ref[...], bi_ref[...])
        grads = vjp((dabr_ref[...], dabi_ref[...], dbbr_ref[...], dbbi_ref[...]))
        for ref, val in zip((dld_ref, dar_ref, dai_ref, dbr_ref, dbi_ref), grads):
            ref[...] = val

    return pl.pallas_call(
        body, name="ssm_discretise_bwd", in_specs=_whole(9), out_specs=_whole(5),
        out_shape=[SDS(log_dt.shape, F32), SDS(ar.shape, F32), SDS(ar.shape, F32), SDS(br.shape, F32),
                   SDS(br.shape, F32)],
    )(log_dt, ar, ai, br, bi, dabr, dabi, dbbr, dbbi)


def _block_diag(t):
    g, a, b = t.shape
    per = g // SSM_BLOCKS
    t = t.reshape(SSM_BLOCKS, per, a, b)
    same = jnp.eye(per, dtype=bool)[None, :, None, :, None]
    return jnp.where(same, t[:, :, :, None, :], 0.0).reshape(SSM_BLOCKS, per * a, per * b)


def _diag_blocks(m, g):
    per = g // SSM_BLOCKS
    a, b = m.shape[1] // per, m.shape[2] // per
    d = jnp.diagonal(m.reshape(SSM_BLOCKS, per, a, per, b), axis1=1, axis2=3)
    return jnp.moveaxis(d, -1, 1).reshape(g, a, b)


def _adamw(name, parts, w, m, v):
    rows, cols = w.shape
    tr = ADAM_ROWS if rows % ADAM_ROWS == 0 else rows

    def body(p_ref, w_ref, m_ref, v_ref, g_ref, d_ref, nm_ref, nv_ref):
        g = p_ref[0].astype(F32)
        for dev in range(1, N_DEV):
            g = g + p_ref[dev].astype(F32)
        g_ref[...] = g
        nm = ADAM_B1 * m_ref[...] + (1.0 - ADAM_B1) * g
        nv = ADAM_B2 * v_ref[...] + (1.0 - ADAM_B2) * jnp.square(g)
        nm_ref[...] = nm
        nv_ref[...] = nv
        m_hat = nm / (1.0 - ADAM_B1 ** ADAM_STEP)
        v_hat = nv / (1.0 - ADAM_B2 ** ADAM_STEP)
        d_ref[...] = -ADAM_LR * (m_hat / (jnp.sqrt(v_hat) + ADAM_EPS) + ADAM_WD * w_ref[...])

    tile = _rows(tr, cols)
    out = SDS((rows, cols), F32)
    return pl.pallas_call(
        body, name=name, grid=(rows // tr,),
        in_specs=[pl.BlockSpec((N_DEV, tr, cols), lambda i: (0, i, 0)), tile, tile, tile],
        out_specs=[tile, tile, tile, tile],
        out_shape=[out, out, out, out],
        compiler_params=_seq_params(32),
    )(parts, w, m, v)


def _as_rows(flat):
    n = flat.shape[-1]
    padded = _round_up(n, PACK_TILE)
    if padded != n:
        flat = jnp.pad(flat, [(0, 0)] * (flat.ndim - 1) + [(0, padded - n)])
    return flat.reshape(flat.shape[:-1] + (padded // LANES, LANES))


def _columns_to_shards(full):
    r, c8 = full.shape
    return jnp.transpose(full.reshape(r, N_DEV, c8 // N_DEV), (1, 0, 2))


def _shards_to_columns(shards):
    n, r, c = shards.shape
    return jnp.transpose(shards, (1, 0, 2)).reshape(r, n * c)


def _pack_local(tensors):
    return jnp.concatenate([_as_rows(t.reshape(-1)) for t in tensors], axis=0)


def _unpack_local(packed, shapes):
    out, r = [], 0
    for shape in shapes:
        n = 1
        for dim in shape:
            n *= dim
        rows = _round_up(n, PACK_TILE) // LANES
        out.append(packed[r:r + rows].reshape(-1)[:n].reshape(shape))
        r += rows
    return out


def kernel(x, even_norm, even_w_in, pool_w, pool_scale, ssm_log_dt, ssm_a_re, ssm_a_im, ssm_b_re, ssm_b_im, ssm_c_re, ssm_c_im, ssm_d, ssm_w_glu, even_w_out, odd_norm, odd_w_in, conv_w, conv_b, conv_ln_g, conv_ln_b, odd_w_out, final_norm, loss_target, m_even_norm, m_even_w_in, m_pool_w, m_pool_scale, m_ssm_log_dt, m_ssm_a_re, m_ssm_a_im, m_ssm_b_re, m_ssm_b_im, m_ssm_c_re, m_ssm_c_im, m_ssm_d, m_ssm_w_glu, m_even_w_out, m_odd_norm, m_odd_w_in, m_conv_w, m_conv_b, m_conv_ln_g, m_conv_ln_b, m_odd_w_out, m_final_norm, v_even_norm, v_even_w_in, v_pool_w, v_pool_scale, v_ssm_log_dt, v_ssm_a_re, v_ssm_a_im, v_ssm_b_re, v_ssm_b_im, v_ssm_c_re, v_ssm_c_im, v_ssm_d, v_ssm_w_glu, v_even_w_out, v_odd_norm, v_odd_w_in, v_conv_w, v_conv_b, v_conv_ln_g, v_conv_ln_b, v_odd_w_out, v_final_norm):
    given = dict(locals())
    xs = x[0]
    tgt = loss_target[0]
    d_model = xs.shape[1]

    def local(prefix, name):
        t = given[prefix + name]
        return t if name == 'final_norm' else t[0]

    def small_block(prefix):
        parts = [jnp.pad(local(prefix, 'conv_w'), ((0, SMALL_AT['odd_norm'] - CONV_KERNEL), (0, 0)))]
        for n in SMALL_VECTORS:
            parts.append(jnp.pad(local(prefix, n).reshape(1, LANES), ((0, SUBLANES - 1), (0, 0))))
        return jnp.concatenate(parts, axis=0)

    def as_tile(row):
        return jnp.pad(row, ((0, SUBLANES - row.shape[0]), (0, 0)))

    (w_in_e,) = _push_alone("gather_first", [_Push(local('', 'even_w_in').astype(MM), False)])
    later = [_Push(local('', 'ssm_w_glu').astype(MM), False), _Push(local('', 'even_w_out').astype(MM), False),
             _Push(local('', 'odd_w_in').astype(MM), False), _Push(local('', 'odd_w_out').astype(MM), False),
             _Push(small_block(''), False)]

    a_re, a_im = local('', 'ssm_a_re'), local('', 'ssm_a_im')
    groups, state = a_re.shape
    log_dt = local('', 'ssm_log_dt').reshape(groups, 1)
    b_re_t = jnp.transpose(local('', 'ssm_b_re'), (2, 0, 1))
    b_im_t = jnp.transpose(local('', 'ssm_b_im'), (2, 0, 1))
    abr, abi, bbr, bbi = _disc_fwd(log_dt, a_re, a_im, b_re_t, b_im_t)
    a_re_row = abr.reshape(1, groups * state)
    a_im_row = abi.reshape(1, groups * state)
    bb = jnp.concatenate([_block_diag(jnp.transpose(bbr, (1, 0, 2))), _block_diag(jnp.transpose(bbi, (1, 0, 2)))],
                         axis=2).astype(MM)
    cc_re = _block_diag(jnp.transpose(local('', 'ssm_c_re'), (0, 2, 1))).astype(MM)
    cc_im = _block_diag(jnp.transpose(local('', 'ssm_c_im'), (0, 2, 1))).astype(MM)
    pw = local('', 'pool_w').astype(MM)
    g0 = local('', 'even_norm').reshape(1, d_model)
    fg = local('', 'final_norm').reshape(1, d_model)
    scale = local('', 'pool_scale').reshape(1, -1)
    d_skip = local('', 'ssm_d').reshape(1, -1)

    proj0, h0, (wg3, wo_e3) = _even_in(xs, g0, w_in_e, later[:2])
    pooled = _pool_fwd(proj0)
    bu_re, bu_im, (wo_o3, small8) = _ssm_bu(proj0, bb, later[3:])
    x_re, x_im, (wi_o3,) = _scan_fwd(a_re_row, a_im_row, bu_re, bu_im, later[2:3])
    w_glu = _shards_to_columns(wg3)
    w_out_e = wo_e3.reshape(d_model, d_model)
    (x1, ymix, mixed, yin, gv), _ = _even_mix(
        pooled, proj0, x_re, x_im, xs, pw, scale, cc_re, cc_im, d_skip, w_glu, w_out_e, [])
    w_in_o = _shards_to_columns(wi_o3)
    w_out_o = wo_o3.reshape(d_model, d_model)
    small = _shards_to_columns(small8)
    cw = small[:CONV_KERNEL]
    g1, cb, lg, lb = (small[SMALL_AT[n]:SMALL_AT[n] + 1] for n in SMALL_VECTORS)
    proj1, glu, h1 = _odd_in(x1, g1, w_in_o)
    conv = _conv_fwd(glu, cw)

    half = pooled.shape[1]
    n_state = groups * state
    rows_per = d_model // N_DEV
    dc, dz1, dx2, y1, loss_tile, dcb, dlg, dlb, dfg = _odd_out(conv, proj1, x1, tgt, cb, lg, lb, w_out_o, fg)
    g_odd_out = _mm_tn("dw_odd_out", y1, dx2, d_model, d_model).astype(WIRE).reshape(N_DEV, rows_per, d_model)
    dglu, dcw, (r_odd_out,) = _conv_bwd(dc, glu, cw, [_Push(g_odd_out, True)])
    dx1, dproj1, dg1 = _odd_in_bwd(x1, proj1, dglu, dz1, dx2, g1, w_in_o)
    g_odd_in = _columns_to_shards(_mm_tn("dw_odd_in", h1, dproj1, d_model, 3 * d_model).astype(WIRE))
    g_even_out = _mm_tn("dw_even_out", ymix, dx1, d_model, d_model).astype(WIRE).reshape(N_DEV, rows_per, d_model)
    mix_grads, (r_odd_in,) = _even_mix_bwd(pooled, proj0, mixed, gv, dx1, pw, scale, cc_re, cc_im, d_skip, w_glu,
                                           w_out_e, [_Push(g_odd_in, True)])
    dpooled, du_skip, dz0, dx_re, dx_im, dyin, dgv, dpw, dscale, dd = mix_grads
    g_glu = _columns_to_shards(_mm_tn("dw_glu", yin, dgv, half, d_model).astype(WIRE))
    sblk, cblk = n_state // SSM_BLOCKS, half // SSM_BLOCKS
    dcc_re = _mm_tn_blocks("dw_c_re", x_re, dyin, sblk, cblk, SSM_BLOCKS, ts=2048)
    dcc_im = _mm_tn_blocks("dw_c_im", x_im, dyin, sblk, cblk, SSM_BLOCKS, sign=-1.0, ts=2048)
    dc_re = jnp.transpose(_diag_blocks(dcc_re, groups), (0, 2, 1))
    dc_im = jnp.transpose(_diag_blocks(dcc_im, groups), (0, 2, 1))
    g_small = _columns_to_shards(jnp.concatenate(
        [dcw, as_tile(dg1), as_tile(dcb), as_tile(dlg), as_tile(dlb)], axis=0))
    early = {'pool_w': dpw, 'pool_scale': dscale, 'ssm_c_re': dc_re, 'ssm_c_im': dc_im, 'ssm_d': dd,
             'final_norm': dfg}
    g_early = _pack_local([early[n] for n in EARLY_REPLICATED] + [loss_tile])
    g_re, g_im, dabr, dabi, (r_even_out, r_glu, r_small, r_early) = _scan_bwd(
        a_re_row, a_im_row, dx_re, dx_im, x_re, x_im,
        [_Push(g_even_out, True), _Push(g_glu, True), _Push(g_small, True), _Push(g_early, False)])
    du_ssm = _ssm_bu_bwd(g_re, g_im, du_skip, bb)
    du_pool = _pool_bwd(dpooled)
    grad_x, dg0 = _even_in_bwd(xs, du_pool, du_ssm, dz0, dx1, g0, w_in_e)
    g_even_in = _columns_to_shards(jnp.concatenate(
        [_mm_tn("dw_even_in_pool", h0, du_pool, d_model, half), _mm_tn("dw_even_in_ssm", h0, du_ssm, d_model, half),
         _mm_tn("dw_even_in_gate", h0, dz0, d_model, d_model)], axis=1).astype(WIRE))
    g_mid = _pack_local([dg0])
    dbb_re, dbb_im, (r_even_in, r_mid) = _dw_bbar(proj0, g_re, g_im, cblk, sblk,
                                                  [_Push(g_even_in, True), _Push(g_mid, False)])
    dbbr = jnp.transpose(_diag_blocks(dbb_re, groups), (2, 0, 1))
    dbbi = jnp.transpose(_diag_blocks(dbb_im, groups), (2, 0, 1))
    dld, dar, dai, dbr, dbi = _disc_bwd(log_dt, a_re, a_im, b_re_t, b_im_t,
                                        dabr.reshape(groups, state), dabi.reshape(groups, state), dbbr, dbbi)
    late = {'ssm_log_dt': dld, 'ssm_a_re': dar, 'ssm_a_im': dai,
            'ssm_b_re': jnp.transpose(dbr, (1, 2, 0)), 'ssm_b_im': jnp.transpose(dbi, (1, 2, 0))}
    g_late = _pack_local([late[n] for n in LATE_REPLICATED])
    (r_late,) = _push_alone("exchange_tail", [_Push(g_late, False)])

    results = {}
    for n, parts in (('even_w_in', r_even_in), ('ssm_w_glu', r_glu), ('even_w_out', r_even_out),
                     ('odd_w_in', r_odd_in), ('odd_w_out', r_odd_out)):
        results[n] = _adamw("adamw_" + n, parts, local('', n), local('m_', n), local('v_', n))
    small_out = _adamw("adamw_small", r_small, small_block(''), small_block('m_'), small_block('v_'))
    results['conv_w'] = [o[:CONV_KERNEL] for o in small_out]
    for n in SMALL_VECTORS:
        results[n] = [o[SMALL_AT[n]] for o in small_out]
    loss_rows = [jnp.zeros_like(loss_tile)]
    for names, parts, extra in ((EARLY_REPLICATED, r_early, loss_rows), (MID_REPLICATED, r_mid, []),
                               (LATE_REPLICATED, r_late, [])):
        packed = [_pack_local([local(p, n) for n in names] + extra) for p in ('', 'm_', 'v_')]
        out = _adamw("adamw_" + names[0], parts, *packed)
        unpacked = [_unpack_local(o, [given[n].shape for n in names]) for o in out]
        for k, n in enumerate(names):
            results[n] = [u[k] for u in unpacked]
        if extra:
            loss = out[0][-SUBLANES, 0]

    outs = [loss, grad_x[None]]
    for kind in range(4):
        outs.extend(results[n][kind].reshape(given[n].shape) for n in WEIGHTS)
    return tuple(outs)
```

```python
import functools

import jax
import jax.numpy as jnp
from jax import lax
from jax.experimental import pallas as pl
from jax.experimental.pallas import tpu as pltpu

F32 = jnp.float32
MM = jnp.bfloat16
WIRE = jnp.bfloat16
SDS = jax.ShapeDtypeStruct

RMS_EPS = 1e-6
LN_EPS = 1e-5
POOL_WINDOWS = (2, 4, 8, 16)
POOL_HALO = 16
CONV_KERNEL = 31
CONV_HALO = 32
N_DEV = 8
LANES = 128
SUBLANES = 8
PACK_TILE = SUBLANES * LANES
ADAM_ROWS = 128
MIB = 1024 * 1024

ADAM_LR = 0.001
ADAM_B1 = 0.9
ADAM_B2 = 0.999
ADAM_EPS = 1e-08
ADAM_WD = 0.01
ADAM_STEP = 10

WEIGHTS = ['even_norm', 'even_w_in', 'pool_w', 'pool_scale', 'ssm_log_dt', 'ssm_a_re', 'ssm_a_im', 'ssm_b_re',
           'ssm_b_im', 'ssm_c_re', 'ssm_c_im', 'ssm_d', 'ssm_w_glu', 'even_w_out', 'odd_norm', 'odd_w_in', 'conv_w',
           'conv_b', 'conv_ln_g', 'conv_ln_b', 'odd_w_out', 'final_norm']
SMALL_VECTORS = ('odd_norm', 'conv_b', 'conv_ln_g', 'conv_ln_b')
SMALL_AT = {'odd_norm': 32, 'conv_b': 40, 'conv_ln_g': 48, 'conv_ln_b': 56}
EARLY_REPLICATED = ['pool_w', 'pool_scale', 'ssm_c_re', 'ssm_c_im', 'ssm_d', 'final_norm']
MID_REPLICATED = ['even_norm']
LATE_REPLICATED = ['ssm_log_dt', 'ssm_a_re', 'ssm_a_im', 'ssm_b_re', 'ssm_b_im']


def _round_up(n, m):
    return (n + m - 1) // m * m


def _sigmoid(x):
    return jax.nn.sigmoid(x)


def _silu(x):
    return x * jax.nn.sigmoid(x)


def _rms(x, g):
    return x * lax.rsqrt(jnp.mean(x * x, axis=-1, keepdims=True) + RMS_EPS) * g


def _dot(a, b):
    return jnp.dot(a.astype(MM), b.astype(MM), preferred_element_type=F32)


def _dot_nt(a, b):
    return lax.dot_general(a.astype(MM), b.astype(MM), (((1,), (1,)), ((), ())), preferred_element_type=F32)


def _dot_tn(a, b):
    return lax.dot_general(a.astype(MM), b.astype(MM), (((0,), (0,)), ((), ())), preferred_element_type=F32)


def _rows(ts, width, col=0):
    return pl.BlockSpec((ts, width), lambda i: (i, col))


def _rows_rev(ts, width, n, col=0):
    return pl.BlockSpec((ts, width), lambda i: (n - 1 - i, col))


def _full(shape):
    zeros = (0,) * len(shape)
    return pl.BlockSpec(shape, lambda i: zeros)


def _seq_params(vmem_mib=48, dims=1):
    return pltpu.CompilerParams(dimension_semantics=("arbitrary",) * dims, vmem_limit_bytes=vmem_mib * MIB)


def _mesh_position():
    x, y, c = lax.axis_index("x"), lax.axis_index("y"), lax.axis_index("c")
    return x, y, c


def _peer(pos, relation):
    x, y, c = pos
    px = 1 - x if relation & 4 else x
    py = 1 - y if relation & 2 else y
    pc = 1 - c if relation & 1 else c
    return px, py, pc


ANY = pl.BlockSpec(memory_space=pl.ANY)


class _Push:
    def __init__(self, array, scatter):
        self.array = array
        self.scatter = scatter
        self.out = SDS(array.shape if scatter else (N_DEV,) + array.shape, array.dtype)


def _push_scratch(n):
    if n == 0:
        return []
    return [pltpu.SemaphoreType.DMA((n * (N_DEV - 1),)), pltpu.SemaphoreType.DMA((n * (N_DEV - 1),)),
            pltpu.SemaphoreType.DMA((n,))]


def _push_copies(pushes, srcs, dsts, send_sems, recv_sems, local_sems):
    pos = _mesh_position()
    me = 4 * pos[0] + 2 * pos[1] + pos[2]
    copies = []
    for a, (push, src, dst) in enumerate(zip(pushes, srcs, dsts)):
        copies.append(pltpu.make_async_copy(src.at[me] if push.scatter else src, dst.at[me], local_sems.at[a]))
        for relation in range(1, N_DEV):
            peer = _peer(pos, relation)
            peer_id = 4 * peer[0] + 2 * peer[1] + peer[2]
            k = a * (N_DEV - 1) + relation - 1
            copies.append(pltpu.make_async_remote_copy(
                src_ref=src.at[peer_id] if push.scatter else src, dst_ref=dst.at[me],
                send_sem=send_sems.at[k], recv_sem=recv_sems.at[k], device_id=peer,
                device_id_type=pl.DeviceIdType.MESH))
    return copies


def _push_alone(name, pushes):
    n = len(pushes)

    def body(*refs):
        copies = _push_copies(pushes, refs[:n], refs[n:2 * n], *refs[2 * n:])
        for cp in copies:
            cp.start()
        for cp in copies:
            cp.wait()

    return pl.pallas_call(
        body, name=name, out_shape=[p.out for p in pushes], in_specs=[ANY] * n, out_specs=[ANY] * n,
        scratch_shapes=_push_scratch(n),
    )(*[p.array for p in pushes])


def _riding_start(pushes, first, refs):
    n = len(pushes)
    if n == 0:
        return

    @pl.when(first)
    def _():
        for cp in _push_copies(pushes, refs[:n], refs[n:2 * n], *refs[2 * n:]):
            cp.start()


def _riding_wait(pushes, last, refs):
    n = len(pushes)
    if n == 0:
        return

    @pl.when(last)
    def _():
        for cp in _push_copies(pushes, refs[:n], refs[n:2 * n], *refs[2 * n:]):
            cp.wait()


def _split_refs(refs, n_in, n_out, n_push):
    ins = refs[:n_in]
    srcs = refs[n_in:n_in + n_push]
    o0 = n_in + n_push
    outs = refs[o0:o0 + n_out]
    dsts = refs[o0 + n_out:o0 + n_out + n_push]
    s0 = o0 + n_out + n_push
    n_sem = 3 if n_push else 0
    sems = refs[s0:s0 + n_sem]
    scratch = refs[s0 + n_sem:]
    return ins, outs, scratch, tuple(srcs) + tuple(dsts) + tuple(sems)


def _mm_tn(name, a, b, m, n, a_col=0, b_col=0, ts=512):
    s = a.shape[0]
    tn = n
    nj = n // tn

    def body(a_ref, b_ref, o_ref):
        @pl.when(pl.program_id(1) == 0)
        def _():
            o_ref[...] = jnp.zeros_like(o_ref)

        o_ref[...] += _dot_tn(a_ref[...], b_ref[...])

    return pl.pallas_call(
        body, name=name, grid=(nj, s // ts),
        in_specs=[pl.BlockSpec((ts, m), lambda j, k: (k, a_col)),
                  pl.BlockSpec((ts, tn), lambda j, k: (k, b_col * nj + j))],
        out_specs=pl.BlockSpec((m, tn), lambda j, k: (0, j)),
        out_shape=SDS((m, n), F32),
        compiler_params=_seq_params(56, dims=2),
    )(a, b)


def _mm_tn_blocks(name, a, b, m, n, nb, a_col=0, a_step=1, b_col=0, sign=1.0, ts=512):
    s = a.shape[0]

    def body(a_ref, b_ref, o_ref):
        @pl.when(pl.program_id(1) == 0)
        def _():
            o_ref[...] = jnp.zeros_like(o_ref)

        o_ref[...] += sign * _dot_tn(a_ref[...], b_ref[...])

    return pl.pallas_call(
        body, name=name, grid=(nb, s // ts),
        in_specs=[pl.BlockSpec((ts, m), lambda j, k: (k, a_col + a_step * j)),
                  pl.BlockSpec((ts, n), lambda j, k: (k, b_col + j))],
        out_specs=pl.BlockSpec((None, m, n), lambda j, k: (j, 0, 0)),
        out_shape=SDS((nb, m, n), F32),
        compiler_params=_seq_params(48, dims=2),
    )(a, b)


def _dw_bbar(proj, g_re, g_im, cb, sb, pushes, ts=2048):
    s = proj.shape[0]
    nb, nk = SSM_BLOCKS, s // ts
    np_ = len(pushes)

    def body(*refs):
        ins, outs, _, push_refs = _split_refs(refs, 3, 2, np_)
        u_ref, gre_ref, gim_ref = ins
        ore_ref, oim_ref = outs
        j, k = pl.program_id(0), pl.program_id(1)
        _riding_start(pushes, jnp.logical_and(j == 0, k == 0), push_refs)

        @pl.when(k == 0)
        def _():
            ore_ref[...] = jnp.zeros_like(ore_ref)
            oim_ref[...] = jnp.zeros_like(oim_ref)

        u = u_ref[...].astype(MM)
        ore_ref[...] += _dot_tn(gre_ref[...], u)
        oim_ref[...] += _dot_tn(gim_ref[...], u)
        _riding_wait(pushes, jnp.logical_and(j == nb - 1, k == nk - 1), push_refs)

    wide = pl.BlockSpec((ts, sb), lambda j, k: (k, j))
    out = pl.BlockSpec((None, sb, cb), lambda j, k: (j, 0, 0))
    res = pl.pallas_call(
        body, name="dw_bbar", grid=(nb, nk),
        in_specs=[pl.BlockSpec((ts, cb), lambda j, k: (k, nb + j)), wide, wide] + [ANY] * np_,
        out_specs=[out, out] + [ANY] * np_,
        out_shape=[SDS((nb, sb, cb), F32), SDS((nb, sb, cb), F32)] + [p.out for p in pushes],
        scratch_shapes=_push_scratch(np_),
        compiler_params=_seq_params(48, dims=2),
    )(proj, g_re, g_im, *[p.array for p in pushes])
    return res[0], res[1], res[2:]


def _even_in(x, g0, w_in, pushes):
    s, d = x.shape
    shards, _, wc = w_in.shape
    e = shards * wc
    ts = 512
    nt = s // ts
    np_ = len(pushes)

    def body(*refs):
        (x_ref, g_ref, w_ref), (proj_ref, h_ref), _, push_refs = _split_refs(refs, 3, 2, np_)
        i = pl.program_id(0)
        _riding_start(pushes, i == 0, push_refs)
        hb = _rms(x_ref[...], g_ref[...]).astype(MM)
        h_ref[...] = hb
        for k in range(shards):
            proj_ref[:, k * wc:(k + 1) * wc] = jnp.dot(hb, w_ref[k], preferred_element_type=F32)
        _riding_wait(pushes, i == nt - 1, push_refs)

    out = pl.pallas_call(
        body, name="even_in", grid=(nt,),
        in_specs=[_rows(ts, d), _full((1, d)), _full(w_in.shape)] + [ANY] * np_,
        out_specs=[_rows(ts, e), _rows(ts, d)] + [ANY] * np_,
        out_shape=[SDS((s, e), F32), SDS((s, d), MM)] + [p.out for p in pushes],
        scratch_shapes=_push_scratch(np_),
        compiler_params=_seq_params(48),
    )(x, g0, w_in, *[p.array for p in pushes])
    return out[0], out[1], out[2:]


def _pool_counts(t0, ts, w):
    pos = (t0 + lax.broadcasted_iota(jnp.int32, (ts, LANES), 0) + 1).astype(F32)
    return jnp.minimum(pos, float(w))


def _pool_fwd(proj):
    s = proj.shape[0]
    width = LANES * len(POOL_WINDOWS)
    ts = 512
    per = ts // POOL_HALO

    def body(prev_ref, u_ref, out_ref, ext):
        i = pl.program_id(0)
        ext[0:POOL_HALO, 0:width] = jnp.where(i == 0, 0.0, prev_ref[...])
        ext[POOL_HALO:, 0:width] = u_ref[...]
        for g, w in enumerate(POOL_WINDOWS):
            cols = slice(g * LANES, (g + 1) * LANES)
            tok = ext[pl.ds(POOL_HALO, ts), cols]
            acc = tok
            for k in range(1, w):
                acc = acc + ext[pl.ds(POOL_HALO - k, ts), cols]
            out_ref[:, cols] = acc / _pool_counts(i * ts, ts, w) - tok

    return pl.pallas_call(
        body, name="pool_fwd", grid=(s // ts,),
        in_specs=[pl.BlockSpec((POOL_HALO, width), lambda i: (jnp.maximum(i * per - 1, 0), 0)),
                  _rows(ts, width)],
        out_specs=_rows(ts, width),
        out_shape=SDS((s, width), F32),
        scratch_shapes=[pltpu.VMEM((ts + POOL_HALO, width + LANES), F32)],
        compiler_params=_seq_params(32),
    )(proj, proj)


def _pool_bwd(dp):
    s, width = dp.shape
    ts = 512
    per = ts // POOL_HALO
    n = s // ts

    def body(dp_ref, next_ref, out_ref, ext):
        i = pl.program_id(0)
        nxt = jnp.where(i == n - 1, 0.0, next_ref[...])
        for g, w in enumerate(POOL_WINDOWS):
            cols = slice(g * LANES, (g + 1) * LANES)
            cur = dp_ref[:, cols]
            ext[0:ts, cols] = cur / _pool_counts(i * ts, ts, w)
            ext[ts:, cols] = nxt[:, cols] / _pool_counts((i + 1) * ts, POOL_HALO, w)
            acc = -cur
            for k in range(w):
                acc = acc + ext[pl.ds(k, ts), cols]
            out_ref[:, cols] = acc

    return pl.pallas_call(
        body, name="pool_bwd", grid=(n,),
        in_specs=[_rows(ts, width),
                  pl.BlockSpec((POOL_HALO, width), lambda i: (jnp.minimum((i + 1) * per, s // POOL_HALO - 1), 0))],
        out_specs=_rows(ts, width),
        out_shape=SDS((s, width), F32),
        scratch_shapes=[pltpu.VMEM((ts + POOL_HALO, width + LANES), F32)],
        compiler_params=_seq_params(32),
    )(dp, dp)


SSM_BLOCKS = 4


def _ssm_bu(proj, bb, pushes):
    s = proj.shape[0]
    nb, cb, two_nb = bb.shape
    sb = two_nb // 2
    cin, n = nb * cb, nb * sb
    ts = 512
    nt = s // ts
    np_ = len(pushes)

    def body(*refs):
        (u_ref, bb_ref), (re_ref, im_ref), _, push_refs = _split_refs(refs, 2, 2, np_)
        i = pl.program_id(0)
        _riding_start(pushes, i == 0, push_refs)
        for b in range(nb):
            bu = _dot(u_ref[:, b * cb:(b + 1) * cb], bb_ref[b])
            re_ref[:, b * sb:(b + 1) * sb] = bu[:, :sb]
            im_ref[:, b * sb:(b + 1) * sb] = bu[:, sb:]
        _riding_wait(pushes, i == nt - 1, push_refs)

    out = pl.pallas_call(
        body, name="ssm_bu", grid=(nt,),
        in_specs=[_rows(ts, cin, col=1), _full(bb.shape)] + [ANY] * np_,
        out_specs=[_rows(ts, n), _rows(ts, n)] + [ANY] * np_,
        out_shape=[SDS((s, n), F32), SDS((s, n), F32)] + [p.out for p in pushes],
        scratch_shapes=_push_scratch(np_),
        compiler_params=_seq_params(48),
    )(proj, bb, *[p.array for p in pushes])
    return out[0], out[1], out[2:]


SCAN_LANES = 256


def _cmul(a, b):
    return a[0] * b[0] - a[1] * b[1], a[0] * b[1] + a[1] * b[0]


SCAN_STEPS = (1, 2, 4)


def _fill_scan_tables(tab, ar_row, ai_row, reverse):
    shape = (SUBLANES, ar_row.shape[1])
    a1 = (jnp.broadcast_to(ar_row, shape), jnp.broadcast_to(ai_row, shape))
    a2 = _cmul(a1, a1)
    a4 = _cmul(a2, a2)
    a8 = _cmul(a4, a4)
    row = lax.broadcasted_iota(jnp.int32, shape, 0)
    for idx, (k, q) in enumerate(zip(SCAN_STEPS, (a1, a2, a4))):
        keep = (row < SUBLANES - k) if reverse else (row >= k)
        tab[2 * idx] = jnp.where(keep, q[0], 0.0)
        tab[2 * idx + 1] = jnp.where(keep, q[1], 0.0)
    expo = (SUBLANES - row) if reverse else (row + 1)
    pr, pi = jnp.ones(shape, F32), jnp.zeros(shape, F32)
    for bit, q in enumerate((a1, a2, a4, a8)):
        take = ((expo >> bit) & 1) == 1
        nr, ni = _cmul((pr, pi), q)
        pr, pi = jnp.where(take, nr, pr), jnp.where(take, ni, pi)
    tab[2 * len(SCAN_STEPS)] = pr
    tab[2 * len(SCAN_STEPS) + 1] = pi


def _group_scan(v, tab, cols, carry, reverse):
    vr, vi = v
    for idx, k in enumerate(SCAN_STEPS):
        shift = SUBLANES - k if reverse else k
        tr, ti = _cmul((tab[2 * idx, :, cols], tab[2 * idx + 1, :, cols]),
                       (pltpu.roll(vr, shift, 0), pltpu.roll(vi, shift, 0)))
        vr, vi = vr + tr, vi + ti
    last = 2 * len(SCAN_STEPS)
    tr, ti = _cmul((tab[last, :, cols], tab[last + 1, :, cols]), carry)
    return vr + tr, vi + ti


def _scan_fwd(a_re, a_im, bu_re, bu_im, pushes):
    s, n = bu_re.shape
    ts = 512
    nt = s // ts
    groups = ts // SUBLANES
    np_ = len(pushes)

    def body(*refs):
        ins, outs, scratch, push_refs = _split_refs(refs, 4, 2, np_)
        ar_ref, ai_ref, bre_ref, bim_ref = ins
        xre_ref, xim_ref = outs
        tab, cre, cim = scratch
        i = pl.program_id(0)
        _riding_start(pushes, i == 0, push_refs)

        @pl.when(i == 0)
        def _():
            _fill_scan_tables(tab, ar_ref[...], ai_ref[...], False)
            cre[...] = jnp.zeros_like(cre)
            cim[...] = jnp.zeros_like(cim)

        def group(gi, carry):
            r0 = pl.multiple_of(gi * SUBLANES, SUBLANES)
            for c in range(n // SCAN_LANES):
                cols = slice(c * SCAN_LANES, (c + 1) * SCAN_LANES)
                v = (bre_ref[pl.ds(r0, SUBLANES), cols], bim_ref[pl.ds(r0, SUBLANES), cols])
                vr, vi = _group_scan(v, tab, cols, (cre[:, cols], cim[:, cols]), False)
                xre_ref[pl.ds(r0, SUBLANES), cols] = vr
                xim_ref[pl.ds(r0, SUBLANES), cols] = vi
                cre[:, cols] = jnp.broadcast_to(vr[SUBLANES - 1:SUBLANES, :], vr.shape)
                cim[:, cols] = jnp.broadcast_to(vi[SUBLANES - 1:SUBLANES, :], vi.shape)
            return carry

        lax.fori_loop(0, groups, group, 0, unroll=2)
        _riding_wait(pushes, i == nt - 1, push_refs)

    out = pl.pallas_call(
        body, name="ssm_scan", grid=(nt,),
        in_specs=[_full((1, n)), _full((1, n)), _rows(ts, n), _rows(ts, n)] + [ANY] * np_,
        out_specs=[_rows(ts, n), _rows(ts, n)] + [ANY] * np_,
        out_shape=[SDS((s, n), F32), SDS((s, n), F32)] + [p.out for p in pushes],
        scratch_shapes=_push_scratch(np_) + [pltpu.VMEM((2 * len(SCAN_STEPS) + 2, SUBLANES, n), F32),
                                             pltpu.VMEM((SUBLANES, n), F32), pltpu.VMEM((SUBLANES, n), F32)],
        compiler_params=_seq_params(48),
    )(a_re, a_im, bu_re, bu_im, *[p.array for p in pushes])
    return out[0], out[1], out[2:]


def _scan_bwd(a_re, a_im, dx_re, dx_im, x_re, x_im, pushes):
    s, n = dx_re.shape
    ts = 256
    nt = s // ts
    groups = ts // SUBLANES
    np_ = len(pushes)

    def body(*refs):
        ins, outs, scratch, push_refs = _split_refs(refs, 6, 4, np_)
        ar_ref, ai_ref, dre_ref, dim_ref, xre_ref, xim_ref = ins
        gre_ref, gim_ref, dar_ref, dai_ref = outs
        tab, cre, cim, accr, acci = scratch
        i = pl.program_id(0)
        _riding_start(pushes, i == 0, push_refs)

        @pl.when(i == 0)
        def _():
            _fill_scan_tables(tab, ar_ref[...], -ai_ref[...], True)
            for ref in (cre, cim, accr, acci):
                ref[...] = jnp.zeros_like(ref)

        inner = lax.broadcasted_iota(jnp.int32, (SUBLANES, SCAN_LANES), 0) < SUBLANES - 1

        def group(k, carry):
            r0 = pl.multiple_of((groups - 1 - k) * SUBLANES, SUBLANES)
            for c in range(n // SCAN_LANES):
                cols = slice(c * SCAN_LANES, (c + 1) * SCAN_LANES)
                after = (cre[:, cols], cim[:, cols])
                v = (dre_ref[pl.ds(r0, SUBLANES), cols], dim_ref[pl.ds(r0, SUBLANES), cols])
                vr, vi = _group_scan(v, tab, cols, after, True)
                gre_ref[pl.ds(r0, SUBLANES), cols] = vr
                gim_ref[pl.ds(r0, SUBLANES), cols] = vi
                nr = jnp.where(inner, pltpu.roll(vr, SUBLANES - 1, 0), after[0])
                ni = jnp.where(inner, pltpu.roll(vi, SUBLANES - 1, 0), after[1])
                xr, xi = xre_ref[pl.ds(r0, SUBLANES), cols], xim_ref[pl.ds(r0, SUBLANES), cols]
                accr[:, cols] += nr * xr + ni * xi
                acci[:, cols] += ni * xr - nr * xi
                cre[:, cols] = jnp.broadcast_to(vr[0:1, :], vr.shape)
                cim[:, cols] = jnp.broadcast_to(vi[0:1, :], vi.shape)
            return carry

        lax.fori_loop(0, groups, group, 0, unroll=2)

        @pl.when(i == nt - 1)
        def _():
            dar_ref[...] = jnp.sum(accr[...], axis=0, keepdims=True)
            dai_ref[...] = jnp.sum(acci[...], axis=0, keepdims=True)

        _riding_wait(pushes, i == nt - 1, push_refs)

    small = pltpu.VMEM((SUBLANES, n), F32)
    out = pl.pallas_call(
        body, name="ssm_scan_bwd", grid=(nt,),
        in_specs=[_full((1, n)), _full((1, n))] + [_rows_rev(ts, n, nt)] * 4 + [ANY] * np_,
        out_specs=[_rows_rev(ts, n, nt), _rows_rev(ts, n, nt), _full((1, n)), _full((1, n))] + [ANY] * np_,
        out_shape=[SDS((s, n), F32), SDS((s, n), F32), SDS((1, n), F32), SDS((1, n), F32)]
        + [p.out for p in pushes],
        scratch_shapes=_push_scratch(np_) + [pltpu.VMEM((2 * len(SCAN_STEPS) + 2, SUBLANES, n), F32),
                                             small, small, small, small],
        compiler_params=_seq_params(48),
    )(a_re, a_im, dx_re, dx_im, x_re, x_im, *[p.array for p in pushes])
    return out[0], out[1], out[2], out[3], out[4:]


def _mix_a(mixed, scale, za):
    return mixed * scale * _silu(za)


def _mix_b(val, gate, zb):
    return val * _sigmoid(gate) * _silu(zb)


def _even_mix(pooled, proj, x_re, x_im, x, pw, scale, cc_re, cc_im, d_skip, w_glu, w_out, pushes):
    s, d = x.shape
    half = pooled.shape[1]
    n = x_re.shape[1]
    ts = 256
    nt = s // ts
    np_ = len(pushes)

    def body(*refs):
        ins, outs, _, push_refs = _split_refs(refs, 13, 5, np_)
        (pooled_ref, u_ref, z_ref, xre_ref, xim_ref, x_ref, pw_ref, scale_ref, ccre_ref, ccim_ref, d_ref,
         wglu_ref, wout_ref) = ins
        x1_ref, ymix_ref, mixed_ref, yin_ref, gv_ref = outs
        i = pl.program_id(0)
        _riding_start(pushes, i == 0, push_refs)
        for g in range(len(POOL_WINDOWS)):
            cols = slice(g * LANES, (g + 1) * LANES)
            mixed_ref[:, cols] = _dot(pooled_ref[:, cols], pw_ref[g])
        sb, cb = ccre_ref.shape[1:]
        for b in range(SSM_BLOCKS):
            states, chans = slice(b * sb, (b + 1) * sb), slice(b * cb, (b + 1) * cb)
            yin_ref[:, chans] = (_dot(xre_ref[:, states], ccre_ref[b]) - _dot(xim_ref[:, states], ccim_ref[b])
                                 + d_ref[:, chans] * u_ref[:, chans])
        yin = yin_ref[...]
        gv = _dot(yin, wglu_ref[...])
        gv_ref[...] = gv
        ya = _mix_a(mixed_ref[...], scale_ref[...], z_ref[:, :half])
        yb = _mix_b(gv[:, :half], gv[:, half:], z_ref[:, half:])
        ymix = jnp.concatenate([ya, yb], axis=1).astype(MM)
        ymix_ref[...] = ymix
        x1_ref[...] = x_ref[...] + jnp.dot(ymix, wout_ref[...], preferred_element_type=F32)
        _riding_wait(pushes, i == nt - 1, push_refs)

    out = pl.pallas_call(
        body, name="even_mix", grid=(nt,),
        in_specs=[_rows(ts, half), _rows(ts, half, col=1), _rows(ts, d, col=1), _rows(ts, n), _rows(ts, n),
                  _rows(ts, d), _full(pw.shape), _full((1, half)), _full(cc_re.shape), _full(cc_im.shape),
                  _full((1, half)), _full((half, d)), _full((d, d))] + [ANY] * np_,
        out_specs=[_rows(ts, d), _rows(ts, d), _rows(ts, half), _rows(ts, half), _rows(ts, d)] + [ANY] * np_,
        out_shape=[SDS((s, d), F32), SDS((s, d), MM), SDS((s, half), F32), SDS((s, half), F32), SDS((s, d), F32)]
        + [p.out for p in pushes],
        scratch_shapes=_push_scratch(np_),
        compiler_params=_seq_params(56),
    )(pooled, proj, proj, x_re, x_im, x, pw, scale, cc_re, cc_im, d_skip, w_glu, w_out,
      *[p.array for p in pushes])
    return out[:5], out[5:]


def _glu(val, gt):
    return val * _sigmoid(gt)


def _odd_in(x1, g1, w_in):
    s, d = x1.shape
    e = w_in.shape[1]
    ts = 512

    def body(x_ref, g_ref, w_ref, proj_ref, glu_ref, h_ref):
        hb = _rms(x_ref[...], g_ref[...]).astype(MM)
        h_ref[...] = hb
        proj_ref[...] = jnp.dot(hb, w_ref[...], preferred_element_type=F32)
        glu_ref[...] = _glu(proj_ref[:, :d], proj_ref[:, d:2 * d])

    return pl.pallas_call(
        body, name="odd_in", grid=(s // ts,),
        in_specs=[_rows(ts, d), _full((1, d)), _full((d, e))],
        out_specs=[_rows(ts, e), _rows(ts, d), _rows(ts, d)],
        out_shape=[SDS((s, e), F32), SDS((s, d), F32), SDS((s, d), MM)],
        compiler_params=_seq_params(56),
    )(x1, g1, w_in)


CONV_ROWS = 32
CONV_COLS = 256


def _conv_scratch(ts, ch):
    return pltpu.VMEM((SUBLANES, ts + CONV_HALO, ch + LANES), F32)


def _phase_copies(sh, rows, ch):
    for o in range(1, SUBLANES):
        sh[o, 0:rows, 0:ch] = sh[0, pl.ds(o, rows), 0:ch]


def _conv_taps(sh, w_ref, offsets, r0, cols):
    acc = jnp.zeros((CONV_ROWS, cols.stop - cols.start), F32)
    for o in range(SUBLANES):
        taps = [(j, e // SUBLANES) for j, e in offsets if e % SUBLANES == o]
        if not taps:
            continue
        q0 = min(q for _, q in taps)
        q1 = max(q for _, q in taps)
        win = sh[o, pl.ds(r0 + SUBLANES * q0, CONV_ROWS + SUBLANES * (q1 - q0)), cols]
        for j, q in taps:
            lo = SUBLANES * (q - q0)
            acc = acc + w_ref[j:j + 1, cols] * win[lo:lo + CONV_ROWS]
    return acc


def _conv_fwd(g, w):
    s, ch = g.shape
    ts = 256
    per = ts // CONV_HALO
    lead = CONV_HALO - (CONV_KERNEL - 1)
    span = ts + CONV_HALO - SUBLANES

    def body(prev_ref, g_ref, w_ref, out_ref, sh):
        i = pl.program_id(0)
        sh[0, 0:CONV_HALO, 0:ch] = jnp.where(i == 0, 0.0, prev_ref[...])
        sh[0, CONV_HALO:, 0:ch] = g_ref[...]
        _phase_copies(sh, span, ch)

        offsets = [(j, lead + j) for j in range(CONV_KERNEL)]

        def block(rb, carry):
            r0 = pl.multiple_of(rb * CONV_ROWS, CONV_ROWS)
            for c in range(ch // CONV_COLS):
                cols = slice(c * CONV_COLS, (c + 1) * CONV_COLS)
                out_ref[pl.ds(r0, CONV_ROWS), cols] = _conv_taps(sh, w_ref, offsets, r0, cols)
            return carry

        lax.fori_loop(0, ts // CONV_ROWS, block, 0)

    return pl.pallas_call(
        body, name="conv_fwd", grid=(s // ts,),
        in_specs=[pl.BlockSpec((CONV_HALO, ch), lambda i: (jnp.maximum(i * per - 1, 0), 0)),
                  _rows(ts, ch), _full(w.shape)],
        out_specs=_rows(ts, ch),
        out_shape=SDS((s, ch), F32),
        scratch_shapes=[_conv_scratch(ts, ch)],
        compiler_params=_seq_params(40),
    )(g, g, w)


def _conv_bwd(dc, g, w, pushes):
    s, ch = dc.shape
    ts = 256
    per = ts // CONV_HALO
    n = s // ts
    lead = CONV_HALO - (CONV_KERNEL - 1)
    span = ts + CONV_HALO - SUBLANES
    np_ = len(pushes)

    def body(*refs):
        ins, outs, scratch, push_refs = _split_refs(refs, 5, 2, np_)
        dc_ref, next_ref, prev_ref, g_ref, w_ref = ins
        dg_ref, dw_ref = outs
        shd, shg, wacc = scratch
        i = pl.program_id(0)
        _riding_start(pushes, i == 0, push_refs)

        @pl.when(i == 0)
        def _():
            wacc[...] = jnp.zeros_like(wacc)

        shd[0, 0:ts, 0:ch] = dc_ref[...]
        shd[0, ts:, 0:ch] = jnp.where(i == n - 1, 0.0, next_ref[...])
        shg[0, 0:CONV_HALO, 0:ch] = jnp.where(i == 0, 0.0, prev_ref[...])
        shg[0, CONV_HALO:, 0:ch] = g_ref[...]
        _phase_copies(shd, span, ch)
        _phase_copies(shg, span, ch)

        offsets = [(j, CONV_KERNEL - 1 - j) for j in range(CONV_KERNEL)]

        def dg_block(rb, carry):
            r0 = pl.multiple_of(rb * CONV_ROWS, CONV_ROWS)
            for c in range(ch // CONV_COLS):
                cols = slice(c * CONV_COLS, (c + 1) * CONV_COLS)
                dg_ref[pl.ds(r0, CONV_ROWS), cols] = _conv_taps(shd, w_ref, offsets, r0, cols)
            return carry

        lax.fori_loop(0, ts // CONV_ROWS, dg_block, 0)

        tiles = ts // SUBLANES
        for c in range(ch // LANES):
            cols = slice(c * LANES, (c + 1) * LANES)
            cur = [shd[0, k * SUBLANES:(k + 1) * SUBLANES, cols] for k in range(tiles)]
            for j in range(CONV_KERNEL):
                q, o = divmod(lead + j, SUBLANES)
                parts = [None] * 4
                for k in range(tiles):
                    term = cur[k] * shg[o, (k + q) * SUBLANES:(k + q + 1) * SUBLANES, cols]
                    parts[k % 4] = term if parts[k % 4] is None else parts[k % 4] + term
                wacc[j, :, cols] += (parts[0] + parts[1]) + (parts[2] + parts[3])

        @pl.when(i == n - 1)
        def _():
            dw_ref[...] = jnp.zeros_like(dw_ref)
            for j in range(CONV_KERNEL):
                dw_ref[j:j + 1, :] = jnp.sum(wacc[j], axis=0, keepdims=True)

        _riding_wait(pushes, i == n - 1, push_refs)

    out = pl.pallas_call(
        body, name="conv_bwd", grid=(n,),
        in_specs=[_rows(ts, ch),
                  pl.BlockSpec((CONV_HALO, ch), lambda i: (jnp.minimum((i + 1) * per, s // CONV_HALO - 1), 0)),
                  pl.BlockSpec((CONV_HALO, ch), lambda i: (jnp.maximum(i * per - 1, 0), 0)),
                  _rows(ts, ch), _full(w.shape)] + [ANY] * np_,
        out_specs=[_rows(ts, ch), _full((CONV_HALO, ch))] + [ANY] * np_,
        out_shape=[SDS((s, ch), F32), SDS((CONV_HALO, ch), F32)] + [p.out for p in pushes],
        scratch_shapes=_push_scratch(np_) + [_conv_scratch(ts, ch), _conv_scratch(ts, ch),
                                             pltpu.VMEM((CONV_HALO, SUBLANES, ch), F32)],
        compiler_params=_seq_params(56),
    )(dc, dc, g, g, w, *[p.array for p in pushes])
    return out[0], out[1], out[2:]


def _conv_act(c, z1, cb, lg, lb):
    cc = c + cb
    mu = jnp.mean(cc, axis=-1, keepdims=True)
    dev = cc - mu
    var = jnp.mean(dev * dev, axis=-1, keepdims=True)
    cn = dev * lax.rsqrt(var + LN_EPS) * lg + lb
    return _silu(cn) * _silu(z1)


def _odd_out(c, proj1, x1, tgt, cb, lg, lb, w_out, fg):
    s, d = c.shape
    ts = 256

    def body(c_ref, z_ref, x1_ref, t_ref, cb_ref, lg_ref, lb_ref, w_ref, fg_ref,
             dc_ref, dz_ref, dx2_ref, y1_ref, loss_ref, dcb_ref, dlg_ref, dlb_ref, dfg_ref):
        @pl.when(pl.program_id(0) == 0)
        def _():
            for ref in (loss_ref, dcb_ref, dlg_ref, dlb_ref, dfg_ref):
                ref[...] = jnp.zeros_like(ref)

        y1, act_vjp = jax.vjp(_conv_act, c_ref[...], z_ref[...], cb_ref[...], lg_ref[...], lb_ref[...])
        y1b = y1.astype(MM)
        y1_ref[...] = y1b
        x2 = x1_ref[...] + jnp.dot(y1b, w_ref[...], preferred_element_type=F32)
        tgt_tile = t_ref[...]

        def head(x2, fg):
            err = jnp.square(_rms(x2, fg) - tgt_tile)
            return 0.5 * jnp.sum(jnp.mean(err, axis=-1))

        loss, (dx2, dfg) = jax.value_and_grad(head, argnums=(0, 1))(x2, fg_ref[...])
        loss_ref[...] += loss
        dfg_ref[...] += dfg
        dx2_ref[...] = dx2
        dy1 = _dot_nt(dx2, w_ref[...])
        dc, dz, dcb, dlg, dlb = act_vjp(dy1)
        dc_ref[...] = dc
        dz_ref[...] = dz
        dcb_ref[...] += dcb
        dlg_ref[...] += dlg
        dlb_ref[...] += dlb

    vec = SDS((1, d), F32)
    return pl.pallas_call(
        body, name="odd_out", grid=(s // ts,),
        in_specs=[_rows(ts, d), _rows(ts, d, col=2), _rows(ts, d), _rows(ts, d), _full((1, d)), _full((1, d)),
                  _full((1, d)), _full((d, d)), _full((1, d))],
        out_specs=[_rows(ts, d), _rows(ts, d), _rows(ts, d), _rows(ts, d), _full((SUBLANES, LANES)),
                   _full((1, d)), _full((1, d)), _full((1, d)), _full((1, d))],
        out_shape=[SDS((s, d), F32), SDS((s, d), F32), SDS((s, d), F32), SDS((s, d), MM),
                   SDS((SUBLANES, LANES), F32), vec, vec, vec, vec],
        compiler_params=_seq_params(56),
    )(c, proj1, x1, tgt, cb, lg, lb, w_out, fg)


def _odd_in_bwd(x1, proj1, dglu, dz1, dx2, g1, w_in):
    s, d = x1.shape
    e = w_in.shape[1]
    ts = 256

    def body(x_ref, vg_ref, dglu_ref, dz_ref, dx2_ref, g_ref, w_ref, dx1_ref, dp_ref, dg_ref):
        @pl.when(pl.program_id(0) == 0)
        def _():
            dg_ref[...] = jnp.zeros_like(dg_ref)

        _, glu_vjp = jax.vjp(_glu, vg_ref[:, :d], vg_ref[:, d:])
        dval, dgt = glu_vjp(dglu_ref[...])
        dp = jnp.concatenate([dval, dgt, dz_ref[...]], axis=1).astype(MM)
        dp_ref[...] = dp
        dh = _dot_nt(dp, w_ref[...])
        _, rms_vjp = jax.vjp(_rms, x_ref[...], g_ref[...])
        dxa, dg = rms_vjp(dh)
        dx1_ref[...] = dxa + dx2_ref[...]
        dg_ref[...] += dg

    return pl.pallas_call(
        body, name="odd_in_bwd", grid=(s // ts,),
        in_specs=[_rows(ts, d), _rows(ts, 2 * d), _rows(ts, d), _rows(ts, d), _rows(ts, d), _full((1, d)),
                  _full((d, e))],
        out_specs=[_rows(ts, d), _rows(ts, e), _full((1, d))],
        out_shape=[SDS((s, d), F32), SDS((s, e), MM), SDS((1, d), F32)],
        compiler_params=_seq_params(56),
    )(x1, proj1, dglu, dz1, dx2, g1, w_in)


def _even_mix_bwd(pooled, proj, mixed, gv, dx1, pw, scale, cc_re, cc_im, d_skip, w_glu, w_out, pushes):
    s, d = dx1.shape
    half = pooled.shape[1]
    n = cc_re.shape[0] * cc_re.shape[1]
    ts = 256
    nt = s // ts
    groups = len(POOL_WINDOWS)
    np_ = len(pushes)

    def body(*refs):
        ins, outs, _, push_refs = _split_refs(refs, 13, 10, np_)
        (pooled_ref, u_ref, z_ref, mixed_ref, gv_ref, dx1_ref, pw_ref, scale_ref, ccre_ref, ccim_ref, d_ref,
         wglu_ref, wout_ref) = ins
        (dpooled_ref, du_ref, dz_ref, dxre_ref, dxim_ref, dyin_ref, dgv_ref, dpw_ref, dscale_ref, dd_ref) = outs
        i = pl.program_id(0)
        _riding_start(pushes, i == 0, push_refs)

        @pl.when(i == 0)
        def _():
            for ref in (dpw_ref, dscale_ref, dd_ref):
                ref[...] = jnp.zeros_like(ref)

        dymix = _dot_nt(dx1_ref[...], wout_ref[...])
        _, a_vjp = jax.vjp(_mix_a, mixed_ref[...], scale_ref[...], z_ref[:, :half])
        dmixed, dscale, dza = a_vjp(dymix[:, :half])
        _, b_vjp = jax.vjp(_mix_b, gv_ref[:, :half], gv_ref[:, half:], z_ref[:, half:])
        dval, dgate, dzb = b_vjp(dymix[:, half:])
        dz_ref[:, :half] = dza
        dz_ref[:, half:] = dzb
        dscale_ref[...] += dscale
        dgv = jnp.concatenate([dval, dgate], axis=1).astype(MM)
        dgv_ref[...] = dgv
        dyin = _dot_nt(dgv, wglu_ref[...])
        dd_ref[...] += jnp.sum(dyin * u_ref[...], axis=0, keepdims=True)
        du_ref[...] = d_ref[...] * dyin
        dyb = dyin.astype(MM)
        dyin_ref[...] = dyb
        sb, cb = ccre_ref.shape[1:]
        for b in range(SSM_BLOCKS):
            states, chans = slice(b * sb, (b + 1) * sb), slice(b * cb, (b + 1) * cb)
            dxre_ref[:, states] = _dot_nt(dyb[:, chans], ccre_ref[b])
            dxim_ref[:, states] = -_dot_nt(dyb[:, chans], ccim_ref[b])
        for g in range(groups):
            cols = slice(g * LANES, (g + 1) * LANES)
            dm = dmixed[:, cols].astype(MM)
            dpooled_ref[:, cols] = _dot_nt(dm, pw_ref[g])
            dpw_ref[g] += _dot_tn(pooled_ref[:, cols], dm)

        _riding_wait(pushes, i == nt - 1, push_refs)

    out = pl.pallas_call(
        body, name="even_mix_bwd", grid=(nt,),
        in_specs=[_rows(ts, half), _rows(ts, half, col=1), _rows(ts, d, col=1), _rows(ts, half), _rows(ts, d),
                  _rows(ts, d), _full(pw.shape), _full((1, half)), _full(cc_re.shape), _full(cc_im.shape),
                  _full((1, half)), _full((half, d)), _full((d, d))] + [ANY] * np_,
        out_specs=[_rows(ts, half), _rows(ts, half), _rows(ts, d), _rows(ts, n), _rows(ts, n), _rows(ts, half),
                   _rows(ts, d), _full(pw.shape), _full((1, half)), _full((1, half))] + [ANY] * np_,
        out_shape=[SDS((s, half), F32), SDS((s, half), F32), SDS((s, d), F32), SDS((s, n), F32), SDS((s, n), F32),
                   SDS((s, half), MM), SDS((s, d), MM), SDS(pw.shape, F32), SDS((1, half), F32),
                   SDS((1, half), F32)] + [p.out for p in pushes],
        scratch_shapes=_push_scratch(np_),
        compiler_params=_seq_params(56),
    )(pooled, proj, proj, mixed, gv, dx1, pw, scale, cc_re, cc_im, d_skip, w_glu, w_out,
      *[p.array for p in pushes])
    return out[:10], out[10:]


def _ssm_bu_bwd(g_re, g_im, du_skip, bb):
    s, n = g_re.shape
    nb, cb, two_nb = bb.shape
    sb = two_nb // 2
    cin = nb * cb
    ts = 512

    def body(gre_ref, gim_ref, du_ref, bb_ref, out_ref):
        for b in range(nb):
            states, chans = slice(b * sb, (b + 1) * sb), slice(b * cb, (b + 1) * cb)
            out_ref[:, chans] = (du_ref[:, chans] + _dot_nt(gre_ref[:, states], bb_ref[b, :, :sb])
                                 + _dot_nt(gim_ref[:, states], bb_ref[b, :, sb:]))

    return pl.pallas_call(
        body, name="ssm_bu_bwd", grid=(s // ts,),
        in_specs=[_rows(ts, n), _rows(ts, n), _rows(ts, cin), _full(bb.shape)],
        out_specs=_rows(ts, cin),
        out_shape=SDS((s, cin), F32),
        compiler_params=_seq_params(48),
    )(g_re, g_im, du_skip, bb)


def _even_in_bwd(x, du_pool, du_ssm, dz, dx1, g0, w_in):
    s, d = x.shape
    half = du_pool.shape[1]
    shards, _, wc = w_in.shape
    ts = 512

    def body(x_ref, dup_ref, dus_ref, dz_ref, dx1_ref, g_ref, w_ref, gx_ref, dg_ref):
        @pl.when(pl.program_id(0) == 0)
        def _():
            dg_ref[...] = jnp.zeros_like(dg_ref)

        dp = jnp.concatenate([dup_ref[...], dus_ref[...], dz_ref[...]], axis=1).astype(MM)
        dh = _dot_nt(dp[:, :wc], w_ref[0])
        for k in range(1, shards):
            dh = dh + _dot_nt(dp[:, k * wc:(k + 1) * wc], w_ref[k])
        _, rms_vjp = jax.vjp(_rms, x_ref[...], g_ref[...])
        dxa, dg = rms_vjp(dh)
        gx_ref[...] = dxa + dx1_ref[...]
        dg_ref[...] += dg

    return pl.pallas_call(
        body, name="even_in_bwd", grid=(s // ts,),
        in_specs=[_rows(ts, d), _rows(ts, half), _rows(ts, half), _rows(ts, d), _rows(ts, d), _full((1, d)),
                  _full(w_in.shape)],
        out_specs=[_rows(ts, d), _full((1, d))],
        out_shape=[SDS((s, d), F32), SDS((1, d), F32)],
        compiler_params=_seq_params(48),
    )(x, du_pool, du_ssm, dz, dx1, g0, w_in)


def _discretise(log_dt, ar, ai, br, bi):
    dt = jnp.exp(log_dt)
    mag = jnp.exp(ar * dt)
    ang = ai * dt
    abr = mag * jnp.cos(ang)
    abi = mag * jnp.sin(ang)
    den = ar * ar + ai * ai
    nr = abr - 1.0
    ni = abi
    kr = (nr * ar + ni * ai) / den
    ki = (ni * ar - nr * ai) / den
    bbr = kr[None] * br - ki[None] * bi
    bbi = kr[None] * bi + ki[None] * br
    return abr, abi, bbr, bbi


def _whole(n):
    return [pl.BlockSpec(memory_space=pltpu.VMEM)] * n


def _disc_fwd(log_dt, ar, ai, br, bi):
    def body(ld_ref, ar_ref, ai_ref, br_ref, bi_ref, abr_ref, abi_ref, bbr_ref, bbi_ref):
        out = _discretise(ld_ref[...], ar_ref[...], ai_ref[...], br_ref[...], bi_ref[...])
        for ref, val in zip((abr_ref, abi_ref, bbr_ref, bbi_ref), out):
            ref[...] = val

    return pl.pallas_call(
        body, name="ssm_discretise", in_specs=_whole(5), out_specs=_whole(4),
        out_shape=[SDS(ar.shape, F32), SDS(ar.shape, F32), SDS(br.shape, F32), SDS(br.shape, F32)],
    )(log_dt, ar, ai, br, bi)


def _disc_bwd(log_dt, ar, ai, br, bi, dabr, dabi, dbbr, dbbi):
    def body(ld_ref, ar_ref, ai_ref, br_ref, bi_ref, dabr_ref, dabi_ref, dbbr_ref, dbbi_ref,
             dld_ref, dar_ref, dai_ref, dbr_ref, dbi_ref):
        _, vjp = jax.vjp(_discretise, ld_ref[...], ar_ref[...], ai_ref[...], br_ref[...], bi_ref[...])
        grads = vjp((dabr_ref[...], dabi_ref[...], dbbr_ref[...], dbbi_ref[...]))
        for ref, val in zip((dld_ref, dar_ref, dai_ref, dbr_ref, dbi_ref), grads):
            ref[...] = val

    return pl.pallas_call(
        body, name="ssm_discretise_bwd", in_specs=_whole(9), out_specs=_whole(5),
        out_shape=[SDS(log_dt.shape, F32), SDS(ar.shape, F32), SDS(ar.shape, F32), SDS(br.shape, F32),
                   SDS(br.shape, F32)],
    )(log_dt, ar, ai, br, bi, dabr, dabi, dbbr, dbbi)


def _block_diag(t):
    g, a, b = t.shape
    per = g // SSM_BLOCKS
    t = t.reshape(SSM_BLOCKS, per, a, b)
    same = jnp.eye(per, dtype=bool)[None, :, None, :, None]
    return jnp.where(same, t[:, :, :, None, :], 0.0).reshape(SSM_BLOCKS, per * a, per * b)


def _diag_blocks(m, g):
    per = g // SSM_BLOCKS
    a, b = m.shape[1] // per, m.shape[2] // per
    d = jnp.diagonal(m.reshape(SSM_BLOCKS, per, a, per, b), axis1=1, axis2=3)
    return jnp.moveaxis(d, -1, 1).reshape(g, a, b)


def _adamw(name, parts, w, m, v):
    rows, cols = w.shape
    tr = ADAM_ROWS if rows % ADAM_ROWS == 0 else rows

    def body(p_ref, w_ref, m_ref, v_ref, g_ref, d_ref, nm_ref, nv_ref):
        g = p_ref[0].astype(F32)
        for dev in range(1, N_DEV):
            g = g + p_ref[dev].astype(F32)
        g_ref[...] = g
        nm = ADAM_B1 * m_ref[...] + (1.0 - ADAM_B1) * g
        nv = ADAM_B2 * v_ref[...] + (1.0 - ADAM_B2) * jnp.square(g)
        nm_ref[...] = nm
        nv_ref[...] = nv
        m_hat = nm / (1.0 - ADAM_B1 ** ADAM_STEP)
        v_hat = nv / (1.0 - ADAM_B2 ** ADAM_STEP)
        d_ref[...] = -ADAM_LR * (m_hat / (jnp.sqrt(v_hat) + ADAM_EPS) + ADAM_WD * w_ref[...])

    tile = _rows(tr, cols)
    out = SDS((rows, cols), F32)
    return pl.pallas_call(
        body, name=name, grid=(rows // tr,),
        in_specs=[pl.BlockSpec((N_DEV, tr, cols), lambda i: (0, i, 0)), tile, tile, tile],
        out_specs=[tile, tile, tile, tile],
        out_shape=[out, out, out, out],
        compiler_params=_seq_params(32),
    )(parts, w, m, v)


def _as_rows(flat):
    n = flat.shape[-1]
    padded = _round_up(n, PACK_TILE)
    if padded != n:
        flat = jnp.pad(flat, [(0, 0)] * (flat.ndim - 1) + [(0, padded - n)])
    return flat.reshape(flat.shape[:-1] + (padded // LANES, LANES))


def _columns_to_shards(full):
    r, c8 = full.shape
    return jnp.transpose(full.reshape(r, N_DEV, c8 // N_DEV), (1, 0, 2))


def _shards_to_columns(shards):
    n, r, c = shards.shape
    return jnp.transpose(shards, (1, 0, 2)).reshape(r, n * c)


def _pack_local(tensors):
    return jnp.concatenate([_as_rows(t.reshape(-1)) for t in tensors], axis=0)


def _unpack_local(packed, shapes):
    out, r = [], 0
    for shape in shapes:
        n = 1
        for dim in shape:
            n *= dim
        rows = _round_up(n, PACK_TILE) // LANES
        out.append(packed[r:r + rows].reshape(-1)[:n].reshape(shape))
        r += rows
    return out


def kernel(x, even_norm, even_w_in, pool_w, pool_scale, ssm_log_dt, ssm_a_re, ssm_a_im, ssm_b_re, ssm_b_im, ssm_c_re, ssm_c_im, ssm_d, ssm_w_glu, even_w_out, odd_norm, odd_w_in, conv_w, conv_b, conv_ln_g, conv_ln_b, odd_w_out, final_norm, loss_target, m_even_norm, m_even_w_in, m_pool_w, m_pool_scale, m_ssm_log_dt, m_ssm_a_re, m_ssm_a_im, m_ssm_b_re, m_ssm_b_im, m_ssm_c_re, m_ssm_c_im, m_ssm_d, m_ssm_w_glu, m_even_w_out, m_odd_norm, m_odd_w_in, m_conv_w, m_conv_b, m_conv_ln_g, m_conv_ln_b, m_odd_w_out, m_final_norm, v_even_norm, v_even_w_in, v_pool_w, v_pool_scale, v_ssm_log_dt, v_ssm_a_re, v_ssm_a_im, v_ssm_b_re, v_ssm_b_im, v_ssm_c_re, v_ssm_c_im, v_ssm_d, v_ssm_w_glu, v_even_w_out, v_odd_norm, v_odd_w_in, v_conv_w, v_conv_b, v_conv_ln_g, v_conv_ln_b, v_odd_w_out, v_final_norm):
    given = dict(locals())
    xs = x[0]
    tgt = loss_target[0]
    d_model = xs.shape[1]

    def local(prefix, name):
        t = given[prefix + name]
        return t if name == 'final_norm' else t[0]

    def small_block(prefix):
        parts = [jnp.pad(local(prefix, 'conv_w'), ((0, SMALL_AT['odd_norm'] - CONV_KERNEL), (0, 0)))]
        for n in SMALL_VECTORS:
            parts.append(jnp.pad(local(prefix, n).reshape(1, LANES), ((0, SUBLANES - 1), (0, 0))))
        return jnp.concatenate(parts, axis=0)

    def as_tile(row):
        return jnp.pad(row, ((0, SUBLANES - row.shape[0]), (0, 0)))

    (w_in_e,) = _push_alone("gather_first", [_Push(local('', 'even_w_in').astype(MM), False)])
    later = [_Push(local('', 'ssm_w_glu').astype(MM), False), _Push(local('', 'even_w_out').astype(MM), False),
             _Push(local('', 'odd_w_in').astype(MM), False), _Push(local('', 'odd_w_out').astype(MM), False),
             _Push(small_block(''), False)]

    a_re, a_im = local('', 'ssm_a_re'), local('', 'ssm_a_im')
    groups, state = a_re.shape
    log_dt = local('', 'ssm_log_dt').reshape(groups, 1)
    b_re_t = jnp.transpose(local('', 'ssm_b_re'), (2, 0, 1))
    b_im_t = jnp.transpose(local('', 'ssm_b_im'), (2, 0, 1))
    abr, abi, bbr, bbi = _disc_fwd(log_dt, a_re, a_im, b_re_t, b_im_t)
    a_re_row = abr.reshape(1, groups * state)
    a_im_row = abi.reshape(1, groups * state)
    bb = jnp.concatenate([_block_diag(jnp.transpose(bbr, (1, 0, 2))), _block_diag(jnp.transpose(bbi, (1, 0, 2)))],
                         axis=2).astype(MM)
    cc_re = _block_diag(jnp.transpose(local('', 'ssm_c_re'), (0, 2, 1))).astype(MM)
    cc_im = _block_diag(jnp.transpose(local('', 'ssm_c_im'), (0, 2, 1))).astype(MM)
    pw = local('', 'pool_w').astype(MM)
    g0 = local('', 'even_norm').reshape(1, d_model)
    fg = local('', 'final_norm').reshape(1, d_model)
    scale = local('', 'pool_scale').reshape(1, -1)
    d_skip = local('', 'ssm_d').reshape(1, -1)

    proj0, h0, (wg3, wo_e3) = _even_in(xs, g0, w_in_e, later[:2])
    pooled = _pool_fwd(proj0)
    bu_re, bu_im, (wo_o3, small8) = _ssm_bu(proj0, bb, later[3:])
    x_re, x_im, (wi_o3,) = _scan_fwd(a_re_row, a_im_row, bu_re, bu_im, later[2:3])
    w_glu = _shards_to_columns(wg3)
    w_out_e = wo_e3.reshape(d_model, d_model)
    (x1, ymix, mixed, yin, gv), _ = _even_mix(
        pooled, proj0, x_re, x_im, xs, pw, scale, cc_re, cc_im, d_skip, w_glu, w_out_e, [])
    w_in_o = _shards_to_columns(wi_o3)
    w_out_o = wo_o3.reshape(d_model, d_model)
    small = _shards_to_columns(small8)
    cw = small[:CONV_KERNEL]
    g1, cb, lg, lb = (small[SMALL_AT[n]:SMALL_AT[n] + 1] for n in SMALL_VECTORS)
    proj1, glu, h1 = _odd_in(x1, g1, w_in_o)
    conv = _conv_fwd(glu, cw)

    half = pooled.shape[1]
    n_state = groups * state
    rows_per = d_model // N_DEV
    dc, dz1, dx2, y1, loss_tile, dcb, dlg, dlb, dfg = _odd_out(conv, proj1, x1, tgt, cb, lg, lb, w_out_o, fg)
    g_odd_out = _mm_tn("dw_odd_out", y1, dx2, d_model, d_model).astype(WIRE).reshape(N_DEV, rows_per, d_model)
    dglu, dcw, (r_odd_out,) = _conv_bwd(dc, glu, cw, [_Push(g_odd_out, True)])
    dx1, dproj1, dg1 = _odd_in_bwd(x1, proj1, dglu, dz1, dx2, g1, w_in_o)
    g_odd_in = _columns_to_shards(_mm_tn("dw_odd_in", h1, dproj1, d_model, 3 * d_model).astype(WIRE))
    g_even_out = _mm_tn("dw_even_out", ymix, dx1, d_model, d_model).astype(WIRE).reshape(N_DEV, rows_per, d_model)
    mix_grads, (r_odd_in,) = _even_mix_bwd(pooled, proj0, mixed, gv, dx1, pw, scale, cc_re, cc_im, d_skip, w_glu,
                                           w_out_e, [_Push(g_odd_in, True)])
    dpooled, du_skip, dz0, dx_re, dx_im, dyin, dgv, dpw, dscale, dd = mix_grads
    g_glu = _columns_to_shards(_mm_tn("dw_glu", yin, dgv, half, d_model).astype(WIRE))
    sblk, cblk = n_state // SSM_BLOCKS, half // SSM_BLOCKS
    dcc_re = _mm_tn_blocks("dw_c_re", x_re, dyin, sblk, cblk, SSM_BLOCKS, ts=2048)
    dcc_im = _mm_tn_blocks("dw_c_im", x_im, dyin, sblk, cblk, SSM_BLOCKS, sign=-1.0, ts=2048)
    dc_re = jnp.transpose(_diag_blocks(dcc_re, groups), (0, 2, 1))
    dc_im = jnp.transpose(_diag_blocks(dcc_im, groups), (0, 2, 1))
    g_small = _columns_to_shards(jnp.concatenate(
        [dcw, as_tile(dg1), as_tile(dcb), as_tile(dlg), as_tile(dlb)], axis=0))
    early = {'pool_w': dpw, 'pool_scale': dscale, 'ssm_c_re': dc_re, 'ssm_c_im': dc_im, 'ssm_d': dd,
             'final_norm': dfg}
    g_early = _pack_local([early[n] for n in EARLY_REPLICATED] + [loss_tile])
    g_re, g_im, dabr, dabi, (r_even_out, r_glu, r_small, r_early) = _scan_bwd(
        a_re_row, a_im_row, dx_re, dx_im, x_re, x_im,
        [_Push(g_even_out, True), _Push(g_glu, True), _Push(g_small, True), _Push(g_early, False)])
    du_ssm = _ssm_bu_bwd(g_re, g_im, du_skip, bb)
    du_pool = _pool_bwd(dpooled)
    grad_x, dg0 = _even_in_bwd(xs, du_pool, du_ssm, dz0, dx1, g0, w_in_e)
    g_even_in = _columns_to_shards(jnp.concatenate(
        [_mm_tn("dw_even_in_pool", h0, du_pool, d_model, half), _mm_tn("dw_even_in_ssm", h0, du_ssm, d_model, half),
         _mm_tn("dw_even_in_gate", h0, dz0, d_model, d_model)], axis=1).astype(WIRE))
    g_mid = _pack_local([dg0])
    dbb_re, dbb_im, (r_even_in, r_mid) = _dw_bbar(proj0, g_re, g_im, cblk, sblk,
                                                  [_Push(g_even_in, True), _Push(g_mid, False)])
    dbbr = jnp.transpose(_diag_blocks(dbb_re, groups), (2, 0, 1))
    dbbi = jnp.transpose(_diag_blocks(dbb_im, groups), (2, 0, 1))
    dld, dar, dai, dbr, dbi = _disc_bwd(log_dt, a_re, a_im, b_re_t, b_im_t,
                                        dabr.reshape(groups, state), dabi.reshape(groups, state), dbbr, dbbi)
    late = {'ssm_log_dt': dld, 'ssm_a_re': dar, 'ssm_a_im': dai,
            'ssm_b_re': jnp.transpose(dbr, (1, 2, 0)), 'ssm_b_im': jnp.transpose(dbi, (1, 2, 0))}
    g_late = _pack_local([late[n] for n in LATE_REPLICATED])
    (r_late,) = _push_alone("exchange_tail", [_Push(g_late, False)])

    results = {}
    for n, parts in (('even_w_in', r_even_in), ('ssm_w_glu', r_glu), ('even_w_out', r_even_out),
                     ('odd_w_in', r_odd_in), ('odd_w_out', r_odd_out)):
        results[n] = _adamw("adamw_" + n, parts, local('', n), local('m_', n), local('v_', n))
    small_out = _adamw("adamw_small", r_small, small_block(''), small_block('m_'), small_block('v_'))
    results['conv_w'] = [o[:CONV_KERNEL] for o in small_out]
    for n in SMALL_VECTORS:
        results[n] = [o[SMALL_AT[n]] for o in small_out]
    loss_rows = [jnp.zeros_like(loss_tile)]
    for names, parts, extra in ((EARLY_REPLICATED, r_early, loss_rows), (MID_REPLICATED, r_mid, []),
                               (LATE_REPLICATED, r_late, [])):
        packed = [_pack_local([local(p, n) for n in names] + extra) for p in ('', 'm_', 'v_')]
        out = _adamw("adamw_" + names[0], parts, *packed)
        unpacked = [_unpack_local(o, [given[n].shape for n in names]) for o in out]
        for k, n in enumerate(names):
            results[n] = [u[k] for u in unpacked]
        if extra:
            loss = out[0][-SUBLANES, 0]

    outs = [loss, grad_x[None]]
    for kind in range(4):
        outs.extend(results[n][kind].reshape(given[n].shape) for n in WEIGHTS)
    return tuple(outs)
```

```python
import functools

import jax
import jax.numpy as jnp
from jax import lax
from jax.experimental import pallas as pl
from jax.experimental.pallas import tpu as pltpu

F32 = jnp.float32
MM = jnp.bfloat16
WIRE = jnp.bfloat16
SDS = jax.ShapeDtypeStruct

RMS_EPS = 1e-6
LN_EPS = 1e-5
POOL_WINDOWS = (2, 4, 8, 16)
POOL_HALO = 16
CONV_KERNEL = 31
CONV_HALO = 32
N_DEV = 8
LANES = 128
SUBLANES = 8
PACK_TILE = SUBLANES * LANES
ADAM_ROWS = 128
MIB = 1024 * 1024

ADAM_LR = 0.001
ADAM_B1 = 0.9
ADAM_B2 = 0.999
ADAM_EPS = 1e-08
ADAM_WD = 0.01
ADAM_STEP = 10

WEIGHTS = ['even_norm', 'even_w_in', 'pool_w', 'pool_scale', 'ssm_log_dt', 'ssm_a_re', 'ssm_a_im', 'ssm_b_re',
           'ssm_b_im', 'ssm_c_re', 'ssm_c_im', 'ssm_d', 'ssm_w_glu', 'even_w_out', 'odd_norm', 'odd_w_in', 'conv_w',
           'conv_b', 'conv_ln_g', 'conv_ln_b', 'odd_w_out', 'final_norm']
SMALL_VECTORS = ('odd_norm', 'conv_b', 'conv_ln_g', 'conv_ln_b')
SMALL_AT = {'odd_norm': 32, 'conv_b': 40, 'conv_ln_g': 48, 'conv_ln_b': 56}
EARLY_REPLICATED = ['pool_w', 'pool_scale', 'ssm_c_re', 'ssm_c_im', 'ssm_d', 'final_norm']
MID_REPLICATED = ['even_norm']
LATE_REPLICATED = ['ssm_log_dt', 'ssm_a_re', 'ssm_a_im', 'ssm_b_re', 'ssm_b_im']


def _round_up(n, m):
    return (n + m - 1) // m * m


def _sigmoid(x):
    return jax.nn.sigmoid(x)


def _silu(x):
    return x * jax.nn.sigmoid(x)


def _rms(x, g):
    return x * lax.rsqrt(jnp.mean(x * x, axis=-1, keepdims=True) + RMS_EPS) * g


def _dot(a, b):
    return jnp.dot(a.astype(MM), b.astype(MM), preferred_element_type=F32)


def _dot_nt(a, b):
    return lax.dot_general(a.astype(MM), b.astype(MM), (((1,), (1,)), ((), ())), preferred_element_type=F32)


def _dot_tn(a, b):
    return lax.dot_general(a.astype(MM), b.astype(MM), (((0,), (0,)), ((), ())), preferred_element_type=F32)


def _rows(ts, width, col=0):
    return pl.BlockSpec((ts, width), lambda i: (i, col))


def _rows_rev(ts, width, n, col=0):
    return pl.BlockSpec((ts, width), lambda i: (n - 1 - i, col))


def _full(shape):
    zeros = (0,) * len(shape)
    return pl.BlockSpec(shape, lambda i: zeros)


def _seq_params(vmem_mib=48, dims=1):
    return pltpu.CompilerParams(dimension_semantics=("arbitrary",) * dims, vmem_limit_bytes=vmem_mib * MIB)


def _mesh_position():
    x, y, c = lax.axis_index("x"), lax.axis_index("y"), lax.axis_index("c")
    return x, y, c


def _peer(pos, relation):
    x, y, c = pos
    px = 1 - x if relation & 4 else x
    py = 1 - y if relation & 2 else y
    pc = 1 - c if relation & 1 else c
    return px, py, pc


ANY = pl.BlockSpec(memory_space=pl.ANY)


class _Push:
    def __init__(self, array, scatter):
        self.array = array
        self.scatter = scatter
        self.out = SDS(array.shape if scatter else (N_DEV,) + array.shape, array.dtype)


def _push_scratch(n):
    if n == 0:
        return []
    return [pltpu.SemaphoreType.DMA((n * (N_DEV - 1),)), pltpu.SemaphoreType.DMA((n * (N_DEV - 1),)),
            pltpu.SemaphoreType.DMA((n,))]


def _push_copies(pushes, srcs, dsts, send_sems, recv_sems, local_sems):
    pos = _mesh_position()
    me = 4 * pos[0] + 2 * pos[1] + pos[2]
    copies = []
    for a, (push, src, dst) in enumerate(zip(pushes, srcs, dsts)):
        copies.append(pltpu.make_async_copy(src.at[me] if push.scatter else src, dst.at[me], local_sems.at[a]))
        for relation in range(1, N_DEV):
            peer = _peer(pos, relation)
            peer_id = 4 * peer[0] + 2 * peer[1] + peer[2]
            k = a * (N_DEV - 1) + relation - 1
            copies.append(pltpu.make_async_remote_copy(
                src_ref=src.at[peer_id] if push.scatter else src, dst_ref=dst.at[me],
                send_sem=send_sems.at[k], recv_sem=recv_sems.at[k], device_id=peer,
                device_id_type=pl.DeviceIdType.MESH))
    return copies


def _push_alone(name, pushes):
    n = len(pushes)

    def body(*refs):
        copies = _push_copies(pushes, refs[:n], refs[n:2 * n], *refs[2 * n:])
        for cp in copies:
            cp.start()
        for cp in copies:
            cp.wait()

    return pl.pallas_call(
        body, name=name, out_shape=[p.out for p in pushes], in_specs=[ANY] * n, out_specs=[ANY] * n,
        scratch_shapes=_push_scratch(n),
    )(*[p.array for p in pushes])


def _riding_start(pushes, first, refs):
    n = len(pushes)
    if n == 0:
        return

    @pl.when(first)
    def _():
        for cp in _push_copies(pushes, refs[:n], refs[n:2 * n], *refs[2 * n:]):
            cp.start()


def _riding_wait(pushes, last, refs):
    n = len(pushes)
    if n == 0:
        return

    @pl.when(last)
    def _():
        for cp in _push_copies(pushes, refs[:n], refs[n:2 * n], *refs[2 * n:]):
            cp.wait()


def _split_refs(refs, n_in, n_out, n_push):
    ins = refs[:n_in]
    srcs = refs[n_in:n_in + n_push]
    o0 = n_in + n_push
    outs = refs[o0:o0 + n_out]
    dsts = refs[o0 + n_out:o0 + n_out + n_push]
    s0 = o0 + n_out + n_push
    n_sem = 3 if n_push else 0
    sems = refs[s0:s0 + n_sem]
    scratch = refs[s0 + n_sem:]
    return ins, outs, scratch, tuple(srcs) + tuple(dsts) + tuple(sems)


def _mm_tn(name, a, b, m, n, out_dtype, a_col=0, b_col=0, ts=512):
    s = a.shape[0]
    nk = s // ts

    def body(a_ref, b_ref, o_ref, acc):
        k = pl.program_id(0)

        @pl.when(k == 0)
        def _():
            acc[...] = jnp.zeros_like(acc)

        acc[...] += _dot_tn(a_ref[...], b_ref[...])

        @pl.when(k == nk - 1)
        def _():
            o_ref[...] = acc[...].astype(out_dtype)

    return pl.pallas_call(
        body, name=name, grid=(nk,),
        in_specs=[pl.BlockSpec((ts, m), lambda k: (k, a_col)), pl.BlockSpec((ts, n), lambda k: (k, b_col))],
        out_specs=pl.BlockSpec((m, n), lambda k: (0, 0)),
        out_shape=SDS((m, n), out_dtype),
        scratch_shapes=[pltpu.VMEM((m, n), F32)],
        compiler_params=_seq_params(56),
    )(a, b)


def _mm_tn_blocks(name, a, b, m, n, nb, a_col=0, a_step=1, b_col=0, sign=1.0, ts=512):
    s = a.shape[0]

    def body(a_ref, b_ref, o_ref):
        @pl.when(pl.program_id(1) == 0)
        def _():
            o_ref[...] = jnp.zeros_like(o_ref)

        o_ref[...] += sign * _dot_tn(a_ref[...], b_ref[...])

    return pl.pallas_call(
        body, name=name, grid=(nb, s // ts),
        in_specs=[pl.BlockSpec((ts, m), lambda j, k: (k, a_col + a_step * j)),
                  pl.BlockSpec((ts, n), lambda j, k: (k, b_col + j))],
        out_specs=pl.BlockSpec((None, m, n), lambda j, k: (j, 0, 0)),
        out_shape=SDS((nb, m, n), F32),
        compiler_params=_seq_params(48, dims=2),
    )(a, b)


def _dw_bbar(proj, g_re, g_im, cb, sb, pushes, ts=2048):
    s = proj.shape[0]
    nb, nk = SSM_BLOCKS, s // ts
    np_ = len(pushes)

    def body(*refs):
        ins, outs, _, push_refs = _split_refs(refs, 3, 2, np_)
        u_ref, gre_ref, gim_ref = ins
        ore_ref, oim_ref = outs
        j, k = pl.program_id(0), pl.program_id(1)
        _riding_start(pushes, jnp.logical_and(j == 0, k == 0), push_refs)

        @pl.when(k == 0)
        def _():
            ore_ref[...] = jnp.zeros_like(ore_ref)
            oim_ref[...] = jnp.zeros_like(oim_ref)

        u = u_ref[...].astype(MM)
        ore_ref[...] += _dot_tn(gre_ref[...], u)
        oim_ref[...] += _dot_tn(gim_ref[...], u)
        _riding_wait(pushes, jnp.logical_and(j == nb - 1, k == nk - 1), push_refs)

    wide = pl.BlockSpec((ts, sb), lambda j, k: (k, j))
    out = pl.BlockSpec((None, sb, cb), lambda j, k: (j, 0, 0))
    res = pl.pallas_call(
        body, name="dw_bbar", grid=(nb, nk),
        in_specs=[pl.BlockSpec((ts, cb), lambda j, k: (k, nb + j)), wide, wide] + [ANY] * np_,
        out_specs=[out, out] + [ANY] * np_,
        out_shape=[SDS((nb, sb, cb), F32), SDS((nb, sb, cb), F32)] + [p.out for p in pushes],
        scratch_shapes=_push_scratch(np_),
        compiler_params=_seq_params(48, dims=2),
    )(proj, g_re, g_im, *[p.array for p in pushes])
    return res[0], res[1], res[2:]


def _even_in(x, g0, w_in, pushes):
    s, d = x.shape
    shards, _, wc = w_in.shape
    e = shards * wc
    ts = 512
    nt = s // ts
    np_ = len(pushes)

    def body(*refs):
        (x_ref, g_ref, w_ref), (proj_ref, h_ref), _, push_refs = _split_refs(refs, 3, 2, np_)
        i = pl.program_id(0)
        _riding_start(pushes, i == 0, push_refs)
        hb = _rms(x_ref[...], g_ref[...]).astype(MM)
        h_ref[...] = hb
        for k in range(shards):
            proj_ref[:, k * wc:(k + 1) * wc] = jnp.dot(hb, w_ref[k], preferred_element_type=F32)
        _riding_wait(pushes, i == nt - 1, push_refs)

    out = pl.pallas_call(
        body, name="even_in", grid=(nt,),
        in_specs=[_rows(ts, d), _full((1, d)), _full(w_in.shape)] + [ANY] * np_,
        out_specs=[_rows(ts, e), _rows(ts, d)] + [ANY] * np_,
        out_shape=[SDS((s, e), F32), SDS((s, d), MM)] + [p.out for p in pushes],
        scratch_shapes=_push_scratch(np_),
        compiler_params=_seq_params(48),
    )(x, g0, w_in, *[p.array for p in pushes])
    return out[0], out[1], out[2:]


def _pool_counts(t0, ts, w):
    pos = (t0 + lax.broadcasted_iota(jnp.int32, (ts, LANES), 0) + 1).astype(F32)
    return jnp.minimum(pos, float(w))


def _pool_fwd(proj):
    s = proj.shape[0]
    width = LANES * len(POOL_WINDOWS)
    ts = 512
    per = ts // POOL_HALO

    def body(prev_ref, u_ref, out_ref, ext):
        i = pl.program_id(0)
        ext[0:POOL_HALO, 0:width] = jnp.where(i == 0, 0.0, prev_ref[...])
        ext[POOL_HALO:, 0:width] = u_ref[...]
        for g, w in enumerate(POOL_WINDOWS):
            cols = slice(g * LANES, (g + 1) * LANES)
            tok = ext[pl.ds(POOL_HALO, ts), cols]
            acc = tok
            for k in range(1, w):
                acc = acc + ext[pl.ds(POOL_HALO - k, ts), cols]
            out_ref[:, cols] = acc / _pool_counts(i * ts, ts, w) - tok

    return pl.pallas_call(
        body, name="pool_fwd", grid=(s // ts,),
        in_specs=[pl.BlockSpec((POOL_HALO, width), lambda i: (jnp.maximum(i * per - 1, 0), 0)),
                  _rows(ts, width)],
        out_specs=_rows(ts, width),
        out_shape=SDS((s, width), F32),
        scratch_shapes=[pltpu.VMEM((ts + POOL_HALO, width + LANES), F32)],
        compiler_params=_seq_params(32),
    )(proj, proj)


def _pool_bwd(dp):
    s, width = dp.shape
    ts = 512
    per = ts // POOL_HALO
    n = s // ts

    def body(dp_ref, next_ref, out_ref, ext):
        i = pl.program_id(0)
        nxt = jnp.where(i == n - 1, 0.0, next_ref[...])
        for g, w in enumerate(POOL_WINDOWS):
            cols = slice(g * LANES, (g + 1) * LANES)
            cur = dp_ref[:, cols]
            ext[0:ts, cols] = cur / _pool_counts(i * ts, ts, w)
            ext[ts:, cols] = nxt[:, cols] / _pool_counts((i + 1) * ts, POOL_HALO, w)
            acc = -cur
            for k in range(w):
                acc = acc + ext[pl.ds(k, ts), cols]
            out_ref[:, cols] = acc

    return pl.pallas_call(
        body, name="pool_bwd", grid=(n,),
        in_specs=[_rows(ts, width),
                  pl.BlockSpec((POOL_HALO, width), lambda i: (jnp.minimum((i + 1) * per, s // POOL_HALO - 1), 0))],
        out_specs=_rows(ts, width),
        out_shape=SDS((s, width), F32),
        scratch_shapes=[pltpu.VMEM((ts + POOL_HALO, width + LANES), F32)],
        compiler_params=_seq_params(32),
    )(dp, dp)


SSM_BLOCKS = 4


def _ssm_bu(proj, bb, pushes):
    s = proj.shape[0]
    nb, cb, two_nb = bb.shape
    sb = two_nb // 2
    cin, n = nb * cb, nb * sb
    ts = 512
    nt = s // ts
    np_ = len(pushes)

    def body(*refs):
        (u_ref, bb_ref), (re_ref, im_ref), _, push_refs = _split_refs(refs, 2, 2, np_)
        i = pl.program_id(0)
        _riding_start(pushes, i == 0, push_refs)
        for b in range(nb):
            bu = _dot(u_ref[:, b * cb:(b + 1) * cb], bb_ref[b])
            re_ref[:, b * sb:(b + 1) * sb] = bu[:, :sb]
            im_ref[:, b * sb:(b + 1) * sb] = bu[:, sb:]
        _riding_wait(pushes, i == nt - 1, push_refs)

    out = pl.pallas_call(
        body, name="ssm_bu", grid=(nt,),
        in_specs=[_rows(ts, cin, col=1), _full(bb.shape)] + [ANY] * np_,
        out_specs=[_rows(ts, n), _rows(ts, n)] + [ANY] * np_,
        out_shape=[SDS((s, n), F32), SDS((s, n), F32)] + [p.out for p in pushes],
        scratch_shapes=_push_scratch(np_),
        compiler_params=_seq_params(48),
    )(proj, bb, *[p.array for p in pushes])
    return out[0], out[1], out[2:]


SCAN_LANES = 256


def _cmul(a, b):
    return a[0] * b[0] - a[1] * b[1], a[0] * b[1] + a[1] * b[0]


SCAN_STEPS = (1, 2, 4)


def _fill_scan_tables(tab, ar_row, ai_row, reverse):
    shape = (SUBLANES, ar_row.shape[1])
    a1 = (jnp.broadcast_to(ar_row, shape), jnp.broadcast_to(ai_row, shape))
    a2 = _cmul(a1, a1)
    a4 = _cmul(a2, a2)
    a8 = _cmul(a4, a4)
    row = lax.broadcasted_iota(jnp.int32, shape, 0)
    for idx, (k, q) in enumerate(zip(SCAN_STEPS, (a1, a2, a4))):
        keep = (row < SUBLANES - k) if reverse else (row >= k)
        tab[2 * idx] = jnp.where(keep, q[0], 0.0)
        tab[2 * idx + 1] = jnp.where(keep, q[1], 0.0)
    expo = (SUBLANES - row) if reverse else (row + 1)
    pr, pi = jnp.ones(shape, F32), jnp.zeros(shape, F32)
    for bit, q in enumerate((a1, a2, a4, a8)):
        take = ((expo >> bit) & 1) == 1
        nr, ni = _cmul((pr, pi), q)
        pr, pi = jnp.where(take, nr, pr), jnp.where(take, ni, pi)
    tab[2 * len(SCAN_STEPS)] = pr
    tab[2 * len(SCAN_STEPS) + 1] = pi


def _group_scan(v, tab, cols, carry, reverse):
    vr, vi = v
    for idx, k in enumerate(SCAN_STEPS):
        shift = SUBLANES - k if reverse else k
        tr, ti = _cmul((tab[2 * idx, :, cols], tab[2 * idx + 1, :, cols]),
                       (pltpu.roll(vr, shift, 0), pltpu.roll(vi, shift, 0)))
        vr, vi = vr + tr, vi + ti
    last = 2 * len(SCAN_STEPS)
    tr, ti = _cmul((tab[last, :, cols], tab[last + 1, :, cols]), carry)
    return vr + tr, vi + ti


def _scan_fwd(a_re, a_im, bu_re, bu_im, pushes):
    s, n = bu_re.shape
    ts = 512
    nt = s // ts
    groups = ts // SUBLANES
    np_ = len(pushes)

    def body(*refs):
        ins, outs, scratch, push_refs = _split_refs(refs, 4, 2, np_)
        ar_ref, ai_ref, bre_ref, bim_ref = ins
        xre_ref, xim_ref = outs
        tab, cre, cim = scratch
        i = pl.program_id(0)
        _riding_start(pushes, i == 0, push_refs)

        @pl.when(i == 0)
        def _():
            _fill_scan_tables(tab, ar_ref[...], ai_ref[...], False)
            cre[...] = jnp.zeros_like(cre)
            cim[...] = jnp.zeros_like(cim)

        def group(gi, carry):
            r0 = pl.multiple_of(gi * SUBLANES, SUBLANES)
            for c in range(n // SCAN_LANES):
                cols = slice(c * SCAN_LANES, (c + 1) * SCAN_LANES)
                v = (bre_ref[pl.ds(r0, SUBLANES), cols], bim_ref[pl.ds(r0, SUBLANES), cols])
                vr, vi = _group_scan(v, tab, cols, (cre[:, cols], cim[:, cols]), False)
                xre_ref[pl.ds(r0, SUBLANES), cols] = vr
                xim_ref[pl.ds(r0, SUBLANES), cols] = vi
                cre[:, cols] = jnp.broadcast_to(vr[SUBLANES - 1:SUBLANES, :], vr.shape)
                cim[:, cols] = jnp.broadcast_to(vi[SUBLANES - 1:SUBLANES, :], vi.shape)
            return carry

        lax.fori_loop(0, groups, group, 0, unroll=2)
        _riding_wait(pushes, i == nt - 1, push_refs)

    out = pl.pallas_call(
        body, name="ssm_scan", grid=(nt,),
        in_specs=[_full((1, n)), _full((1, n)), _rows(ts, n), _rows(ts, n)] + [ANY] * np_,
        out_specs=[_rows(ts, n), _rows(ts, n)] + [ANY] * np_,
        out_shape=[SDS((s, n), F32), SDS((s, n), F32)] + [p.out for p in pushes],
        scratch_shapes=_push_scratch(np_) + [pltpu.VMEM((2 * len(SCAN_STEPS) + 2, SUBLANES, n), F32),
                                             pltpu.VMEM((SUBLANES, n), F32), pltpu.VMEM((SUBLANES, n), F32)],
        compiler_params=_seq_params(48),
    )(a_re, a_im, bu_re, bu_im, *[p.array for p in pushes])
    return out[0], out[1], out[2:]


def _scan_bwd(a_re, a_im, dx_re, dx_im, x_re, x_im, pushes):
    s, n = dx_re.shape
    ts = 256
    nt = s // ts
    groups = ts // SUBLANES
    np_ = len(pushes)

    def body(*refs):
        ins, outs, scratch, push_refs = _split_refs(refs, 6, 4, np_)
        ar_ref, ai_ref, dre_ref, dim_ref, xre_ref, xim_ref = ins
        gre_ref, gim_ref, dar_ref, dai_ref = outs
        tab, cre, cim, accr, acci = scratch
        i = pl.program_id(0)
        _riding_start(pushes, i == 0, push_refs)

        @pl.when(i == 0)
        def _():
            _fill_scan_tables(tab, ar_ref[...], -ai_ref[...], True)
            for ref in (cre, cim, accr, acci):
                ref[...] = jnp.zeros_like(ref)

        inner = lax.broadcasted_iota(jnp.int32, (SUBLANES, SCAN_LANES), 0) < SUBLANES - 1

        def group(k, carry):
            r0 = pl.multiple_of((groups - 1 - k) * SUBLANES, SUBLANES)
            for c in range(n // SCAN_LANES):
                cols = slice(c * SCAN_LANES, (c + 1) * SCAN_LANES)
                after = (cre[:, cols], cim[:, cols])
                v = (dre_ref[pl.ds(r0, SUBLANES), cols], dim_ref[pl.ds(r0, SUBLANES), cols])
                vr, vi = _group_scan(v, tab, cols, after, True)
                gre_ref[pl.ds(r0, SUBLANES), cols] = vr
                gim_ref[pl.ds(r0, SUBLANES), cols] = vi
                nr = jnp.where(inner, pltpu.roll(vr, SUBLANES - 1, 0), after[0])
                ni = jnp.where(inner, pltpu.roll(vi, SUBLANES - 1, 0), after[1])
                xr, xi = xre_ref[pl.ds(r0, SUBLANES), cols], xim_ref[pl.ds(r0, SUBLANES), cols]
                accr[:, cols] += nr * xr + ni * xi
                acci[:, cols] += ni * xr - nr * xi
                cre[:, cols] = jnp.broadcast_to(vr[0:1, :], vr.shape)
                cim[:, cols] = jnp.broadcast_to(vi[0:1, :], vi.shape)
            return carry

        lax.fori_loop(0, groups, group, 0, unroll=2)

        @pl.when(i == nt - 1)
        def _():
            dar_ref[...] = jnp.sum(accr[...], axis=0, keepdims=True)
            dai_ref[...] = jnp.sum(acci[...], axis=0, keepdims=True)

        _riding_wait(pushes, i == nt - 1, push_refs)

    small = pltpu.VMEM((SUBLANES, n), F32)
    out = pl.pallas_call(
        body, name="ssm_scan_bwd", grid=(nt,),
        in_specs=[_full((1, n)), _full((1, n))] + [_rows_rev(ts, n, nt)] * 4 + [ANY] * np_,
        out_specs=[_rows_rev(ts, n, nt), _rows_rev(ts, n, nt), _full((1, n)), _full((1, n))] + [ANY] * np_,
        out_shape=[SDS((s, n), F32), SDS((s, n), F32), SDS((1, n), F32), SDS((1, n), F32)]
        + [p.out for p in pushes],
        scratch_shapes=_push_scratch(np_) + [pltpu.VMEM((2 * len(SCAN_STEPS) + 2, SUBLANES, n), F32),
                                             small, small, small, small],
        compiler_params=_seq_params(48),
    )(a_re, a_im, dx_re, dx_im, x_re, x_im, *[p.array for p in pushes])
    return out[0], out[1], out[2], out[3], out[4:]


def _mix_a(mixed, scale, za):
    return mixed * scale * _silu(za)


def _mix_b(val, gate, zb):
    return val * _sigmoid(gate) * _silu(zb)


def _even_mix(pooled, proj, x_re, x_im, x, pw, scale, cc_re, cc_im, d_skip, w_glu, w_out, pushes):
    s, d = x.shape
    half = pooled.shape[1]
    n = x_re.shape[1]
    ts = 256
    nt = s // ts
    np_ = len(pushes)

    def body(*refs):
        ins, outs, _, push_refs = _split_refs(refs, 13, 5, np_)
        (pooled_ref, u_ref, z_ref, xre_ref, xim_ref, x_ref, pw_ref, scale_ref, ccre_ref, ccim_ref, d_ref,
         wglu_ref, wout_ref) = ins
        x1_ref, ymix_ref, mixed_ref, yin_ref, gv_ref = outs
        i = pl.program_id(0)
        _riding_start(pushes, i == 0, push_refs)
        for g in range(len(POOL_WINDOWS)):
            cols = slice(g * LANES, (g + 1) * LANES)
            mixed_ref[:, cols] = _dot(pooled_ref[:, cols], pw_ref[g])
        sb, cb = ccre_ref.shape[1:]
        for b in range(SSM_BLOCKS):
            states, chans = slice(b * sb, (b + 1) * sb), slice(b * cb, (b + 1) * cb)
            yin_ref[:, chans] = (_dot(xre_ref[:, states], ccre_ref[b]) - _dot(xim_ref[:, states], ccim_ref[b])
                                 + d_ref[:, chans] * u_ref[:, chans])
        yin = yin_ref[...]
        gv = _dot(yin, wglu_ref[...])
        gv_ref[...] = gv
        ya = _mix_a(mixed_ref[...], scale_ref[...], z_ref[:, :half])
        yb = _mix_b(gv[:, :half], gv[:, half:], z_ref[:, half:])
        ymix = jnp.concatenate([ya, yb], axis=1).astype(MM)
        ymix_ref[...] = ymix
        x1_ref[...] = x_ref[...] + jnp.dot(ymix, wout_ref[...], preferred_element_type=F32)
        _riding_wait(pushes, i == nt - 1, push_refs)

    out = pl.pallas_call(
        body, name="even_mix", grid=(nt,),
        in_specs=[_rows(ts, half), _rows(ts, half, col=1), _rows(ts, d, col=1), _rows(ts, n), _rows(ts, n),
                  _rows(ts, d), _full(pw.shape), _full((1, half)), _full(cc_re.shape), _full(cc_im.shape),
                  _full((1, half)), _full((half, d)), _full((d, d))] + [ANY] * np_,
        out_specs=[_rows(ts, d), _rows(ts, d), _rows(ts, half), _rows(ts, half), _rows(ts, d)] + [ANY] * np_,
        out_shape=[SDS((s, d), F32), SDS((s, d), MM), SDS((s, half), F32), SDS((s, half), F32), SDS((s, d), F32)]
        + [p.out for p in pushes],
        scratch_shapes=_push_scratch(np_),
        compiler_params=_seq_params(56),
    )(pooled, proj, proj, x_re, x_im, x, pw, scale, cc_re, cc_im, d_skip, w_glu, w_out,
      *[p.array for p in pushes])
    return out[:5], out[5:]


def _glu(val, gt):
    return val * _sigmoid(gt)


def _odd_in(x1, g1, w_in):
    s, d = x1.shape
    e = w_in.shape[1]
    ts = 512

    def body(x_ref, g_ref, w_ref, proj_ref, glu_ref, h_ref):
        hb = _rms(x_ref[...], g_ref[...]).astype(MM)
        h_ref[...] = hb
        proj_ref[...] = jnp.dot(hb, w_ref[...], preferred_element_type=F32)
        glu_ref[...] = _glu(proj_ref[:, :d], proj_ref[:, d:2 * d])

    return pl.pallas_call(
        body, name="odd_in", grid=(s // ts,),
        in_specs=[_rows(ts, d), _full((1, d)), _full((d, e))],
        out_specs=[_rows(ts, e), _rows(ts, d), _rows(ts, d)],
        out_shape=[SDS((s, e), F32), SDS((s, d), F32), SDS((s, d), MM)],
        compiler_params=_seq_params(56),
    )(x1, g1, w_in)


CONV_ROWS = 32
CONV_COLS = 256


def _conv_scratch(ts, ch):
    return pltpu.VMEM((SUBLANES, ts + CONV_HALO, ch + LANES), F32)


def _phase_copies(sh, rows, ch):
    for o in range(1, SUBLANES):
        sh[o, 0:rows, 0:ch] = sh[0, pl.ds(o, rows), 0:ch]


def _conv_taps(sh, w_ref, offsets, r0, cols):
    acc = jnp.zeros((CONV_ROWS, cols.stop - cols.start), F32)
    for o in range(SUBLANES):
        taps = [(j, e // SUBLANES) for j, e in offsets if e % SUBLANES == o]
        if not taps:
            continue
        q0 = min(q for _, q in taps)
        q1 = max(q for _, q in taps)
        win = sh[o, pl.ds(r0 + SUBLANES * q0, CONV_ROWS + SUBLANES * (q1 - q0)), cols]
        for j, q in taps:
            lo = SUBLANES * (q - q0)
            acc = acc + w_ref[j:j + 1, cols] * win[lo:lo + CONV_ROWS]
    return acc


def _conv_fwd(g, w):
    s, ch = g.shape
    ts = 256
    per = ts // CONV_HALO
    lead = CONV_HALO - (CONV_KERNEL - 1)
    span = ts + CONV_HALO - SUBLANES

    def body(prev_ref, g_ref, w_ref, out_ref, sh):
        i = pl.program_id(0)
        sh[0, 0:CONV_HALO, 0:ch] = jnp.where(i == 0, 0.0, prev_ref[...])
        sh[0, CONV_HALO:, 0:ch] = g_ref[...]
        _phase_copies(sh, span, ch)

        offsets = [(j, lead + j) for j in range(CONV_KERNEL)]

        def block(rb, carry):
            r0 = pl.multiple_of(rb * CONV_ROWS, CONV_ROWS)
            for c in range(ch // CONV_COLS):
                cols = slice(c * CONV_COLS, (c + 1) * CONV_COLS)
                out_ref[pl.ds(r0, CONV_ROWS), cols] = _conv_taps(sh, w_ref, offsets, r0, cols)
            return carry

        lax.fori_loop(0, ts // CONV_ROWS, block, 0)

    return pl.pallas_call(
        body, name="conv_fwd", grid=(s // ts,),
        in_specs=[pl.BlockSpec((CONV_HALO, ch), lambda i: (jnp.maximum(i * per - 1, 0), 0)),
                  _rows(ts, ch), _full(w.shape)],
        out_specs=_rows(ts, ch),
        out_shape=SDS((s, ch), F32),
        scratch_shapes=[_conv_scratch(ts, ch)],
        compiler_params=_seq_params(40),
    )(g, g, w)


def _conv_bwd(dc, g, w, pushes):
    s, ch = dc.shape
    ts = 256
    per = ts // CONV_HALO
    n = s // ts
    lead = CONV_HALO - (CONV_KERNEL - 1)
    span = ts + CONV_HALO - SUBLANES
    np_ = len(pushes)

    def body(*refs):
        ins, outs, scratch, push_refs = _split_refs(refs, 5, 2, np_)
        dc_ref, next_ref, prev_ref, g_ref, w_ref = ins
        dg_ref, dw_ref = outs
        shd, shg, wacc = scratch
        i = pl.program_id(0)
        _riding_start(pushes, i == 0, push_refs)

        @pl.when(i == 0)
        def _():
            wacc[...] = jnp.zeros_like(wacc)

        shd[0, 0:ts, 0:ch] = dc_ref[...]
        shd[0, ts:, 0:ch] = jnp.where(i == n - 1, 0.0, next_ref[...])
        shg[0, 0:CONV_HALO, 0:ch] = jnp.where(i == 0, 0.0, prev_ref[...])
        shg[0, CONV_HALO:, 0:ch] = g_ref[...]
        _phase_copies(shd, span, ch)
        _phase_copies(shg, span, ch)

        offsets = [(j, CONV_KERNEL - 1 - j) for j in range(CONV_KERNEL)]

        def dg_block(rb, carry):
            r0 = pl.multiple_of(rb * CONV_ROWS, CONV_ROWS)
            for c in range(ch // CONV_COLS):
                cols = slice(c * CONV_COLS, (c + 1) * CONV_COLS)
                dg_ref[pl.ds(r0, CONV_ROWS), cols] = _conv_taps(shd, w_ref, offsets, r0, cols)
            return carry

        lax.fori_loop(0, ts // CONV_ROWS, dg_block, 0)

        tiles = ts // SUBLANES
        for c in range(ch // LANES):
            cols = slice(c * LANES, (c + 1) * LANES)
            cur = [shd[0, k * SUBLANES:(k + 1) * SUBLANES, cols] for k in range(tiles)]
            for j in range(CONV_KERNEL):
                q, o = divmod(lead + j, SUBLANES)
                parts = [None] * 4
                for k in range(tiles):
                    term = cur[k] * shg[o, (k + q) * SUBLANES:(k + q + 1) * SUBLANES, cols]
                    parts[k % 4] = term if parts[k % 4] is None else parts[k % 4] + term
                wacc[j, :, cols] += (parts[0] + parts[1]) + (parts[2] + parts[3])

        @pl.when(i == n - 1)
        def _():
            dw_ref[...] = jnp.zeros_like(dw_ref)
            for j in range(CONV_KERNEL):
                dw_ref[j:j + 1, :] = jnp.sum(wacc[j], axis=0, keepdims=True)

        _riding_wait(pushes, i == n - 1, push_refs)

    out = pl.pallas_call(
        body, name="conv_bwd", grid=(n,),
        in_specs=[_rows(ts, ch),
                  pl.BlockSpec((CONV_HALO, ch), lambda i: (jnp.minimum((i + 1) * per, s // CONV_HALO - 1), 0)),
                  pl.BlockSpec((CONV_HALO, ch), lambda i: (jnp.maximum(i * per - 1, 0), 0)),
                  _rows(ts, ch), _full(w.shape)] + [ANY] * np_,
        out_specs=[_rows(ts, ch), _full((CONV_HALO, ch))] + [ANY] * np_,
        out_shape=[SDS((s, ch), F32), SDS((CONV_HALO, ch), F32)] + [p.out for p in pushes],
        scratch_shapes=_push_scratch(np_) + [_conv_scratch(ts, ch), _conv_scratch(ts, ch),
                                             pltpu.VMEM((CONV_HALO, SUBLANES, ch), F32)],
        compiler_params=_seq_params(56),
    )(dc, dc, g, g, w, *[p.array for p in pushes])
    return out[0], out[1], out[2:]


def _conv_act(c, z1, cb, lg, lb):
    cc = c + cb
    mu = jnp.mean(cc, axis=-1, keepdims=True)
    dev = cc - mu
    var = jnp.mean(dev * dev, axis=-1, keepdims=True)
    cn = dev * lax.rsqrt(var + LN_EPS) * lg + lb
    return _silu(cn) * _silu(z1)


def _odd_out(c, proj1, x1, tgt, cb, lg, lb, w_out, fg):
    s, d = c.shape
    ts = 256

    def body(c_ref, z_ref, x1_ref, t_ref, cb_ref, lg_ref, lb_ref, w_ref, fg_ref,
             dc_ref, dz_ref, dx2_ref, y1_ref, loss_ref, dcb_ref, dlg_ref, dlb_ref, dfg_ref):
        @pl.when(pl.program_id(0) == 0)
        def _():
            for ref in (loss_ref, dcb_ref, dlg_ref, dlb_ref, dfg_ref):
                ref[...] = jnp.zeros_like(ref)

        y1, act_vjp = jax.vjp(_conv_act, c_ref[...], z_ref[...], cb_ref[...], lg_ref[...], lb_ref[...])
        y1b = y1.astype(MM)
        y1_ref[...] = y1b
        x2 = x1_ref[...] + jnp.dot(y1b, w_ref[...], preferred_element_type=F32)
        tgt_tile = t_ref[...]

        def head(x2, fg):
            err = jnp.square(_rms(x2, fg) - tgt_tile)
            return 0.5 * jnp.sum(jnp.mean(err, axis=-1))

        loss, (dx2, dfg) = jax.value_and_grad(head, argnums=(0, 1))(x2, fg_ref[...])
        loss_ref[...] += loss
        dfg_ref[...] += dfg
        dx2_ref[...] = dx2
        dy1 = _dot_nt(dx2, w_ref[...])
        dc, dz, dcb, dlg, dlb = act_vjp(dy1)
        dc_ref[...] = dc
        dz_ref[...] = dz
        dcb_ref[...] += dcb
        dlg_ref[...] += dlg
        dlb_ref[...] += dlb

    vec = SDS((1, d), F32)
    return pl.pallas_call(
        body, name="odd_out", grid=(s // ts,),
        in_specs=[_rows(ts, d), _rows(ts, d, col=2), _rows(ts, d), _rows(ts, d), _full((1, d)), _full((1, d)),
                  _full((1, d)), _full((d, d)), _full((1, d))],
        out_specs=[_rows(ts, d), _rows(ts, d), _rows(ts, d), _rows(ts, d), _full((SUBLANES, LANES)),
                   _full((1, d)), _full((1, d)), _full((1, d)), _full((1, d))],
        out_shape=[SDS((s, d), F32), SDS((s, d), F32), SDS((s, d), F32), SDS((s, d), MM),
                   SDS((SUBLANES, LANES), F32), vec, vec, vec, vec],
        compiler_params=_seq_params(56),
    )(c, proj1, x1, tgt, cb, lg, lb, w_out, fg)


def _odd_in_bwd(x1, proj1, dglu, dz1, dx2, g1, w_in):
    s, d = x1.shape
    e = w_in.shape[1]
    ts = 256

    def body(x_ref, vg_ref, dglu_ref, dz_ref, dx2_ref, g_ref, w_ref, dx1_ref, dp_ref, dg_ref):
        @pl.when(pl.program_id(0) == 0)
        def _():
            dg_ref[...] = jnp.zeros_like(dg_ref)

        _, glu_vjp = jax.vjp(_glu, vg_ref[:, :d], vg_ref[:, d:])
        dval, dgt = glu_vjp(dglu_ref[...])
        dp = jnp.concatenate([dval, dgt, dz_ref[...]], axis=1).astype(MM)
        dp_ref[...] = dp
        dh = _dot_nt(dp, w_ref[...])
        _, rms_vjp = jax.vjp(_rms, x_ref[...], g_ref[...])
        dxa, dg = rms_vjp(dh)
        dx1_ref[...] = dxa + dx2_ref[...]
        dg_ref[...] += dg

    return pl.pallas_call(
        body, name="odd_in_bwd", grid=(s // ts,),
        in_specs=[_rows(ts, d), _rows(ts, 2 * d), _rows(ts, d), _rows(ts, d), _rows(ts, d), _full((1, d)),
                  _full((d, e))],
        out_specs=[_rows(ts, d), _rows(ts, e), _full((1, d))],
        out_shape=[SDS((s, d), F32), SDS((s, e), MM), SDS((1, d), F32)],
        compiler_params=_seq_params(56),
    )(x1, proj1, dglu, dz1, dx2, g1, w_in)


def _even_mix_bwd(pooled, proj, mixed, gv, dx1, pw, scale, cc_re, cc_im, d_skip, w_glu, w_out, pushes):
    s, d = dx1.shape
    half = pooled.shape[1]
    n = cc_re.shape[0] * cc_re.shape[1]
    ts = 256
    nt = s // ts
    groups = len(POOL_WINDOWS)
    np_ = len(pushes)

    def body(*refs):
        ins, outs, _, push_refs = _split_refs(refs, 13, 10, np_)
        (pooled_ref, u_ref, z_ref, mixed_ref, gv_ref, dx1_ref, pw_ref, scale_ref, ccre_ref, ccim_ref, d_ref,
         wglu_ref, wout_ref) = ins
        (dpooled_ref, du_ref, dz_ref, dxre_ref, dxim_ref, dyin_ref, dgv_ref, dpw_ref, dscale_ref, dd_ref) = outs
        i = pl.program_id(0)
        _riding_start(pushes, i == 0, push_refs)

        @pl.when(i == 0)
        def _():
            for ref in (dpw_ref, dscale_ref, dd_ref):
                ref[...] = jnp.zeros_like(ref)

        dymix = _dot_nt(dx1_ref[...], wout_ref[...])
        _, a_vjp = jax.vjp(_mix_a, mixed_ref[...], scale_ref[...], z_ref[:, :half])
        dmixed, dscale, dza = a_vjp(dymix[:, :half])
        _, b_vjp = jax.vjp(_mix_b, gv_ref[:, :half], gv_ref[:, half:], z_ref[:, half:])
        dval, dgate, dzb = b_vjp(dymix[:, half:])
        dz_ref[:, :half] = dza
        dz_ref[:, half:] = dzb
        dscale_ref[...] += dscale
        dgv = jnp.concatenate([dval, dgate], axis=1).astype(MM)
        dgv_ref[...] = dgv
        dyin = _dot_nt(dgv, wglu_ref[...])
        dd_ref[...] += jnp.sum(dyin * u_ref[...], axis=0, keepdims=True)
        du_ref[...] = d_ref[...] * dyin
        dyb = dyin.astype(MM)
        dyin_ref[...] = dyb
        sb, cb = ccre_ref.shape[1:]
        for b in range(SSM_BLOCKS):
            states, chans = slice(b * sb, (b + 1) * sb), slice(b * cb, (b + 1) * cb)
            dxre_ref[:, states] = _dot_nt(dyb[:, chans], ccre_ref[b])
            dxim_ref[:, states] = -_dot_nt(dyb[:, chans], ccim_ref[b])
        for g in range(groups):
            cols = slice(g * LANES, (g + 1) * LANES)
            dm = dmixed[:, cols].astype(MM)
            dpooled_ref[:, cols] = _dot_nt(dm, pw_ref[g])
            dpw_ref[g] += _dot_tn(pooled_ref[:, cols], dm)

        _riding_wait(pushes, i == nt - 1, push_refs)

    out = pl.pallas_call(
        body, name="even_mix_bwd", grid=(nt,),
        in_specs=[_rows(ts, half), _rows(ts, half, col=1), _rows(ts, d, col=1), _rows(ts, half), _rows(ts, d),
                  _rows(ts, d), _full(pw.shape), _full((1, half)), _full(cc_re.shape), _full(cc_im.shape),
                  _full((1, half)), _full((half, d)), _full((d, d))] + [ANY] * np_,
        out_specs=[_rows(ts, half), _rows(ts, half), _rows(ts, d), _rows(ts, n), _rows(ts, n), _rows(ts, half),
                   _rows(ts, d), _full(pw.shape), _full((1, half)), _full((1, half))] + [ANY] * np_,
        out_shape=[SDS((s, half), F32), SDS((s, half), F32), SDS((s, d), F32), SDS((s, n), F32), SDS((s, n), F32),
                   SDS((s, half), MM), SDS((s, d), MM), SDS(pw.shape, F32), SDS((1, half), F32),
                   SDS((1, half), F32)] + [p.out for p in pushes],
        scratch_shapes=_push_scratch(np_),
        compiler_params=_seq_params(56),
    )(pooled, proj, proj, mixed, gv, dx1, pw, scale, cc_re, cc_im, d_skip, w_glu, w_out,
      *[p.array for p in pushes])
    return out[:10], out[10:]


def _ssm_bu_bwd(g_re, g_im, du_skip, bb):
    s, n = g_re.shape
    nb, cb, two_nb = bb.shape
    sb = two_nb // 2
    cin = nb * cb
    ts = 512

    def body(gre_ref, gim_ref, du_ref, bb_ref, out_ref):
        for b in range(nb):
            states, chans = slice(b * sb, (b + 1) * sb), slice(b * cb, (b + 1) * cb)
            out_ref[:, chans] = (du_ref[:, chans] + _dot_nt(gre_ref[:, states], bb_ref[b, :, :sb])
                                 + _dot_nt(gim_ref[:, states], bb_ref[b, :, sb:]))

    return pl.pallas_call(
        body, name="ssm_bu_bwd", grid=(s // ts,),
        in_specs=[_rows(ts, n), _rows(ts, n), _rows(ts, cin), _full(bb.shape)],
        out_specs=_rows(ts, cin),
        out_shape=SDS((s, cin), F32),
        compiler_params=_seq_params(48),
    )(g_re, g_im, du_skip, bb)


def _even_in_bwd(x, du_pool, du_ssm, dz, dx1, g0, w_in):
    s, d = x.shape
    half = du_pool.shape[1]
    shards, _, wc = w_in.shape
    ts = 512

    def body(x_ref, dup_ref, dus_ref, dz_ref, dx1_ref, g_ref, w_ref, gx_ref, dg_ref):
        @pl.when(pl.program_id(0) == 0)
        def _():
            dg_ref[...] = jnp.zeros_like(dg_ref)

        dp = jnp.concatenate([dup_ref[...], dus_ref[...], dz_ref[...]], axis=1).astype(MM)
        dh = _dot_nt(dp[:, :wc], w_ref[0])
        for k in range(1, shards):
            dh = dh + _dot_nt(dp[:, k * wc:(k + 1) * wc], w_ref[k])
        _, rms_vjp = jax.vjp(_rms, x_ref[...], g_ref[...])
        dxa, dg = rms_vjp(dh)
        gx_ref[...] = dxa + dx1_ref[...]
        dg_ref[...] += dg

    return pl.pallas_call(
        body, name="even_in_bwd", grid=(s // ts,),
        in_specs=[_rows(ts, d), _rows(ts, half), _rows(ts, half), _rows(ts, d), _rows(ts, d), _full((1, d)),
                  _full(w_in.shape)],
        out_specs=[_rows(ts, d), _full((1, d))],
        out_shape=[SDS((s, d), F32), SDS((1, d), F32)],
        compiler_params=_seq_params(48),
    )(x, du_pool, du_ssm, dz, dx1, g0, w_in)


def _discretise(log_dt, ar, ai, br, bi):
    dt = jnp.exp(log_dt)
    mag = jnp.exp(ar * dt)
    ang = ai * dt
    abr = mag * jnp.cos(ang)
    abi = mag * jnp.sin(ang)
    den = ar * ar + ai * ai
    nr = abr - 1.0
    ni = abi
    kr = (nr * ar + ni * ai) / den
    ki = (ni * ar - nr * ai) / den
    bbr = kr[None] * br - ki[None] * bi
    bbi = kr[None] * bi + ki[None] * br
    return abr, abi, bbr, bbi


def _whole(n):
    return [pl.BlockSpec(memory_space=pltpu.VMEM)] * n


def _disc_fwd(log_dt, ar, ai, br, bi):
    def body(ld_ref, ar_ref, ai_ref, br_ref, bi_ref, abr_ref, abi_ref, bbr_ref, bbi_ref):
        out = _discretise(ld_ref[...], ar_ref[...], ai_ref[...], br_ref[...], bi_ref[...])
        for ref, val in zip((abr_ref, abi_ref, bbr_ref, bbi_ref), out):
            ref[...] = val

    return pl.pallas_call(
        body, name="ssm_discretise", in_specs=_whole(5), out_specs=_whole(4),
        out_shape=[SDS(ar.shape, F32), SDS(ar.shape, F32), SDS(br.shape, F32), SDS(br.shape, F32)],
    )(log_dt, ar, ai, br, bi)


def _disc_bwd(log_dt, ar, ai, br, bi, dabr, dabi, dbbr, dbbi):
    def body(ld_ref, ar_ref, ai_ref, br_ref, bi_ref, dabr_ref, dabi_ref, dbbr_ref, dbbi_ref,
             dld_ref, dar_ref, dai_ref, dbr_ref, dbi_ref):
        _, vjp = jax.vjp(_discretise, ld_ref[...], ar_ref[...], ai_ref[...], br_ref[...], bi_ref[...])
        grads = vjp((dabr_ref[...], dabi_ref[...], dbbr_ref[...], dbbi_ref[...]))
        for ref, val in zip((dld_ref, dar_ref, dai_ref, dbr_ref, dbi_ref), grads):
            ref[...] = val

    return pl.pallas_call(
        body, name="ssm_discretise_bwd", in_specs=_whole(9), out_specs=_whole(5),
        out_shape=[SDS(log_dt.shape, F32), SDS(ar.shape, F32), SDS(ar.shape, F32), SDS(br.shape, F32),
                   SDS(br.shape, F32)],
    )(log_dt, ar, ai, br, bi, dabr, dabi, dbbr, dbbi)


def _block_diag(t):
    g, a, b = t.shape
    per = g // SSM_BLOCKS
    t = t.reshape(SSM_BLOCKS, per, a, b)
    same = jnp.eye(per, dtype=bool)[None, :, None, :, None]
    return jnp.where(same, t[:, :, :, None, :], 0.0).reshape(SSM_BLOCKS, per * a, per * b)


def _diag_blocks(m, g):
    per = g // SSM_BLOCKS
    a, b = m.shape[1] // per, m.shape[2] // per
    d = jnp.diagonal(m.reshape(SSM_BLOCKS, per, a, per, b), axis1=1, axis2=3)
    return jnp.moveaxis(d, -1, 1).reshape(g, a, b)


def _adamw(name, parts, w, m, v):
    rows, cols = w.shape
    tr = ADAM_ROWS if rows % ADAM_ROWS == 0 else rows

    def body(p_ref, w_ref, m_ref, v_ref, g_ref, d_ref, nm_ref, nv_ref):
        g = p_ref[0].astype(F32)
        for dev in range(1, N_DEV):
            g = g + p_ref[dev].astype(F32)
        g_ref[...] = g
        nm = ADAM_B1 * m_ref[...] + (1.0 - ADAM_B1) * g
        nv = ADAM_B2 * v_ref[...] + (1.0 - ADAM_B2) * jnp.square(g)
        nm_ref[...] = nm
        nv_ref[...] = nv
        m_hat = nm / (1.0 - ADAM_B1 ** ADAM_STEP)
        v_hat = nv / (1.0 - ADAM_B2 ** ADAM_STEP)
        d_ref[...] = -ADAM_LR * (m_hat / (jnp.sqrt(v_hat) + ADAM_EPS) + ADAM_WD * w_ref[...])

    tile = _rows(tr, cols)
    out = SDS((rows, cols), F32)
    return pl.pallas_call(
        body, name=name, grid=(rows // tr,),
        in_specs=[pl.BlockSpec((N_DEV, tr, cols), lambda i: (0, i, 0)), tile, tile, tile],
        out_specs=[tile, tile, tile, tile],
        out_shape=[out, out, out, out],
        compiler_params=_seq_params(32),
    )(parts, w, m, v)


def _as_rows(flat):
    n = flat.shape[-1]
    padded = _round_up(n, PACK_TILE)
    if padded != n:
        flat = jnp.pad(flat, [(0, 0)] * (flat.ndim - 1) + [(0, padded - n)])
    return flat.reshape(flat.shape[:-1] + (padded // LANES, LANES))


def _columns_to_shards(full):
    r, c8 = full.shape
    return jnp.transpose(full.reshape(r, N_DEV, c8 // N_DEV), (1, 0, 2))


def _shards_to_columns(shards):
    n, r, c = shards.shape
    return jnp.transpose(shards, (1, 0, 2)).reshape(r, n * c)


def _pack_local(tensors):
    return jnp.concatenate([_as_rows(t.reshape(-1)) for t in tensors], axis=0)


def _unpack_local(packed, shapes):
    out, r = [], 0
    for shape in shapes:
        n = 1
        for dim in shape:
            n *= dim
        rows = _round_up(n, PACK_TILE) // LANES
        out.append(packed[r:r + rows].reshape(-1)[:n].reshape(shape))
        r += rows
    return out


def kernel(x, even_norm, even_w_in, pool_w, pool_scale, ssm_log_dt, ssm_a_re, ssm_a_im, ssm_b_re, ssm_b_im, ssm_c_re, ssm_c_im, ssm_d, ssm_w_glu, even_w_out, odd_norm, odd_w_in, conv_w, conv_b, conv_ln_g, conv_ln_b, odd_w_out, final_norm, loss_target, m_even_norm, m_even_w_in, m_pool_w, m_pool_scale, m_ssm_log_dt, m_ssm_a_re, m_ssm_a_im, m_ssm_b_re, m_ssm_b_im, m_ssm_c_re, m_ssm_c_im, m_ssm_d, m_ssm_w_glu, m_even_w_out, m_odd_norm, m_odd_w_in, m_conv_w, m_conv_b, m_conv_ln_g, m_conv_ln_b, m_odd_w_out, m_final_norm, v_even_norm, v_even_w_in, v_pool_w, v_pool_scale, v_ssm_log_dt, v_ssm_a_re, v_ssm_a_im, v_ssm_b_re, v_ssm_b_im, v_ssm_c_re, v_ssm_c_im, v_ssm_d, v_ssm_w_glu, v_even_w_out, v_odd_norm, v_odd_w_in, v_conv_w, v_conv_b, v_conv_ln_g, v_conv_ln_b, v_odd_w_out, v_final_norm):
    given = dict(locals())
    xs = x[0]
    tgt = loss_target[0]
    d_model = xs.shape[1]

    def local(prefix, name):
        t = given[prefix + name]
        return t if name == 'final_norm' else t[0]

    def small_block(prefix):
        parts = [jnp.pad(local(prefix, 'conv_w'), ((0, SMALL_AT['odd_norm'] - CONV_KERNEL), (0, 0)))]
        for n in SMALL_VECTORS:
            parts.append(jnp.pad(local(prefix, n).reshape(1, LANES), ((0, SUBLANES - 1), (0, 0))))
        return jnp.concatenate(parts, axis=0)

    def as_tile(row):
        return jnp.pad(row, ((0, SUBLANES - row.shape[0]), (0, 0)))

    (w_in_e,) = _push_alone("gather_first", [_Push(local('', 'even_w_in').astype(MM), False)])
    later = [_Push(local('', 'ssm_w_glu').astype(MM), False), _Push(local('', 'even_w_out').astype(MM), False),
             _Push(local('', 'odd_w_in').astype(MM), False), _Push(local('', 'odd_w_out').astype(MM), False),
             _Push(small_block(''), False)]

    a_re, a_im = local('', 'ssm_a_re'), local('', 'ssm_a_im')
    groups, state = a_re.shape
    log_dt = local('', 'ssm_log_dt').reshape(groups, 1)
    b_re_t = jnp.transpose(local('', 'ssm_b_re'), (2, 0, 1))
    b_im_t = jnp.transpose(local('', 'ssm_b_im'), (2, 0, 1))
    abr, abi, bbr, bbi = _disc_fwd(log_dt, a_re, a_im, b_re_t, b_im_t)
    a_re_row = abr.reshape(1, groups * state)
    a_im_row = abi.reshape(1, groups * state)
    bb = jnp.concatenate([_block_diag(jnp.transpose(bbr, (1, 0, 2))), _block_diag(jnp.transpose(bbi, (1, 0, 2)))],
                         axis=2).astype(MM)
    cc_re = _block_diag(jnp.transpose(local('', 'ssm_c_re'), (0, 2, 1))).astype(MM)
    cc_im = _block_diag(jnp.transpose(local('', 'ssm_c_im'), (0, 2, 1))).astype(MM)
    pw = local('', 'pool_w').astype(MM)
    g0 = local('', 'even_norm').reshape(1, d_model)
    fg = local('', 'final_norm').reshape(1, d_model)
    scale = local('', 'pool_scale').reshape(1, -1)
    d_skip = local('', 'ssm_d').reshape(1, -1)

    proj0, h0, (wg3, wo_e3) = _even_in(xs, g0, w_in_e, later[:2])
    pooled = _pool_fwd(proj0)
    bu_re, bu_im, (wo_o3, small8) = _ssm_bu(proj0, bb, later[3:])
    x_re, x_im, (wi_o3,) = _scan_fwd(a_re_row, a_im_row, bu_re, bu_im, later[2:3])
    w_glu = _shards_to_columns(wg3)
    w_out_e = wo_e3.reshape(d_model, d_model)
    (x1, ymix, mixed, yin, gv), _ = _even_mix(
        pooled, proj0, x_re, x_im, xs, pw, scale, cc_re, cc_im, d_skip, w_glu, w_out_e, [])
    w_in_o = _shards_to_columns(wi_o3)
    w_out_o = wo_o3.reshape(d_model, d_model)
    small = _shards_to_columns(small8)
    cw = small[:CONV_KERNEL]
    g1, cb, lg, lb = (small[SMALL_AT[n]:SMALL_AT[n] + 1] for n in SMALL_VECTORS)
    proj1, glu, h1 = _odd_in(x1, g1, w_in_o)
    conv = _conv_fwd(glu, cw)

    half = pooled.shape[1]
    n_state = groups * state
    rows_per = d_model // N_DEV
    dc, dz1, dx2, y1, loss_tile, dcb, dlg, dlb, dfg = _odd_out(conv, proj1, x1, tgt, cb, lg, lb, w_out_o, fg)
    g_odd_out = _mm_tn("dw_odd_out", y1, dx2, d_model, d_model, WIRE).reshape(N_DEV, rows_per, d_model)
    dglu, dcw, (r_odd_out,) = _conv_bwd(dc, glu, cw, [_Push(g_odd_out, True)])
    dx1, dproj1, dg1 = _odd_in_bwd(x1, proj1, dglu, dz1, dx2, g1, w_in_o)
    g_odd_in = _columns_to_shards(_mm_tn("dw_odd_in", h1, dproj1, d_model, 3 * d_model, WIRE))
    g_even_out = _mm_tn("dw_even_out", ymix, dx1, d_model, d_model, WIRE).reshape(N_DEV, rows_per, d_model)
    mix_grads, (r_odd_in,) = _even_mix_bwd(pooled, proj0, mixed, gv, dx1, pw, scale, cc_re, cc_im, d_skip, w_glu,
                                           w_out_e, [_Push(g_odd_in, True)])
    dpooled, du_skip, dz0, dx_re, dx_im, dyin, dgv, dpw, dscale, dd = mix_grads
    g_glu = _columns_to_shards(_mm_tn("dw_glu", yin, dgv, half, d_model, WIRE))
    sblk, cblk = n_state // SSM_BLOCKS, half // SSM_BLOCKS
    dcc_re = _mm_tn_blocks("dw_c_re", x_re, dyin, sblk, cblk, SSM_BLOCKS, ts=2048)
    dcc_im = _mm_tn_blocks("dw_c_im", x_im, dyin, sblk, cblk, SSM_BLOCKS, sign=-1.0, ts=2048)
    dc_re = jnp.transpose(_diag_blocks(dcc_re, groups), (0, 2, 1))
    dc_im = jnp.transpose(_diag_blocks(dcc_im, groups), (0, 2, 1))
    g_small = _columns_to_shards(jnp.concatenate(
        [dcw, as_tile(dg1), as_tile(dcb), as_tile(dlg), as_tile(dlb)], axis=0))
    early = {'pool_w': dpw, 'pool_scale': dscale, 'ssm_c_re': dc_re, 'ssm_c_im': dc_im, 'ssm_d': dd,
             'final_norm': dfg}
    g_early = _pack_local([early[n] for n in EARLY_REPLICATED] + [loss_tile])
    g_re, g_im, dabr, dabi, (r_even_out, r_glu, r_small, r_early) = _scan_bwd(
        a_re_row, a_im_row, dx_re, dx_im, x_re, x_im,
        [_Push(g_even_out, True), _Push(g_glu, True), _Push(g_small, True), _Push(g_early, False)])
    du_ssm = _ssm_bu_bwd(g_re, g_im, du_skip, bb)
    du_pool = _pool_bwd(dpooled)
    grad_x, dg0 = _even_in_bwd(xs, du_pool, du_ssm, dz0, dx1, g0, w_in_e)
    g_even_in = _columns_to_shards(jnp.concatenate(
        [_mm_tn("dw_even_in_pool", h0, du_pool, d_model, half, WIRE),
         _mm_tn("dw_even_in_ssm", h0, du_ssm, d_model, half, WIRE),
         _mm_tn("dw_even_in_gate", h0, dz0, d_model, d_model, WIRE)], axis=1))
    g_mid = _pack_local([dg0])
    dbb_re, dbb_im, (r_even_in, r_mid) = _dw_bbar(proj0, g_re, g_im, cblk, sblk,
                                                  [_Push(g_even_in, True), _Push(g_mid, False)])
    dbbr = jnp.transpose(_diag_blocks(dbb_re, groups), (2, 0, 1))
    dbbi = jnp.transpose(_diag_blocks(dbb_im, groups), (2, 0, 1))
    dld, dar, dai, dbr, dbi = _disc_bwd(log_dt, a_re, a_im, b_re_t, b_im_t,
                                        dabr.reshape(groups, state), dabi.reshape(groups, state), dbbr, dbbi)
    late = {'ssm_log_dt': dld, 'ssm_a_re': dar, 'ssm_a_im': dai,
            'ssm_b_re': jnp.transpose(dbr, (1, 2, 0)), 'ssm_b_im': jnp.transpose(dbi, (1, 2, 0))}
    g_late = _pack_local([late[n] for n in LATE_REPLICATED])
    (r_late,) = _push_alone("exchange_tail", [_Push(g_late, False)])

    results = {}
    for n, parts in (('even_w_in', r_even_in), ('ssm_w_glu', r_glu), ('even_w_out', r_even_out),
                     ('odd_w_in', r_odd_in), ('odd_w_out', r_odd_out)):
        results[n] = _adamw("adamw_" + n, parts, local('', n), local('m_', n), local('v_', n))
    small_out = _adamw("adamw_small", r_small, small_block(''), small_block('m_'), small_block('v_'))
    results['conv_w'] = [o[:CONV_KERNEL] for o in small_out]
    for n in SMALL_VECTORS:
        results[n] = [o[SMALL_AT[n]] for o in small_out]
    loss_rows = [jnp.zeros_like(loss_tile)]
    for names, parts, extra in ((EARLY_REPLICATED, r_early, loss_rows), (MID_REPLICATED, r_mid, []),
                               (LATE_REPLICATED, r_late, [])):
        packed = [_pack_local([local(p, n) for n in names] + extra) for p in ('', 'm_', 'v_')]
        out = _adamw("adamw_" + names[0], parts, *packed)
        unpacked = [_unpack_local(o, [given[n].shape for n in names]) for o in out]
        for k, n in enumerate(names):
            results[n] = [u[k] for u in unpacked]
        if extra:
            loss = out[0][-SUBLANES, 0]

    outs = [loss, grad_x[None]]
    for kind in range(4):
        outs.extend(results[n][kind].reshape(given[n].shape) for n in WEIGHTS)
    return tuple(outs)
```

```python
import functools

import jax
import jax.numpy as jnp
from jax import lax
from jax.experimental import pallas as pl
from jax.experimental.pallas import tpu as pltpu

F32 = jnp.float32
MM = jnp.bfloat16
WIRE = jnp.bfloat16
SDS = jax.ShapeDtypeStruct

RMS_EPS = 1e-6
LN_EPS = 1e-5
POOL_WINDOWS = (2, 4, 8, 16)
POOL_HALO = 16
CONV_KERNEL = 31
CONV_HALO = 32
N_DEV = 8
LANES = 128
SUBLANES = 8
PACK_TILE = SUBLANES * LANES
ADAM_ROWS = 128
MIB = 1024 * 1024

ADAM_LR = 0.001
ADAM_B1 = 0.9
ADAM_B2 = 0.999
ADAM_EPS = 1e-08
ADAM_WD = 0.01
ADAM_STEP = 10

WEIGHTS = ['even_norm', 'even_w_in', 'pool_w', 'pool_scale', 'ssm_log_dt', 'ssm_a_re', 'ssm_a_im', 'ssm_b_re',
           'ssm_b_im', 'ssm_c_re', 'ssm_c_im', 'ssm_d', 'ssm_w_glu', 'even_w_out', 'odd_norm', 'odd_w_in', 'conv_w',
           'conv_b', 'conv_ln_g', 'conv_ln_b', 'odd_w_out', 'final_norm']
SMALL_VECTORS = ('odd_norm', 'conv_b', 'conv_ln_g', 'conv_ln_b')
SMALL_AT = {'odd_norm': 32, 'conv_b': 40, 'conv_ln_g': 48, 'conv_ln_b': 56}
EARLY_REPLICATED = ['pool_w', 'pool_scale', 'ssm_c_re', 'ssm_c_im', 'ssm_d', 'final_norm']
MID_REPLICATED = ['even_norm']
LATE_REPLICATED = ['ssm_log_dt', 'ssm_a_re', 'ssm_a_im', 'ssm_b_re', 'ssm_b_im']


def _round_up(n, m):
    return (n + m - 1) // m * m


def _sigmoid(x):
    return jax.nn.sigmoid(x)


def _silu(x):
    return x * jax.nn.sigmoid(x)


def _rms(x, g):
    return x * lax.rsqrt(jnp.mean(x * x, axis=-1, keepdims=True) + RMS_EPS) * g


def _dot(a, b):
    return jnp.dot(a.astype(MM), b.astype(MM), preferred_element_type=F32)


def _dot_nt(a, b):
    return lax.dot_general(a.astype(MM), b.astype(MM), (((1,), (1,)), ((), ())), preferred_element_type=F32)


def _dot_tn(a, b):
    return lax.dot_general(a.astype(MM), b.astype(MM), (((0,), (0,)), ((), ())), preferred_element_type=F32)


def _rows(ts, width, col=0):
    return pl.BlockSpec((ts, width), lambda i: (i, col))


def _rows_rev(ts, width, n, col=0):
    return pl.BlockSpec((ts, width), lambda i: (n - 1 - i, col))


def _full(shape):
    zeros = (0,) * len(shape)
    return pl.BlockSpec(shape, lambda i: zeros)


def _seq_params(vmem_mib=48, dims=1):
    return pltpu.CompilerParams(dimension_semantics=("arbitrary",) * dims, vmem_limit_bytes=vmem_mib * MIB)


def _mesh_position():
    x, y, c = lax.axis_index("x"), lax.axis_index("y"), lax.axis_index("c")
    return x, y, c


def _peer(pos, relation):
    x, y, c = pos
    px = 1 - x if relation & 4 else x
    py = 1 - y if relation & 2 else y
    pc = 1 - c if relation & 1 else c
    return px, py, pc


ANY = pl.BlockSpec(memory_space=pl.ANY)


class _Push:
    def __init__(self, array, scatter):
        self.array = array
        self.scatter = scatter
        self.out = SDS(array.shape if scatter else (N_DEV,) + array.shape, array.dtype)


def _push_scratch(n):
    if n == 0:
        return []
    return [pltpu.SemaphoreType.DMA((n * (N_DEV - 1),)), pltpu.SemaphoreType.DMA((n * (N_DEV - 1),)),
            pltpu.SemaphoreType.DMA((n,))]


def _push_copies(pushes, srcs, dsts, send_sems, recv_sems, local_sems):
    pos = _mesh_position()
    me = 4 * pos[0] + 2 * pos[1] + pos[2]
    copies = []
    for a, (push, src, dst) in enumerate(zip(pushes, srcs, dsts)):
        copies.append(pltpu.make_async_copy(src.at[me] if push.scatter else src, dst.at[me], local_sems.at[a]))
        for relation in range(1, N_DEV):
            peer = _peer(pos, relation)
            peer_id = 4 * peer[0] + 2 * peer[1] + peer[2]
            k = a * (N_DEV - 1) + relation - 1
            copies.append(pltpu.make_async_remote_copy(
                src_ref=src.at[peer_id] if push.scatter else src, dst_ref=dst.at[me],
                send_sem=send_sems.at[k], recv_sem=recv_sems.at[k], device_id=peer,
                device_id_type=pl.DeviceIdType.MESH))
    return copies


def _push_alone(name, pushes):
    n = len(pushes)

    def body(*refs):
        copies = _push_copies(pushes, refs[:n], refs[n:2 * n], *refs[2 * n:])
        for cp in copies:
            cp.start()
        for cp in copies:
            cp.wait()

    return pl.pallas_call(
        body, name=name, out_shape=[p.out for p in pushes], in_specs=[ANY] * n, out_specs=[ANY] * n,
        scratch_shapes=_push_scratch(n),
    )(*[p.array for p in pushes])


def _riding_start(pushes, first, refs):
    n = len(pushes)
    if n == 0:
        return

    @pl.when(first)
    def _():
        for cp in _push_copies(pushes, refs[:n], refs[n:2 * n], *refs[2 * n:]):
            cp.start()


def _riding_wait(pushes, last, refs):
    n = len(pushes)
    if n == 0:
        return

    @pl.when(last)
    def _():
        for cp in _push_copies(pushes, refs[:n], refs[n:2 * n], *refs[2 * n:]):
            cp.wait()


def _split_refs(refs, n_in, n_out, n_push):
    ins = refs[:n_in]
    srcs = refs[n_in:n_in + n_push]
    o0 = n_in + n_push
    outs = refs[o0:o0 + n_out]
    dsts = refs[o0 + n_out:o0 + n_out + n_push]
    s0 = o0 + n_out + n_push
    n_sem = 3 if n_push else 0
    sems = refs[s0:s0 + n_sem]
    scratch = refs[s0 + n_sem:]
    return ins, outs, scratch, tuple(srcs) + tuple(dsts) + tuple(sems)


def _mm_tn(name, a, b, m, n, out_dtype, a_col=0, b_col=0, ts=512):
    s = a.shape[0]
    nk = s // ts

    def body(a_ref, b_ref, o_ref, acc):
        k = pl.program_id(0)

        @pl.when(k == 0)
        def _():
            acc[...] = jnp.zeros_like(acc)

        acc[...] += _dot_tn(a_ref[...], b_ref[...])

        @pl.when(k == nk - 1)
        def _():
            o_ref[...] = acc[...].astype(out_dtype)

    return pl.pallas_call(
        body, name=name, grid=(nk,),
        in_specs=[pl.BlockSpec((ts, m), lambda k: (k, a_col)), pl.BlockSpec((ts, n), lambda k: (k, b_col))],
        out_specs=pl.BlockSpec((m, n), lambda k: (0, 0)),
        out_shape=SDS((m, n), out_dtype),
        scratch_shapes=[pltpu.VMEM((m, n), F32)],
        compiler_params=_seq_params(56),
    )(a, b)


def _mm_tn_blocks(name, a, b, m, n, nb, a_col=0, a_step=1, b_col=0, sign=1.0, ts=512):
    s = a.shape[0]

    def body(a_ref, b_ref, o_ref):
        @pl.when(pl.program_id(1) == 0)
        def _():
            o_ref[...] = jnp.zeros_like(o_ref)

        o_ref[...] += sign * _dot_tn(a_ref[...], b_ref[...])

    return pl.pallas_call(
        body, name=name, grid=(nb, s // ts),
        in_specs=[pl.BlockSpec((ts, m), lambda j, k: (k, a_col + a_step * j)),
                  pl.BlockSpec((ts, n), lambda j, k: (k, b_col + j))],
        out_specs=pl.BlockSpec((None, m, n), lambda j, k: (j, 0, 0)),
        out_shape=SDS((nb, m, n), F32),
        compiler_params=_seq_params(48, dims=2),
    )(a, b)


def _dw_bbar(proj, g_re, g_im, cb, sb, pushes, ts=2048):
    s = proj.shape[0]
    nb, nk = SSM_BLOCKS, s // ts
    np_ = len(pushes)

    def body(*refs):
        ins, outs, _, push_refs = _split_refs(refs, 3, 2, np_)
        u_ref, gre_ref, gim_ref = ins
        ore_ref, oim_ref = outs
        j, k = pl.program_id(0), pl.program_id(1)
        _riding_start(pushes, jnp.logical_and(j == 0, k == 0), push_refs)

        @pl.when(k == 0)
        def _():
            ore_ref[...] = jnp.zeros_like(ore_ref)
            oim_ref[...] = jnp.zeros_like(oim_ref)

        u = u_ref[...].astype(MM)
        ore_ref[...] += _dot_tn(gre_ref[...], u)
        oim_ref[...] += _dot_tn(gim_ref[...], u)
        _riding_wait(pushes, jnp.logical_and(j == nb - 1, k == nk - 1), push_refs)

    wide = pl.BlockSpec((ts, sb), lambda j, k: (k, j))
    out = pl.BlockSpec((None, sb, cb), lambda j, k: (j, 0, 0))
    res = pl.pallas_call(
        body, name="dw_bbar", grid=(nb, nk),
        in_specs=[pl.BlockSpec((ts, cb), lambda j, k: (k, nb + j)), wide, wide] + [ANY] * np_,
        out_specs=[out, out] + [ANY] * np_,
        out_shape=[SDS((nb, sb, cb), F32), SDS((nb, sb, cb), F32)] + [p.out for p in pushes],
        scratch_shapes=_push_scratch(np_),
        compiler_params=_seq_params(48, dims=2),
    )(proj, g_re, g_im, *[p.array for p in pushes])
    return res[0], res[1], res[2:]


def _even_in(x, g0, w_in, pushes):
    s, d = x.shape
    shards, _, wc = w_in.shape
    e = shards * wc
    ts = 512
    nt = s // ts
    np_ = len(pushes)

    def body(*refs):
        (x_ref, g_ref, w_ref), (proj_ref, h_ref), _, push_refs = _split_refs(refs, 3, 2, np_)
        i = pl.program_id(0)
        _riding_start(pushes, i == 0, push_refs)
        hb = _rms(x_ref[...], g_ref[...]).astype(MM)
        h_ref[...] = hb
        for k in range(shards):
            proj_ref[:, k * wc:(k + 1) * wc] = jnp.dot(hb, w_ref[k], preferred_element_type=F32)
        _riding_wait(pushes, i == nt - 1, push_refs)

    out = pl.pallas_call(
        body, name="even_in", grid=(nt,),
        in_specs=[_rows(ts, d), _full((1, d)), _full(w_in.shape)] + [ANY] * np_,
        out_specs=[_rows(ts, e), _rows(ts, d)] + [ANY] * np_,
        out_shape=[SDS((s, e), F32), SDS((s, d), MM)] + [p.out for p in pushes],
        scratch_shapes=_push_scratch(np_),
        compiler_params=_seq_params(48),
    )(x, g0, w_in, *[p.array for p in pushes])
    return out[0], out[1], out[2:]


def _pool_counts(t0, ts, w):
    pos = (t0 + lax.broadcasted_iota(jnp.int32, (ts, LANES), 0) + 1).astype(F32)
    return jnp.minimum(pos, float(w))


def _pool_fwd(proj):
    s = proj.shape[0]
    width = LANES * len(POOL_WINDOWS)
    ts = 512
    per = ts // POOL_HALO

    def body(prev_ref, u_ref, out_ref, ext):
        i = pl.program_id(0)
        ext[0:POOL_HALO, 0:width] = jnp.where(i == 0, 0.0, prev_ref[...])
        ext[POOL_HALO:, 0:width] = u_ref[...]
        for g, w in enumerate(POOL_WINDOWS):
            cols = slice(g * LANES, (g + 1) * LANES)
            tok = ext[pl.ds(POOL_HALO, ts), cols]
            acc = tok
            for k in range(1, w):
                acc = acc + ext[pl.ds(POOL_HALO - k, ts), cols]
            out_ref[:, cols] = acc / _pool_counts(i * ts, ts, w) - tok

    return pl.pallas_call(
        body, name="pool_fwd", grid=(s // ts,),
        in_specs=[pl.BlockSpec((POOL_HALO, width), lambda i: (jnp.maximum(i * per - 1, 0), 0)),
                  _rows(ts, width)],
        out_specs=_rows(ts, width),
        out_shape=SDS((s, width), F32),
        scratch_shapes=[pltpu.VMEM((ts + POOL_HALO, width + LANES), F32)],
        compiler_params=_seq_params(32),
    )(proj, proj)


def _pool_bwd(dp):
    s, width = dp.shape
    ts = 512
    per = ts // POOL_HALO
    n = s // ts

    def body(dp_ref, next_ref, out_ref, ext):
        i = pl.program_id(0)
        nxt = jnp.where(i == n - 1, 0.0, next_ref[...])
        for g, w in enumerate(POOL_WINDOWS):
            cols = slice(g * LANES, (g + 1) * LANES)
            cur = dp_ref[:, cols]
            ext[0:ts, cols] = cur / _pool_counts(i * ts, ts, w)
            ext[ts:, cols] = nxt[:, cols] / _pool_counts((i + 1) * ts, POOL_HALO, w)
            acc = -cur
            for k in range(w):
                acc = acc + ext[pl.ds(k, ts), cols]
            out_ref[:, cols] = acc

    return pl.pallas_call(
        body, name="pool_bwd", grid=(n,),
        in_specs=[_rows(ts, width),
                  pl.BlockSpec((POOL_HALO, width), lambda i: (jnp.minimum((i + 1) * per, s // POOL_HALO - 1), 0))],
        out_specs=_rows(ts, width),
        out_shape=SDS((s, width), F32),
        scratch_shapes=[pltpu.VMEM((ts + POOL_HALO, width + LANES), F32)],
        compiler_params=_seq_params(32),
    )(dp, dp)


SSM_BLOCKS = 4


def _ssm_bu(proj, bb, pushes):
    s = proj.shape[0]
    nb, cb, two_nb = bb.shape
    sb = two_nb // 2
    cin, n = nb * cb, nb * sb
    ts = 512
    nt = s // ts
    np_ = len(pushes)

    def body(*refs):
        (u_ref, bb_ref), (re_ref, im_ref), _, push_refs = _split_refs(refs, 2, 2, np_)
        i = pl.program_id(0)
        _riding_start(pushes, i == 0, push_refs)
        for b in range(nb):
            bu = _dot(u_ref[:, b * cb:(b + 1) * cb], bb_ref[b])
            re_ref[:, b * sb:(b + 1) * sb] = bu[:, :sb]
            im_ref[:, b * sb:(b + 1) * sb] = bu[:, sb:]
        _riding_wait(pushes, i == nt - 1, push_refs)

    out = pl.pallas_call(
        body, name="ssm_bu", grid=(nt,),
        in_specs=[_rows(ts, cin, col=1), _full(bb.shape)] + [ANY] * np_,
        out_specs=[_rows(ts, n), _rows(ts, n)] + [ANY] * np_,
        out_shape=[SDS((s, n), F32), SDS((s, n), F32)] + [p.out for p in pushes],
        scratch_shapes=_push_scratch(np_),
        compiler_params=_seq_params(48),
    )(proj, bb, *[p.array for p in pushes])
    return out[0], out[1], out[2:]


SCAN_LANES = 256


def _cmul(a, b):
    return a[0] * b[0] - a[1] * b[1], a[0] * b[1] + a[1] * b[0]


SCAN_STEPS = (1, 2, 4)


def _fill_scan_tables(tab, ar_row, ai_row, reverse):
    shape = (SUBLANES, ar_row.shape[1])
    a1 = (jnp.broadcast_to(ar_row, shape), jnp.broadcast_to(ai_row, shape))
    a2 = _cmul(a1, a1)
    a4 = _cmul(a2, a2)
    a8 = _cmul(a4, a4)
    row = lax.broadcasted_iota(jnp.int32, shape, 0)
    for idx, (k, q) in enumerate(zip(SCAN_STEPS, (a1, a2, a4))):
        keep = (row < SUBLANES - k) if reverse else (row >= k)
        tab[2 * idx] = jnp.where(keep, q[0], 0.0)
        tab[2 * idx + 1] = jnp.where(keep, q[1], 0.0)
    expo = (SUBLANES - row) if reverse else (row + 1)
    pr, pi = jnp.ones(shape, F32), jnp.zeros(shape, F32)
    for bit, q in enumerate((a1, a2, a4, a8)):
        take = ((expo >> bit) & 1) == 1
        nr, ni = _cmul((pr, pi), q)
        pr, pi = jnp.where(take, nr, pr), jnp.where(take, ni, pi)
    tab[2 * len(SCAN_STEPS)] = pr
    tab[2 * len(SCAN_STEPS) + 1] = pi


def _group_scan(v, tab, cols, carry, reverse):
    vr, vi = v
    for idx, k in enumerate(SCAN_STEPS):
        shift = SUBLANES - k if reverse else k
        tr, ti = _cmul((tab[2 * idx, :, cols], tab[2 * idx + 1, :, cols]),
                       (pltpu.roll(vr, shift, 0), pltpu.roll(vi, shift, 0)))
        vr, vi = vr + tr, vi + ti
    last = 2 * len(SCAN_STEPS)
    tr, ti = _cmul((tab[last, :, cols], tab[last + 1, :, cols]), carry)
    return vr + tr, vi + ti


def _scan_fwd(a_re, a_im, bu_re, bu_im, pushes):
    s, n = bu_re.shape
    ts = 512
    nt = s // ts
    groups = ts // SUBLANES
    np_ = len(pushes)

    def body(*refs):
        ins, outs, scratch, push_refs = _split_refs(refs, 4, 2, np_)
        ar_ref, ai_ref, bre_ref, bim_ref = ins
        xre_ref, xim_ref = outs
        tab, cre, cim = scratch
        i = pl.program_id(0)
        _riding_start(pushes, i == 0, push_refs)

        @pl.when(i == 0)
        def _():
            _fill_scan_tables(tab, ar_ref[...], ai_ref[...], False)
            cre[...] = jnp.zeros_like(cre)
            cim[...] = jnp.zeros_like(cim)

        def group(gi, carry):
            r0 = pl.multiple_of(gi * SUBLANES, SUBLANES)
            for c in range(n // SCAN_LANES):
                cols = slice(c * SCAN_LANES, (c + 1) * SCAN_LANES)
                v = (bre_ref[pl.ds(r0, SUBLANES), cols], bim_ref[pl.ds(r0, SUBLANES), cols])
                vr, vi = _group_scan(v, tab, cols, (cre[:, cols], cim[:, cols]), False)
                xre_ref[pl.ds(r0, SUBLANES), cols] = vr
                xim_ref[pl.ds(r0, SUBLANES), cols] = vi
                cre[:, cols] = jnp.broadcast_to(vr[SUBLANES - 1:SUBLANES, :], vr.shape)
                cim[:, cols] = jnp.broadcast_to(vi[SUBLANES - 1:SUBLANES, :], vi.shape)
            return carry

        lax.fori_loop(0, groups, group, 0, unroll=2)
        _riding_wait(pushes, i == nt - 1, push_refs)

    out = pl.pallas_call(
        body, name="ssm_scan", grid=(nt,),
        in_specs=[_full((1, n)), _full((1, n)), _rows(ts, n), _rows(ts, n)] + [ANY] * np_,
        out_specs=[_rows(ts, n), _rows(ts, n)] + [ANY] * np_,
        out_shape=[SDS((s, n), F32), SDS((s, n), F32)] + [p.out for p in pushes],
        scratch_shapes=_push_scratch(np_) + [pltpu.VMEM((2 * len(SCAN_STEPS) + 2, SUBLANES, n), F32),
                                             pltpu.VMEM((SUBLANES, n), F32), pltpu.VMEM((SUBLANES, n), F32)],
        compiler_params=_seq_params(48),
    )(a_re, a_im, bu_re, bu_im, *[p.array for p in pushes])
    return out[0], out[1], out[2:]


def _scan_bwd(a_re, a_im, dx_re, dx_im, x_re, x_im, pushes):
    s, n = dx_re.shape
    ts = 256
    nt = s // ts
    groups = ts // SUBLANES
    np_ = len(pushes)

    def body(*refs):
        ins, outs, scratch, push_refs = _split_refs(refs, 6, 4, np_)
        ar_ref, ai_ref, dre_ref, dim_ref, xre_ref, xim_ref = ins
        gre_ref, gim_ref, dar_ref, dai_ref = outs
        tab, cre, cim, accr, acci = scratch
        i = pl.program_id(0)
        _riding_start(pushes, i == 0, push_refs)

        @pl.when(i == 0)
        def _():
            _fill_scan_tables(tab, ar_ref[...], -ai_ref[...], True)
            for ref in (cre, cim, accr, acci):
                ref[...] = jnp.zeros_like(ref)

        inner = lax.broadcasted_iota(jnp.int32, (SUBLANES, SCAN_LANES), 0) < SUBLANES - 1

        def group(k, carry):
            r0 = pl.multiple_of((groups - 1 - k) * SUBLANES, SUBLANES)
            for c in range(n // SCAN_LANES):
                cols = slice(c * SCAN_LANES, (c + 1) * SCAN_LANES)
                after = (cre[:, cols], cim[:, cols])
                v = (dre_ref[pl.ds(r0, SUBLANES), cols], dim_ref[pl.ds(r0, SUBLANES), cols])
                vr, vi = _group_scan(v, tab, cols, after, True)
                gre_ref[pl.ds(r0, SUBLANES), cols] = vr
                gim_ref[pl.ds(r0, SUBLANES), cols] = vi
                nr = jnp.where(inner, pltpu.roll(vr, SUBLANES - 1, 0), after[0])
                ni = jnp.where(inner, pltpu.roll(vi, SUBLANES - 1, 0), after[1])
                xr, xi = xre_ref[pl.ds(r0, SUBLANES), cols], xim_ref[pl.ds(r0, SUBLANES), cols]
                accr[:, cols] += nr * xr + ni * xi
                acci[:, cols] += ni * xr - nr * xi
                cre[:, cols] = jnp.broadcast_to(vr[0:1, :], vr.shape)
                cim[:, cols] = jnp.broadcast_to(vi[0:1, :], vi.shape)
            return carry

        lax.fori_loop(0, groups, group, 0, unroll=2)

        @pl.when(i == nt - 1)
        def _():
            dar_ref[...] = jnp.sum(accr[...], axis=0, keepdims=True)
            dai_ref[...] = jnp.sum(acci[...], axis=0, keepdims=True)

        _riding_wait(pushes, i == nt - 1, push_refs)

    small = pltpu.VMEM((SUBLANES, n), F32)
    out = pl.pallas_call(
        body, name="ssm_scan_bwd", grid=(nt,),
        in_specs=[_full((1, n)), _full((1, n))] + [_rows_rev(ts, n, nt)] * 4 + [ANY] * np_,
        out_specs=[_rows_rev(ts, n, nt), _rows_rev(ts, n, nt), _full((1, n)), _full((1, n))] + [ANY] * np_,
        out_shape=[SDS((s, n), F32), SDS((s, n), F32), SDS((1, n), F32), SDS((1, n), F32)]
        + [p.out for p in pushes],
        scratch_shapes=_push_scratch(np_) + [pltpu.VMEM((2 * len(SCAN_STEPS) + 2, SUBLANES, n), F32),
                                             small, small, small, small],
        compiler_params=_seq_params(48),
    )(a_re, a_im, dx_re, dx_im, x_re, x_im, *[p.array for p in pushes])
    return out[0], out[1], out[2], out[3], out[4:]


def _mix_a(mixed, scale, za):
    return mixed * scale * _silu(za)


def _mix_b(val, gate, zb):
    return val * _sigmoid(gate) * _silu(zb)


def _even_mix(pooled, proj, x_re, x_im, x, pw, scale, cc_re, cc_im, d_skip, w_glu, w_out, pushes):
    s, d = x.shape
    half = pooled.shape[1]
    n = x_re.shape[1]
    ts = 256
    nt = s // ts
    np_ = len(pushes)

    def body(*refs):
        ins, outs, _, push_refs = _split_refs(refs, 13, 5, np_)
        (pooled_ref, u_ref, z_ref, xre_ref, xim_ref, x_ref, pw_ref, scale_ref, ccre_ref, ccim_ref, d_ref,
         wglu_ref, wout_ref) = ins
        x1_ref, ymix_ref, mixed_ref, yin_ref, gv_ref = outs
        i = pl.program_id(0)
        _riding_start(pushes, i == 0, push_refs)
        for g in range(len(POOL_WINDOWS)):
            cols = slice(g * LANES, (g + 1) * LANES)
            mixed_ref[:, cols] = _dot(pooled_ref[:, cols], pw_ref[g])
        sb, cb = ccre_ref.shape[1:]
        for b in range(SSM_BLOCKS):
            states, chans = slice(b * sb, (b + 1) * sb), slice(b * cb, (b + 1) * cb)
            yin_ref[:, chans] = (_dot(xre_ref[:, states], ccre_ref[b]) - _dot(xim_ref[:, states], ccim_ref[b])
                                 + d_ref[:, chans] * u_ref[:, chans])
        yin = yin_ref[...]
        gv = _dot(yin, wglu_ref[...])
        gv_ref[...] = gv
        ya = _mix_a(mixed_ref[...], scale_ref[...], z_ref[:, :half])
        yb = _mix_b(gv[:, :half], gv[:, half:], z_ref[:, half:])
        ymix = jnp.concatenate([ya, yb], axis=1).astype(MM)
        ymix_ref[...] = ymix
        x1_ref[...] = x_ref[...] + jnp.dot(ymix, wout_ref[...], preferred_element_type=F32)
        _riding_wait(pushes, i == nt - 1, push_refs)

    out = pl.pallas_call(
        body, name="even_mix", grid=(nt,),
        in_specs=[_rows(ts, half), _rows(ts, half, col=1), _rows(ts, d, col=1), _rows(ts, n), _rows(ts, n),
                  _rows(ts, d), _full(pw.shape), _full((1, half)), _full(cc_re.shape), _full(cc_im.shape),
                  _full((1, half)), _full((half, d)), _full((d, d))] + [ANY] * np_,
        out_specs=[_rows(ts, d), _rows(ts, d), _rows(ts, half), _rows(ts, half), _rows(ts, d)] + [ANY] * np_,
        out_shape=[SDS((s, d), F32), SDS((s, d), MM), SDS((s, half), F32), SDS((s, half), F32), SDS((s, d), F32)]
        + [p.out for p in pushes],
        scratch_shapes=_push_scratch(np_),
        compiler_params=_seq_params(56),
    )(pooled, proj, proj, x_re, x_im, x, pw, scale, cc_re, cc_im, d_skip, w_glu, w_out,
      *[p.array for p in pushes])
    return out[:5], out[5:]


def _glu(val, gt):
    return val * _sigmoid(gt)


def _odd_in(x1, g1, w_in):
    s, d = x1.shape
    e = w_in.shape[1]
    ts = 512

    def body(x_ref, g_ref, w_ref, proj_ref, glu_ref, h_ref):
        hb = _rms(x_ref[...], g_ref[...]).astype(MM)
        h_ref[...] = hb
        proj_ref[...] = jnp.dot(hb, w_ref[...], preferred_element_type=F32)
        glu_ref[...] = _glu(proj_ref[:, :d], proj_ref[:, d:2 * d])

    return pl.pallas_call(
        body, name="odd_in", grid=(s // ts,),
        in_specs=[_rows(ts, d), _full((1, d)), _full((d, e))],
        out_specs=[_rows(ts, e), _rows(ts, d), _rows(ts, d)],
        out_shape=[SDS((s, e), F32), SDS((s, d), F32), SDS((s, d), MM)],
        compiler_params=_seq_params(56),
    )(x1, g1, w_in)


CONV_ROWS = 32
CONV_COLS = 256


def _conv_scratch(ts, ch):
    return pltpu.VMEM((SUBLANES, ts + CONV_HALO, ch + LANES), F32)


def _phase_copies(sh, rows, ch):
    for o in range(1, SUBLANES):
        sh[o, 0:rows, 0:ch] = sh[0, pl.ds(o, rows), 0:ch]


def _conv_taps(sh, w_ref, offsets, r0, cols):
    acc = jnp.zeros((CONV_ROWS, cols.stop - cols.start), F32)
    for o in range(SUBLANES):
        taps = [(j, e // SUBLANES) for j, e in offsets if e % SUBLANES == o]
        if not taps:
            continue
        q0 = min(q for _, q in taps)
        q1 = max(q for _, q in taps)
        win = sh[o, pl.ds(r0 + SUBLANES * q0, CONV_ROWS + SUBLANES * (q1 - q0)), cols]
        for j, q in taps:
            lo = SUBLANES * (q - q0)
            acc = acc + w_ref[j:j + 1, cols] * win[lo:lo + CONV_ROWS]
    return acc


def _conv_fwd(g, w):
    s, ch = g.shape
    ts = 256
    per = ts // CONV_HALO
    lead = CONV_HALO - (CONV_KERNEL - 1)
    span = ts + CONV_HALO - SUBLANES

    def body(prev_ref, g_ref, w_ref, out_ref, sh):
        i = pl.program_id(0)
        sh[0, 0:CONV_HALO, 0:ch] = jnp.where(i == 0, 0.0, prev_ref[...])
        sh[0, CONV_HALO:, 0:ch] = g_ref[...]
        _phase_copies(sh, span, ch)

        offsets = [(j, lead + j) for j in range(CONV_KERNEL)]

        def block(rb, carry):
            r0 = pl.multiple_of(rb * CONV_ROWS, CONV_ROWS)
            for c in range(ch // CONV_COLS):
                cols = slice(c * CONV_COLS, (c + 1) * CONV_COLS)
                out_ref[pl.ds(r0, CONV_ROWS), cols] = _conv_taps(sh, w_ref, offsets, r0, cols)
            return carry

        lax.fori_loop(0, ts // CONV_ROWS, block, 0)

    return pl.pallas_call(
        body, name="conv_fwd", grid=(s // ts,),
        in_specs=[pl.BlockSpec((CONV_HALO, ch), lambda i: (jnp.maximum(i * per - 1, 0), 0)),
                  _rows(ts, ch), _full(w.shape)],
        out_specs=_rows(ts, ch),
        out_shape=SDS((s, ch), F32),
        scratch_shapes=[_conv_scratch(ts, ch)],
        compiler_params=_seq_params(40),
    )(g, g, w)


def _conv_bwd(dc, g, w, pushes):
    s, ch = dc.shape
    ts = 256
    per = ts // CONV_HALO
    n = s // ts
    lead = CONV_HALO - (CONV_KERNEL - 1)
    span = ts + CONV_HALO - SUBLANES
    np_ = len(pushes)

    def body(*refs):
        ins, outs, scratch, push_refs = _split_refs(refs, 5, 2, np_)
        dc_ref, next_ref, prev_ref, g_ref, w_ref = ins
        dg_ref, dw_ref = outs
        shd, shg, wacc = scratch
        i = pl.program_id(0)
        _riding_start(pushes, i == 0, push_refs)

        @pl.when(i == 0)
        def _():
            wacc[...] = jnp.zeros_like(wacc)

        shd[0, 0:ts, 0:ch] = dc_ref[...]
        shd[0, ts:, 0:ch] = jnp.where(i == n - 1, 0.0, next_ref[...])
        shg[0, 0:CONV_HALO, 0:ch] = jnp.where(i == 0, 0.0, prev_ref[...])
        shg[0, CONV_HALO:, 0:ch] = g_ref[...]
        _phase_copies(shd, span, ch)
        _phase_copies(shg, span, ch)

        offsets = [(j, CONV_KERNEL - 1 - j) for j in range(CONV_KERNEL)]

        def dg_block(rb, carry):
            r0 = pl.multiple_of(rb * CONV_ROWS, CONV_ROWS)
            for c in range(ch // CONV_COLS):
                cols = slice(c * CONV_COLS, (c + 1) * CONV_COLS)
                dg_ref[pl.ds(r0, CONV_ROWS), cols] = _conv_taps(shd, w_ref, offsets, r0, cols)
            return carry

        lax.fori_loop(0, ts // CONV_ROWS, dg_block, 0)

        tiles = ts // SUBLANES
        for c in range(ch // LANES):
            cols = slice(c * LANES, (c + 1) * LANES)
            cur = [shd[0, k * SUBLANES:(k + 1) * SUBLANES, cols] for k in range(tiles)]
            for j in range(CONV_KERNEL):
                q, o = divmod(lead + j, SUBLANES)
                parts = [None] * 4
                for k in range(tiles):
                    term = cur[k] * shg[o, (k + q) * SUBLANES:(k + q + 1) * SUBLANES, cols]
                    parts[k % 4] = term if parts[k % 4] is None else parts[k % 4] + term
                wacc[j, :, cols] += (parts[0] + parts[1]) + (parts[2] + parts[3])

        @pl.when(i == n - 1)
        def _():
            dw_ref[...] = jnp.zeros_like(dw_ref)
            for j in range(CONV_KERNEL):
                dw_ref[j:j + 1, :] = jnp.sum(wacc[j], axis=0, keepdims=True)

        _riding_wait(pushes, i == n - 1, push_refs)

    out = pl.pallas_call(
        body, name="conv_bwd", grid=(n,),
        in_specs=[_rows(ts, ch),
                  pl.BlockSpec((CONV_HALO, ch), lambda i: (jnp.minimum((i + 1) * per, s // CONV_HALO - 1), 0)),
                  pl.BlockSpec((CONV_HALO, ch), lambda i: (jnp.maximum(i * per - 1, 0), 0)),
                  _rows(ts, ch), _full(w.shape)] + [ANY] * np_,
        out_specs=[_rows(ts, ch), _full((CONV_HALO, ch))] + [ANY] * np_,
        out_shape=[SDS((s, ch), F32), SDS((CONV_HALO, ch), F32)] + [p.out for p in pushes],
        scratch_shapes=_push_scratch(np_) + [_conv_scratch(ts, ch), _conv_scratch(ts, ch),
                                             pltpu.VMEM((CONV_HALO, SUBLANES, ch), F32)],
        compiler_params=_seq_params(56),
    )(dc, dc, g, g, w, *[p.array for p in pushes])
    return out[0], out[1], out[2:]


def _conv_act(c, z1, cb, lg, lb):
    cc = c + cb
    mu = jnp.mean(cc, axis=-1, keepdims=True)
    dev = cc - mu
    var = jnp.mean(dev * dev, axis=-1, keepdims=True)
    cn = dev * lax.rsqrt(var + LN_EPS) * lg + lb
    return _silu(cn) * _silu(z1)


def _odd_out(c, proj1, x1, tgt, cb, lg, lb, w_out, fg):
    s, d = c.shape
    ts = 256

    def body(c_ref, z_ref, x1_ref, t_ref, cb_ref, lg_ref, lb_ref, w_ref, fg_ref,
             dc_ref, dz_ref, dx2_ref, y1_ref, loss_ref, dcb_ref, dlg_ref, dlb_ref, dfg_ref):
        @pl.when(pl.program_id(0) == 0)
        def _():
            for ref in (loss_ref, dcb_ref, dlg_ref, dlb_ref, dfg_ref):
                ref[...] = jnp.zeros_like(ref)

        y1, act_vjp = jax.vjp(_conv_act, c_ref[...], z_ref[...], cb_ref[...], lg_ref[...], lb_ref[...])
        y1b = y1.astype(MM)
        y1_ref[...] = y1b
        x2 = x1_ref[...] + jnp.dot(y1b, w_ref[...], preferred_element_type=F32)
        tgt_tile = t_ref[...]

        def head(x2, fg):
            err = jnp.square(_rms(x2, fg) - tgt_tile)
            return 0.5 * jnp.sum(jnp.mean(err, axis=-1))

        loss, (dx2, dfg) = jax.value_and_grad(head, argnums=(0, 1))(x2, fg_ref[...])
        loss_ref[...] += loss
        dfg_ref[...] += dfg
        dx2_ref[...] = dx2
        dy1 = _dot_nt(dx2, w_ref[...])
        dc, dz, dcb, dlg, dlb = act_vjp(dy1)
        dc_ref[...] = dc
        dz_ref[...] = dz
        dcb_ref[...] += dcb
        dlg_ref[...] += dlg
        dlb_ref[...] += dlb

    vec = SDS((1, d), F32)
    return pl.pallas_call(
        body, name="odd_out", grid=(s // ts,),
        in_specs=[_rows(ts, d), _rows(ts, d, col=2), _rows(ts, d), _rows(ts, d), _full((1, d)), _full((1, d)),
                  _full((1, d)), _full((d, d)), _full((1, d))],
        out_specs=[_rows(ts, d), _rows(ts, d), _rows(ts, d), _rows(ts, d), _full((SUBLANES, LANES)),
                   _full((1, d)), _full((1, d)), _full((1, d)), _full((1, d))],
        out_shape=[SDS((s, d), F32), SDS((s, d), F32), SDS((s, d), F32), SDS((s, d), MM),
                   SDS((SUBLANES, LANES), F32), vec, vec, vec, vec],
        compiler_params=_seq_params(56),
    )(c, proj1, x1, tgt, cb, lg, lb, w_out, fg)


def _odd_in_bwd(x1, proj1, dglu, dz1, dx2, g1, w_in):
    s, d = x1.shape
    e = w_in.shape[1]
    ts = 256

    def body(x_ref, vg_ref, dglu_ref, dz_ref, dx2_ref, g_ref, w_ref, dx1_ref, dp_ref, dg_ref):
        @pl.when(pl.program_id(0) == 0)
        def _():
            dg_ref[...] = jnp.zeros_like(dg_ref)

        _, glu_vjp = jax.vjp(_glu, vg_ref[:, :d], vg_ref[:, d:])
        dval, dgt = glu_vjp(dglu_ref[...])
        dp = jnp.concatenate([dval, dgt, dz_ref[...]], axis=1).astype(MM)
        dp_ref[...] = dp
        dh = _dot_nt(dp, w_ref[...])
        _, rms_vjp = jax.vjp(_rms, x_ref[...], g_ref[...])
        dxa, dg = rms_vjp(dh)
        dx1_ref[...] = dxa + dx2_ref[...]
        dg_ref[...] += dg

    return pl.pallas_call(
        body, name="odd_in_bwd", grid=(s // ts,),
        in_specs=[_rows(ts, d), _rows(ts, 2 * d), _rows(ts, d), _rows(ts, d), _rows(ts, d), _full((1, d)),
                  _full((d, e))],
        out_specs=[_rows(ts, d), _rows(ts, e), _full((1, d))],
        out_shape=[SDS((s, d), F32), SDS((s, e), MM), SDS((1, d), F32)],
        compiler_params=_seq_params(56),
    )(x1, proj1, dglu, dz1, dx2, g1, w_in)


def _even_mix_bwd(pooled, proj, mixed, gv, dx1, pw, scale, cc_re, cc_im, d_skip, w_glu, w_out, pushes):
    s, d = dx1.shape
    half = pooled.shape[1]
    n = cc_re.shape[0] * cc_re.shape[1]
    ts = 256
    nt = s // ts
    groups = len(POOL_WINDOWS)
    np_ = len(pushes)

    def body(*refs):
        ins, outs, _, push_refs = _split_refs(refs, 13, 10, np_)
        (pooled_ref, u_ref, z_ref, mixed_ref, gv_ref, dx1_ref, pw_ref, scale_ref, ccre_ref, ccim_ref, d_ref,
         wglu_ref, wout_ref) = ins
        (dpooled_ref, du_ref, dz_ref, dxre_ref, dxim_ref, dyin_ref, dgv_ref, dpw_ref, dscale_ref, dd_ref) = outs
        i = pl.program_id(0)
        _riding_start(pushes, i == 0, push_refs)

        @pl.when(i == 0)
        def _():
            for ref in (dpw_ref, dscale_ref, dd_ref):
                ref[...] = jnp.zeros_like(ref)

        dymix = _dot_nt(dx1_ref[...], wout_ref[...])
        _, a_vjp = jax.vjp(_mix_a, mixed_ref[...], scale_ref[...], z_ref[:, :half])
        dmixed, dscale, dza = a_vjp(dymix[:, :half])
        _, b_vjp = jax.vjp(_mix_b, gv_ref[:, :half], gv_ref[:, half:], z_ref[:, half:])
        dval, dgate, dzb = b_vjp(dymix[:, half:])
        dz_ref[:, :half] = dza
        dz_ref[:, half:] = dzb
        dscale_ref[...] += dscale
        dgv = jnp.concatenate([dval, dgate], axis=1).astype(MM)
        dgv_ref[...] = dgv
        dyin = _dot_nt(dgv, wglu_ref[...])
        dd_ref[...] += jnp.sum(dyin * u_ref[...], axis=0, keepdims=True)
        du_ref[...] = d_ref[...] * dyin
        dyb = dyin.astype(MM)
        dyin_ref[...] = dyb
        sb, cb = ccre_ref.shape[1:]
        for b in range(SSM_BLOCKS):
            states, chans = slice(b * sb, (b + 1) * sb), slice(b * cb, (b + 1) * cb)
            dxre_ref[:, states] = _dot_nt(dyb[:, chans], ccre_ref[b])
            dxim_ref[:, states] = -_dot_nt(dyb[:, chans], ccim_ref[b])
        for g in range(groups):
            cols = slice(g * LANES, (g + 1) * LANES)
            dm = dmixed[:, cols].astype(MM)
            dpooled_ref[:, cols] = _dot_nt(dm, pw_ref[g])
            dpw_ref[g] += _dot_tn(pooled_ref[:, cols], dm)

        _riding_wait(pushes, i == nt - 1, push_refs)

    out = pl.pallas_call(
        body, name="even_mix_bwd", grid=(nt,),
        in_specs=[_rows(ts, half), _rows(ts, half, col=1), _rows(ts, d, col=1), _rows(ts, half), _rows(ts, d),
                  _rows(ts, d), _full(pw.shape), _full((1, half)), _full(cc_re.shape), _full(cc_im.shape),
                  _full((1, half)), _full((half, d)), _full((d, d))] + [ANY] * np_,
        out_specs=[_rows(ts, half), _rows(ts, half), _rows(ts, d), _rows(ts, n), _rows(ts, n), _rows(ts, half),
                   _rows(ts, d), _full(pw.shape), _full((1, half)), _full((1, half))] + [ANY] * np_,
        out_shape=[SDS((s, half), F32), SDS((s, half), F32), SDS((s, d), F32), SDS((s, n), F32), SDS((s, n), F32),
                   SDS((s, half), MM), SDS((s, d), MM), SDS(pw.shape, F32), SDS((1, half), F32),
                   SDS((1, half), F32)] + [p.out for p in pushes],
        scratch_shapes=_push_scratch(np_),
        compiler_params=_seq_params(56),
    )(pooled, proj, proj, mixed, gv, dx1, pw, scale, cc_re, cc_im, d_skip, w_glu, w_out,
      *[p.array for p in pushes])
    return out[:10], out[10:]


def _ssm_bu_bwd(g_re, g_im, du_skip, bb):
    s, n = g_re.shape
    nb, cb, two_nb = bb.shape
    sb = two_nb // 2
    cin = nb * cb
    ts = 512

    def body(gre_ref, gim_ref, du_ref, bb_ref, out_ref):
        for b in range(nb):
            states, chans = slice(b * sb, (b + 1) * sb), slice(b * cb, (b + 1) * cb)
            out_ref[:, chans] = (du_ref[:, chans] + _dot_nt(gre_ref[:, states], bb_ref[b, :, :sb])
                                 + _dot_nt(gim_ref[:, states], bb_ref[b, :, sb:]))

    return pl.pallas_call(
        body, name="ssm_bu_bwd", grid=(s // ts,),
        in_specs=[_rows(ts, n), _rows(ts, n), _rows(ts, cin), _full(bb.shape)],
        out_specs=_rows(ts, cin),
        out_shape=SDS((s, cin), F32),
        compiler_params=_seq_params(48),
    )(g_re, g_im, du_skip, bb)


def _even_in_bwd(x, du_pool, du_ssm, dz, dx1, g0, w_in):
    s, d = x.shape
    half = du_pool.shape[1]
    shards, _, wc = w_in.shape
    ts = 512

    def body(x_ref, dup_ref, dus_ref, dz_ref, dx1_ref, g_ref, w_ref, gx_ref, dg_ref):
        @pl.when(pl.program_id(0) == 0)
        def _():
            dg_ref[...] = jnp.zeros_like(dg_ref)

        dp = jnp.concatenate([dup_ref[...], dus_ref[...], dz_ref[...]], axis=1).astype(MM)
        dh = _dot_nt(dp[:, :wc], w_ref[0])
        for k in range(1, shards):
            dh = dh + _dot_nt(dp[:, k * wc:(k + 1) * wc], w_ref[k])
        _, rms_vjp = jax.vjp(_rms, x_ref[...], g_ref[...])
        dxa, dg = rms_vjp(dh)
        gx_ref[...] = dxa + dx1_ref[...]
        dg_ref[...] += dg

    return pl.pallas_call(
        body, name="even_in_bwd", grid=(s // ts,),
        in_specs=[_rows(ts, d), _rows(ts, half), _rows(ts, half), _rows(ts, d), _rows(ts, d), _full((1, d)),
                  _full(w_in.shape)],
        out_specs=[_rows(ts, d), _full((1, d))],
        out_shape=[SDS((s, d), F32), SDS((1, d), F32)],
        compiler_params=_seq_params(48),
    )(x, du_pool, du_ssm, dz, dx1, g0, w_in)


def _discretise(log_dt, ar, ai, br, bi):
    dt = jnp.exp(log_dt)
    mag = jnp.exp(ar * dt)
    ang = ai * dt
    abr = mag * jnp.cos(ang)
    abi = mag * jnp.sin(ang)
    den = ar * ar + ai * ai
    nr = abr - 1.0
    ni = abi
    kr = (nr * ar + ni * ai) / den
    ki = (ni * ar - nr * ai) / den
    bbr = kr[None] * br - ki[None] * bi
    bbi = kr[None] * bi + ki[None] * br
    return abr, abi, bbr, bbi


def _whole(n):
    return [pl.BlockSpec(memory_space=pltpu.VMEM)] * n


def _disc_fwd(log_dt, ar, ai, br, bi):
    def body(ld_ref, ar_ref, ai_ref, br_ref, bi_ref, abr_ref, abi_ref, bbr_ref, bbi_ref):
        out = _discretise(ld_ref[...], ar_ref[...], ai_ref[...], br_ref[...], bi_ref[...])
        for ref, val in zip((abr_ref, abi_ref, bbr_ref, bbi_ref), out):
            ref[...] = val

    return pl.pallas_call(
        body, name="ssm_discretise", in_specs=_whole(5), out_specs=_whole(4),
        out_shape=[SDS(ar.shape, F32), SDS(ar.shape, F32), SDS(br.shape, F32), SDS(br.shape, F32)],
    )(log_dt, ar, ai, br, bi)


def _disc_bwd(log_dt, ar, ai, br, bi, dabr, dabi, dbbr, dbbi):
    def body(ld_ref, ar_ref, ai_ref, br_ref, bi_ref, dabr_ref, dabi_ref, dbbr_ref, dbbi_ref,
             dld_ref, dar_ref, dai_ref, dbr_ref, dbi_ref):
        _, vjp = jax.vjp(_discretise, ld_ref[...], ar_ref[...], ai_ref[...], br_ref[...], bi_ref[...])
        grads = vjp((dabr_ref[...], dabi_ref[...], dbbr_ref[...], dbbi_ref[...]))
        for ref, val in zip((dld_ref, dar_ref, dai_ref, dbr_ref, dbi_ref), grads):
            ref[...] = val

    return pl.pallas_call(
        body, name="ssm_discretise_bwd", in_specs=_whole(9), out_specs=_whole(5),
        out_shape=[SDS(log_dt.shape, F32), SDS(ar.shape, F32), SDS(ar.shape, F32), SDS(br.shape, F32),
                   SDS(br.shape, F32)],
    )(log_dt, ar, ai, br, bi, dabr, dabi, dbbr, dbbi)


def _block_diag(t):
    g, a, b = t.shape
    per = g // SSM_BLOCKS
    t = t.reshape(SSM_BLOCKS, per, a, b)
    same = jnp.eye(per, dtype=bool)[None, :, None, :, None]
    return jnp.where(same, t[:, :, :, None, :], 0.0).reshape(SSM_BLOCKS, per * a, per * b)


def _diag_blocks(m, g):
    per = g // SSM_BLOCKS
    a, b = m.shape[1] // per, m.shape[2] // per
    d = jnp.diagonal(m.reshape(SSM_BLOCKS, per, a, per, b), axis1=1, axis2=3)
    return jnp.moveaxis(d, -1, 1).reshape(g, a, b)


def _adamw(name, parts, w, m, v):
    rows, cols = w.shape
    tr = ADAM_ROWS if rows % ADAM_ROWS == 0 else rows

    def body(p_ref, w_ref, m_ref, v_ref, g_ref, d_ref, nm_ref, nv_ref):
        g = p_ref[0].astype(F32)
        for dev in range(1, N_DEV):
            g = g + p_ref[dev].astype(F32)
        g_ref[...] = g
        nm = ADAM_B1 * m_ref[...] + (1.0 - ADAM_B1) * g
        nv = ADAM_B2 * v_ref[...] + (1.0 - ADAM_B2) * jnp.square(g)
        nm_ref[...] = nm
        nv_ref[...] = nv
        m_hat = nm / (1.0 - ADAM_B1 ** ADAM_STEP)
        v_hat = nv / (1.0 - ADAM_B2 ** ADAM_STEP)
        d_ref[...] = -ADAM_LR * (m_hat / (jnp.sqrt(v_hat) + ADAM_EPS) + ADAM_WD * w_ref[...])

    tile = _rows(tr, cols)
    out = SDS((rows, cols), F32)
    return pl.pallas_call(
        body, name=name, grid=(rows // tr,),
        in_specs=[pl.BlockSpec((N_DEV, tr, cols), lambda i: (0, i, 0)), tile, tile, tile],
        out_specs=[tile, tile, tile, tile],
        out_shape=[out, out, out, out],
        compiler_params=_seq_params(32),
    )(parts, w, m, v)


def _as_rows(flat):
    n = flat.shape[-1]
    padded = _round_up(n, PACK_TILE)
    if padded != n:
        flat = jnp.pad(flat, [(0, 0)] * (flat.ndim - 1) + [(0, padded - n)])
    return flat.reshape(flat.shape[:-1] + (padded // LANES, LANES))


def _columns_to_shards(full):
    r, c8 = full.shape
    return jnp.transpose(full.reshape(r, N_DEV, c8 // N_DEV), (1, 0, 2))


def _shards_to_columns(shards):
    n, r, c = shards.shape
    return jnp.transpose(shards, (1, 0, 2)).reshape(r, n * c)


def _pack_local(tensors):
    return jnp.concatenate([_as_rows(t.reshape(-1)) for t in tensors], axis=0)


def _unpack_local(packed, shapes):
    out, r = [], 0
    for shape in shapes:
        n = 1
        for dim in shape:
            n *= dim
        rows = _round_up(n, PACK_TILE) // LANES
        out.append(packed[r:r + rows].reshape(-1)[:n].reshape(shape))
        r += rows
    return out


def kernel(x, even_norm, even_w_in, pool_w, pool_scale, ssm_log_dt, ssm_a_re, ssm_a_im, ssm_b_re, ssm_b_im, ssm_c_re, ssm_c_im, ssm_d, ssm_w_glu, even_w_out, odd_norm, odd_w_in, conv_w, conv_b, conv_ln_g, conv_ln_b, odd_w_out, final_norm, loss_target, m_even_norm, m_even_w_in, m_pool_w, m_pool_scale, m_ssm_log_dt, m_ssm_a_re, m_ssm_a_im, m_ssm_b_re, m_ssm_b_im, m_ssm_c_re, m_ssm_c_im, m_ssm_d, m_ssm_w_glu, m_even_w_out, m_odd_norm, m_odd_w_in, m_conv_w, m_conv_b, m_conv_ln_g, m_conv_ln_b, m_odd_w_out, m_final_norm, v_even_norm, v_even_w_in, v_pool_w, v_pool_scale, v_ssm_log_dt, v_ssm_a_re, v_ssm_a_im, v_ssm_b_re, v_ssm_b_im, v_ssm_c_re, v_ssm_c_im, v_ssm_d, v_ssm_w_glu, v_even_w_out, v_odd_norm, v_odd_w_in, v_conv_w, v_conv_b, v_conv_ln_g, v_conv_ln_b, v_odd_w_out, v_final_norm):
    given = dict(locals())
    xs = x[0]
    tgt = loss_target[0]
    d_model = xs.shape[1]

    def local(prefix, name):
        t = given[prefix + name]
        return t if name == 'final_norm' else t[0]

    def small_block(prefix):
        parts = [jnp.pad(local(prefix, 'conv_w'), ((0, SMALL_AT['odd_norm'] - CONV_KERNEL), (0, 0)))]
        for n in SMALL_VECTORS:
            parts.append(jnp.pad(local(prefix, n).reshape(1, LANES), ((0, SUBLANES - 1), (0, 0))))
        return jnp.concatenate(parts, axis=0)

    def as_tile(row):
        return jnp.pad(row, ((0, SUBLANES - row.shape[0]), (0, 0)))

    (w_in_e,) = _push_alone("gather_first", [_Push(local('', 'even_w_in').astype(MM), False)])
    later = [_Push(local('', 'ssm_w_glu').astype(MM), False), _Push(local('', 'even_w_out').astype(MM), False),
             _Push(local('', 'odd_w_in').astype(MM), False), _Push(local('', 'odd_w_out').astype(MM), False),
             _Push(small_block(''), False)]

    a_re, a_im = local('', 'ssm_a_re'), local('', 'ssm_a_im')
    groups, state = a_re.shape
    log_dt = local('', 'ssm_log_dt').reshape(groups, 1)
    b_re_t = jnp.transpose(local('', 'ssm_b_re'), (2, 0, 1))
    b_im_t = jnp.transpose(local('', 'ssm_b_im'), (2, 0, 1))
    abr, abi, bbr, bbi = _disc_fwd(log_dt, a_re, a_im, b_re_t, b_im_t)
    a_re_row = abr.reshape(1, groups * state)
    a_im_row = abi.reshape(1, groups * state)
    bb = jnp.concatenate([_block_diag(jnp.transpose(bbr, (1, 0, 2))), _block_diag(jnp.transpose(bbi, (1, 0, 2)))],
                         axis=2).astype(MM)
    cc_re = _block_diag(jnp.transpose(local('', 'ssm_c_re'), (0, 2, 1))).astype(MM)
    cc_im = _block_diag(jnp.transpose(local('', 'ssm_c_im'), (0, 2, 1))).astype(MM)
    pw = local('', 'pool_w').astype(MM)
    g0 = local('', 'even_norm').reshape(1, d_model)
    fg = local('', 'final_norm').reshape(1, d_model)
    scale = local('', 'pool_scale').reshape(1, -1)
    d_skip = local('', 'ssm_d').reshape(1, -1)

    proj0, h0, (wg3, wo_e3) = _even_in(xs, g0, w_in_e, later[:2])
    pooled = _pool_fwd(proj0)
    bu_re, bu_im, (wo_o3, small8) = _ssm_bu(proj0, bb, later[3:])
    x_re, x_im, (wi_o3,) = _scan_fwd(a_re_row, a_im_row, bu_re, bu_im, later[2:3])
    w_glu = _shards_to_columns(wg3)
    w_out_e = wo_e3.reshape(d_model, d_model)
    (x1, ymix, mixed, yin, gv), _ = _even_mix(
        pooled, proj0, x_re, x_im, xs, pw, scale, cc_re, cc_im, d_skip, w_glu, w_out_e, [])
    w_in_o = _shards_to_columns(wi_o3)
    w_out_o = wo_o3.reshape(d_model, d_model)
    small = _shards_to_columns(small8)
    cw = small[:CONV_KERNEL]
    g1, cb, lg, lb = (small[SMALL_AT[n]:SMALL_AT[n] + 1] for n in SMALL_VECTORS)
    proj1, glu, h1 = _odd_in(x1, g1, w_in_o)
    conv = _conv_fwd(glu, cw)

    half = pooled.shape[1]
    n_state = groups * state
    rows_per = d_model // N_DEV
    dc, dz1, dx2, y1, loss_tile, dcb, dlg, dlb, dfg = _odd_out(conv, proj1, x1, tgt, cb, lg, lb, w_out_o, fg)
    g_odd_out = _mm_tn("dw_odd_out", y1, dx2, d_model, d_model, WIRE, ts=1024).reshape(N_DEV, rows_per, d_model)
    dglu, dcw, (r_odd_out,) = _conv_bwd(dc, glu, cw, [_Push(g_odd_out, True)])
    dx1, dproj1, dg1 = _odd_in_bwd(x1, proj1, dglu, dz1, dx2, g1, w_in_o)
    g_odd_in = _columns_to_shards(_mm_tn("dw_odd_in", h1, dproj1, d_model, 3 * d_model, WIRE))
    g_even_out = _mm_tn("dw_even_out", ymix, dx1, d_model, d_model, WIRE, ts=1024).reshape(N_DEV, rows_per, d_model)
    mix_grads, (r_odd_in,) = _even_mix_bwd(pooled, proj0, mixed, gv, dx1, pw, scale, cc_re, cc_im, d_skip, w_glu,
                                           w_out_e, [_Push(g_odd_in, True)])
    dpooled, du_skip, dz0, dx_re, dx_im, dyin, dgv, dpw, dscale, dd = mix_grads
    g_glu = _columns_to_shards(_mm_tn("dw_glu", yin, dgv, half, d_model, WIRE, ts=1024))
    sblk, cblk = n_state // SSM_BLOCKS, half // SSM_BLOCKS
    dcc_re = _mm_tn_blocks("dw_c_re", x_re, dyin, sblk, cblk, SSM_BLOCKS, ts=2048)
    dcc_im = _mm_tn_blocks("dw_c_im", x_im, dyin, sblk, cblk, SSM_BLOCKS, sign=-1.0, ts=2048)
    dc_re = jnp.transpose(_diag_blocks(dcc_re, groups), (0, 2, 1))
    dc_im = jnp.transpose(_diag_blocks(dcc_im, groups), (0, 2, 1))
    g_small = _columns_to_shards(jnp.concatenate(
        [dcw, as_tile(dg1), as_tile(dcb), as_tile(dlg), as_tile(dlb)], axis=0))
    early = {'pool_w': dpw, 'pool_scale': dscale, 'ssm_c_re': dc_re, 'ssm_c_im': dc_im, 'ssm_d': dd,
             'final_norm': dfg}
    g_early = _pack_local([early[n] for n in EARLY_REPLICATED] + [loss_tile])
    g_re, g_im, dabr, dabi, (r_even_out, r_glu, r_small, r_early) = _scan_bwd(
        a_re_row, a_im_row, dx_re, dx_im, x_re, x_im,
        [_Push(g_even_out, True), _Push(g_glu, True), _Push(g_small, True), _Push(g_early, False)])
    du_ssm = _ssm_bu_bwd(g_re, g_im, du_skip, bb)
    du_pool = _pool_bwd(dpooled)
    grad_x, dg0 = _even_in_bwd(xs, du_pool, du_ssm, dz0, dx1, g0, w_in_e)
    g_even_in = _columns_to_shards(jnp.concatenate(
        [_mm_tn("dw_even_in_pool", h0, du_pool, d_model, half, WIRE, ts=1024),
         _mm_tn("dw_even_in_ssm", h0, du_ssm, d_model, half, WIRE, ts=1024),
         _mm_tn("dw_even_in_gate", h0, dz0, d_model, d_model, WIRE, ts=1024)], axis=1))
    g_mid = _pack_local([dg0])
    dbb_re, dbb_im, (r_even_in, r_mid) = _dw_bbar(proj0, g_re, g_im, cblk, sblk,
                                                  [_Push(g_even_in, True), _Push(g_mid, False)])
    dbbr = jnp.transpose(_diag_blocks(dbb_re, groups), (2, 0, 1))
    dbbi = jnp.transpose(_diag_blocks(dbb_im, groups), (2, 0, 1))
    dld, dar, dai, dbr, dbi = _disc_bwd(log_dt, a_re, a_im, b_re_t, b_im_t,
                                        dabr.reshape(groups, state), dabi.reshape(groups, state), dbbr, dbbi)
    late = {'ssm_log_dt': dld, 'ssm_a_re': dar, 'ssm_a_im': dai,
            'ssm_b_re': jnp.transpose(dbr, (1, 2, 0)), 'ssm_b_im': jnp.transpose(dbi, (1, 2, 0))}
    g_late = _pack_local([late[n] for n in LATE_REPLICATED])
    (r_late,) = _push_alone("exchange_tail", [_Push(g_late, False)])

    results = {}
    for n, parts in (('even_w_in', r_even_in), ('ssm_w_glu', r_glu), ('even_w_out', r_even_out),
                     ('odd_w_in', r_odd_in), ('odd_w_out', r_odd_out)):
        results[n] = _adamw("adamw_" + n, parts, local('', n), local('m_', n), local('v_', n))
    small_out = _adamw("adamw_small", r_small, small_block(''), small_block('m_'), small_block('v_'))
    results['conv_w'] = [o[:CONV_KERNEL] for o in small_out]
    for n in SMALL_VECTORS:
        results[n] = [o[SMALL_AT[n]] for o in small_out]
    loss_rows = [jnp.zeros_like(loss_tile)]
    for names, parts, extra in ((EARLY_REPLICATED, r_early, loss_rows), (MID_REPLICATED, r_mid, []),
                               (LATE_REPLICATED, r_late, [])):
        packed = [_pack_local([local(p, n) for n in names] + extra) for p in ('', 'm_', 'v_')]
        out = _adamw("adamw_" + names[0], parts, *packed)
        unpacked = [_unpack_local(o, [given[n].shape for n in names]) for o in out]
        for k, n in enumerate(names):
            results[n] = [u[k] for u in unpacked]
        if extra:
            loss = out[0][-SUBLANES, 0]

    outs = [loss, grad_x[None]]
    for kind in range(4):
        outs.extend(results[n][kind].reshape(given[n].shape) for n in WEIGHTS)
    return tuple(outs)
```

```python
import functools

import jax
import jax.numpy as jnp
from jax import lax
from jax.experimental import pallas as pl
from jax.experimental.pallas import tpu as pltpu

F32 = jnp.float32
MM = jnp.bfloat16
WIRE = jnp.bfloat16
SDS = jax.ShapeDtypeStruct

RMS_EPS = 1e-6
LN_EPS = 1e-5
POOL_WINDOWS = (2, 4, 8, 16)
POOL_HALO = 16
CONV_KERNEL = 31
CONV_HALO = 32
N_DEV = 8
LANES = 128
SUBLANES = 8
PACK_TILE = SUBLANES * LANES
ADAM_ROWS = 128
MIB = 1024 * 1024

ADAM_LR = 0.001
ADAM_B1 = 0.9
ADAM_B2 = 0.999
ADAM_EPS = 1e-08
ADAM_WD = 0.01
ADAM_STEP = 10

WEIGHTS = ['even_norm', 'even_w_in', 'pool_w', 'pool_scale', 'ssm_log_dt', 'ssm_a_re', 'ssm_a_im', 'ssm_b_re',
           'ssm_b_im', 'ssm_c_re', 'ssm_c_im', 'ssm_d', 'ssm_w_glu', 'even_w_out', 'odd_norm', 'odd_w_in', 'conv_w',
           'conv_b', 'conv_ln_g', 'conv_ln_b', 'odd_w_out', 'final_norm']
SMALL_VECTORS = ('odd_norm', 'conv_b', 'conv_ln_g', 'conv_ln_b')
SMALL_AT = {'odd_norm': 32, 'conv_b': 40, 'conv_ln_g': 48, 'conv_ln_b': 56}
EARLY_REPLICATED = ['pool_w', 'pool_scale', 'ssm_c_re', 'ssm_c_im', 'ssm_d', 'final_norm']
MID_REPLICATED = ['even_norm']
LATE_REPLICATED = ['ssm_log_dt', 'ssm_a_re', 'ssm_a_im', 'ssm_b_re', 'ssm_b_im']


def _round_up(n, m):
    return (n + m - 1) // m * m


def _sigmoid(x):
    return jax.nn.sigmoid(x)


def _silu(x):
    return x * jax.nn.sigmoid(x)


def _rms(x, g):
    return x * lax.rsqrt(jnp.mean(x * x, axis=-1, keepdims=True) + RMS_EPS) * g


def _dot(a, b):
    return jnp.dot(a.astype(MM), b.astype(MM), preferred_element_type=F32)


def _dot_nt(a, b):
    return lax.dot_general(a.astype(MM), b.astype(MM), (((1,), (1,)), ((), ())), preferred_element_type=F32)


def _dot_tn(a, b):
    return lax.dot_general(a.astype(MM), b.astype(MM), (((0,), (0,)), ((), ())), preferred_element_type=F32)


def _rows(ts, width, col=0):
    return pl.BlockSpec((ts, width), lambda i: (i, col))


def _rows_rev(ts, width, n, col=0):
    return pl.BlockSpec((ts, width), lambda i: (n - 1 - i, col))


def _full(shape):
    zeros = (0,) * len(shape)
    return pl.BlockSpec(shape, lambda i: zeros)


def _seq_params(vmem_mib=48, dims=1):
    return pltpu.CompilerParams(dimension_semantics=("arbitrary",) * dims, vmem_limit_bytes=vmem_mib * MIB)


def _mesh_position():
    x, y, c = lax.axis_index("x"), lax.axis_index("y"), lax.axis_index("c")
    return x, y, c


def _peer(pos, relation):
    x, y, c = pos
    px = 1 - x if relation & 4 else x
    py = 1 - y if relation & 2 else y
    pc = 1 - c if relation & 1 else c
    return px, py, pc


ANY = pl.BlockSpec(memory_space=pl.ANY)


class _Push:
    def __init__(self, array, scatter):
        self.array = array
        self.scatter = scatter
        self.out = SDS(array.shape if scatter else (N_DEV,) + array.shape, array.dtype)


def _push_scratch(n):
    if n == 0:
        return []
    return [pltpu.SemaphoreType.DMA((n * (N_DEV - 1),)), pltpu.SemaphoreType.DMA((n * (N_DEV - 1),)),
            pltpu.SemaphoreType.DMA((n,))]


def _push_copies(pushes, srcs, dsts, send_sems, recv_sems, local_sems):
    pos = _mesh_position()
    me = 4 * pos[0] + 2 * pos[1] + pos[2]
    copies = []
    for a, (push, src, dst) in enumerate(zip(pushes, srcs, dsts)):
        copies.append(pltpu.make_async_copy(src.at[me] if push.scatter else src, dst.at[me], local_sems.at[a]))
        for relation in range(1, N_DEV):
            peer = _peer(pos, relation)
            peer_id = 4 * peer[0] + 2 * peer[1] + peer[2]
            k = a * (N_DEV - 1) + relation - 1
            copies.append(pltpu.make_async_remote_copy(
                src_ref=src.at[peer_id] if push.scatter else src, dst_ref=dst.at[me],
                send_sem=send_sems.at[k], recv_sem=recv_sems.at[k], device_id=peer,
                device_id_type=pl.DeviceIdType.MESH))
    return copies


def _push_alone(name, pushes):
    n = len(pushes)

    def body(*refs):
        copies = _push_copies(pushes, refs[:n], refs[n:2 * n], *refs[2 * n:])
        for cp in copies:
            cp.start()
        for cp in copies:
            cp.wait()

    return pl.pallas_call(
        body, name=name, out_shape=[p.out for p in pushes], in_specs=[ANY] * n, out_specs=[ANY] * n,
        scratch_shapes=_push_scratch(n),
    )(*[p.array for p in pushes])


def _riding_start(pushes, first, refs):
    n = len(pushes)
    if n == 0:
        return

    @pl.when(first)
    def _():
        for cp in _push_copies(pushes, refs[:n], refs[n:2 * n], *refs[2 * n:]):
            cp.start()


def _riding_wait(pushes, last, refs):
    n = len(pushes)
    if n == 0:
        return

    @pl.when(last)
    def _():
        for cp in _push_copies(pushes, refs[:n], refs[n:2 * n], *refs[2 * n:]):
            cp.wait()


def _split_refs(refs, n_in, n_out, n_push):
    ins = refs[:n_in]
    srcs = refs[n_in:n_in + n_push]
    o0 = n_in + n_push
    outs = refs[o0:o0 + n_out]
    dsts = refs[o0 + n_out:o0 + n_out + n_push]
    s0 = o0 + n_out + n_push
    n_sem = 3 if n_push else 0
    sems = refs[s0:s0 + n_sem]
    scratch = refs[s0 + n_sem:]
    return ins, outs, scratch, tuple(srcs) + tuple(dsts) + tuple(sems)


def _mm_tn(name, a, b, m, n, out_dtype, a_col=0, b_col=0, ts=512):
    s = a.shape[0]
    nk = s // ts

    def body(a_ref, b_ref, o_ref, acc):
        k = pl.program_id(0)

        @pl.when(k == 0)
        def _():
            acc[...] = jnp.zeros_like(acc)

        acc[...] += _dot_tn(a_ref[...], b_ref[...])

        @pl.when(k == nk - 1)
        def _():
            o_ref[...] = acc[...].astype(out_dtype)

    return pl.pallas_call(
        body, name=name, grid=(nk,),
        in_specs=[pl.BlockSpec((ts, m), lambda k: (k, a_col)), pl.BlockSpec((ts, n), lambda k: (k, b_col))],
        out_specs=pl.BlockSpec((m, n), lambda k: (0, 0)),
        out_shape=SDS((m, n), out_dtype),
        scratch_shapes=[pltpu.VMEM((m, n), F32)],
        compiler_params=_seq_params(56),
    )(a, b)


def _mm_tn_blocks(name, a, b, m, n, nb, a_col=0, a_step=1, b_col=0, sign=1.0, ts=512):
    s = a.shape[0]

    def body(a_ref, b_ref, o_ref):
        @pl.when(pl.program_id(1) == 0)
        def _():
            o_ref[...] = jnp.zeros_like(o_ref)

        o_ref[...] += sign * _dot_tn(a_ref[...], b_ref[...])

    return pl.pallas_call(
        body, name=name, grid=(nb, s // ts),
        in_specs=[pl.BlockSpec((ts, m), lambda j, k: (k, a_col + a_step * j)),
                  pl.BlockSpec((ts, n), lambda j, k: (k, b_col + j))],
        out_specs=pl.BlockSpec((None, m, n), lambda j, k: (j, 0, 0)),
        out_shape=SDS((nb, m, n), F32),
        compiler_params=_seq_params(48, dims=2),
    )(a, b)


def _dw_bbar(proj, g_re, g_im, cb, sb, pushes, ts=2048):
    s = proj.shape[0]
    nb, nk = SSM_BLOCKS, s // ts
    np_ = len(pushes)

    def body(*refs):
        ins, outs, _, push_refs = _split_refs(refs, 3, 2, np_)
        u_ref, gre_ref, gim_ref = ins
        ore_ref, oim_ref = outs
        j, k = pl.program_id(0), pl.program_id(1)
        _riding_start(pushes, jnp.logical_and(j == 0, k == 0), push_refs)

        @pl.when(k == 0)
        def _():
            ore_ref[...] = jnp.zeros_like(ore_ref)
            oim_ref[...] = jnp.zeros_like(oim_ref)

        u = u_ref[...].astype(MM)
        ore_ref[...] += _dot_tn(gre_ref[...], u)
        oim_ref[...] += _dot_tn(gim_ref[...], u)
        _riding_wait(pushes, jnp.logical_and(j == nb - 1, k == nk - 1), push_refs)

    wide = pl.BlockSpec((ts, sb), lambda j, k: (k, j))
    out = pl.BlockSpec((None, sb, cb), lambda j, k: (j, 0, 0))
    res = pl.pallas_call(
        body, name="dw_bbar", grid=(nb, nk),
        in_specs=[pl.BlockSpec((ts, cb), lambda j, k: (k, nb + j)), wide, wide] + [ANY] * np_,
        out_specs=[out, out] + [ANY] * np_,
        out_shape=[SDS((nb, sb, cb), F32), SDS((nb, sb, cb), F32)] + [p.out for p in pushes],
        scratch_shapes=_push_scratch(np_),
        compiler_params=_seq_params(48, dims=2),
    )(proj, g_re, g_im, *[p.array for p in pushes])
    return res[0], res[1], res[2:]


def _even_in(x, g0, w_in, pushes):
    s, d = x.shape
    shards, _, wc = w_in.shape
    e = shards * wc
    ts = 512
    nt = s // ts
    np_ = len(pushes)

    def body(*refs):
        (x_ref, g_ref, w_ref), (proj_ref, h_ref), _, push_refs = _split_refs(refs, 3, 2, np_)
        i = pl.program_id(0)
        _riding_start(pushes, i == 0, push_refs)
        hb = _rms(x_ref[...], g_ref[...]).astype(MM)
        h_ref[...] = hb
        for k in range(shards):
            proj_ref[:, k * wc:(k + 1) * wc] = jnp.dot(hb, w_ref[k], preferred_element_type=F32)
        _riding_wait(pushes, i == nt - 1, push_refs)

    out = pl.pallas_call(
        body, name="even_in", grid=(nt,),
        in_specs=[_rows(ts, d), _full((1, d)), _full(w_in.shape)] + [ANY] * np_,
        out_specs=[_rows(ts, e), _rows(ts, d)] + [ANY] * np_,
        out_shape=[SDS((s, e), F32), SDS((s, d), MM)] + [p.out for p in pushes],
        scratch_shapes=_push_scratch(np_),
        compiler_params=_seq_params(48),
    )(x, g0, w_in, *[p.array for p in pushes])
    return out[0], out[1], out[2:]


def _pool_counts(t0, ts, w):
    pos = (t0 + lax.broadcasted_iota(jnp.int32, (ts, LANES), 0) + 1).astype(F32)
    return jnp.minimum(pos, float(w))


def _pool_fwd(proj):
    s = proj.shape[0]
    width = LANES * len(POOL_WINDOWS)
    ts = 512
    per = ts // POOL_HALO

    def body(prev_ref, u_ref, out_ref, ext):
        i = pl.program_id(0)
        ext[0:POOL_HALO, 0:width] = jnp.where(i == 0, 0.0, prev_ref[...])
        ext[POOL_HALO:, 0:width] = u_ref[...]
        for g, w in enumerate(POOL_WINDOWS):
            cols = slice(g * LANES, (g + 1) * LANES)
            tok = ext[pl.ds(POOL_HALO, ts), cols]
            acc = tok
            for k in range(1, w):
                acc = acc + ext[pl.ds(POOL_HALO - k, ts), cols]
            out_ref[:, cols] = acc / _pool_counts(i * ts, ts, w) - tok

    return pl.pallas_call(
        body, name="pool_fwd", grid=(s // ts,),
        in_specs=[pl.BlockSpec((POOL_HALO, width), lambda i: (jnp.maximum(i * per - 1, 0), 0)),
                  _rows(ts, width)],
        out_specs=_rows(ts, width),
        out_shape=SDS((s, width), F32),
        scratch_shapes=[pltpu.VMEM((ts + POOL_HALO, width + LANES), F32)],
        compiler_params=_seq_params(32),
    )(proj, proj)


def _pool_bwd(dp):
    s, width = dp.shape
    ts = 512
    per = ts // POOL_HALO
    n = s // ts

    def body(dp_ref, next_ref, out_ref, ext):
        i = pl.program_id(0)
        nxt = jnp.where(i == n - 1, 0.0, next_ref[...])
        for g, w in enumerate(POOL_WINDOWS):
            cols = slice(g * LANES, (g + 1) * LANES)
            cur = dp_ref[:, cols]
            ext[0:ts, cols] = cur / _pool_counts(i * ts, ts, w)
            ext[ts:, cols] = nxt[:, cols] / _pool_counts((i + 1) * ts, POOL_HALO, w)
            acc = -cur
            for k in range(w):
                acc = acc + ext[pl.ds(k, ts), cols]
            out_ref[:, cols] = acc

    return pl.pallas_call(
        body, name="pool_bwd", grid=(n,),
        in_specs=[_rows(ts, width),
                  pl.BlockSpec((POOL_HALO, width), lambda i: (jnp.minimum((i + 1) * per, s // POOL_HALO - 1), 0))],
        out_specs=_rows(ts, width),
        out_shape=SDS((s, width), F32),
        scratch_shapes=[pltpu.VMEM((ts + POOL_HALO, width + LANES), F32)],
        compiler_params=_seq_params(32),
    )(dp, dp)


SSM_BLOCKS = 4


def _ssm_bu(proj, bb, pushes):
    s = proj.shape[0]
    nb, cb, two_nb = bb.shape
    sb = two_nb // 2
    cin, n = nb * cb, nb * sb
    ts = 512
    nt = s // ts
    np_ = len(pushes)

    def body(*refs):
        (u_ref, bb_ref), (re_ref, im_ref), _, push_refs = _split_refs(refs, 2, 2, np_)
        i = pl.program_id(0)
        _riding_start(pushes, i == 0, push_refs)
        for b in range(nb):
            bu = _dot(u_ref[:, b * cb:(b + 1) * cb], bb_ref[b])
            re_ref[:, b * sb:(b + 1) * sb] = bu[:, :sb]
            im_ref[:, b * sb:(b + 1) * sb] = bu[:, sb:]
        _riding_wait(pushes, i == nt - 1, push_refs)

    out = pl.pallas_call(
        body, name="ssm_bu", grid=(nt,),
        in_specs=[_rows(ts, cin, col=1), _full(bb.shape)] + [ANY] * np_,
        out_specs=[_rows(ts, n), _rows(ts, n)] + [ANY] * np_,
        out_shape=[SDS((s, n), F32), SDS((s, n), F32)] + [p.out for p in pushes],
        scratch_shapes=_push_scratch(np_),
        compiler_params=_seq_params(48),
    )(proj, bb, *[p.array for p in pushes])
    return out[0], out[1], out[2:]


SCAN_LANES = 256


def _cmul(a, b):
    return a[0] * b[0] - a[1] * b[1], a[0] * b[1] + a[1] * b[0]


SCAN_STEPS = (1, 2, 4)


def _fill_scan_tables(tab, ar_row, ai_row, reverse):
    shape = (SUBLANES, ar_row.shape[1])
    a1 = (jnp.broadcast_to(ar_row, shape), jnp.broadcast_to(ai_row, shape))
    a2 = _cmul(a1, a1)
    a4 = _cmul(a2, a2)
    a8 = _cmul(a4, a4)
    row = lax.broadcasted_iota(jnp.int32, shape, 0)
    for idx, (k, q) in enumerate(zip(SCAN_STEPS, (a1, a2, a4))):
        keep = (row < SUBLANES - k) if reverse else (row >= k)
        tab[2 * idx] = jnp.where(keep, q[0], 0.0)
        tab[2 * idx + 1] = jnp.where(keep, q[1], 0.0)
    expo = (SUBLANES - row) if reverse else (row + 1)
    pr, pi = jnp.ones(shape, F32), jnp.zeros(shape, F32)
    for bit, q in enumerate((a1, a2, a4, a8)):
        take = ((expo >> bit) & 1) == 1
        nr, ni = _cmul((pr, pi), q)
        pr, pi = jnp.where(take, nr, pr), jnp.where(take, ni, pi)
    tab[2 * len(SCAN_STEPS)] = pr
    tab[2 * len(SCAN_STEPS) + 1] = pi


def _group_scan(v, tab, cols, carry, reverse):
    vr, vi = v
    for idx, k in enumerate(SCAN_STEPS):
        shift = SUBLANES - k if reverse else k
        tr, ti = _cmul((tab[2 * idx, :, cols], tab[2 * idx + 1, :, cols]),
                       (pltpu.roll(vr, shift, 0), pltpu.roll(vi, shift, 0)))
        vr, vi = vr + tr, vi + ti
    last = 2 * len(SCAN_STEPS)
    tr, ti = _cmul((tab[last, :, cols], tab[last + 1, :, cols]), carry)
    return vr + tr, vi + ti


def _scan_fwd(a_re, a_im, proj, bb, pushes):
    s = proj.shape[0]
    nb, cb, two_sb = bb.shape
    sb = two_sb // 2
    cin, n = nb * cb, nb * sb
    ts = 512
    nt = s // ts
    groups = ts // SUBLANES
    np_ = len(pushes)

    def body(*refs):
        ins, outs, scratch, push_refs = _split_refs(refs, 4, 2, np_)
        ar_ref, ai_ref, u_ref, bb_ref = ins
        xre_ref, xim_ref = outs
        tab, cre, cim, bre_ref, bim_ref = scratch
        i = pl.program_id(0)
        _riding_start(pushes, i == 0, push_refs)

        @pl.when(i == 0)
        def _():
            _fill_scan_tables(tab, ar_ref[...], ai_ref[...], False)
            cre[...] = jnp.zeros_like(cre)
            cim[...] = jnp.zeros_like(cim)

        for b in range(nb):
            bu = _dot(u_ref[:, b * cb:(b + 1) * cb], bb_ref[b])
            bre_ref[:, b * sb:(b + 1) * sb] = bu[:, :sb]
            bim_ref[:, b * sb:(b + 1) * sb] = bu[:, sb:]

        def group(gi, carry):
            r0 = pl.multiple_of(gi * SUBLANES, SUBLANES)
            for c in range(n // SCAN_LANES):
                cols = slice(c * SCAN_LANES, (c + 1) * SCAN_LANES)
                v = (bre_ref[pl.ds(r0, SUBLANES), cols], bim_ref[pl.ds(r0, SUBLANES), cols])
                vr, vi = _group_scan(v, tab, cols, (cre[:, cols], cim[:, cols]), False)
                xre_ref[pl.ds(r0, SUBLANES), cols] = vr
                xim_ref[pl.ds(r0, SUBLANES), cols] = vi
                cre[:, cols] = jnp.broadcast_to(vr[SUBLANES - 1:SUBLANES, :], vr.shape)
                cim[:, cols] = jnp.broadcast_to(vi[SUBLANES - 1:SUBLANES, :], vi.shape)
            return carry

        lax.fori_loop(0, groups, group, 0, unroll=2)
        _riding_wait(pushes, i == nt - 1, push_refs)

    out = pl.pallas_call(
        body, name="ssm_scan", grid=(nt,),
        in_specs=[_full((1, n)), _full((1, n)), _rows(ts, cin, col=1), _full(bb.shape)] + [ANY] * np_,
        out_specs=[_rows(ts, n), _rows(ts, n)] + [ANY] * np_,
        out_shape=[SDS((s, n), F32), SDS((s, n), F32)] + [p.out for p in pushes],
        scratch_shapes=_push_scratch(np_) + [pltpu.VMEM((2 * len(SCAN_STEPS) + 2, SUBLANES, n), F32),
                                             pltpu.VMEM((SUBLANES, n), F32), pltpu.VMEM((SUBLANES, n), F32),
                                             pltpu.VMEM((ts, n), F32), pltpu.VMEM((ts, n), F32)],
        compiler_params=_seq_params(48),
    )(a_re, a_im, proj, bb, *[p.array for p in pushes])
    return out[0], out[1], out[2:]


def _scan_bwd(a_re, a_im, dx_re, dx_im, x_re, x_im, pushes):
    s, n = dx_re.shape
    ts = 256
    nt = s // ts
    groups = ts // SUBLANES
    np_ = len(pushes)

    def body(*refs):
        ins, outs, scratch, push_refs = _split_refs(refs, 6, 4, np_)
        ar_ref, ai_ref, dre_ref, dim_ref, xre_ref, xim_ref = ins
        gre_ref, gim_ref, dar_ref, dai_ref = outs
        tab, cre, cim, accr, acci = scratch
        i = pl.program_id(0)
        _riding_start(pushes, i == 0, push_refs)

        @pl.when(i == 0)
        def _():
            _fill_scan_tables(tab, ar_ref[...], -ai_ref[...], True)
            for ref in (cre, cim, accr, acci):
                ref[...] = jnp.zeros_like(ref)

        inner = lax.broadcasted_iota(jnp.int32, (SUBLANES, SCAN_LANES), 0) < SUBLANES - 1

        def group(k, carry):
            r0 = pl.multiple_of((groups - 1 - k) * SUBLANES, SUBLANES)
            for c in range(n // SCAN_LANES):
                cols = slice(c * SCAN_LANES, (c + 1) * SCAN_LANES)
                after = (cre[:, cols], cim[:, cols])
                v = (dre_ref[pl.ds(r0, SUBLANES), cols], dim_ref[pl.ds(r0, SUBLANES), cols])
                vr, vi = _group_scan(v, tab, cols, after, True)
                gre_ref[pl.ds(r0, SUBLANES), cols] = vr
                gim_ref[pl.ds(r0, SUBLANES), cols] = vi
                nr = jnp.where(inner, pltpu.roll(vr, SUBLANES - 1, 0), after[0])
                ni = jnp.where(inner, pltpu.roll(vi, SUBLANES - 1, 0), after[1])
                xr, xi = xre_ref[pl.ds(r0, SUBLANES), cols], xim_ref[pl.ds(r0, SUBLANES), cols]
                accr[:, cols] += nr * xr + ni * xi
                acci[:, cols] += ni * xr - nr * xi
                cre[:, cols] = jnp.broadcast_to(vr[0:1, :], vr.shape)
                cim[:, cols] = jnp.broadcast_to(vi[0:1, :], vi.shape)
            return carry

        lax.fori_loop(0, groups, group, 0, unroll=2)

        @pl.when(i == nt - 1)
        def _():
            dar_ref[...] = jnp.sum(accr[...], axis=0, keepdims=True)
            dai_ref[...] = jnp.sum(acci[...], axis=0, keepdims=True)

        _riding_wait(pushes, i == nt - 1, push_refs)

    small = pltpu.VMEM((SUBLANES, n), F32)
    out = pl.pallas_call(
        body, name="ssm_scan_bwd", grid=(nt,),
        in_specs=[_full((1, n)), _full((1, n))] + [_rows_rev(ts, n, nt)] * 4 + [ANY] * np_,
        out_specs=[_rows_rev(ts, n, nt), _rows_rev(ts, n, nt), _full((1, n)), _full((1, n))] + [ANY] * np_,
        out_shape=[SDS((s, n), F32), SDS((s, n), F32), SDS((1, n), F32), SDS((1, n), F32)]
        + [p.out for p in pushes],
        scratch_shapes=_push_scratch(np_) + [pltpu.VMEM((2 * len(SCAN_STEPS) + 2, SUBLANES, n), F32),
                                             small, small, small, small],
        compiler_params=_seq_params(48),
    )(a_re, a_im, dx_re, dx_im, x_re, x_im, *[p.array for p in pushes])
    return out[0], out[1], out[2], out[3], out[4:]


def _mix_a(mixed, scale, za):
    return mixed * scale * _silu(za)


def _mix_b(val, gate, zb):
    return val * _sigmoid(gate) * _silu(zb)


def _even_mix(pooled, proj, x_re, x_im, x, pw, scale, cc_re, cc_im, d_skip, w_glu, w_out, pushes):
    s, d = x.shape
    half = pooled.shape[1]
    n = x_re.shape[1]
    ts = 256
    nt = s // ts
    np_ = len(pushes)

    def body(*refs):
        ins, outs, _, push_refs = _split_refs(refs, 13, 5, np_)
        (pooled_ref, u_ref, z_ref, xre_ref, xim_ref, x_ref, pw_ref, scale_ref, ccre_ref, ccim_ref, d_ref,
         wglu_ref, wout_ref) = ins
        x1_ref, ymix_ref, mixed_ref, yin_ref, gv_ref = outs
        i = pl.program_id(0)
        _riding_start(pushes, i == 0, push_refs)
        for g in range(len(POOL_WINDOWS)):
            cols = slice(g * LANES, (g + 1) * LANES)
            mixed_ref[:, cols] = _dot(pooled_ref[:, cols], pw_ref[g])
        sb, cb = ccre_ref.shape[1:]
        for b in range(SSM_BLOCKS):
            states, chans = slice(b * sb, (b + 1) * sb), slice(b * cb, (b + 1) * cb)
            yin_ref[:, chans] = (_dot(xre_ref[:, states], ccre_ref[b]) - _dot(xim_ref[:, states], ccim_ref[b])
                                 + d_ref[:, chans] * u_ref[:, chans])
        yin = yin_ref[...]
        gv = _dot(yin, wglu_ref[...])
        gv_ref[...] = gv
        ya = _mix_a(mixed_ref[...], scale_ref[...], z_ref[:, :half])
        yb = _mix_b(gv[:, :half], gv[:, half:], z_ref[:, half:])
        ymix = jnp.concatenate([ya, yb], axis=1).astype(MM)
        ymix_ref[...] = ymix
        x1_ref[...] = x_ref[...] + jnp.dot(ymix, wout_ref[...], preferred_element_type=F32)
        _riding_wait(pushes, i == nt - 1, push_refs)

    out = pl.pallas_call(
        body, name="even_mix", grid=(nt,),
        in_specs=[_rows(ts, half), _rows(ts, half, col=1), _rows(ts, d, col=1), _rows(ts, n), _rows(ts, n),
                  _rows(ts, d), _full(pw.shape), _full((1, half)), _full(cc_re.shape), _full(cc_im.shape),
                  _full((1, half)), _full((half, d)), _full((d, d))] + [ANY] * np_,
        out_specs=[_rows(ts, d), _rows(ts, d), _rows(ts, half), _rows(ts, half), _rows(ts, d)] + [ANY] * np_,
        out_shape=[SDS((s, d), F32), SDS((s, d), MM), SDS((s, half), F32), SDS((s, half), F32), SDS((s, d), F32)]
        + [p.out for p in pushes],
        scratch_shapes=_push_scratch(np_),
        compiler_params=_seq_params(56),
    )(pooled, proj, proj, x_re, x_im, x, pw, scale, cc_re, cc_im, d_skip, w_glu, w_out,
      *[p.array for p in pushes])
    return out[:5], out[5:]


def _glu(val, gt):
    return val * _sigmoid(gt)


def _odd_in(x1, g1, w_in):
    s, d = x1.shape
    e = w_in.shape[1]
    ts = 512

    def body(x_ref, g_ref, w_ref, proj_ref, glu_ref, h_ref):
        hb = _rms(x_ref[...], g_ref[...]).astype(MM)
        h_ref[...] = hb
        proj_ref[...] = jnp.dot(hb, w_ref[...], preferred_element_type=F32)
        glu_ref[...] = _glu(proj_ref[:, :d], proj_ref[:, d:2 * d])

    return pl.pallas_call(
        body, name="odd_in", grid=(s // ts,),
        in_specs=[_rows(ts, d), _full((1, d)), _full((d, e))],
        out_specs=[_rows(ts, e), _rows(ts, d), _rows(ts, d)],
        out_shape=[SDS((s, e), F32), SDS((s, d), F32), SDS((s, d), MM)],
        compiler_params=_seq_params(56),
    )(x1, g1, w_in)


CONV_ROWS = 32
CONV_COLS = 256


def _conv_scratch(ts, ch):
    return pltpu.VMEM((SUBLANES, ts + CONV_HALO, ch + LANES), F32)


def _phase_copies(sh, rows, ch):
    for o in range(1, SUBLANES):
        sh[o, 0:rows, 0:ch] = sh[0, pl.ds(o, rows), 0:ch]


def _conv_taps(sh, w_ref, offsets, r0, cols):
    acc = jnp.zeros((CONV_ROWS, cols.stop - cols.start), F32)
    for o in range(SUBLANES):
        taps = [(j, e // SUBLANES) for j, e in offsets if e % SUBLANES == o]
        if not taps:
            continue
        q0 = min(q for _, q in taps)
        q1 = max(q for _, q in taps)
        win = sh[o, pl.ds(r0 + SUBLANES * q0, CONV_ROWS + SUBLANES * (q1 - q0)), cols]
        for j, q in taps:
            lo = SUBLANES * (q - q0)
            acc = acc + w_ref[j:j + 1, cols] * win[lo:lo + CONV_ROWS]
    return acc


def _conv_fwd(g, w):
    s, ch = g.shape
    ts = 256
    per = ts // CONV_HALO
    lead = CONV_HALO - (CONV_KERNEL - 1)
    span = ts + CONV_HALO - SUBLANES

    def body(prev_ref, g_ref, w_ref, out_ref, sh):
        i = pl.program_id(0)
        sh[0, 0:CONV_HALO, 0:ch] = jnp.where(i == 0, 0.0, prev_ref[...])
        sh[0, CONV_HALO:, 0:ch] = g_ref[...]
        _phase_copies(sh, span, ch)

        offsets = [(j, lead + j) for j in range(CONV_KERNEL)]

        def block(rb, carry):
            r0 = pl.multiple_of(rb * CONV_ROWS, CONV_ROWS)
            for c in range(ch // CONV_COLS):
                cols = slice(c * CONV_COLS, (c + 1) * CONV_COLS)
                out_ref[pl.ds(r0, CONV_ROWS), cols] = _conv_taps(sh, w_ref, offsets, r0, cols)
            return carry

        lax.fori_loop(0, ts // CONV_ROWS, block, 0)

    return pl.pallas_call(
        body, name="conv_fwd", grid=(s // ts,),
        in_specs=[pl.BlockSpec((CONV_HALO, ch), lambda i: (jnp.maximum(i * per - 1, 0), 0)),
                  _rows(ts, ch), _full(w.shape)],
        out_specs=_rows(ts, ch),
        out_shape=SDS((s, ch), F32),
        scratch_shapes=[_conv_scratch(ts, ch)],
        compiler_params=_seq_params(40),
    )(g, g, w)


def _conv_bwd(dc, g, w, pushes):
    s, ch = dc.shape
    ts = 256
    per = ts // CONV_HALO
    n = s // ts
    lead = CONV_HALO - (CONV_KERNEL - 1)
    span = ts + CONV_HALO - SUBLANES
    np_ = len(pushes)

    def body(*refs):
        ins, outs, scratch, push_refs = _split_refs(refs, 5, 2, np_)
        dc_ref, next_ref, prev_ref, g_ref, w_ref = ins
        dg_ref, dw_ref = outs
        shd, shg, wacc = scratch
        i = pl.program_id(0)
        _riding_start(pushes, i == 0, push_refs)

        @pl.when(i == 0)
        def _():
            wacc[...] = jnp.zeros_like(wacc)

        shd[0, 0:ts, 0:ch] = dc_ref[...]
        shd[0, ts:, 0:ch] = jnp.where(i == n - 1, 0.0, next_ref[...])
        shg[0, 0:CONV_HALO, 0:ch] = jnp.where(i == 0, 0.0, prev_ref[...])
        shg[0, CONV_HALO:, 0:ch] = g_ref[...]
        _phase_copies(shd, span, ch)
        _phase_copies(shg, span, ch)

        offsets = [(j, CONV_KERNEL - 1 - j) for j in range(CONV_KERNEL)]

        def dg_block(rb, carry):
            r0 = pl.multiple_of(rb * CONV_ROWS, CONV_ROWS)
            for c in range(ch // CONV_COLS):
                cols = slice(c * CONV_COLS, (c + 1) * CONV_COLS)
                dg_ref[pl.ds(r0, CONV_ROWS), cols] = _conv_taps(shd, w_ref, offsets, r0, cols)
            return carry

        lax.fori_loop(0, ts // CONV_ROWS, dg_block, 0)

        tiles = ts // SUBLANES
        for c in range(ch // LANES):
            cols = slice(c * LANES, (c + 1) * LANES)
            cur = [shd[0, k * SUBLANES:(k + 1) * SUBLANES, cols] for k in range(tiles)]
            for j in range(CONV_KERNEL):
                q, o = divmod(lead + j, SUBLANES)
                parts = [None] * 4
                for k in range(tiles):
                    term = cur[k] * shg[o, (k + q) * SUBLANES:(k + q + 1) * SUBLANES, cols]
                    parts[k % 4] = term if parts[k % 4] is None else parts[k % 4] + term
                wacc[j, :, cols] += (parts[0] + parts[1]) + (parts[2] + parts[3])

        @pl.when(i == n - 1)
        def _():
            dw_ref[...] = jnp.zeros_like(dw_ref)
            for j in range(CONV_KERNEL):
                dw_ref[j:j + 1, :] = jnp.sum(wacc[j], axis=0, keepdims=True)

        _riding_wait(pushes, i == n - 1, push_refs)

    out = pl.pallas_call(
        body, name="conv_bwd", grid=(n,),
        in_specs=[_rows(ts, ch),
                  pl.BlockSpec((CONV_HALO, ch), lambda i: (jnp.minimum((i + 1) * per, s // CONV_HALO - 1), 0)),
                  pl.BlockSpec((CONV_HALO, ch), lambda i: (jnp.maximum(i * per - 1, 0), 0)),
                  _rows(ts, ch), _full(w.shape)] + [ANY] * np_,
        out_specs=[_rows(ts, ch), _full((CONV_HALO, ch))] + [ANY] * np_,
        out_shape=[SDS((s, ch), F32), SDS((CONV_HALO, ch), F32)] + [p.out for p in pushes],
        scratch_shapes=_push_scratch(np_) + [_conv_scratch(ts, ch), _conv_scratch(ts, ch),
                                             pltpu.VMEM((CONV_HALO, SUBLANES, ch), F32)],
        compiler_params=_seq_params(56),
    )(dc, dc, g, g, w, *[p.array for p in pushes])
    return out[0], out[1], out[2:]


def _conv_act(c, z1, cb, lg, lb):
    cc = c + cb
    mu = jnp.mean(cc, axis=-1, keepdims=True)
    dev = cc - mu
    var = jnp.mean(dev * dev, axis=-1, keepdims=True)
    cn = dev * lax.rsqrt(var + LN_EPS) * lg + lb
    return _silu(cn) * _silu(z1)


def _odd_out(c, proj1, x1, tgt, cb, lg, lb, w_out, fg):
    s, d = c.shape
    ts = 256

    def body(c_ref, z_ref, x1_ref, t_ref, cb_ref, lg_ref, lb_ref, w_ref, fg_ref,
             dc_ref, dz_ref, dx2_ref, y1_ref, loss_ref, dcb_ref, dlg_ref, dlb_ref, dfg_ref):
        @pl.when(pl.program_id(0) == 0)
        def _():
            for ref in (loss_ref, dcb_ref, dlg_ref, dlb_ref, dfg_ref):
                ref[...] = jnp.zeros_like(ref)

        y1, act_vjp = jax.vjp(_conv_act, c_ref[...], z_ref[...], cb_ref[...], lg_ref[...], lb_ref[...])
        y1b = y1.astype(MM)
        y1_ref[...] = y1b
        x2 = x1_ref[...] + jnp.dot(y1b, w_ref[...], preferred_element_type=F32)
        tgt_tile = t_ref[...]

        def head(x2, fg):
            err = jnp.square(_rms(x2, fg) - tgt_tile)
            return 0.5 * jnp.sum(jnp.mean(err, axis=-1))

        loss, (dx2, dfg) = jax.value_and_grad(head, argnums=(0, 1))(x2, fg_ref[...])
        loss_ref[...] += loss
        dfg_ref[...] += dfg
        dx2_ref[...] = dx2
        dy1 = _dot_nt(dx2, w_ref[...])
        dc, dz, dcb, dlg, dlb = act_vjp(dy1)
        dc_ref[...] = dc
        dz_ref[...] = dz
        dcb_ref[...] += dcb
        dlg_ref[...] += dlg
        dlb_ref[...] += dlb

    vec = SDS((1, d), F32)
    return pl.pallas_call(
        body, name="odd_out", grid=(s // ts,),
        in_specs=[_rows(ts, d), _rows(ts, d, col=2), _rows(ts, d), _rows(ts, d), _full((1, d)), _full((1, d)),
                  _full((1, d)), _full((d, d)), _full((1, d))],
        out_specs=[_rows(ts, d), _rows(ts, d), _rows(ts, d), _rows(ts, d), _full((SUBLANES, LANES)),
                   _full((1, d)), _full((1, d)), _full((1, d)), _full((1, d))],
        out_shape=[SDS((s, d), F32), SDS((s, d), F32), SDS((s, d), F32), SDS((s, d), MM),
                   SDS((SUBLANES, LANES), F32), vec, vec, vec, vec],
        compiler_params=_seq_params(56),
    )(c, proj1, x1, tgt, cb, lg, lb, w_out, fg)


def _odd_in_bwd(x1, proj1, dglu, dz1, dx2, g1, w_in):
    s, d = x1.shape
    e = w_in.shape[1]
    ts = 256

    def body(x_ref, vg_ref, dglu_ref, dz_ref, dx2_ref, g_ref, w_ref, dx1_ref, dp_ref, dg_ref):
        @pl.when(pl.program_id(0) == 0)
        def _():
            dg_ref[...] = jnp.zeros_like(dg_ref)

        _, glu_vjp = jax.vjp(_glu, vg_ref[:, :d], vg_ref[:, d:])
        dval, dgt = glu_vjp(dglu_ref[...])
        dp = jnp.concatenate([dval, dgt, dz_ref[...]], axis=1).astype(MM)
        dp_ref[...] = dp
        dh = _dot_nt(dp, w_ref[...])
        _, rms_vjp = jax.vjp(_rms, x_ref[...], g_ref[...])
        dxa, dg = rms_vjp(dh)
        dx1_ref[...] = dxa + dx2_ref[...]
        dg_ref[...] += dg

    return pl.pallas_call(
        body, name="odd_in_bwd", grid=(s // ts,),
        in_specs=[_rows(ts, d), _rows(ts, 2 * d), _rows(ts, d), _rows(ts, d), _rows(ts, d), _full((1, d)),
                  _full((d, e))],
        out_specs=[_rows(ts, d), _rows(ts, e), _full((1, d))],
        out_shape=[SDS((s, d), F32), SDS((s, e), MM), SDS((1, d), F32)],
        compiler_params=_seq_params(56),
    )(x1, proj1, dglu, dz1, dx2, g1, w_in)


def _even_mix_bwd(pooled, proj, mixed, gv, dx1, pw, scale, cc_re, cc_im, d_skip, w_glu, w_out, pushes):
    s, d = dx1.shape
    half = pooled.shape[1]
    n = cc_re.shape[0] * cc_re.shape[1]
    ts = 256
    nt = s // ts
    groups = len(POOL_WINDOWS)
    np_ = len(pushes)

    def body(*refs):
        ins, outs, _, push_refs = _split_refs(refs, 13, 10, np_)
        (pooled_ref, u_ref, z_ref, mixed_ref, gv_ref, dx1_ref, pw_ref, scale_ref, ccre_ref, ccim_ref, d_ref,
         wglu_ref, wout_ref) = ins
        (dpooled_ref, du_ref, dz_ref, dxre_ref, dxim_ref, dyin_ref, dgv_ref, dpw_ref, dscale_ref, dd_ref) = outs
        i = pl.program_id(0)
        _riding_start(pushes, i == 0, push_refs)

        @pl.when(i == 0)
        def _():
            for ref in (dpw_ref, dscale_ref, dd_ref):
                ref[...] = jnp.zeros_like(ref)

        dymix = _dot_nt(dx1_ref[...], wout_ref[...])
        _, a_vjp = jax.vjp(_mix_a, mixed_ref[...], scale_ref[...], z_ref[:, :half])
        dmixed, dscale, dza = a_vjp(dymix[:, :half])
        _, b_vjp = jax.vjp(_mix_b, gv_ref[:, :half], gv_ref[:, half:], z_ref[:, half:])
        dval, dgate, dzb = b_vjp(dymix[:, half:])
        dz_ref[:, :half] = dza
        dz_ref[:, half:] = dzb
        dscale_ref[...] += dscale
        dgv = jnp.concatenate([dval, dgate], axis=1).astype(MM)
        dgv_ref[...] = dgv
        dyin = _dot_nt(dgv, wglu_ref[...])
        dd_ref[...] += jnp.sum(dyin * u_ref[...], axis=0, keepdims=True)
        du_ref[...] = d_ref[...] * dyin
        dyb = dyin.astype(MM)
        dyin_ref[...] = dyb
        sb, cb = ccre_ref.shape[1:]
        for b in range(SSM_BLOCKS):
            states, chans = slice(b * sb, (b + 1) * sb), slice(b * cb, (b + 1) * cb)
            dxre_ref[:, states] = _dot_nt(dyb[:, chans], ccre_ref[b])
            dxim_ref[:, states] = -_dot_nt(dyb[:, chans], ccim_ref[b])
        for g in range(groups):
            cols = slice(g * LANES, (g + 1) * LANES)
            dm = dmixed[:, cols].astype(MM)
            dpooled_ref[:, cols] = _dot_nt(dm, pw_ref[g])
            dpw_ref[g] += _dot_tn(pooled_ref[:, cols], dm)

        _riding_wait(pushes, i == nt - 1, push_refs)

    out = pl.pallas_call(
        body, name="even_mix_bwd", grid=(nt,),
        in_specs=[_rows(ts, half), _rows(ts, half, col=1), _rows(ts, d, col=1), _rows(ts, half), _rows(ts, d),
                  _rows(ts, d), _full(pw.shape), _full((1, half)), _full(cc_re.shape), _full(cc_im.shape),
                  _full((1, half)), _full((half, d)), _full((d, d))] + [ANY] * np_,
        out_specs=[_rows(ts, half), _rows(ts, half), _rows(ts, d), _rows(ts, n), _rows(ts, n), _rows(ts, half),
                   _rows(ts, d), _full(pw.shape), _full((1, half)), _full((1, half))] + [ANY] * np_,
        out_shape=[SDS((s, half), F32), SDS((s, half), F32), SDS((s, d), F32), SDS((s, n), F32), SDS((s, n), F32),
                   SDS((s, half), MM), SDS((s, d), MM), SDS(pw.shape, F32), SDS((1, half), F32),
                   SDS((1, half), F32)] + [p.out for p in pushes],
        scratch_shapes=_push_scratch(np_),
        compiler_params=_seq_params(56),
    )(pooled, proj, proj, mixed, gv, dx1, pw, scale, cc_re, cc_im, d_skip, w_glu, w_out,
      *[p.array for p in pushes])
    return out[:10], out[10:]


def _ssm_bu_bwd(g_re, g_im, du_skip, bb):
    s, n = g_re.shape
    nb, cb, two_nb = bb.shape
    sb = two_nb // 2
    cin = nb * cb
    ts = 512

    def body(gre_ref, gim_ref, du_ref, bb_ref, out_ref):
        for b in range(nb):
            states, chans = slice(b * sb, (b + 1) * sb), slice(b * cb, (b + 1) * cb)
            out_ref[:, chans] = (du_ref[:, chans] + _dot_nt(gre_ref[:, states], bb_ref[b, :, :sb])
                                 + _dot_nt(gim_ref[:, states], bb_ref[b, :, sb:]))

    return pl.pallas_call(
        body, name="ssm_bu_bwd", grid=(s // ts,),
        in_specs=[_rows(ts, n), _rows(ts, n), _rows(ts, cin), _full(bb.shape)],
        out_specs=_rows(ts, cin),
        out_shape=SDS((s, cin), F32),
        compiler_params=_seq_params(48),
    )(g_re, g_im, du_skip, bb)


def _even_in_bwd(x, du_pool, du_ssm, dz, dx1, g0, w_in):
    s, d = x.shape
    half = du_pool.shape[1]
    shards, _, wc = w_in.shape
    ts = 512

    def body(x_ref, dup_ref, dus_ref, dz_ref, dx1_ref, g_ref, w_ref, gx_ref, dg_ref):
        @pl.when(pl.program_id(0) == 0)
        def _():
            dg_ref[...] = jnp.zeros_like(dg_ref)

        dp = jnp.concatenate([dup_ref[...], dus_ref[...], dz_ref[...]], axis=1).astype(MM)
        dh = _dot_nt(dp[:, :wc], w_ref[0])
        for k in range(1, shards):
            dh = dh + _dot_nt(dp[:, k * wc:(k + 1) * wc], w_ref[k])
        _, rms_vjp = jax.vjp(_rms, x_ref[...], g_ref[...])
        dxa, dg = rms_vjp(dh)
        gx_ref[...] = dxa + dx1_ref[...]
        dg_ref[...] += dg

    return pl.pallas_call(
        body, name="even_in_bwd", grid=(s // ts,),
        in_specs=[_rows(ts, d), _rows(ts, half), _rows(ts, half), _rows(ts, d), _rows(ts, d), _full((1, d)),
                  _full(w_in.shape)],
        out_specs=[_rows(ts, d), _full((1, d))],
        out_shape=[SDS((s, d), F32), SDS((1, d), F32)],
        compiler_params=_seq_params(48),
    )(x, du_pool, du_ssm, dz, dx1, g0, w_in)


def _discretise(log_dt, ar, ai, br, bi):
    dt = jnp.exp(log_dt)
    mag = jnp.exp(ar * dt)
    ang = ai * dt
    abr = mag * jnp.cos(ang)
    abi = mag * jnp.sin(ang)
    den = ar * ar + ai * ai
    nr = abr - 1.0
    ni = abi
    kr = (nr * ar + ni * ai) / den
    ki = (ni * ar - nr * ai) / den
    bbr = kr[None] * br - ki[None] * bi
    bbi = kr[None] * bi + ki[None] * br
    return abr, abi, bbr, bbi


def _whole(n):
    return [pl.BlockSpec(memory_space=pltpu.VMEM)] * n


def _disc_fwd(log_dt, ar, ai, br, bi):
    def body(ld_ref, ar_ref, ai_ref, br_ref, bi_ref, abr_ref, abi_ref, bbr_ref, bbi_ref):
        out = _discretise(ld_ref[...], ar_ref[...], ai_ref[...], br_ref[...], bi_ref[...])
        for ref, val in zip((abr_ref, abi_ref, bbr_ref, bbi_ref), out):
            ref[...] = val

    return pl.pallas_call(
        body, name="ssm_discretise", in_specs=_whole(5), out_specs=_whole(4),
        out_shape=[SDS(ar.shape, F32), SDS(ar.shape, F32), SDS(br.shape, F32), SDS(br.shape, F32)],
    )(log_dt, ar, ai, br, bi)


def _disc_bwd(log_dt, ar, ai, br, bi, dabr, dabi, dbbr, dbbi):
    def body(ld_ref, ar_ref, ai_ref, br_ref, bi_ref, dabr_ref, dabi_ref, dbbr_ref, dbbi_ref,
             dld_ref, dar_ref, dai_ref, dbr_ref, dbi_ref):
        _, vjp = jax.vjp(_discretise, ld_ref[...], ar_ref[...], ai_ref[...], br_ref[...], bi_ref[...])
        grads = vjp((dabr_ref[...], dabi_ref[...], dbbr_ref[...], dbbi_ref[...]))
        for ref, val in zip((dld_ref, dar_ref, dai_ref, dbr_ref, dbi_ref), grads):
            ref[...] = val

    return pl.pallas_call(
        body, name="ssm_discretise_bwd", in_specs=_whole(9), out_specs=_whole(5),
        out_shape=[SDS(log_dt.shape, F32), SDS(ar.shape, F32), SDS(ar.shape, F32), SDS(br.shape, F32),
                   SDS(br.shape, F32)],
    )(log_dt, ar, ai, br, bi, dabr, dabi, dbbr, dbbi)


def _block_diag(t):
    g, a, b = t.shape
    per = g // SSM_BLOCKS
    t = t.reshape(SSM_BLOCKS, per, a, b)
    same = jnp.eye(per, dtype=bool)[None, :, None, :, None]
    return jnp.where(same, t[:, :, :, None, :], 0.0).reshape(SSM_BLOCKS, per * a, per * b)


def _diag_blocks(m, g):
    per = g // SSM_BLOCKS
    a, b = m.shape[1] // per, m.shape[2] // per
    d = jnp.diagonal(m.reshape(SSM_BLOCKS, per, a, per, b), axis1=1, axis2=3)
    return jnp.moveaxis(d, -1, 1).reshape(g, a, b)


def _adamw(name, parts, w, m, v):
    rows, cols = w.shape
    tr = ADAM_ROWS if rows % ADAM_ROWS == 0 else rows

    def body(p_ref, w_ref, m_ref, v_ref, g_ref, d_ref, nm_ref, nv_ref):
        g = p_ref[0].astype(F32)
        for dev in range(1, N_DEV):
            g = g + p_ref[dev].astype(F32)
        g_ref[...] = g
        nm = ADAM_B1 * m_ref[...] + (1.0 - ADAM_B1) * g
        nv = ADAM_B2 * v_ref[...] + (1.0 - ADAM_B2) * jnp.square(g)
        nm_ref[...] = nm
        nv_ref[...] = nv
        m_hat = nm / (1.0 - ADAM_B1 ** ADAM_STEP)
        v_hat = nv / (1.0 - ADAM_B2 ** ADAM_STEP)
        d_ref[...] = -ADAM_LR * (m_hat / (jnp.sqrt(v_hat) + ADAM_EPS) + ADAM_WD * w_ref[...])

    tile = _rows(tr, cols)
    out = SDS((rows, cols), F32)
    return pl.pallas_call(
        body, name=name, grid=(rows // tr,),
        in_specs=[pl.BlockSpec((N_DEV, tr, cols), lambda i: (0, i, 0)), tile, tile, tile],
        out_specs=[tile, tile, tile, tile],
        out_shape=[out, out, out, out],
        compiler_params=_seq_params(32),
    )(parts, w, m, v)


def _as_rows(flat):
    n = flat.shape[-1]
    padded = _round_up(n, PACK_TILE)
    if padded != n:
        flat = jnp.pad(flat, [(0, 0)] * (flat.ndim - 1) + [(0, padded - n)])
    return flat.reshape(flat.shape[:-1] + (padded // LANES, LANES))


def _columns_to_shards(full):
    r, c8 = full.shape
    return jnp.transpose(full.reshape(r, N_DEV, c8 // N_DEV), (1, 0, 2))


def _shards_to_columns(shards):
    n, r, c = shards.shape
    return jnp.transpose(shards, (1, 0, 2)).reshape(r, n * c)


def _pack_local(tensors):
    return jnp.concatenate([_as_rows(t.reshape(-1)) for t in tensors], axis=0)


def _unpack_local(packed, shapes):
    out, r = [], 0
    for shape in shapes:
        n = 1
        for dim in shape:
            n *= dim
        rows = _round_up(n, PACK_TILE) // LANES
        out.append(packed[r:r + rows].reshape(-1)[:n].reshape(shape))
        r += rows
    return out


def kernel(x, even_norm, even_w_in, pool_w, pool_scale, ssm_log_dt, ssm_a_re, ssm_a_im, ssm_b_re, ssm_b_im, ssm_c_re, ssm_c_im, ssm_d, ssm_w_glu, even_w_out, odd_norm, odd_w_in, conv_w, conv_b, conv_ln_g, conv_ln_b, odd_w_out, final_norm, loss_target, m_even_norm, m_even_w_in, m_pool_w, m_pool_scale, m_ssm_log_dt, m_ssm_a_re, m_ssm_a_im, m_ssm_b_re, m_ssm_b_im, m_ssm_c_re, m_ssm_c_im, m_ssm_d, m_ssm_w_glu, m_even_w_out, m_odd_norm, m_odd_w_in, m_conv_w, m_conv_b, m_conv_ln_g, m_conv_ln_b, m_odd_w_out, m_final_norm, v_even_norm, v_even_w_in, v_pool_w, v_pool_scale, v_ssm_log_dt, v_ssm_a_re, v_ssm_a_im, v_ssm_b_re, v_ssm_b_im, v_ssm_c_re, v_ssm_c_im, v_ssm_d, v_ssm_w_glu, v_even_w_out, v_odd_norm, v_odd_w_in, v_conv_w, v_conv_b, v_conv_ln_g, v_conv_ln_b, v_odd_w_out, v_final_norm):
    given = dict(locals())
    xs = x[0]
    tgt = loss_target[0]
    d_model = xs.shape[1]

    def local(prefix, name):
        t = given[prefix + name]
        return t if name == 'final_norm' else t[0]

    def small_block(prefix):
        parts = [jnp.pad(local(prefix, 'conv_w'), ((0, SMALL_AT['odd_norm'] - CONV_KERNEL), (0, 0)))]
        for n in SMALL_VECTORS:
            parts.append(jnp.pad(local(prefix, n).reshape(1, LANES), ((0, SUBLANES - 1), (0, 0))))
        return jnp.concatenate(parts, axis=0)

    def as_tile(row):
        return jnp.pad(row, ((0, SUBLANES - row.shape[0]), (0, 0)))

    (w_in_e,) = _push_alone("gather_first", [_Push(local('', 'even_w_in').astype(MM), False)])
    later = [_Push(local('', 'ssm_w_glu').astype(MM), False), _Push(local('', 'even_w_out').astype(MM), False),
             _Push(local('', 'odd_w_in').astype(MM), False), _Push(local('', 'odd_w_out').astype(MM), False),
             _Push(small_block(''), False)]

    a_re, a_im = local('', 'ssm_a_re'), local('', 'ssm_a_im')
    groups, state = a_re.shape
    log_dt = local('', 'ssm_log_dt').reshape(groups, 1)
    b_re_t = jnp.transpose(local('', 'ssm_b_re'), (2, 0, 1))
    b_im_t = jnp.transpose(local('', 'ssm_b_im'), (2, 0, 1))
    abr, abi, bbr, bbi = _disc_fwd(log_dt, a_re, a_im, b_re_t, b_im_t)
    a_re_row = abr.reshape(1, groups * state)
    a_im_row = abi.reshape(1, groups * state)
    bb = jnp.concatenate([_block_diag(jnp.transpose(bbr, (1, 0, 2))), _block_diag(jnp.transpose(bbi, (1, 0, 2)))],
                         axis=2).astype(MM)
    cc_re = _block_diag(jnp.transpose(local('', 'ssm_c_re'), (0, 2, 1))).astype(MM)
    cc_im = _block_diag(jnp.transpose(local('', 'ssm_c_im'), (0, 2, 1))).astype(MM)
    pw = local('', 'pool_w').astype(MM)
    g0 = local('', 'even_norm').reshape(1, d_model)
    fg = local('', 'final_norm').reshape(1, d_model)
    scale = local('', 'pool_scale').reshape(1, -1)
    d_skip = local('', 'ssm_d').reshape(1, -1)

    proj0, h0, (wg3, wo_e3) = _even_in(xs, g0, w_in_e, later[:2])
    pooled = _pool_fwd(proj0)
    x_re, x_im, (wi_o3, wo_o3, small8) = _scan_fwd(a_re_row, a_im_row, proj0, bb, later[2:])
    w_glu = _shards_to_columns(wg3)
    w_out_e = wo_e3.reshape(d_model, d_model)
    (x1, ymix, mixed, yin, gv), _ = _even_mix(
        pooled, proj0, x_re, x_im, xs, pw, scale, cc_re, cc_im, d_skip, w_glu, w_out_e, [])
    w_in_o = _shards_to_columns(wi_o3)
    w_out_o = wo_o3.reshape(d_model, d_model)
    small = _shards_to_columns(small8)
    cw = small[:CONV_KERNEL]
    g1, cb, lg, lb = (small[SMALL_AT[n]:SMALL_AT[n] + 1] for n in SMALL_VECTORS)
    proj1, glu, h1 = _odd_in(x1, g1, w_in_o)
    conv = _conv_fwd(glu, cw)

    half = pooled.shape[1]
    n_state = groups * state
    rows_per = d_model // N_DEV
    dc, dz1, dx2, y1, loss_tile, dcb, dlg, dlb, dfg = _odd_out(conv, proj1, x1, tgt, cb, lg, lb, w_out_o, fg)
    g_odd_out = _mm_tn("dw_odd_out", y1, dx2, d_model, d_model, WIRE, ts=1024).reshape(N_DEV, rows_per, d_model)
    dglu, dcw, (r_odd_out,) = _conv_bwd(dc, glu, cw, [_Push(g_odd_out, True)])
    dx1, dproj1, dg1 = _odd_in_bwd(x1, proj1, dglu, dz1, dx2, g1, w_in_o)
    g_odd_in = _columns_to_shards(_mm_tn("dw_odd_in", h1, dproj1, d_model, 3 * d_model, WIRE))
    g_even_out = _mm_tn("dw_even_out", ymix, dx1, d_model, d_model, WIRE, ts=1024).reshape(N_DEV, rows_per, d_model)
    mix_grads, (r_odd_in,) = _even_mix_bwd(pooled, proj0, mixed, gv, dx1, pw, scale, cc_re, cc_im, d_skip, w_glu,
                                           w_out_e, [_Push(g_odd_in, True)])
    dpooled, du_skip, dz0, dx_re, dx_im, dyin, dgv, dpw, dscale, dd = mix_grads
    g_glu = _columns_to_shards(_mm_tn("dw_glu", yin, dgv, half, d_model, WIRE, ts=1024))
    sblk, cblk = n_state // SSM_BLOCKS, half // SSM_BLOCKS
    dcc_re = _mm_tn_blocks("dw_c_re", x_re, dyin, sblk, cblk, SSM_BLOCKS, ts=2048)
    dcc_im = _mm_tn_blocks("dw_c_im", x_im, dyin, sblk, cblk, SSM_BLOCKS, sign=-1.0, ts=2048)
    dc_re = jnp.transpose(_diag_blocks(dcc_re, groups), (0, 2, 1))
    dc_im = jnp.transpose(_diag_blocks(dcc_im, groups), (0, 2, 1))
    g_small = _columns_to_shards(jnp.concatenate(
        [dcw, as_tile(dg1), as_tile(dcb), as_tile(dlg), as_tile(dlb)], axis=0))
    early = {'pool_w': dpw, 'pool_scale': dscale, 'ssm_c_re': dc_re, 'ssm_c_im': dc_im, 'ssm_d': dd,
             'final_norm': dfg}
    g_early = _pack_local([early[n] for n in EARLY_REPLICATED] + [loss_tile])
    g_re, g_im, dabr, dabi, (r_even_out, r_glu, r_small, r_early) = _scan_bwd(
        a_re_row, a_im_row, dx_re, dx_im, x_re, x_im,
        [_Push(g_even_out, True), _Push(g_glu, True), _Push(g_small, True), _Push(g_early, False)])
    du_ssm = _ssm_bu_bwd(g_re, g_im, du_skip, bb)
    du_pool = _pool_bwd(dpooled)
    grad_x, dg0 = _even_in_bwd(xs, du_pool, du_ssm, dz0, dx1, g0, w_in_e)
    g_even_in = _columns_to_shards(jnp.concatenate(
        [_mm_tn("dw_even_in_pool", h0, du_pool, d_model, half, WIRE, ts=1024),
         _mm_tn("dw_even_in_ssm", h0, du_ssm, d_model, half, WIRE, ts=1024),
         _mm_tn("dw_even_in_gate", h0, dz0, d_model, d_model, WIRE, ts=1024)], axis=1))
    g_mid = _pack_local([dg0])
    dbb_re, dbb_im, (r_even_in, r_mid) = _dw_bbar(proj0, g_re, g_im, cblk, sblk,
                                                  [_Push(g_even_in, True), _Push(g_mid, False)])
    dbbr = jnp.transpose(_diag_blocks(dbb_re, groups), (2, 0, 1))
    dbbi = jnp.transpose(_diag_blocks(dbb_im, groups), (2, 0, 1))
    dld, dar, dai, dbr, dbi = _disc_bwd(log_dt, a_re, a_im, b_re_t, b_im_t,
                                        dabr.reshape(groups, state), dabi.reshape(groups, state), dbbr, dbbi)
    late = {'ssm_log_dt': dld, 'ssm_a_re': dar, 'ssm_a_im': dai,
            'ssm_b_re': jnp.transpose(dbr, (1, 2, 0)), 'ssm_b_im': jnp.transpose(dbi, (1, 2, 0))}
    g_late = _pack_local([late[n] for n in LATE_REPLICATED])
    (r_late,) = _push_alone("exchange_tail", [_Push(g_late, False)])

    results = {}
    for n, parts in (('even_w_in', r_even_in), ('ssm_w_glu', r_glu), ('even_w_out', r_even_out),
                     ('odd_w_in', r_odd_in), ('odd_w_out', r_odd_out)):
        results[n] = _adamw("adamw_" + n, parts, local('', n), local('m_', n), local('v_', n))
    small_out = _adamw("adamw_small", r_small, small_block(''), small_block('m_'), small_block('v_'))
    results['conv_w'] = [o[:CONV_KERNEL] for o in small_out]
    for n in SMALL_VECTORS:
        results[n] = [o[SMALL_AT[n]] for o in small_out]
    loss_rows = [jnp.zeros_like(loss_tile)]
    for names, parts, extra in ((EARLY_REPLICATED, r_early, loss_rows), (MID_REPLICATED, r_mid, []),
                               (LATE_REPLICATED, r_late, [])):
        packed = [_pack_local([local(p, n) for n in names] + extra) for p in ('', 'm_', 'v_')]
        out = _adamw("adamw_" + names[0], parts, *packed)
        unpacked = [_unpack_local(o, [given[n].shape for n in names]) for o in out]
        for k, n in enumerate(names):
            results[n] = [u[k] for u in unpacked]
        if extra:
            loss = out[0][-SUBLANES, 0]

    outs = [loss, grad_x[None]]
    for kind in range(4):
        outs.extend(results[n][kind].reshape(given[n].shape) for n in WEIGHTS)
    return tuple(outs)
```

```python
import functools

import jax
import jax.numpy as jnp
from jax import lax
from jax.experimental import pallas as pl
from jax.experimental.pallas import tpu as pltpu

F32 = jnp.float32
MM = jnp.bfloat16
WIRE = jnp.bfloat16
SDS = jax.ShapeDtypeStruct

RMS_EPS = 1e-6
LN_EPS = 1e-5
POOL_WINDOWS = (2, 4, 8, 16)
POOL_HALO = 16
CONV_KERNEL = 31
CONV_HALO = 32
N_DEV = 8
LANES = 128
SUBLANES = 8
PACK_TILE = SUBLANES * LANES
ADAM_ROWS = 128
MIB = 1024 * 1024

ADAM_LR = 0.001
ADAM_B1 = 0.9
ADAM_B2 = 0.999
ADAM_EPS = 1e-08
ADAM_WD = 0.01
ADAM_STEP = 10

WEIGHTS = ['even_norm', 'even_w_in', 'pool_w', 'pool_scale', 'ssm_log_dt', 'ssm_a_re', 'ssm_a_im', 'ssm_b_re',
           'ssm_b_im', 'ssm_c_re', 'ssm_c_im', 'ssm_d', 'ssm_w_glu', 'even_w_out', 'odd_norm', 'odd_w_in', 'conv_w',
           'conv_b', 'conv_ln_g', 'conv_ln_b', 'odd_w_out', 'final_norm']
SMALL_VECTORS = ('odd_norm', 'conv_b', 'conv_ln_g', 'conv_ln_b')
SMALL_AT = {'odd_norm': 32, 'conv_b': 40, 'conv_ln_g': 48, 'conv_ln_b': 56}
EARLY_REPLICATED = ['pool_w', 'pool_scale', 'ssm_c_re', 'ssm_c_im', 'ssm_d', 'final_norm']
MID_REPLICATED = ['even_norm']
LATE_REPLICATED = ['ssm_log_dt', 'ssm_a_re', 'ssm_a_im', 'ssm_b_re', 'ssm_b_im']


def _round_up(n, m):
    return (n + m - 1) // m * m


def _sigmoid(x):
    return jax.nn.sigmoid(x)


def _silu(x):
    return x * jax.nn.sigmoid(x)


def _rms(x, g):
    return x * lax.rsqrt(jnp.mean(x * x, axis=-1, keepdims=True) + RMS_EPS) * g


def _dot(a, b):
    return jnp.dot(a.astype(MM), b.astype(MM), preferred_element_type=F32)


def _dot_nt(a, b):
    return lax.dot_general(a.astype(MM), b.astype(MM), (((1,), (1,)), ((), ())), preferred_element_type=F32)


def _dot_tn(a, b):
    return lax.dot_general(a.astype(MM), b.astype(MM), (((0,), (0,)), ((), ())), preferred_element_type=F32)


def _rows(ts, width, col=0):
    return pl.BlockSpec((ts, width), lambda i: (i, col))


def _rows_rev(ts, width, n, col=0):
    return pl.BlockSpec((ts, width), lambda i: (n - 1 - i, col))


def _full(shape):
    zeros = (0,) * len(shape)
    return pl.BlockSpec(shape, lambda i: zeros)


def _seq_params(vmem_mib=48, dims=1):
    return pltpu.CompilerParams(dimension_semantics=("arbitrary",) * dims, vmem_limit_bytes=vmem_mib * MIB)


def _mesh_position():
    x, y, c = lax.axis_index("x"), lax.axis_index("y"), lax.axis_index("c")
    return x, y, c


def _peer(pos, relation):
    x, y, c = pos
    px = 1 - x if relation & 4 else x
    py = 1 - y if relation & 2 else y
    pc = 1 - c if relation & 1 else c
    return px, py, pc


ANY = pl.BlockSpec(memory_space=pl.ANY)


class _Push:
    def __init__(self, array, scatter):
        self.array = array
        self.scatter = scatter
        self.out = SDS(array.shape if scatter else (N_DEV,) + array.shape, array.dtype)


def _push_scratch(n):
    if n == 0:
        return []
    return [pltpu.SemaphoreType.DMA((n * (N_DEV - 1),)), pltpu.SemaphoreType.DMA((n * (N_DEV - 1),)),
            pltpu.SemaphoreType.DMA((n,))]


def _push_copies(pushes, srcs, dsts, send_sems, recv_sems, local_sems):
    pos = _mesh_position()
    me = 4 * pos[0] + 2 * pos[1] + pos[2]
    copies = []
    for a, (push, src, dst) in enumerate(zip(pushes, srcs, dsts)):
        copies.append(pltpu.make_async_copy(src.at[me] if push.scatter else src, dst.at[me], local_sems.at[a]))
        for relation in range(1, N_DEV):
            peer = _peer(pos, relation)
            peer_id = 4 * peer[0] + 2 * peer[1] + peer[2]
            k = a * (N_DEV - 1) + relation - 1
            copies.append(pltpu.make_async_remote_copy(
                src_ref=src.at[peer_id] if push.scatter else src, dst_ref=dst.at[me],
                send_sem=send_sems.at[k], recv_sem=recv_sems.at[k], device_id=peer,
                device_id_type=pl.DeviceIdType.MESH))
    return copies


def _push_alone(name, pushes):
    n = len(pushes)

    def body(*refs):
        copies = _push_copies(pushes, refs[:n], refs[n:2 * n], *refs[2 * n:])
        for cp in copies:
            cp.start()
        for cp in copies:
            cp.wait()

    return pl.pallas_call(
        body, name=name, out_shape=[p.out for p in pushes], in_specs=[ANY] * n, out_specs=[ANY] * n,
        scratch_shapes=_push_scratch(n),
    )(*[p.array for p in pushes])


def _riding_start(pushes, first, refs):
    n = len(pushes)
    if n == 0:
        return

    @pl.when(first)
    def _():
        for cp in _push_copies(pushes, refs[:n], refs[n:2 * n], *refs[2 * n:]):
            cp.start()


def _riding_wait(pushes, last, refs):
    n = len(pushes)
    if n == 0:
        return

    @pl.when(last)
    def _():
        for cp in _push_copies(pushes, refs[:n], refs[n:2 * n], *refs[2 * n:]):
            cp.wait()


def _split_refs(refs, n_in, n_out, n_push):
    ins = refs[:n_in]
    srcs = refs[n_in:n_in + n_push]
    o0 = n_in + n_push
    outs = refs[o0:o0 + n_out]
    dsts = refs[o0 + n_out:o0 + n_out + n_push]
    s0 = o0 + n_out + n_push
    n_sem = 3 if n_push else 0
    sems = refs[s0:s0 + n_sem]
    scratch = refs[s0 + n_sem:]
    return ins, outs, scratch, tuple(srcs) + tuple(dsts) + tuple(sems)


def _mm_tn(name, a, b, m, n, out_dtype, a_col=0, b_col=0, ts=512):
    s = a.shape[0]
    nk = s // ts

    def body(a_ref, b_ref, o_ref, acc):
        k = pl.program_id(0)

        @pl.when(k == 0)
        def _():
            acc[...] = jnp.zeros_like(acc)

        acc[...] += _dot_tn(a_ref[...], b_ref[...])

        @pl.when(k == nk - 1)
        def _():
            o_ref[...] = acc[...].astype(out_dtype)

    return pl.pallas_call(
        body, name=name, grid=(nk,),
        in_specs=[pl.BlockSpec((ts, m), lambda k: (k, a_col)), pl.BlockSpec((ts, n), lambda k: (k, b_col))],
        out_specs=pl.BlockSpec((m, n), lambda k: (0, 0)),
        out_shape=SDS((m, n), out_dtype),
        scratch_shapes=[pltpu.VMEM((m, n), F32)],
        compiler_params=_seq_params(56),
    )(a, b)


def _mm_tn_blocks(name, a, b, m, n, nb, a_col=0, a_step=1, b_col=0, sign=1.0, ts=512):
    s = a.shape[0]

    def body(a_ref, b_ref, o_ref):
        @pl.when(pl.program_id(1) == 0)
        def _():
            o_ref[...] = jnp.zeros_like(o_ref)

        o_ref[...] += sign * _dot_tn(a_ref[...], b_ref[...])

    return pl.pallas_call(
        body, name=name, grid=(nb, s // ts),
        in_specs=[pl.BlockSpec((ts, m), lambda j, k: (k, a_col + a_step * j)),
                  pl.BlockSpec((ts, n), lambda j, k: (k, b_col + j))],
        out_specs=pl.BlockSpec((None, m, n), lambda j, k: (j, 0, 0)),
        out_shape=SDS((nb, m, n), F32),
        compiler_params=_seq_params(48, dims=2),
    )(a, b)


def _dw_bbar(proj, g_re, g_im, cb, sb, pushes, ts=2048):
    s = proj.shape[0]
    nb, nk = SSM_BLOCKS, s // ts
    np_ = len(pushes)

    def body(*refs):
        ins, outs, _, push_refs = _split_refs(refs, 3, 2, np_)
        u_ref, gre_ref, gim_ref = ins
        ore_ref, oim_ref = outs
        j, k = pl.program_id(0), pl.program_id(1)
        _riding_start(pushes, jnp.logical_and(j == 0, k == 0), push_refs)

        @pl.when(k == 0)
        def _():
            ore_ref[...] = jnp.zeros_like(ore_ref)
            oim_ref[...] = jnp.zeros_like(oim_ref)

        u = u_ref[...].astype(MM)
        ore_ref[...] += _dot_tn(gre_ref[...], u)
        oim_ref[...] += _dot_tn(gim_ref[...], u)
        _riding_wait(pushes, jnp.logical_and(j == nb - 1, k == nk - 1), push_refs)

    wide = pl.BlockSpec((ts, sb), lambda j, k: (k, j))
    out = pl.BlockSpec((None, sb, cb), lambda j, k: (j, 0, 0))
    res = pl.pallas_call(
        body, name="dw_bbar", grid=(nb, nk),
        in_specs=[pl.BlockSpec((ts, cb), lambda j, k: (k, nb + j)), wide, wide] + [ANY] * np_,
        out_specs=[out, out] + [ANY] * np_,
        out_shape=[SDS((nb, sb, cb), F32), SDS((nb, sb, cb), F32)] + [p.out for p in pushes],
        scratch_shapes=_push_scratch(np_),
        compiler_params=_seq_params(48, dims=2),
    )(proj, g_re, g_im, *[p.array for p in pushes])
    return res[0], res[1], res[2:]


def _even_in(x, g0, w_in, pushes):
    s, d = x.shape
    shards, _, wc = w_in.shape
    e = shards * wc
    ts = 512
    nt = s // ts
    np_ = len(pushes)

    def body(*refs):
        (x_ref, g_ref, w_ref), (proj_ref, h_ref), _, push_refs = _split_refs(refs, 3, 2, np_)
        i = pl.program_id(0)
        _riding_start(pushes, i == 0, push_refs)
        hb = _rms(x_ref[...], g_ref[...]).astype(MM)
        h_ref[...] = hb
        for k in range(shards):
            proj_ref[:, k * wc:(k + 1) * wc] = jnp.dot(hb, w_ref[k], preferred_element_type=F32)
        _riding_wait(pushes, i == nt - 1, push_refs)

    out = pl.pallas_call(
        body, name="even_in", grid=(nt,),
        in_specs=[_rows(ts, d), _full((1, d)), _full(w_in.shape)] + [ANY] * np_,
        out_specs=[_rows(ts, e), _rows(ts, d)] + [ANY] * np_,
        out_shape=[SDS((s, e), F32), SDS((s, d), MM)] + [p.out for p in pushes],
        scratch_shapes=_push_scratch(np_),
        compiler_params=_seq_params(48),
    )(x, g0, w_in, *[p.array for p in pushes])
    return out[0], out[1], out[2:]


def _pool_counts(t0, ts, w):
    pos = (t0 + lax.broadcasted_iota(jnp.int32, (ts, LANES), 0) + 1).astype(F32)
    return jnp.minimum(pos, float(w))


def _pool_fwd(proj):
    s = proj.shape[0]
    width = LANES * len(POOL_WINDOWS)
    ts = 512
    per = ts // POOL_HALO

    def body(prev_ref, u_ref, out_ref, ext):
        i = pl.program_id(0)
        ext[0:POOL_HALO, 0:width] = jnp.where(i == 0, 0.0, prev_ref[...])
        ext[POOL_HALO:, 0:width] = u_ref[...]
        for g, w in enumerate(POOL_WINDOWS):
            cols = slice(g * LANES, (g + 1) * LANES)
            tok = ext[pl.ds(POOL_HALO, ts), cols]
            acc = tok
            for k in range(1, w):
                acc = acc + ext[pl.ds(POOL_HALO - k, ts), cols]
            out_ref[:, cols] = acc / _pool_counts(i * ts, ts, w) - tok

    return pl.pallas_call(
        body, name="pool_fwd", grid=(s // ts,),
        in_specs=[pl.BlockSpec((POOL_HALO, width), lambda i: (jnp.maximum(i * per - 1, 0), 0)),
                  _rows(ts, width)],
        out_specs=_rows(ts, width),
        out_shape=SDS((s, width), F32),
        scratch_shapes=[pltpu.VMEM((ts + POOL_HALO, width + LANES), F32)],
        compiler_params=_seq_params(32),
    )(proj, proj)


def _pool_bwd(dp):
    s, width = dp.shape
    ts = 512
    per = ts // POOL_HALO
    n = s // ts

    def body(dp_ref, next_ref, out_ref, ext):
        i = pl.program_id(0)
        nxt = jnp.where(i == n - 1, 0.0, next_ref[...])
        for g, w in enumerate(POOL_WINDOWS):
            cols = slice(g * LANES, (g + 1) * LANES)
            cur = dp_ref[:, cols]
            ext[0:ts, cols] = cur / _pool_counts(i * ts, ts, w)
            ext[ts:, cols] = nxt[:, cols] / _pool_counts((i + 1) * ts, POOL_HALO, w)
            acc = -cur
            for k in range(w):
                acc = acc + ext[pl.ds(k, ts), cols]
            out_ref[:, cols] = acc

    return pl.pallas_call(
        body, name="pool_bwd", grid=(n,),
        in_specs=[_rows(ts, width),
                  pl.BlockSpec((POOL_HALO, width), lambda i: (jnp.minimum((i + 1) * per, s // POOL_HALO - 1), 0))],
        out_specs=_rows(ts, width),
        out_shape=SDS((s, width), F32),
        scratch_shapes=[pltpu.VMEM((ts + POOL_HALO, width + LANES), F32)],
        compiler_params=_seq_params(32),
    )(dp, dp)


SSM_BLOCKS = 4


def _ssm_bu(proj, bb, pushes):
    s = proj.shape[0]
    nb, cb, two_nb = bb.shape
    sb = two_nb // 2
    cin, n = nb * cb, nb * sb
    ts = 512
    nt = s // ts
    np_ = len(pushes)

    def body(*refs):
        (u_ref, bb_ref), (re_ref, im_ref), _, push_refs = _split_refs(refs, 2, 2, np_)
        i = pl.program_id(0)
        _riding_start(pushes, i == 0, push_refs)
        for b in range(nb):
            bu = _dot(u_ref[:, b * cb:(b + 1) * cb], bb_ref[b])
            re_ref[:, b * sb:(b + 1) * sb] = bu[:, :sb]
            im_ref[:, b * sb:(b + 1) * sb] = bu[:, sb:]
        _riding_wait(pushes, i == nt - 1, push_refs)

    out = pl.pallas_call(
        body, name="ssm_bu", grid=(nt,),
        in_specs=[_rows(ts, cin, col=1), _full(bb.shape)] + [ANY] * np_,
        out_specs=[_rows(ts, n), _rows(ts, n)] + [ANY] * np_,
        out_shape=[SDS((s, n), F32), SDS((s, n), F32)] + [p.out for p in pushes],
        scratch_shapes=_push_scratch(np_),
        compiler_params=_seq_params(48),
    )(proj, bb, *[p.array for p in pushes])
    return out[0], out[1], out[2:]


SCAN_LANES = 256


def _cmul(a, b):
    return a[0] * b[0] - a[1] * b[1], a[0] * b[1] + a[1] * b[0]


SCAN_STEPS = (1, 2, 4)


def _fill_scan_tables(tab, ar_row, ai_row, reverse):
    shape = (SUBLANES, ar_row.shape[1])
    a1 = (jnp.broadcast_to(ar_row, shape), jnp.broadcast_to(ai_row, shape))
    a2 = _cmul(a1, a1)
    a4 = _cmul(a2, a2)
    a8 = _cmul(a4, a4)
    row = lax.broadcasted_iota(jnp.int32, shape, 0)
    for idx, (k, q) in enumerate(zip(SCAN_STEPS, (a1, a2, a4))):
        keep = (row < SUBLANES - k) if reverse else (row >= k)
        tab[2 * idx] = jnp.where(keep, q[0], 0.0)
        tab[2 * idx + 1] = jnp.where(keep, q[1], 0.0)
    expo = (SUBLANES - row) if reverse else (row + 1)
    pr, pi = jnp.ones(shape, F32), jnp.zeros(shape, F32)
    for bit, q in enumerate((a1, a2, a4, a8)):
        take = ((expo >> bit) & 1) == 1
        nr, ni = _cmul((pr, pi), q)
        pr, pi = jnp.where(take, nr, pr), jnp.where(take, ni, pi)
    tab[2 * len(SCAN_STEPS)] = pr
    tab[2 * len(SCAN_STEPS) + 1] = pi


def _group_scan(v, tab, cols, carry, reverse):
    vr, vi = v
    for idx, k in enumerate(SCAN_STEPS):
        shift = SUBLANES - k if reverse else k
        tr, ti = _cmul((tab[2 * idx, :, cols], tab[2 * idx + 1, :, cols]),
                       (pltpu.roll(vr, shift, 0), pltpu.roll(vi, shift, 0)))
        vr, vi = vr + tr, vi + ti
    last = 2 * len(SCAN_STEPS)
    tr, ti = _cmul((tab[last, :, cols], tab[last + 1, :, cols]), carry)
    return vr + tr, vi + ti


def _scan_fwd(a_re, a_im, proj, bb, pushes):
    s = proj.shape[0]
    nb, cb, two_sb = bb.shape
    sb = two_sb // 2
    cin, n = nb * cb, nb * sb
    ts = 512
    nt = s // ts
    groups = ts // SUBLANES
    np_ = len(pushes)

    def body(*refs):
        ins, outs, scratch, push_refs = _split_refs(refs, 4, 2, np_)
        ar_ref, ai_ref, u_ref, bb_ref = ins
        xre_ref, xim_ref = outs
        tab, cre, cim, bre_ref, bim_ref = scratch
        i = pl.program_id(0)
        _riding_start(pushes, i == 0, push_refs)

        @pl.when(i == 0)
        def _():
            _fill_scan_tables(tab, ar_ref[...], ai_ref[...], False)
            cre[...] = jnp.zeros_like(cre)
            cim[...] = jnp.zeros_like(cim)

        for b in range(nb):
            bu = _dot(u_ref[:, b * cb:(b + 1) * cb], bb_ref[b])
            bre_ref[:, b * sb:(b + 1) * sb] = bu[:, :sb]
            bim_ref[:, b * sb:(b + 1) * sb] = bu[:, sb:]

        def group(gi, carry):
            r0 = pl.multiple_of(gi * SUBLANES, SUBLANES)
            for c in range(n // SCAN_LANES):
                cols = slice(c * SCAN_LANES, (c + 1) * SCAN_LANES)
                v = (bre_ref[pl.ds(r0, SUBLANES), cols], bim_ref[pl.ds(r0, SUBLANES), cols])
                vr, vi = _group_scan(v, tab, cols, (cre[:, cols], cim[:, cols]), False)
                xre_ref[pl.ds(r0, SUBLANES), cols] = vr
                xim_ref[pl.ds(r0, SUBLANES), cols] = vi
                cre[:, cols] = jnp.broadcast_to(vr[SUBLANES - 1:SUBLANES, :], vr.shape)
                cim[:, cols] = jnp.broadcast_to(vi[SUBLANES - 1:SUBLANES, :], vi.shape)
            return carry

        lax.fori_loop(0, groups, group, 0, unroll=2)
        _riding_wait(pushes, i == nt - 1, push_refs)

    out = pl.pallas_call(
        body, name="ssm_scan", grid=(nt,),
        in_specs=[_full((1, n)), _full((1, n)), _rows(ts, cin, col=1), _full(bb.shape)] + [ANY] * np_,
        out_specs=[_rows(ts, n), _rows(ts, n)] + [ANY] * np_,
        out_shape=[SDS((s, n), F32), SDS((s, n), F32)] + [p.out for p in pushes],
        scratch_shapes=_push_scratch(np_) + [pltpu.VMEM((2 * len(SCAN_STEPS) + 2, SUBLANES, n), F32),
                                             pltpu.VMEM((SUBLANES, n), F32), pltpu.VMEM((SUBLANES, n), F32),
                                             pltpu.VMEM((ts, n), F32), pltpu.VMEM((ts, n), F32)],
        compiler_params=_seq_params(48),
    )(a_re, a_im, proj, bb, *[p.array for p in pushes])
    return out[0], out[1], out[2:]


def _scan_bwd(a_re, a_im, dx_re, dx_im, x_re, x_im, du_skip, bb, pushes):
    s, n = dx_re.shape
    nb, cb, two_sb = bb.shape
    sb = two_sb // 2
    cin = nb * cb
    ts = 256
    nt = s // ts
    groups = ts // SUBLANES
    np_ = len(pushes)

    def body(*refs):
        ins, outs, scratch, push_refs = _split_refs(refs, 8, 5, np_)
        ar_ref, ai_ref, dre_ref, dim_ref, xre_ref, xim_ref, dus_ref, bb_ref = ins
        gre_ref, gim_ref, dar_ref, dai_ref, du_ref = outs
        tab, cre, cim, accr, acci = scratch
        i = pl.program_id(0)
        _riding_start(pushes, i == 0, push_refs)

        @pl.when(i == 0)
        def _():
            _fill_scan_tables(tab, ar_ref[...], -ai_ref[...], True)
            for ref in (cre, cim, accr, acci):
                ref[...] = jnp.zeros_like(ref)

        inner = lax.broadcasted_iota(jnp.int32, (SUBLANES, SCAN_LANES), 0) < SUBLANES - 1

        def group(k, carry):
            r0 = pl.multiple_of((groups - 1 - k) * SUBLANES, SUBLANES)
            for c in range(n // SCAN_LANES):
                cols = slice(c * SCAN_LANES, (c + 1) * SCAN_LANES)
                after = (cre[:, cols], cim[:, cols])
                v = (dre_ref[pl.ds(r0, SUBLANES), cols], dim_ref[pl.ds(r0, SUBLANES), cols])
                vr, vi = _group_scan(v, tab, cols, after, True)
                gre_ref[pl.ds(r0, SUBLANES), cols] = vr
                gim_ref[pl.ds(r0, SUBLANES), cols] = vi
                nr = jnp.where(inner, pltpu.roll(vr, SUBLANES - 1, 0), after[0])
                ni = jnp.where(inner, pltpu.roll(vi, SUBLANES - 1, 0), after[1])
                xr, xi = xre_ref[pl.ds(r0, SUBLANES), cols], xim_ref[pl.ds(r0, SUBLANES), cols]
                accr[:, cols] += nr * xr + ni * xi
                acci[:, cols] += ni * xr - nr * xi
                cre[:, cols] = jnp.broadcast_to(vr[0:1, :], vr.shape)
                cim[:, cols] = jnp.broadcast_to(vi[0:1, :], vi.shape)
            return carry

        lax.fori_loop(0, groups, group, 0, unroll=2)

        for b in range(nb):
            states, chans = slice(b * sb, (b + 1) * sb), slice(b * cb, (b + 1) * cb)
            du_ref[:, chans] = (dus_ref[:, chans] + _dot_nt(gre_ref[:, states], bb_ref[b, :, :sb])
                                + _dot_nt(gim_ref[:, states], bb_ref[b, :, sb:]))

        @pl.when(i == nt - 1)
        def _():
            dar_ref[...] = jnp.sum(accr[...], axis=0, keepdims=True)
            dai_ref[...] = jnp.sum(acci[...], axis=0, keepdims=True)

        _riding_wait(pushes, i == nt - 1, push_refs)

    small = pltpu.VMEM((SUBLANES, n), F32)
    out = pl.pallas_call(
        body, name="ssm_scan_bwd", grid=(nt,),
        in_specs=[_full((1, n)), _full((1, n))] + [_rows_rev(ts, n, nt)] * 4
        + [_rows_rev(ts, cin, nt), _full(bb.shape)] + [ANY] * np_,
        out_specs=[_rows_rev(ts, n, nt), _rows_rev(ts, n, nt), _full((1, n)), _full((1, n)),
                   _rows_rev(ts, cin, nt)] + [ANY] * np_,
        out_shape=[SDS((s, n), F32), SDS((s, n), F32), SDS((1, n), F32), SDS((1, n), F32), SDS((s, cin), F32)]
        + [p.out for p in pushes],
        scratch_shapes=_push_scratch(np_) + [pltpu.VMEM((2 * len(SCAN_STEPS) + 2, SUBLANES, n), F32),
                                             small, small, small, small],
        compiler_params=_seq_params(48),
    )(a_re, a_im, dx_re, dx_im, x_re, x_im, du_skip, bb, *[p.array for p in pushes])
    return out[0], out[1], out[2], out[3], out[4], out[5:]


def _mix_a(mixed, scale, za):
    return mixed * scale * _silu(za)


def _mix_b(val, gate, zb):
    return val * _sigmoid(gate) * _silu(zb)


def _even_mix(pooled, proj, x_re, x_im, x, pw, scale, cc_re, cc_im, d_skip, w_glu, w_out, pushes):
    s, d = x.shape
    half = pooled.shape[1]
    n = x_re.shape[1]
    ts = 256
    nt = s // ts
    np_ = len(pushes)

    def body(*refs):
        ins, outs, _, push_refs = _split_refs(refs, 13, 5, np_)
        (pooled_ref, u_ref, z_ref, xre_ref, xim_ref, x_ref, pw_ref, scale_ref, ccre_ref, ccim_ref, d_ref,
         wglu_ref, wout_ref) = ins
        x1_ref, ymix_ref, mixed_ref, yin_ref, gv_ref = outs
        i = pl.program_id(0)
        _riding_start(pushes, i == 0, push_refs)
        for g in range(len(POOL_WINDOWS)):
            cols = slice(g * LANES, (g + 1) * LANES)
            mixed_ref[:, cols] = _dot(pooled_ref[:, cols], pw_ref[g])
        sb, cb = ccre_ref.shape[1:]
        for b in range(SSM_BLOCKS):
            states, chans = slice(b * sb, (b + 1) * sb), slice(b * cb, (b + 1) * cb)
            yin_ref[:, chans] = (_dot(xre_ref[:, states], ccre_ref[b]) - _dot(xim_ref[:, states], ccim_ref[b])
                                 + d_ref[:, chans] * u_ref[:, chans])
        yin = yin_ref[...]
        gv = _dot(yin, wglu_ref[...])
        gv_ref[...] = gv
        ya = _mix_a(mixed_ref[...], scale_ref[...], z_ref[:, :half])
        yb = _mix_b(gv[:, :half], gv[:, half:], z_ref[:, half:])
        ymix = jnp.concatenate([ya, yb], axis=1).astype(MM)
        ymix_ref[...] = ymix
        x1_ref[...] = x_ref[...] + jnp.dot(ymix, wout_ref[...], preferred_element_type=F32)
        _riding_wait(pushes, i == nt - 1, push_refs)

    out = pl.pallas_call(
        body, name="even_mix", grid=(nt,),
        in_specs=[_rows(ts, half), _rows(ts, half, col=1), _rows(ts, d, col=1), _rows(ts, n), _rows(ts, n),
                  _rows(ts, d), _full(pw.shape), _full((1, half)), _full(cc_re.shape), _full(cc_im.shape),
                  _full((1, half)), _full((half, d)), _full((d, d))] + [ANY] * np_,
        out_specs=[_rows(ts, d), _rows(ts, d), _rows(ts, half), _rows(ts, half), _rows(ts, d)] + [ANY] * np_,
        out_shape=[SDS((s, d), F32), SDS((s, d), MM), SDS((s, half), F32), SDS((s, half), F32), SDS((s, d), F32)]
        + [p.out for p in pushes],
        scratch_shapes=_push_scratch(np_),
        compiler_params=_seq_params(56),
    )(pooled, proj, proj, x_re, x_im, x, pw, scale, cc_re, cc_im, d_skip, w_glu, w_out,
      *[p.array for p in pushes])
    return out[:5], out[5:]


def _glu(val, gt):
    return val * _sigmoid(gt)


def _odd_in(x1, g1, w_in):
    s, d = x1.shape
    e = w_in.shape[1]
    ts = 512

    def body(x_ref, g_ref, w_ref, proj_ref, glu_ref, h_ref):
        hb = _rms(x_ref[...], g_ref[...]).astype(MM)
        h_ref[...] = hb
        proj_ref[...] = jnp.dot(hb, w_ref[...], preferred_element_type=F32)
        glu_ref[...] = _glu(proj_ref[:, :d], proj_ref[:, d:2 * d])

    return pl.pallas_call(
        body, name="odd_in", grid=(s // ts,),
        in_specs=[_rows(ts, d), _full((1, d)), _full((d, e))],
        out_specs=[_rows(ts, e), _rows(ts, d), _rows(ts, d)],
        out_shape=[SDS((s, e), F32), SDS((s, d), F32), SDS((s, d), MM)],
        compiler_params=_seq_params(56),
    )(x1, g1, w_in)


CONV_ROWS = 32
CONV_COLS = 256


def _conv_scratch(ts, ch):
    return pltpu.VMEM((SUBLANES, ts + CONV_HALO, ch + LANES), F32)


def _phase_copies(sh, rows, ch):
    for o in range(1, SUBLANES):
        sh[o, 0:rows, 0:ch] = sh[0, pl.ds(o, rows), 0:ch]


def _conv_taps(sh, w_ref, offsets, r0, cols):
    acc = jnp.zeros((CONV_ROWS, cols.stop - cols.start), F32)
    for o in range(SUBLANES):
        taps = [(j, e // SUBLANES) for j, e in offsets if e % SUBLANES == o]
        if not taps:
            continue
        q0 = min(q for _, q in taps)
        q1 = max(q for _, q in taps)
        win = sh[o, pl.ds(r0 + SUBLANES * q0, CONV_ROWS + SUBLANES * (q1 - q0)), cols]
        for j, q in taps:
            lo = SUBLANES * (q - q0)
            acc = acc + w_ref[j:j + 1, cols] * win[lo:lo + CONV_ROWS]
    return acc


def _conv_fwd(g, w):
    s, ch = g.shape
    ts = 256
    per = ts // CONV_HALO
    lead = CONV_HALO - (CONV_KERNEL - 1)
    span = ts + CONV_HALO - SUBLANES

    def body(prev_ref, g_ref, w_ref, out_ref, sh):
        i = pl.program_id(0)
        sh[0, 0:CONV_HALO, 0:ch] = jnp.where(i == 0, 0.0, prev_ref[...])
        sh[0, CONV_HALO:, 0:ch] = g_ref[...]
        _phase_copies(sh, span, ch)

        offsets = [(j, lead + j) for j in range(CONV_KERNEL)]

        def block(rb, carry):
            r0 = pl.multiple_of(rb * CONV_ROWS, CONV_ROWS)
            for c in range(ch // CONV_COLS):
                cols = slice(c * CONV_COLS, (c + 1) * CONV_COLS)
                out_ref[pl.ds(r0, CONV_ROWS), cols] = _conv_taps(sh, w_ref, offsets, r0, cols)
            return carry

        lax.fori_loop(0, ts // CONV_ROWS, block, 0)

    return pl.pallas_call(
        body, name="conv_fwd", grid=(s // ts,),
        in_specs=[pl.BlockSpec((CONV_HALO, ch), lambda i: (jnp.maximum(i * per - 1, 0), 0)),
                  _rows(ts, ch), _full(w.shape)],
        out_specs=_rows(ts, ch),
        out_shape=SDS((s, ch), F32),
        scratch_shapes=[_conv_scratch(ts, ch)],
        compiler_params=_seq_params(40),
    )(g, g, w)


def _conv_bwd(dc, g, w, pushes):
    s, ch = dc.shape
    ts = 256
    per = ts // CONV_HALO
    n = s // ts
    lead = CONV_HALO - (CONV_KERNEL - 1)
    span = ts + CONV_HALO - SUBLANES
    np_ = len(pushes)

    def body(*refs):
        ins, outs, scratch, push_refs = _split_refs(refs, 5, 2, np_)
        dc_ref, next_ref, prev_ref, g_ref, w_ref = ins
        dg_ref, dw_ref = outs
        shd, shg, wacc = scratch
        i = pl.program_id(0)
        _riding_start(pushes, i == 0, push_refs)

        @pl.when(i == 0)
        def _():
            wacc[...] = jnp.zeros_like(wacc)

        shd[0, 0:ts, 0:ch] = dc_ref[...]
        shd[0, ts:, 0:ch] = jnp.where(i == n - 1, 0.0, next_ref[...])
        shg[0, 0:CONV_HALO, 0:ch] = jnp.where(i == 0, 0.0, prev_ref[...])
        shg[0, CONV_HALO:, 0:ch] = g_ref[...]
        _phase_copies(shd, span, ch)
        _phase_copies(shg, span, ch)

        offsets = [(j, CONV_KERNEL - 1 - j) for j in range(CONV_KERNEL)]

        def dg_block(rb, carry):
            r0 = pl.multiple_of(rb * CONV_ROWS, CONV_ROWS)
            for c in range(ch // CONV_COLS):
                cols = slice(c * CONV_COLS, (c + 1) * CONV_COLS)
                dg_ref[pl.ds(r0, CONV_ROWS), cols] = _conv_taps(shd, w_ref, offsets, r0, cols)
            return carry

        lax.fori_loop(0, ts // CONV_ROWS, dg_block, 0)

        tiles = ts // SUBLANES
        for c in range(ch // LANES):
            cols = slice(c * LANES, (c + 1) * LANES)
            cur = [shd[0, k * SUBLANES:(k + 1) * SUBLANES, cols] for k in range(tiles)]
            for j in range(CONV_KERNEL):
                q, o = divmod(lead + j, SUBLANES)
                parts = [None] * 4
                for k in range(tiles):
                    term = cur[k] * shg[o, (k + q) * SUBLANES:(k + q + 1) * SUBLANES, cols]
                    parts[k % 4] = term if parts[k % 4] is None else parts[k % 4] + term
                wacc[j, :, cols] += (parts[0] + parts[1]) + (parts[2] + parts[3])

        @pl.when(i == n - 1)
        def _():
            dw_ref[...] = jnp.zeros_like(dw_ref)
            for j in range(CONV_KERNEL):
                dw_ref[j:j + 1, :] = jnp.sum(wacc[j], axis=0, keepdims=True)

        _riding_wait(pushes, i == n - 1, push_refs)

    out = pl.pallas_call(
        body, name="conv_bwd", grid=(n,),
        in_specs=[_rows(ts, ch),
                  pl.BlockSpec((CONV_HALO, ch), lambda i: (jnp.minimum((i + 1) * per, s // CONV_HALO - 1), 0)),
                  pl.BlockSpec((CONV_HALO, ch), lambda i: (jnp.maximum(i * per - 1, 0), 0)),
                  _rows(ts, ch), _full(w.shape)] + [ANY] * np_,
        out_specs=[_rows(ts, ch), _full((CONV_HALO, ch))] + [ANY] * np_,
        out_shape=[SDS((s, ch), F32), SDS((CONV_HALO, ch), F32)] + [p.out for p in pushes],
        scratch_shapes=_push_scratch(np_) + [_conv_scratch(ts, ch), _conv_scratch(ts, ch),
                                             pltpu.VMEM((CONV_HALO, SUBLANES, ch), F32)],
        compiler_params=_seq_params(56),
    )(dc, dc, g, g, w, *[p.array for p in pushes])
    return out[0], out[1], out[2:]


def _conv_act(c, z1, cb, lg, lb):
    cc = c + cb
    mu = jnp.mean(cc, axis=-1, keepdims=True)
    dev = cc - mu
    var = jnp.mean(dev * dev, axis=-1, keepdims=True)
    cn = dev * lax.rsqrt(var + LN_EPS) * lg + lb
    return _silu(cn) * _silu(z1)


def _odd_out(c, proj1, x1, tgt, cb, lg, lb, w_out, fg):
    s, d = c.shape
    ts = 256

    def body(c_ref, z_ref, x1_ref, t_ref, cb_ref, lg_ref, lb_ref, w_ref, fg_ref,
             dc_ref, dz_ref, dx2_ref, y1_ref, loss_ref, dcb_ref, dlg_ref, dlb_ref, dfg_ref):
        @pl.when(pl.program_id(0) == 0)
        def _():
            for ref in (loss_ref, dcb_ref, dlg_ref, dlb_ref, dfg_ref):
                ref[...] = jnp.zeros_like(ref)

        y1, act_vjp = jax.vjp(_conv_act, c_ref[...], z_ref[...], cb_ref[...], lg_ref[...], lb_ref[...])
        y1b = y1.astype(MM)
        y1_ref[...] = y1b
        x2 = x1_ref[...] + jnp.dot(y1b, w_ref[...], preferred_element_type=F32)
        tgt_tile = t_ref[...]

        def head(x2, fg):
            err = jnp.square(_rms(x2, fg) - tgt_tile)
            return 0.5 * jnp.sum(jnp.mean(err, axis=-1))

        loss, (dx2, dfg) = jax.value_and_grad(head, argnums=(0, 1))(x2, fg_ref[...])
        loss_ref[...] += loss
        dfg_ref[...] += dfg
        dx2_ref[...] = dx2
        dy1 = _dot_nt(dx2, w_ref[...])
        dc, dz, dcb, dlg, dlb = act_vjp(dy1)
        dc_ref[...] = dc
        dz_ref[...] = dz
        dcb_ref[...] += dcb
        dlg_ref[...] += dlg
        dlb_ref[...] += dlb

    vec = SDS((1, d), F32)
    return pl.pallas_call(
        body, name="odd_out", grid=(s // ts,),
        in_specs=[_rows(ts, d), _rows(ts, d, col=2), _rows(ts, d), _rows(ts, d), _full((1, d)), _full((1, d)),
                  _full((1, d)), _full((d, d)), _full((1, d))],
        out_specs=[_rows(ts, d), _rows(ts, d), _rows(ts, d), _rows(ts, d), _full((SUBLANES, LANES)),
                   _full((1, d)), _full((1, d)), _full((1, d)), _full((1, d))],
        out_shape=[SDS((s, d), F32), SDS((s, d), F32), SDS((s, d), F32), SDS((s, d), MM),
                   SDS((SUBLANES, LANES), F32), vec, vec, vec, vec],
        compiler_params=_seq_params(56),
    )(c, proj1, x1, tgt, cb, lg, lb, w_out, fg)


def _odd_in_bwd(x1, proj1, dglu, dz1, dx2, g1, w_in):
    s, d = x1.shape
    e = w_in.shape[1]
    ts = 256

    def body(x_ref, vg_ref, dglu_ref, dz_ref, dx2_ref, g_ref, w_ref, dx1_ref, dp_ref, dg_ref):
        @pl.when(pl.program_id(0) == 0)
        def _():
            dg_ref[...] = jnp.zeros_like(dg_ref)

        _, glu_vjp = jax.vjp(_glu, vg_ref[:, :d], vg_ref[:, d:])
        dval, dgt = glu_vjp(dglu_ref[...])
        dp = jnp.concatenate([dval, dgt, dz_ref[...]], axis=1).astype(MM)
        dp_ref[...] = dp
        dh = _dot_nt(dp, w_ref[...])
        _, rms_vjp = jax.vjp(_rms, x_ref[...], g_ref[...])
        dxa, dg = rms_vjp(dh)
        dx1_ref[...] = dxa + dx2_ref[...]
        dg_ref[...] += dg

    return pl.pallas_call(
        body, name="odd_in_bwd", grid=(s // ts,),
        in_specs=[_rows(ts, d), _rows(ts, 2 * d), _rows(ts, d), _rows(ts, d), _rows(ts, d), _full((1, d)),
                  _full((d, e))],
        out_specs=[_rows(ts, d), _rows(ts, e), _full((1, d))],
        out_shape=[SDS((s, d), F32), SDS((s, e), MM), SDS((1, d), F32)],
        compiler_params=_seq_params(56),
    )(x1, proj1, dglu, dz1, dx2, g1, w_in)


def _even_mix_bwd(pooled, proj, mixed, gv, dx1, pw, scale, cc_re, cc_im, d_skip, w_glu, w_out, pushes):
    s, d = dx1.shape
    half = pooled.shape[1]
    n = cc_re.shape[0] * cc_re.shape[1]
    ts = 256
    nt = s // ts
    groups = len(POOL_WINDOWS)
    np_ = len(pushes)

    def body(*refs):
        ins, outs, _, push_refs = _split_refs(refs, 13, 10, np_)
        (pooled_ref, u_ref, z_ref, mixed_ref, gv_ref, dx1_ref, pw_ref, scale_ref, ccre_ref, ccim_ref, d_ref,
         wglu_ref, wout_ref) = ins
        (dpooled_ref, du_ref, dz_ref, dxre_ref, dxim_ref, dyin_ref, dgv_ref, dpw_ref, dscale_ref, dd_ref) = outs
        i = pl.program_id(0)
        _riding_start(pushes, i == 0, push_refs)

        @pl.when(i == 0)
        def _():
            for ref in (dpw_ref, dscale_ref, dd_ref):
                ref[...] = jnp.zeros_like(ref)

        dymix = _dot_nt(dx1_ref[...], wout_ref[...])
        _, a_vjp = jax.vjp(_mix_a, mixed_ref[...], scale_ref[...], z_ref[:, :half])
        dmixed, dscale, dza = a_vjp(dymix[:, :half])
        _, b_vjp = jax.vjp(_mix_b, gv_ref[:, :half], gv_ref[:, half:], z_ref[:, half:])
        dval, dgate, dzb = b_vjp(dymix[:, half:])
        dz_ref[:, :half] = dza
        dz_ref[:, half:] = dzb
        dscale_ref[...] += dscale
        dgv = jnp.concatenate([dval, dgate], axis=1).astype(MM)
        dgv_ref[...] = dgv
        dyin = _dot_nt(dgv, wglu_ref[...])
        dd_ref[...] += jnp.sum(dyin * u_ref[...], axis=0, keepdims=True)
        du_ref[...] = d_ref[...] * dyin
        dyb = dyin.astype(MM)
        dyin_ref[...] = dyb
        sb, cb = ccre_ref.shape[1:]
        for b in range(SSM_BLOCKS):
            states, chans = slice(b * sb, (b + 1) * sb), slice(b * cb, (b + 1) * cb)
            dxre_ref[:, states] = _dot_nt(dyb[:, chans], ccre_ref[b])
            dxim_ref[:, states] = -_dot_nt(dyb[:, chans], ccim_ref[b])
        for g in range(groups):
            cols = slice(g * LANES, (g + 1) * LANES)
            dm = dmixed[:, cols].astype(MM)
            dpooled_ref[:, cols] = _dot_nt(dm, pw_ref[g])
            dpw_ref[g] += _dot_tn(pooled_ref[:, cols], dm)

        _riding_wait(pushes, i == nt - 1, push_refs)

    out = pl.pallas_call(
        body, name="even_mix_bwd", grid=(nt,),
        in_specs=[_rows(ts, half), _rows(ts, half, col=1), _rows(ts, d, col=1), _rows(ts, half), _rows(ts, d),
                  _rows(ts, d), _full(pw.shape), _full((1, half)), _full(cc_re.shape), _full(cc_im.shape),
                  _full((1, half)), _full((half, d)), _full((d, d))] + [ANY] * np_,
        out_specs=[_rows(ts, half), _rows(ts, half), _rows(ts, d), _rows(ts, n), _rows(ts, n), _rows(ts, half),
                   _rows(ts, d), _full(pw.shape), _full((1, half)), _full((1, half))] + [ANY] * np_,
        out_shape=[SDS((s, half), F32), SDS((s, half), F32), SDS((s, d), F32), SDS((s, n), F32), SDS((s, n), F32),
                   SDS((s, half), MM), SDS((s, d), MM), SDS(pw.shape, F32), SDS((1, half), F32),
                   SDS((1, half), F32)] + [p.out for p in pushes],
        scratch_shapes=_push_scratch(np_),
        compiler_params=_seq_params(56),
    )(pooled, proj, proj, mixed, gv, dx1, pw, scale, cc_re, cc_im, d_skip, w_glu, w_out,
      *[p.array for p in pushes])
    return out[:10], out[10:]


def _ssm_bu_bwd(g_re, g_im, du_skip, bb):
    s, n = g_re.shape
    nb, cb, two_nb = bb.shape
    sb = two_nb // 2
    cin = nb * cb
    ts = 512

    def body(gre_ref, gim_ref, du_ref, bb_ref, out_ref):
        for b in range(nb):
            states, chans = slice(b * sb, (b + 1) * sb), slice(b * cb, (b + 1) * cb)
            out_ref[:, chans] = (du_ref[:, chans] + _dot_nt(gre_ref[:, states], bb_ref[b, :, :sb])
                                 + _dot_nt(gim_ref[:, states], bb_ref[b, :, sb:]))

    return pl.pallas_call(
        body, name="ssm_bu_bwd", grid=(s // ts,),
        in_specs=[_rows(ts, n), _rows(ts, n), _rows(ts, cin), _full(bb.shape)],
        out_specs=_rows(ts, cin),
        out_shape=SDS((s, cin), F32),
        compiler_params=_seq_params(48),
    )(g_re, g_im, du_skip, bb)


def _even_in_bwd(x, du_pool, du_ssm, dz, dx1, g0, w_in):
    s, d = x.shape
    half = du_pool.shape[1]
    shards, _, wc = w_in.shape
    ts = 512

    def body(x_ref, dup_ref, dus_ref, dz_ref, dx1_ref, g_ref, w_ref, gx_ref, dg_ref):
        @pl.when(pl.program_id(0) == 0)
        def _():
            dg_ref[...] = jnp.zeros_like(dg_ref)

        dp = jnp.concatenate([dup_ref[...], dus_ref[...], dz_ref[...]], axis=1).astype(MM)
        dh = _dot_nt(dp[:, :wc], w_ref[0])
        for k in range(1, shards):
            dh = dh + _dot_nt(dp[:, k * wc:(k + 1) * wc], w_ref[k])
        _, rms_vjp = jax.vjp(_rms, x_ref[...], g_ref[...])
        dxa, dg = rms_vjp(dh)
        gx_ref[...] = dxa + dx1_ref[...]
        dg_ref[...] += dg

    return pl.pallas_call(
        body, name="even_in_bwd", grid=(s // ts,),
        in_specs=[_rows(ts, d), _rows(ts, half), _rows(ts, half), _rows(ts, d), _rows(ts, d), _full((1, d)),
                  _full(w_in.shape)],
        out_specs=[_rows(ts, d), _full((1, d))],
        out_shape=[SDS((s, d), F32), SDS((1, d), F32)],
        compiler_params=_seq_params(48),
    )(x, du_pool, du_ssm, dz, dx1, g0, w_in)


def _discretise(log_dt, ar, ai, br, bi):
    dt = jnp.exp(log_dt)
    mag = jnp.exp(ar * dt)
    ang = ai * dt
    abr = mag * jnp.cos(ang)
    abi = mag * jnp.sin(ang)
    den = ar * ar + ai * ai
    nr = abr - 1.0
    ni = abi
    kr = (nr * ar + ni * ai) / den
    ki = (ni * ar - nr * ai) / den
    bbr = kr[None] * br - ki[None] * bi
    bbi = kr[None] * bi + ki[None] * br
    return abr, abi, bbr, bbi


def _whole(n):
    return [pl.BlockSpec(memory_space=pltpu.VMEM)] * n


def _disc_fwd(log_dt, ar, ai, br, bi):
    def body(ld_ref, ar_ref, ai_ref, br_ref, bi_ref, abr_ref, abi_ref, bbr_ref, bbi_ref):
        out = _discretise(ld_ref[...], ar_ref[...], ai_ref[...], br_ref[...], bi_ref[...])
        for ref, val in zip((abr_ref, abi_ref, bbr_ref, bbi_ref), out):
            ref[...] = val

    return pl.pallas_call(
        body, name="ssm_discretise", in_specs=_whole(5), out_specs=_whole(4),
        out_shape=[SDS(ar.shape, F32), SDS(ar.shape, F32), SDS(br.shape, F32), SDS(br.shape, F32)],
    )(log_dt, ar, ai, br, bi)


def _disc_bwd(log_dt, ar, ai, br, bi, dabr, dabi, dbbr, dbbi):
    def body(ld_ref, ar_ref, ai_ref, br_ref, bi_ref, dabr_ref, dabi_ref, dbbr_ref, dbbi_ref,
             dld_ref, dar_ref, dai_ref, dbr_ref, dbi_ref):
        _, vjp = jax.vjp(_discretise, ld_ref[...], ar_ref[...], ai_ref[...], br_ref[...], bi_ref[...])
        grads = vjp((dabr_ref[...], dabi_ref[...], dbbr_ref[...], dbbi_ref[...]))
        for ref, val in zip((dld_ref, dar_ref, dai_ref, dbr_ref, dbi_ref), grads):
            ref[...] = val

    return pl.pallas_call(
        body, name="ssm_discretise_bwd", in_specs=_whole(9), out_specs=_whole(5),
        out_shape=[SDS(log_dt.shape, F32), SDS(ar.shape, F32), SDS(ar.shape, F32), SDS(br.shape, F32),
                   SDS(br.shape, F32)],
    )(log_dt, ar, ai, br, bi, dabr, dabi, dbbr, dbbi)


def _block_diag(t):
    g, a, b = t.shape
    per = g // SSM_BLOCKS
    t = t.reshape(SSM_BLOCKS, per, a, b)
    same = jnp.eye(per, dtype=bool)[None, :, None, :, None]
    return jnp.where(same, t[:, :, :, None, :], 0.0).reshape(SSM_BLOCKS, per * a, per * b)


def _diag_blocks(m, g):
    per = g // SSM_BLOCKS
    a, b = m.shape[1] // per, m.shape[2] // per
    d = jnp.diagonal(m.reshape(SSM_BLOCKS, per, a, per, b), axis1=1, axis2=3)
    return jnp.moveaxis(d, -1, 1).reshape(g, a, b)


def _adamw(name, parts, w, m, v):
    rows, cols = w.shape
    tr = ADAM_ROWS if rows % ADAM_ROWS == 0 else rows

    def body(p_ref, w_ref, m_ref, v_ref, g_ref, d_ref, nm_ref, nv_ref):
        g = p_ref[0].astype(F32)
        for dev in range(1, N_DEV):
            g = g + p_ref[dev].astype(F32)
        g_ref[...] = g
        nm = ADAM_B1 * m_ref[...] + (1.0 - ADAM_B1) * g
        nv = ADAM_B2 * v_ref[...] + (1.0 - ADAM_B2) * jnp.square(g)
        nm_ref[...] = nm
        nv_ref[...] = nv
        m_hat = nm / (1.0 - ADAM_B1 ** ADAM_STEP)
        v_hat = nv / (1.0 - ADAM_B2 ** ADAM_STEP)
        d_ref[...] = -ADAM_LR * (m_hat / (jnp.sqrt(v_hat) + ADAM_EPS) + ADAM_WD * w_ref[...])

    tile = _rows(tr, cols)
    out = SDS((rows, cols), F32)
    return pl.pallas_call(
        body, name=name, grid=(rows // tr,),
        in_specs=[pl.BlockSpec((N_DEV, tr, cols), lambda i: (0, i, 0)), tile, tile, tile],
        out_specs=[tile, tile, tile, tile],
        out_shape=[out, out, out, out],
        compiler_params=_seq_params(32),
    )(parts, w, m, v)


def _as_rows(flat):
    n = flat.shape[-1]
    padded = _round_up(n, PACK_TILE)
    if padded != n:
        flat = jnp.pad(flat, [(0, 0)] * (flat.ndim - 1) + [(0, padded - n)])
    return flat.reshape(flat.shape[:-1] + (padded // LANES, LANES))


def _columns_to_shards(full):
    r, c8 = full.shape
    return jnp.transpose(full.reshape(r, N_DEV, c8 // N_DEV), (1, 0, 2))


def _shards_to_columns(shards):
    n, r, c = shards.shape
    return jnp.transpose(shards, (1, 0, 2)).reshape(r, n * c)


def _pack_local(tensors):
    return jnp.concatenate([_as_rows(t.reshape(-1)) for t in tensors], axis=0)


def _unpack_local(packed, shapes):
    out, r = [], 0
    for shape in shapes:
        n = 1
        for dim in shape:
            n *= dim
        rows = _round_up(n, PACK_TILE) // LANES
        out.append(packed[r:r + rows].reshape(-1)[:n].reshape(shape))
        r += rows
    return out


def kernel(x, even_norm, even_w_in, pool_w, pool_scale, ssm_log_dt, ssm_a_re, ssm_a_im, ssm_b_re, ssm_b_im, ssm_c_re, ssm_c_im, ssm_d, ssm_w_glu, even_w_out, odd_norm, odd_w_in, conv_w, conv_b, conv_ln_g, conv_ln_b, odd_w_out, final_norm, loss_target, m_even_norm, m_even_w_in, m_pool_w, m_pool_scale, m_ssm_log_dt, m_ssm_a_re, m_ssm_a_im, m_ssm_b_re, m_ssm_b_im, m_ssm_c_re, m_ssm_c_im, m_ssm_d, m_ssm_w_glu, m_even_w_out, m_odd_norm, m_odd_w_in, m_conv_w, m_conv_b, m_conv_ln_g, m_conv_ln_b, m_odd_w_out, m_final_norm, v_even_norm, v_even_w_in, v_pool_w, v_pool_scale, v_ssm_log_dt, v_ssm_a_re, v_ssm_a_im, v_ssm_b_re, v_ssm_b_im, v_ssm_c_re, v_ssm_c_im, v_ssm_d, v_ssm_w_glu, v_even_w_out, v_odd_norm, v_odd_w_in, v_conv_w, v_conv_b, v_conv_ln_g, v_conv_ln_b, v_odd_w_out, v_final_norm):
    given = dict(locals())
    xs = x[0]
    tgt = loss_target[0]
    d_model = xs.shape[1]

    def local(prefix, name):
        t = given[prefix + name]
        return t if name == 'final_norm' else t[0]

    def small_block(prefix):
        parts = [jnp.pad(local(prefix, 'conv_w'), ((0, SMALL_AT['odd_norm'] - CONV_KERNEL), (0, 0)))]
        for n in SMALL_VECTORS:
            parts.append(jnp.pad(local(prefix, n).reshape(1, LANES), ((0, SUBLANES - 1), (0, 0))))
        return jnp.concatenate(parts, axis=0)

    def as_tile(row):
        return jnp.pad(row, ((0, SUBLANES - row.shape[0]), (0, 0)))

    (w_in_e,) = _push_alone("gather_first", [_Push(local('', 'even_w_in').astype(MM), False)])
    later = [_Push(local('', 'ssm_w_glu').astype(MM), False), _Push(local('', 'even_w_out').astype(MM), False),
             _Push(local('', 'odd_w_in').astype(MM), False), _Push(local('', 'odd_w_out').astype(MM), False),
             _Push(small_block(''), False)]

    a_re, a_im = local('', 'ssm_a_re'), local('', 'ssm_a_im')
    groups, state = a_re.shape
    log_dt = local('', 'ssm_log_dt').reshape(groups, 1)
    b_re_t = jnp.transpose(local('', 'ssm_b_re'), (2, 0, 1))
    b_im_t = jnp.transpose(local('', 'ssm_b_im'), (2, 0, 1))
    abr, abi, bbr, bbi = _disc_fwd(log_dt, a_re, a_im, b_re_t, b_im_t)
    a_re_row = abr.reshape(1, groups * state)
    a_im_row = abi.reshape(1, groups * state)
    bb = jnp.concatenate([_block_diag(jnp.transpose(bbr, (1, 0, 2))), _block_diag(jnp.transpose(bbi, (1, 0, 2)))],
                         axis=2).astype(MM)
    cc_re = _block_diag(jnp.transpose(local('', 'ssm_c_re'), (0, 2, 1))).astype(MM)
    cc_im = _block_diag(jnp.transpose(local('', 'ssm_c_im'), (0, 2, 1))).astype(MM)
    pw = local('', 'pool_w').astype(MM)
    g0 = local('', 'even_norm').reshape(1, d_model)
    fg = local('', 'final_norm').reshape(1, d_model)
    scale = local('', 'pool_scale').reshape(1, -1)
    d_skip = local('', 'ssm_d').reshape(1, -1)

    proj0, h0, (wg3, wo_e3) = _even_in(xs, g0, w_in_e, later[:2])
    pooled = _pool_fwd(proj0)
    x_re, x_im, (wi_o3, wo_o3, small8) = _scan_fwd(a_re_row, a_im_row, proj0, bb, later[2:])
    w_glu = _shards_to_columns(wg3)
    w_out_e = wo_e3.reshape(d_model, d_model)
    (x1, ymix, mixed, yin, gv), _ = _even_mix(
        pooled, proj0, x_re, x_im, xs, pw, scale, cc_re, cc_im, d_skip, w_glu, w_out_e, [])
    w_in_o = _shards_to_columns(wi_o3)
    w_out_o = wo_o3.reshape(d_model, d_model)
    small = _shards_to_columns(small8)
    cw = small[:CONV_KERNEL]
    g1, cb, lg, lb = (small[SMALL_AT[n]:SMALL_AT[n] + 1] for n in SMALL_VECTORS)
    proj1, glu, h1 = _odd_in(x1, g1, w_in_o)
    conv = _conv_fwd(glu, cw)

    half = pooled.shape[1]
    n_state = groups * state
    rows_per = d_model // N_DEV
    dc, dz1, dx2, y1, loss_tile, dcb, dlg, dlb, dfg = _odd_out(conv, proj1, x1, tgt, cb, lg, lb, w_out_o, fg)
    g_odd_out = _mm_tn("dw_odd_out", y1, dx2, d_model, d_model, WIRE, ts=1024).reshape(N_DEV, rows_per, d_model)
    dglu, dcw, (r_odd_out,) = _conv_bwd(dc, glu, cw, [_Push(g_odd_out, True)])
    dx1, dproj1, dg1 = _odd_in_bwd(x1, proj1, dglu, dz1, dx2, g1, w_in_o)
    g_odd_in = _columns_to_shards(_mm_tn("dw_odd_in", h1, dproj1, d_model, 3 * d_model, WIRE))
    g_even_out = _mm_tn("dw_even_out", ymix, dx1, d_model, d_model, WIRE, ts=1024).reshape(N_DEV, rows_per, d_model)
    mix_grads, (r_odd_in,) = _even_mix_bwd(pooled, proj0, mixed, gv, dx1, pw, scale, cc_re, cc_im, d_skip, w_glu,
                                           w_out_e, [_Push(g_odd_in, True)])
    dpooled, du_skip, dz0, dx_re, dx_im, dyin, dgv, dpw, dscale, dd = mix_grads
    g_glu = _columns_to_shards(_mm_tn("dw_glu", yin, dgv, half, d_model, WIRE, ts=1024))
    sblk, cblk = n_state // SSM_BLOCKS, half // SSM_BLOCKS
    dcc_re = _mm_tn_blocks("dw_c_re", x_re, dyin, sblk, cblk, SSM_BLOCKS, ts=2048)
    dcc_im = _mm_tn_blocks("dw_c_im", x_im, dyin, sblk, cblk, SSM_BLOCKS, sign=-1.0, ts=2048)
    dc_re = jnp.transpose(_diag_blocks(dcc_re, groups), (0, 2, 1))
    dc_im = jnp.transpose(_diag_blocks(dcc_im, groups), (0, 2, 1))
    g_small = _columns_to_shards(jnp.concatenate(
        [dcw, as_tile(dg1), as_tile(dcb), as_tile(dlg), as_tile(dlb)], axis=0))
    early = {'pool_w': dpw, 'pool_scale': dscale, 'ssm_c_re': dc_re, 'ssm_c_im': dc_im, 'ssm_d': dd,
             'final_norm': dfg}
    g_early = _pack_local([early[n] for n in EARLY_REPLICATED] + [loss_tile])
    g_re, g_im, dabr, dabi, du_ssm, (r_even_out, r_glu, r_small, r_early) = _scan_bwd(
        a_re_row, a_im_row, dx_re, dx_im, x_re, x_im, du_skip, bb,
        [_Push(g_even_out, True), _Push(g_glu, True), _Push(g_small, True), _Push(g_early, False)])
    du_pool = _pool_bwd(dpooled)
    grad_x, dg0 = _even_in_bwd(xs, du_pool, du_ssm, dz0, dx1, g0, w_in_e)
    g_even_in = _columns_to_shards(jnp.concatenate(
        [_mm_tn("dw_even_in_pool", h0, du_pool, d_model, half, WIRE, ts=1024),
         _mm_tn("dw_even_in_ssm", h0, du_ssm, d_model, half, WIRE, ts=1024),
         _mm_tn("dw_even_in_gate", h0, dz0, d_model, d_model, WIRE, ts=1024)], axis=1))
    g_mid = _pack_local([dg0])
    dbb_re, dbb_im, (r_even_in, r_mid) = _dw_bbar(proj0, g_re, g_im, cblk, sblk,
                                                  [_Push(g_even_in, True), _Push(g_mid, False)])
    dbbr = jnp.transpose(_diag_blocks(dbb_re, groups), (2, 0, 1))
    dbbi = jnp.transpose(_diag_blocks(dbb_im, groups), (2, 0, 1))
    dld, dar, dai, dbr, dbi = _disc_bwd(log_dt, a_re, a_im, b_re_t, b_im_t,
                                        dabr.reshape(groups, state), dabi.reshape(groups, state), dbbr, dbbi)
    late = {'ssm_log_dt': dld, 'ssm_a_re': dar, 'ssm_a_im': dai,
            'ssm_b_re': jnp.transpose(dbr, (1, 2, 0)), 'ssm_b_im': jnp.transpose(dbi, (1, 2, 0))}
    g_late = _pack_local([late[n] for n in LATE_REPLICATED])
    (r_late,) = _push_alone("exchange_tail", [_Push(g_late, False)])

    results = {}
    for n, parts in (('even_w_in', r_even_in), ('ssm_w_glu', r_glu), ('even_w_out', r_even_out),
                     ('odd_w_in', r_odd_in), ('odd_w_out', r_odd_out)):
        results[n] = _adamw("adamw_" + n, parts, local('', n), local('m_', n), local('v_', n))
    small_out = _adamw("adamw_small", r_small, small_block(''), small_block('m_'), small_block('v_'))
    results['conv_w'] = [o[:CONV_KERNEL] for o in small_out]
    for n in SMALL_VECTORS:
        results[n] = [o[SMALL_AT[n]] for o in small_out]
    loss_rows = [jnp.zeros_like(loss_tile)]
    for names, parts, extra in ((EARLY_REPLICATED, r_early, loss_rows), (MID_REPLICATED, r_mid, []),
                               (LATE_REPLICATED, r_late, [])):
        packed = [_pack_local([local(p, n) for n in names] + extra) for p in ('', 'm_', 'v_')]
        out = _adamw("adamw_" + names[0], parts, *packed)
        unpacked = [_unpack_local(o, [given[n].shape for n in names]) for o in out]
        for k, n in enumerate(names):
            results[n] = [u[k] for u in unpacked]
        if extra:
            loss = out[0][-SUBLANES, 0]

    outs = [loss, grad_x[None]]
    for kind in range(4):
        outs.extend(results[n][kind].reshape(given[n].shape) for n in WEIGHTS)
    return tuple(outs)
```
